```python
import jax
import jax.numpy as jnp
from jax import lax
import numpy as np

D_MODEL = 1024
BATCH = 8
SEQ = 4096
DEPTH = 2

A_HEADS = 8
A_HEAD_DIM = 64
A_OUT = A_HEADS * A_HEAD_DIM
IDX_HEADS = 4
IDX_DIM = 64
TOPK_MAX = 256
Q_BLOCK = 128
ROPE_THETA = 500000.0
ROT_DIM = A_HEAD_DIM // 4
B_HEADS = 4
B_KEY_DIM = 64
B_VAL_DIM = 128
B_OUT = B_HEADS * B_VAL_DIM
B_GATE_RANK = 16
B_GATE_TAU = 16.0
B_CHUNK = 64
C_GROUPS = 4
C_GROUP_DIM = 128
C_OUT = C_GROUPS * C_GROUP_DIM
C_CHUNK = 128
N_GROUPS = 4
EXPERTS_PER_GROUP = 8
N_EXPERTS = N_GROUPS * EXPERTS_PER_GROUP
D_EXPERT = 256
TOP_K_INNER = 2
DN_ALPHA = (2 * DEPTH) ** 0.25
DN_BETA = (8 * DEPTH) ** -0.25
LN_EPS = 1e-5
RMS_EPS = 1e-6

SPLIT_WIDTHS = (A_OUT, A_HEAD_DIM, A_HEAD_DIM,
                IDX_HEADS * IDX_DIM, IDX_DIM, IDX_HEADS,
                B_HEADS * B_KEY_DIM, B_HEADS * B_KEY_DIM,
                B_OUT, B_GATE_RANK, B_OUT,
                2 * C_OUT,
                D_MODEL, D_MODEL, D_MODEL)
N_IN = sum(SPLIT_WIDTHS)

kernel_name = 'hybrid_dsa_gla_gmlp_hmoe'


def layer_norm(x, g, b=None):
    xf = x.astype(jnp.float32)
    mu = jnp.mean(xf, axis=-1, keepdims=True)
    var = jnp.mean(jnp.square(xf - mu), axis=-1, keepdims=True)
    y = (xf - mu) * lax.rsqrt(var + LN_EPS) * g.astype(jnp.float32)
    if b is not None:
        y = y + b.astype(jnp.float32)
    return y.astype(x.dtype)


def rms_norm(x, g):
    xf = x.astype(jnp.float32)
    y = xf * lax.rsqrt(jnp.mean(jnp.square(xf), axis=-1, keepdims=True) + RMS_EPS)
    return (y * g.astype(jnp.float32)).astype(x.dtype)


def rope_tables(positions):
    inv_freq = ROPE_THETA ** (-jnp.arange(0, ROT_DIM, 2, dtype=jnp.float32) / ROT_DIM)
    ang = positions.astype(jnp.float32)[..., None] * inv_freq
    return jnp.cos(ang), jnp.sin(ang)


def apply_partial_rope(x, cos, sin):
    half = ROT_DIM // 2
    c = cos[:, :, None, :].astype(x.dtype)
    s = sin[:, :, None, :].astype(x.dtype)
    x1 = x[..., :half]
    x2 = x[..., half:ROT_DIM]
    return jnp.concatenate([x1 * c - x2 * s, x1 * s + x2 * c, x[..., ROT_DIM:]], axis=-1)


def dsa_attention(q, k, v, qi, ki, wi):
    f32 = jnp.float32
    bsz, seq_len = q.shape[0], q.shape[1]
    n_sel = min(TOPK_MAX, seq_len // 4)
    n_blk = seq_len // Q_BLOCK
    scale = A_HEAD_DIM ** -0.5
    key_pos = jnp.arange(seq_len, dtype=jnp.int32)
    ki32 = ki.astype(f32)

    def to_blocks(t):
        return jnp.moveaxis(t.reshape((bsz, n_blk, Q_BLOCK) + t.shape[2:]), 1, 0)

    def block_fn(args):
        qb, qib, wib, start = args
        qpos = start + jnp.arange(Q_BLOCK, dtype=jnp.int32)
        causal = key_pos[None, :] <= qpos[:, None]
        dots = jnp.einsum('bqhd,bsd->bqhs', qib.astype(f32), ki32)
        score = jnp.einsum('bqhs,bqh->bqs', jax.nn.relu(dots), wib.astype(f32))
        score = jnp.where(causal[None], score, -jnp.inf)
        _, sel = lax.top_k(score, n_sel)
        valid = sel <= qpos[None, :, None]
        k_sel = jax.vmap(lambda kk, ii: kk[ii])(k, sel)
        v_sel = jax.vmap(lambda vv, ii: vv[ii])(v, sel)
        logits = jnp.einsum('bqhd,bqkd->bqhk', qb, k_sel).astype(f32) * scale
        logits = jnp.where(valid[:, :, None, :], logits, -jnp.inf)
        p = jax.nn.softmax(logits, axis=-1)
        return jnp.einsum('bqhk,bqkd->bqhd', p.astype(v.dtype), v_sel)

    starts = jnp.arange(n_blk, dtype=jnp.int32) * Q_BLOCK
    out = lax.map(block_fn, (to_blocks(q), to_blocks(qi), to_blocks(wi), starts))
    return jnp.moveaxis(out, 0, 1).reshape(bsz, seq_len, A_OUT)


def gla_chunked(q, k, v, log_a):
    f32 = jnp.float32
    bsz, seq_len, n_h, dk = q.shape
    dv = v.shape[-1]
    n_c = seq_len // B_CHUNK

    def chunks(t):
        return t.astype(f32).reshape(bsz, n_c, B_CHUNK, n_h, t.shape[-1]).transpose(1, 0, 3, 2, 4)

    tri = jnp.tril(jnp.ones((B_CHUNK, B_CHUNK), dtype=bool))

    def step(state, inp):
        qc, kc, vc, gc = inp
        b = jnp.cumsum(gc, axis=2)
        diff = b[:, :, :, None, :] - b[:, :, None, :, :]
        decay = jnp.exp(jnp.where(tri[:, :, None], diff, -jnp.inf))
        att = jnp.einsum('bhtk,bhtsk,bhsk->bhts', qc, decay, kc)
        o = jnp.einsum('bhts,bhsv->bhtv', att, vc) + jnp.einsum('bhtk,bhkv->bhtv', qc * jnp.exp(b), state)
        b_last = b[:, :, -1:, :]
        state = jnp.exp(b_last[:, :, 0, :])[..., None] * state + jnp.einsum('bhsk,bhsv->bhkv', kc * jnp.exp(b_last - b), vc)
        return state, o

    s0 = jnp.zeros((bsz, n_h, dk, dv), f32)
    _, o = lax.scan(step, s0, (chunks(q), chunks(k), chunks(v), chunks(log_a)))
    return o.transpose(1, 0, 3, 2, 4).reshape(bsz, seq_len, n_h, dv)


def chunked_spatial_gating(c_uv, ln_g, ln_b, w_s, b_s):
    bsz, seq_len = c_uv.shape[0], c_uv.shape[1]
    z = jax.nn.gelu(c_uv)
    u, v = z[..., :C_OUT], z[..., C_OUT:]
    v = layer_norm(v, ln_g, ln_b)
    n_c = seq_len // C_CHUNK
    vb = v.reshape(bsz, n_c, C_CHUNK, C_GROUPS, C_GROUP_DIM)
    w = w_s * jnp.tril(jnp.ones((C_CHUNK, C_CHUNK), dtype=w_s.dtype))[None]
    mixed = jnp.einsum('gts,bcsgd->bctgd', w, vb) + b_s.T[:, :, None]
    return u * mixed.reshape(bsz, seq_len, C_OUT)


def mixer_sublayer(x, cos, sin, w_in, idx_k_g, gla_wa2, gla_ba, gla_norm_g,
                   gm_ln_g, gm_ln_b, gm_ws, gm_bs, w_branch_a, w_branch_b, w_branch_c, w_out):
    bsz, seq_len, _ = x.shape
    points = np.cumsum(SPLIT_WIDTHS)[:-1].tolist()
    (a_q, a_k, a_v, i_q, i_k, i_w, b_q, b_k, b_v, b_glr, b_r, c_uv,
     g_a, g_b, g_c) = jnp.split(x @ w_in, points, axis=-1)

    q = apply_partial_rope(a_q.reshape(bsz, seq_len, A_HEADS, A_HEAD_DIM), cos, sin)
    k = apply_partial_rope(a_k[:, :, None, :], cos, sin)[:, :, 0]
    qi = apply_partial_rope(i_q.reshape(bsz, seq_len, IDX_HEADS, IDX_DIM), cos, sin)
    ki = apply_partial_rope(layer_norm(i_k, idx_k_g)[:, :, None, :], cos, sin)[:, :, 0]
    wi = i_w * (IDX_HEADS ** -0.5 * IDX_DIM ** -0.5)
    y_a = dsa_attention(q, k, a_v, qi, ki, wi)

    log_a = jax.nn.log_sigmoid((b_glr @ gla_wa2 + gla_ba).astype(jnp.float32)) / B_GATE_TAU
    o_b = gla_chunked(b_q.reshape(bsz, seq_len, B_HEADS, B_KEY_DIM) * (B_KEY_DIM ** -0.5),
                      b_k.reshape(bsz, seq_len, B_HEADS, B_KEY_DIM),
                      b_v.reshape(bsz, seq_len, B_HEADS, B_VAL_DIM),
                      log_a.reshape(bsz, seq_len, B_HEADS, B_KEY_DIM)).astype(x.dtype)
    y_b = jax.nn.silu(b_r) * rms_norm(o_b, gla_norm_g).reshape(bsz, seq_len, B_OUT)

    y_c = chunked_spatial_gating(c_uv, gm_ln_g, gm_ln_b, gm_ws, gm_bs)

    h = (jax.nn.sigmoid(g_a) * (y_a @ w_branch_a)
         + jax.nn.sigmoid(g_b) * (y_b @ w_branch_b)
         + jax.nn.sigmoid(g_c) * (y_c @ w_branch_c))
    return h @ w_out


def moe_sublayer(x, w_rg, b_rg, w_re, b_re, w_gate, w_up, w_down):
    bsz, seq_len, d = x.shape
    xf = x.reshape(-1, d)
    n_tok = xf.shape[0]
    g_prob = jax.nn.softmax((xf @ w_rg + b_rg).astype(jnp.float32), axis=-1)
    g_p, g_idx = lax.top_k(g_prob, 1)
    e_logits = (xf @ w_re + b_re).astype(jnp.float32).reshape(n_tok, N_GROUPS, EXPERTS_PER_GROUP)
    e_in = e_logits[jnp.arange(n_tok), g_idx[:, 0]]
    e_top, e_idx = lax.top_k(e_in, TOP_K_INNER)
    e_w = jax.nn.softmax(e_top, axis=-1) * g_p
    expert_id = g_idx * EXPERTS_PER_GROUP + e_idx
    combine = jnp.sum(jax.nn.one_hot(expert_id, N_EXPERTS, dtype=jnp.float32) * e_w[..., None], axis=1)
    combine = combine.astype(x.dtype).reshape(bsz, seq_len, N_EXPERTS)
    w_down_flat = w_down.reshape(N_EXPERTS * D_EXPERT, d)

    def per_seq(args):
        xs, cs = args
        hid = jax.nn.silu(jnp.einsum('ld,edf->lef', xs, w_gate)) * jnp.einsum('ld,edf->lef', xs, w_up)
        hid = hid * cs[:, :, None]
        return hid.reshape(xs.shape[0], N_EXPERTS * D_EXPERT) @ w_down_flat

    return lax.map(per_seq, (x, combine))


def setup_inputs(seed: int = 0) -> dict:
    key = jax.random.key(seed)
    ks = jax.random.split(key, 32)
    f32 = jnp.float32

    def nrm(k, shape, scale):
        return jax.random.normal(k, shape, f32) * scale

    def gain(k, shape):
        return 1.0 + 0.02 * jax.random.normal(k, shape, f32)

    def small(k, shape, s=0.02):
        return s * jax.random.normal(k, shape, f32)

    L = DEPTH
    x = jax.random.normal(ks[0], (BATCH, SEQ, D_MODEL), f32)
    offs = jax.random.randint(ks[1], (BATCH, 1), 0, 1024, dtype=jnp.int32)
    positions = offs + jnp.arange(SEQ, dtype=jnp.int32)[None, :]
    return {
        'x': x,
        'positions': positions,
        'ln_in_g': gain(ks[2], (D_MODEL,)),
        'ln_in_b': small(ks[3], (D_MODEL,)),
        'w_in': nrm(ks[4], (L, D_MODEL, N_IN), D_MODEL ** -0.5),
        'idx_k_g': gain(ks[5], (L, IDX_DIM)),
        'gla_wa2': nrm(ks[6], (L, B_GATE_RANK, B_HEADS * B_KEY_DIM), B_GATE_RANK ** -0.5),
        'gla_ba': small(ks[7], (L, B_HEADS * B_KEY_DIM), 0.1),
        'gla_norm_g': gain(ks[8], (L, B_VAL_DIM)),
        'gm_ln_g': gain(ks[9], (L, C_OUT)),
        'gm_ln_b': small(ks[10], (L, C_OUT)),
        'gm_ws': nrm(ks[11], (L, C_GROUPS, C_CHUNK, C_CHUNK), 0.5 * C_CHUNK ** -0.5),
        'gm_bs': gain(ks[12], (L, C_GROUPS, C_CHUNK)),
        'w_branch_a': nrm(ks[13], (L, A_OUT, D_MODEL), DN_BETA * A_OUT ** -0.5),
        'w_branch_b': nrm(ks[14], (L, B_OUT, D_MODEL), DN_BETA * B_OUT ** -0.5),
        'w_branch_c': nrm(ks[15], (L, C_OUT, D_MODEL), DN_BETA * C_OUT ** -0.5),
        'w_out': nrm(ks[16], (L, D_MODEL, D_MODEL), DN_BETA * D_MODEL ** -0.5),
        'ln1_g': gain(ks[17], (L, D_MODEL)),
        'ln1_b': small(ks[18], (L, D_MODEL)),
        'w_rg': nrm(ks[19], (L, D_MODEL, N_GROUPS), D_MODEL ** -0.5),
        'b_rg': small(ks[20], (L, N_GROUPS), 0.01),
        'w_re': nrm(ks[21], (L, D_MODEL, N_EXPERTS), D_MODEL ** -0.5),
        'b_re': small(ks[22], (L, N_EXPERTS), 0.01),
        'w_gate': nrm(ks[23], (L, N_EXPERTS, D_MODEL, D_EXPERT), D_MODEL ** -0.5),
        'w_up': nrm(ks[24], (L, N_EXPERTS, D_MODEL, D_EXPERT), D_MODEL ** -0.5),
        'w_down': nrm(ks[25], (L, N_EXPERTS, D_EXPERT, D_MODEL), DN_BETA * D_EXPERT ** -0.5),
        'ln2_g': gain(ks[26], (L, D_MODEL)),
        'ln2_b': small(ks[27], (L, D_MODEL)),
    }


def reference(x, positions, ln_in_g, ln_in_b, w_in, idx_k_g, gla_wa2, gla_ba, gla_norm_g,
              gm_ln_g, gm_ln_b, gm_ws, gm_bs, w_branch_a, w_branch_b, w_branch_c, w_out,
              ln1_g, ln1_b, w_rg, b_rg, w_re, b_re, w_gate, w_up, w_down, ln2_g, ln2_b):
    cos, sin = rope_tables(positions)
    h = layer_norm(x, ln_in_g, ln_in_b)
    for l in range(DEPTH):
        mix = mixer_sublayer(h, cos, sin, w_in[l], idx_k_g[l], gla_wa2[l], gla_ba[l], gla_norm_g[l],
                             gm_ln_g[l], gm_ln_b[l], gm_ws[l], gm_bs[l],
                             w_branch_a[l], w_branch_b[l], w_branch_c[l], w_out[l])
        h = layer_norm(DN_ALPHA * h + mix, ln1_g[l], ln1_b[l])
        ffn = moe_sublayer(h, w_rg[l], b_rg[l], w_re[l], b_re[l], w_gate[l], w_up[l], w_down[l])
        h = layer_norm(DN_ALPHA * h + ffn, ln2_g[l], ln2_b[l])
    return h
```

```python
import functools

import numpy as np
import jax
import jax.numpy as jnp
from jax import lax
from jax.experimental import pallas as pl
from jax.experimental.pallas import tpu as pltpu

F32 = jnp.float32
BF16 = jnp.bfloat16

D_MODEL = 1024
DEPTH = 2
A_HEADS = 8
A_HEAD_DIM = 64
A_OUT = 512
IDX_HEADS = 4
IDX_DIM = 64
TOPK_MAX = 256
Q_BLOCK = 128
ROPE_THETA = 500000.0
ROT_DIM = 16
B_HEADS = 4
B_KEY_DIM = 64
B_VAL_DIM = 128
B_OUT = 512
B_GATE_RANK = 16
B_GATE_TAU = 16.0
B_CHUNK = 64
B_SUB = 16
N_SUB = B_CHUNK // B_SUB
C_GROUPS = 4
C_GROUP_DIM = 128
C_OUT = 512
C_CHUNK = 128
N_GROUPS = 4
EXPERTS_PER_GROUP = 8
N_EXPERTS = 32
D_EXPERT = 256
DN_ALPHA = (2 * DEPTH) ** 0.25
LN_EPS = 1e-5
RMS_EPS = 1e-6

LANES = 128
NEG_BIG = -1e30
INT_MIN = -2 ** 31
VMEM_LIMIT = 56 * 1024 * 1024

HIGHEST = lax.Precision.HIGHEST


def _cparams(sem):
    return pltpu.CompilerParams(dimension_semantics=sem, vmem_limit_bytes=VMEM_LIMIT)


def _layer_norm(x, g, b):
    mu = jnp.mean(x, axis=-1, keepdims=True)
    xc = x - mu
    var = jnp.mean(xc * xc, axis=-1, keepdims=True)
    return xc * lax.rsqrt(var + LN_EPS) * g + b


def _dot(a, b):
    return jnp.dot(a, b, preferred_element_type=F32)


def _dot_nt(a, b):
    return lax.dot_general(a, b, (((1,), (1,)), ((), ())), preferred_element_type=F32)


def _ln_kernel(x_ref, g_ref, b_ref, o_ref):
    o_ref[...] = _layer_norm(x_ref[...], g_ref[...], b_ref[...])


def _entry_ln(x, g, b, tm=1024):
    n = x.shape[0]
    return pl.pallas_call(
        _ln_kernel,
        grid=(n // tm,),
        in_specs=[pl.BlockSpec((tm, D_MODEL), lambda i: (i, 0)),
                  pl.BlockSpec((1, D_MODEL), lambda i: (0, 0)),
                  pl.BlockSpec((1, D_MODEL), lambda i: (0, 0))],
        out_specs=pl.BlockSpec((tm, D_MODEL), lambda i: (i, 0)),
        out_shape=jax.ShapeDtypeStruct((n, D_MODEL), F32),
        compiler_params=_cparams(("parallel",)),
        name="entry_ln",
    )(x, g.reshape(1, -1), b.reshape(1, -1))


def _rope_slab(xs, c, s):
    lane = lax.broadcasted_iota(jnp.int32, xs.shape, 1) % A_HEAD_DIM
    fwd = pltpu.roll(xs, LANES - ROT_DIM // 2, axis=1)
    bwd = pltpu.roll(xs, ROT_DIM // 2, axis=1)
    partner = jnp.where(lane < ROT_DIM // 2, fwd, bwd)
    return xs * c + partner * s


def _proj_a_kernel(h_ref, w_ref, c_ref, s_ref, g_ref,
                   q_ref, k_ref, v_ref, iq_ref, ki_ref, wi_ref):
    p = _dot(h_ref[...].astype(BF16), w_ref[...])
    c = c_ref[...]
    s = s_ref[...]
    lane = lax.broadcasted_iota(jnp.int32, c.shape, 1)
    lo = lane < A_HEAD_DIM
    for i in range(4):
        sl = slice(i * LANES, (i + 1) * LANES)
        q_ref[:, sl] = (_rope_slab(p[:, sl], c, s) * (A_HEAD_DIM ** -0.5)).astype(BF16)
    kv = p[:, 512:640]
    kv = _rope_slab(kv, jnp.where(lo, c, 1.0), jnp.where(lo, s, 0.0))
    k_ref[...] = kv[:, :A_HEAD_DIM].astype(BF16)
    v_ref[...] = kv[:, A_HEAD_DIM:].astype(BF16)
    for i in range(2):
        sl = slice(640 + i * LANES, 640 + (i + 1) * LANES)
        iq_ref[:, i * LANES:(i + 1) * LANES] = _rope_slab(p[:, sl], c, s).astype(BF16)
    last = p[:, 896:1024]
    mu = jnp.sum(jnp.where(lo, last, 0.0), axis=-1, keepdims=True) * (1.0 / IDX_DIM)
    xc = jnp.where(lo, last - mu, 0.0)
    var = jnp.sum(xc * xc, axis=-1, keepdims=True) * (1.0 / IDX_DIM)
    kin = xc * lax.rsqrt(var + LN_EPS) * g_ref[...]
    kin = _rope_slab(kin, jnp.where(lo, c, 1.0), jnp.where(lo, s, 0.0))
    ki_ref[...] = kin[:, :IDX_DIM].astype(BF16)
    wi_ref[...] = pltpu.roll(last, LANES - IDX_DIM, axis=1) * (IDX_HEADS ** -0.5 * IDX_DIM ** -0.5)


def _proj_a(h, w_a, cos_t, sin_t, ikg, tm=512):
    n = h.shape[0]
    row = lambda w: pl.BlockSpec((tm, w), lambda i: (i, 0))
    return pl.pallas_call(
        _proj_a_kernel,
        grid=(n // tm,),
        in_specs=[row(D_MODEL),
                  pl.BlockSpec((D_MODEL, 1024), lambda i: (0, 0)),
                  row(LANES), row(LANES),
                  pl.BlockSpec((1, LANES), lambda i: (0, 0))],
        out_specs=[row(A_OUT), row(A_HEAD_DIM), row(A_HEAD_DIM),
                   row(IDX_HEADS * IDX_DIM), row(IDX_DIM), row(LANES)],
        out_shape=[jax.ShapeDtypeStruct((n, A_OUT), BF16),
                   jax.ShapeDtypeStruct((n, A_HEAD_DIM), BF16),
                   jax.ShapeDtypeStruct((n, A_HEAD_DIM), BF16),
                   jax.ShapeDtypeStruct((n, IDX_HEADS * IDX_DIM), BF16),
                   jax.ShapeDtypeStruct((n, IDX_DIM), BF16),
                   jax.ShapeDtypeStruct((n, LANES), F32)],
        compiler_params=_cparams(("parallel",)),
        name="proj_a",
    )(h, w_a, cos_t, sin_t, ikg)


def _dsa_kernel(q_ref, iq_ref, wi_ref, k_ref, v_ref, ki_ref, o_ref, bias_ref, *, seq):
    j = pl.program_id(1)
    n_sel = min(TOPK_MAX, seq // 4)
    iq = iq_ref[...]
    ki = ki_ref[...]
    wi = wi_ref[...]
    score = jnp.zeros((Q_BLOCK, seq), F32)
    for h in range(IDX_HEADS):
        d = _dot_nt(iq[:, h * IDX_DIM:(h + 1) * IDX_DIM], ki)
        score = score + wi[:, h:h + 1] * jnp.maximum(d, 0.0)
    qpos = j * Q_BLOCK + lax.broadcasted_iota(jnp.int32, (Q_BLOCK, 1), 0)
    kpos = lax.broadcasted_iota(jnp.int32, (1, seq), 1)
    causal = kpos <= qpos
    bits = pltpu.bitcast(score + 0.0, jnp.int32)
    key = bits ^ ((bits >> 31) & 0x7FFFFFFF)
    key = jnp.where(causal, key, INT_MIN)
    kk = jnp.minimum(n_sel, qpos + 1).astype(F32)

    def search(i, t_u):
        cand_u = t_u | jnp.left_shift(jnp.int32(1), 31 - i)
        cnt = jnp.sum(jnp.where(key >= (cand_u ^ INT_MIN), 1.0, 0.0), axis=1, keepdims=True)
        return jnp.where(cnt >= kk, cand_u, t_u)

    t_u = lax.fori_loop(0, 32, search, jnp.zeros((Q_BLOCK, 1), jnp.int32))
    thr = t_u ^ INT_MIN
    gt = key > thr
    eq = key == thr
    need = kk - jnp.sum(jnp.where(gt, 1.0, 0.0), axis=1, keepdims=True)
    r_i = lax.broadcasted_iota(jnp.int32, (LANES, LANES), 0)
    c_i = lax.broadcasted_iota(jnp.int32, (LANES, LANES), 1)
    upper = jnp.where(r_i < c_i, 1.0, 0.0).astype(BF16)
    carry = jnp.zeros((Q_BLOCK, 1), F32)
    for c in range(seq // LANES):
        sl = slice(c * LANES, (c + 1) * LANES)
        eq_c = eq[:, sl]
        eq_f = jnp.where(eq_c, 1.0, 0.0)
        pre = _dot(eq_f.astype(BF16), upper) + carry
        sel = gt[:, sl] | (eq_c & (pre < need))
        bias_ref[:, sl] = jnp.where(sel, 0.0, NEG_BIG)
        carry = carry + jnp.sum(eq_f, axis=1, keepdims=True)

    q = q_ref[...]
    k = k_ref[...]
    v = v_ref[...]
    for h in range(A_HEADS):
        sl = slice(h * A_HEAD_DIM, (h + 1) * A_HEAD_DIM)
        lg = _dot_nt(q[:, sl], k) + bias_ref[...]
        m = jnp.max(lg, axis=1, keepdims=True)
        p = jnp.exp(lg - m)
        l = jnp.sum(p, axis=1, keepdims=True)
        o = _dot(p.astype(BF16), v) / l
        o_ref[:, sl] = o.astype(BF16)


def _dsa(q, iq, wi, k, v, ki, bsz, seq):
    nq = seq // Q_BLOCK
    qspec = lambda w: pl.BlockSpec((None, Q_BLOCK, w), lambda b, j: (b, j, 0))
    kspec = lambda w: pl.BlockSpec((None, seq, w), lambda b, j: (b, 0, 0))
    r3 = lambda t: t.reshape(bsz, seq, t.shape[-1])
    out = pl.pallas_call(
        functools.partial(_dsa_kernel, seq=seq),
        grid=(bsz, nq),
        in_specs=[qspec(A_OUT), qspec(IDX_HEADS * IDX_DIM), qspec(LANES),
                  kspec(A_HEAD_DIM), kspec(A_HEAD_DIM), kspec(IDX_DIM)],
        out_specs=qspec(A_OUT),
        out_shape=jax.ShapeDtypeStruct((bsz, seq, A_OUT), BF16),
        scratch_shapes=[pltpu.VMEM((Q_BLOCK, seq), F32)],
        compiler_params=_cparams(("parallel", "parallel")),
        name="dsa",
    )(r3(q), r3(iq), r3(wi), r3(k), r3(v), r3(ki))
    return out.reshape(bsz * seq, A_OUT)


GLA_REP = B_HEADS * N_SUB * B_KEY_DIM
GLA_COLS = 2 * GLA_REP + 2 * B_OUT + LANES


def _gla_kernel(h_ref, w_ref, wa_ref, ba_ref, ng_ref, o_ref, p_ref, g_ref, acc_ref, st_ref, *, ts):
    @pl.when(pl.program_id(1) == 0)
    def _():
        st_ref[...] = jnp.zeros_like(st_ref)

    p_ref[...] = _dot(h_ref[...].astype(BF16), w_ref[...])
    x = _dot(p_ref[:, 2 * GLA_REP + 2 * B_OUT:].astype(BF16), wa_ref[...]) + ba_ref[...]
    g_ref[...] = (jnp.minimum(x, 0.0) - jnp.log1p(jnp.exp(-jnp.abs(x)))) * (1.0 / B_GATE_TAU)

    t_i = lax.broadcasted_iota(jnp.int32, (B_CHUNK, B_CHUNK), 0)
    s_i = lax.broadcasted_iota(jnp.int32, (B_CHUNK, B_CHUNK), 1)
    tri = t_i >= s_i
    tri_f = jnp.where(tri, 1.0, 0.0)
    lane_sub = (lax.broadcasted_iota(jnp.int32, (1, GLA_REP), 1) // B_KEY_DIM) % N_SUB
    row_sub = lax.broadcasted_iota(jnp.int32, (B_CHUNK, 1), 0) // B_SUB
    q_mask = lane_sub <= row_sub
    k_mask = lane_sub == row_sub

    def chunk(c, carry):
        rows = pl.ds(pl.multiple_of(c * B_CHUNK, B_CHUNK), B_CHUNK)
        b = jnp.dot(tri_f, g_ref[rows, :], precision=HIGHEST, preferred_element_type=F32)
        ref = jnp.zeros((1, GLA_REP), F32)
        for jj in range(1, N_SUB):
            ref = ref + jnp.where(lane_sub == jj, b[jj * B_SUB - 1:jj * B_SUB, :], 0.0)
        d = b - ref
        e_q = jnp.where(q_mask, jnp.exp(jnp.minimum(d, 0.0)), 0.0)
        e_k = jnp.where(k_mask, jnp.exp(jnp.where(k_mask, -d, 0.0)), 0.0)
        b_last = b[B_CHUNK - 1:B_CHUNK, :]
        q_c = (p_ref[rows, 0:GLA_REP] * e_q).astype(BF16)
        k_raw = p_ref[rows, GLA_REP:2 * GLA_REP]
        k_c = (k_raw * e_k).astype(BF16)
        k_st = (k_raw * jnp.exp(b_last - b)).astype(BF16)
        e_last = jnp.exp(b_last)
        for hh in range(B_HEADS):
            base = hh * N_SUB * B_KEY_DIM
            q_h = q_c[:, base:base + N_SUB * B_KEY_DIM]
            k_h = k_c[:, base:base + N_SUB * B_KEY_DIM]
            att = jnp.where(tri, _dot_nt(q_h, k_h), 0.0)
            v_h = p_ref[rows, 2 * GLA_REP + hh * B_VAL_DIM:2 * GLA_REP + (hh + 1) * B_VAL_DIM].astype(BF16)
            st = st_ref[hh]
            o = _dot(att.astype(BF16), v_h) + _dot_nt(q_h[:, :B_KEY_DIM], st.astype(BF16))
            acc_ref[rows, hh * B_VAL_DIM:(hh + 1) * B_VAL_DIM] = o
            upd = _dot(v_h.T, k_st[:, base:base + B_KEY_DIM])
            st_ref[hh] = st * e_last[:, base:base + B_KEY_DIM] + upd
        return carry

    lax.fori_loop(0, ts // B_CHUNK, chunk, 0)

    r = p_ref[:, 2 * GLA_REP + B_OUT:2 * GLA_REP + 2 * B_OUT]
    gate = r * (1.0 / (1.0 + jnp.exp(-r)))
    ng = ng_ref[...]
    for hh in range(B_HEADS):
        sl = slice(hh * B_VAL_DIM, (hh + 1) * B_VAL_DIM)
        o = acc_ref[:, sl]
        y = o * lax.rsqrt(jnp.mean(o * o, axis=-1, keepdims=True) + RMS_EPS) * ng
        o_ref[:, sl] = (gate[:, sl] * y).astype(BF16)


def _gla(h, w_b, wa_rep, ba_rep, norm_g, bsz, seq, ts=512):
    ns = seq // ts
    const = lambda shape: pl.BlockSpec(shape, lambda b, i: (0, 0))
    return pl.pallas_call(
        functools.partial(_gla_kernel, ts=ts),
        grid=(bsz, ns),
        in_specs=[pl.BlockSpec((ts, D_MODEL), lambda b, i: (b * ns + i, 0)),
                  const((D_MODEL, GLA_COLS)), const((LANES, GLA_REP)),
                  const((1, GLA_REP)), const((1, B_VAL_DIM))],
        out_specs=pl.BlockSpec((ts, B_OUT), lambda b, i: (b * ns + i, 0)),
        out_shape=jax.ShapeDtypeStruct((bsz * seq, B_OUT), BF16),
        scratch_shapes=[pltpu.VMEM((ts, GLA_COLS), F32),
                        pltpu.VMEM((ts, GLA_REP), F32),
                        pltpu.VMEM((ts, B_OUT), F32),
                        pltpu.VMEM((B_HEADS, B_VAL_DIM, B_KEY_DIM), F32)],
        compiler_params=_cparams(("parallel", "arbitrary")),
        name="gla",
    )(h, w_b, wa_rep, ba_rep, norm_g)


def _gmlp_kernel(h_ref, w_ref, lg_ref, lb_ref, ws_ref, bs_ref, o_ref, *, tm):
    z = _dot(h_ref[...].astype(BF16), w_ref[...])
    z = z * (0.5 * (1.0 + jnp.tanh(np.sqrt(2.0 / np.pi) * (z + 0.044715 * (z * z * z)))))
    u = z[:, :C_OUT]
    v = _layer_norm(z[:, C_OUT:], lg_ref[...], lb_ref[...]).astype(BF16)
    t_i = lax.broadcasted_iota(jnp.int32, (C_CHUNK, C_CHUNK), 0)
    s_i = lax.broadcasted_iota(jnp.int32, (C_CHUNK, C_CHUNK), 1)
    bs = bs_ref[...]
    for g in range(C_GROUPS):
        w = jnp.where(t_i >= s_i, ws_ref[g], 0.0).astype(BF16)
        sl = slice(g * C_GROUP_DIM, (g + 1) * C_GROUP_DIM)
        for c in range(tm // C_CHUNK):
            rows = slice(c * C_CHUNK, (c + 1) * C_CHUNK)
            mixed = _dot(w, v[rows, sl]) + bs[:, g:g + 1]
            o_ref[rows, sl] = (u[rows, sl] * mixed).astype(BF16)


def _gmlp(h, w_c, ln_g, ln_b, ws, bs_t, tm=512):
    n = h.shape[0]
    return pl.pallas_call(
        functools.partial(_gmlp_kernel, tm=tm),
        grid=(n // tm,),
        in_specs=[pl.BlockSpec((tm, D_MODEL), lambda i: (i, 0)),
                  pl.BlockSpec((D_MODEL, 2 * C_OUT), lambda i: (0, 0)),
                  pl.BlockSpec((1, C_OUT), lambda i: (0, 0)),
                  pl.BlockSpec((1, C_OUT), lambda i: (0, 0)),
                  pl.BlockSpec((C_GROUPS, C_CHUNK, C_CHUNK), lambda i: (0, 0, 0)),
                  pl.BlockSpec((C_CHUNK, LANES), lambda i: (0, 0))],
        out_specs=pl.BlockSpec((tm, C_OUT), lambda i: (i, 0)),
        out_shape=jax.ShapeDtypeStruct((n, C_OUT), BF16),
        compiler_params=_cparams(("parallel",)),
        name="gmlp",
    )(h, w_c, ln_g, ln_b, ws, bs_t)


def _sigmoid(x):
    return 1.0 / (1.0 + jnp.exp(-x))


def _merge_kernel(h_ref, ya_ref, yb_ref, yc_ref, wg_ref, wa_ref, wb_ref, wc_ref, wo_ref,
                  lg_ref, lb_ref, wr_ref, br_ref, x_ref, xb_ref, comb_ref):
    h = h_ref[...]
    hb = h.astype(BF16)
    acc = None
    for i, (y_ref, wbr_ref) in enumerate(((ya_ref, wa_ref), (yb_ref, wb_ref), (yc_ref, wc_ref))):
        gate = _sigmoid(_dot(hb, wg_ref[:, i * D_MODEL:(i + 1) * D_MODEL]))
        term = gate * _dot(y_ref[...], wbr_ref[...])
        acc = term if acc is None else acc + term
    mix = _dot(acc.astype(BF16), wo_ref[...])
    x1 = _layer_norm(DN_ALPHA * h + mix, lg_ref[...], lb_ref[...])
    x_ref[...] = x1
    xb_ref[...] = x1.astype(BF16)

    logit = jnp.dot(x1, wr_ref[...], precision=HIGHEST, preferred_element_type=F32) + br_ref[...]
    lane = lax.broadcasted_iota(jnp.int32, logit.shape, 1)
    big = jnp.int32(LANES)
    is_g = lane < N_GROUPS
    gl = jnp.where(is_g, logit, -jnp.inf)
    g_max = jnp.max(gl, axis=-1, keepdims=True)
    g_idx = jnp.min(jnp.where(is_g & (gl == g_max), lane, big), axis=-1, keepdims=True)
    g_p = 1.0 / jnp.sum(jnp.where(is_g, jnp.exp(gl - g_max), 0.0), axis=-1, keepdims=True)
    lo = N_GROUPS + g_idx * EXPERTS_PER_GROUP
    in_g = (lane >= lo) & (lane < lo + EXPERTS_PER_GROUP)
    el = jnp.where(in_g, logit, -jnp.inf)
    e1 = jnp.max(el, axis=-1, keepdims=True)
    i1 = jnp.min(jnp.where(in_g & (el == e1), lane, big), axis=-1, keepdims=True)
    el2 = jnp.where(lane == i1, -jnp.inf, el)
    e2 = jnp.max(el2, axis=-1, keepdims=True)
    i2 = jnp.min(jnp.where(in_g & (lane != i1) & (el2 == e2), lane, big), axis=-1, keepdims=True)
    z2 = jnp.exp(e2 - e1)
    den = 1.0 + z2
    comb_ref[...] = jnp.where(lane == i1, (1.0 / den) * g_p,
                              jnp.where(lane == i2, (z2 / den) * g_p, 0.0))


def _merge(h, ya, yb, yc, wg, wa, wb, wc, wo, ln_g, ln_b, wr, br, tm=512):
    n = h.shape[0]
    row = lambda w: pl.BlockSpec((tm, w), lambda i: (i, 0))
    const = lambda r, c: pl.BlockSpec((r, c), lambda i: (0, 0))
    return pl.pallas_call(
        _merge_kernel,
        grid=(n // tm,),
        in_specs=[row(D_MODEL), row(A_OUT), row(B_OUT), row(C_OUT),
                  const(D_MODEL, 3 * D_MODEL), const(A_OUT, D_MODEL), const(B_OUT, D_MODEL),
                  const(C_OUT, D_MODEL), const(D_MODEL, D_MODEL),
                  const(1, D_MODEL), const(1, D_MODEL), const(D_MODEL, LANES), const(1, LANES)],
        out_specs=[row(D_MODEL), row(D_MODEL), row(LANES)],
        out_shape=[jax.ShapeDtypeStruct((n, D_MODEL), F32),
                   jax.ShapeDtypeStruct((n, D_MODEL), BF16),
                   jax.ShapeDtypeStruct((n, LANES), F32)],
        compiler_params=_cparams(("parallel",)),
        name="merge",
    )(h, ya, yb, yc, wg, wa, wb, wc, wo, ln_g, ln_b, wr, br)


def _moe_kernel(x_ref, xb_ref, comb_ref, wg_ref, wu_ref, wd_ref, lg_ref, lb_ref, o_ref, acc_ref):
    e = pl.program_id(1)

    @pl.when(e == 0)
    def _():
        acc_ref[...] = jnp.zeros_like(acc_ref)

    xb = xb_ref[...]
    comb = comb_ref[...]
    lane = lax.broadcasted_iota(jnp.int32, comb.shape, 1)
    c_e = jnp.sum(jnp.where(lane == e + N_GROUPS, comb, 0.0), axis=-1, keepdims=True)
    gte = _dot(xb, wg_ref[...])
    up = _dot(xb, wu_ref[...])
    hid = gte * _sigmoid(gte) * up * c_e
    acc_ref[...] += _dot(hid.astype(BF16), wd_ref[...])

    @pl.when(e == N_EXPERTS - 1)
    def _():
        o_ref[...] = _layer_norm(DN_ALPHA * x_ref[...] + acc_ref[...], lg_ref[...], lb_ref[...])


def _moe(x1, x1b, comb, w_gate, w_up, w_down, ln_g, ln_b, tm=1024):
    n = x1.shape[0]
    row = lambda w: pl.BlockSpec((tm, w), lambda i, e: (i, 0))
    return pl.pallas_call(
        _moe_kernel,
        grid=(n // tm, N_EXPERTS),
        in_specs=[row(D_MODEL), row(D_MODEL), row(LANES),
                  pl.BlockSpec((None, D_MODEL, D_EXPERT), lambda i, e: (e, 0, 0)),
                  pl.BlockSpec((None, D_MODEL, D_EXPERT), lambda i, e: (e, 0, 0)),
                  pl.BlockSpec((None, D_EXPERT, D_MODEL), lambda i, e: (e, 0, 0)),
                  pl.BlockSpec((1, D_MODEL), lambda i, e: (0, 0)),
                  pl.BlockSpec((1, D_MODEL), lambda i, e: (0, 0))],
        out_specs=row(D_MODEL),
        out_shape=jax.ShapeDtypeStruct((n, D_MODEL), F32),
        scratch_shapes=[pltpu.VMEM((tm, D_MODEL), F32)],
        compiler_params=_cparams(("parallel", "arbitrary")),
        name="moe",
    )(x1, x1b, comb, w_gate, w_up, w_down, ln_g, ln_b)


def _rope_tables(positions):
    inv_freq = ROPE_THETA ** (-jnp.arange(0, ROT_DIM, 2, dtype=F32) / ROT_DIM)
    ang = positions.astype(F32).reshape(-1, 1) * inv_freq
    cos, sin = jnp.cos(ang), jnp.sin(ang)
    n = ang.shape[0]
    half = ROT_DIM // 2
    c64 = jnp.concatenate([cos, cos, jnp.ones((n, A_HEAD_DIM - ROT_DIM), F32)], axis=1)
    s64 = jnp.concatenate([-sin, sin, jnp.zeros((n, A_HEAD_DIM - ROT_DIM), F32)], axis=1)
    del half
    return jnp.tile(c64, (1, 2)), jnp.tile(s64, (1, 2))


def _rep_cols(w):
    r = w.shape[0]
    w4 = w.reshape(r, B_HEADS, 1, B_KEY_DIM)
    return jnp.broadcast_to(w4, (r, B_HEADS, N_SUB, B_KEY_DIM)).reshape(r, GLA_REP)


def _split_w_in(w):
    widths = (A_OUT, A_HEAD_DIM, A_HEAD_DIM, IDX_HEADS * IDX_DIM, IDX_DIM, IDX_HEADS,
              B_HEADS * B_KEY_DIM, B_HEADS * B_KEY_DIM, B_OUT, B_GATE_RANK, B_OUT,
              2 * C_OUT, D_MODEL, D_MODEL, D_MODEL)
    pts = np.cumsum((0,) + widths)
    return [w[:, int(pts[i]):int(pts[i + 1])] for i in range(len(widths))]


def kernel(x, positions, ln_in_g, ln_in_b, w_in, idx_k_g, gla_wa2, gla_ba, gla_norm_g, gm_ln_g, gm_ln_b, gm_ws, gm_bs, w_branch_a, w_branch_b, w_branch_c, w_out, ln1_g, ln1_b, w_rg, b_rg, w_re, b_re, w_gate, w_up, w_down, ln2_g, ln2_b):
    bsz, seq, d = x.shape
    n = bsz * seq
    cos_t, sin_t = _rope_tables(positions)
    h = _entry_ln(x.reshape(n, d), ln_in_g, ln_in_b)
    for l in range(DEPTH):
        (a_q, a_k, a_v, i_q, i_k, i_w, b_q, b_k, b_v, b_glr, b_r, c_uv,
         g_a, g_b, g_c) = _split_w_in(w_in[l])
        zpad = lambda c: jnp.zeros((d, c), F32)
        w_a = jnp.concatenate([a_q, a_k, a_v, i_q, i_k, i_w, zpad(LANES - IDX_DIM - IDX_HEADS)],
                              axis=1).astype(BF16)
        w_b = jnp.concatenate([_rep_cols(b_q * (B_KEY_DIM ** -0.5)), _rep_cols(b_k), b_v, b_r,
                               b_glr, zpad(LANES - B_GATE_RANK)], axis=1).astype(BF16)
        wa_rep = jnp.concatenate([_rep_cols(gla_wa2[l]),
                                  jnp.zeros((LANES - B_GATE_RANK, GLA_REP), F32)], axis=0).astype(BF16)
        ba_rep = _rep_cols(gla_ba[l].reshape(1, -1))
        w_g = jnp.concatenate([g_a, g_b, g_c], axis=1).astype(BF16)
        ikg = jnp.concatenate([idx_k_g[l], jnp.zeros((LANES - IDX_DIM,), F32)]).reshape(1, LANES)
        bs_t = jnp.concatenate([gm_bs[l].T, jnp.zeros((C_CHUNK, LANES - C_GROUPS), F32)], axis=1)
        w_r = jnp.concatenate([w_rg[l], w_re[l], zpad(LANES - N_GROUPS - N_EXPERTS)], axis=1)
        b_r_all = jnp.concatenate([b_rg[l], b_re[l],
                                   jnp.zeros((LANES - N_GROUPS - N_EXPERTS,), F32)]).reshape(1, LANES)

        q, k, v, iq, ki, wi = _proj_a(h, w_a, cos_t, sin_t, ikg)
        y_a = _dsa(q, iq, wi, k, v, ki, bsz, seq)
        y_b = _gla(h, w_b, wa_rep, ba_rep, gla_norm_g[l].reshape(1, -1), bsz, seq)
        y_c = _gmlp(h, c_uv.astype(BF16), gm_ln_g[l].reshape(1, -1), gm_ln_b[l].reshape(1, -1),
                    gm_ws[l], bs_t)
        x1, x1b, comb = _merge(h, y_a, y_b, y_c, w_g,
                               w_branch_a[l].astype(BF16), w_branch_b[l].astype(BF16),
                               w_branch_c[l].astype(BF16), w_out[l].astype(BF16),
                               ln1_g[l].reshape(1, -1), ln1_b[l].reshape(1, -1), w_r, b_r_all)
        h = _moe(x1, x1b, comb, w_gate[l].astype(BF16), w_up[l].astype(BF16),
                 w_down[l].astype(BF16), ln2_g[l].reshape(1, -1), ln2_b[l].reshape(1, -1))
    return h.reshape(bsz, seq, d)
```

```python
import functools

import numpy as np
import jax
import jax.numpy as jnp
from jax import lax
from jax.experimental import pallas as pl
from jax.experimental.pallas import tpu as pltpu

F32 = jnp.float32
BF16 = jnp.bfloat16

D_MODEL = 1024
DEPTH = 2
A_HEADS = 8
A_HEAD_DIM = 64
A_OUT = 512
IDX_HEADS = 4
IDX_DIM = 64
TOPK_MAX = 256
Q_BLOCK = 128
ROPE_THETA = 500000.0
ROT_DIM = 16
B_HEADS = 4
B_KEY_DIM = 64
B_VAL_DIM = 128
B_OUT = 512
B_GATE_RANK = 16
B_GATE_TAU = 16.0
B_CHUNK = 64
B_SUB = 16
N_SUB = B_CHUNK // B_SUB
C_GROUPS = 4
C_GROUP_DIM = 128
C_OUT = 512
C_CHUNK = 128
N_GROUPS = 4
EXPERTS_PER_GROUP = 8
N_EXPERTS = 32
D_EXPERT = 256
DN_ALPHA = (2 * DEPTH) ** 0.25
LN_EPS = 1e-5
RMS_EPS = 1e-6

LANES = 128
NEG_BIG = -1e30
INT_MIN = -2 ** 31
VMEM_LIMIT = 56 * 1024 * 1024

HIGHEST = lax.Precision.HIGHEST


def _cparams(sem):
    return pltpu.CompilerParams(dimension_semantics=sem, vmem_limit_bytes=VMEM_LIMIT)


def _layer_norm(x, g, b):
    mu = jnp.mean(x, axis=-1, keepdims=True)
    xc = x - mu
    var = jnp.mean(xc * xc, axis=-1, keepdims=True)
    return xc * lax.rsqrt(var + LN_EPS) * g + b


def _dot(a, b):
    return jnp.dot(a, b, preferred_element_type=F32)


def _dot_nt(a, b):
    return lax.dot_general(a, b, (((1,), (1,)), ((), ())), preferred_element_type=F32)


def _ln_kernel(x_ref, g_ref, b_ref, o_ref):
    o_ref[...] = _layer_norm(x_ref[...], g_ref[...], b_ref[...])


def _entry_ln(x, g, b, tm=1024):
    n = x.shape[0]
    return pl.pallas_call(
        _ln_kernel,
        grid=(n // tm,),
        in_specs=[pl.BlockSpec((tm, D_MODEL), lambda i: (i, 0)),
                  pl.BlockSpec((1, D_MODEL), lambda i: (0, 0)),
                  pl.BlockSpec((1, D_MODEL), lambda i: (0, 0))],
        out_specs=pl.BlockSpec((tm, D_MODEL), lambda i: (i, 0)),
        out_shape=jax.ShapeDtypeStruct((n, D_MODEL), F32),
        compiler_params=_cparams(("parallel",)),
        name="entry_ln",
    )(x, g.reshape(1, -1), b.reshape(1, -1))


def _rope_slab(xs, c, s):
    lane = lax.broadcasted_iota(jnp.int32, xs.shape, 1) % A_HEAD_DIM
    fwd = pltpu.roll(xs, LANES - ROT_DIM // 2, axis=1)
    bwd = pltpu.roll(xs, ROT_DIM // 2, axis=1)
    partner = jnp.where(lane < ROT_DIM // 2, fwd, bwd)
    return xs * c + partner * s


def _proj_a_kernel(h_ref, w_ref, c_ref, s_ref, g_ref,
                   q_ref, k_ref, v_ref, iq_ref, ki_ref, wi_ref):
    p = _dot(h_ref[...].astype(BF16), w_ref[...])
    c = c_ref[...]
    s = s_ref[...]
    lane = lax.broadcasted_iota(jnp.int32, c.shape, 1)
    lo = lane < A_HEAD_DIM
    n_blk = p.shape[0] // Q_BLOCK

    def put_heads(dst_ref, slab, first_head):
        for half in range(2):
            hd = first_head + half
            cols = slab[:, half * A_HEAD_DIM:(half + 1) * A_HEAD_DIM].astype(BF16)
            for qb in range(n_blk):
                dst_ref[qb, hd * Q_BLOCK:(hd + 1) * Q_BLOCK, :] = cols[qb * Q_BLOCK:(qb + 1) * Q_BLOCK, :]

    for i in range(4):
        sl = slice(i * LANES, (i + 1) * LANES)
        put_heads(q_ref, _rope_slab(p[:, sl], c, s) * (A_HEAD_DIM ** -0.5), 2 * i)
    kv = p[:, 512:640]
    kv = _rope_slab(kv, jnp.where(lo, c, 1.0), jnp.where(lo, s, 0.0))
    k_ref[...] = kv[:, :A_HEAD_DIM].astype(BF16)
    v_ref[...] = kv[:, A_HEAD_DIM:].astype(BF16)
    for i in range(2):
        sl = slice(640 + i * LANES, 640 + (i + 1) * LANES)
        put_heads(iq_ref, _rope_slab(p[:, sl], c, s), 2 * i)
    last = p[:, 896:1024]
    mu = jnp.sum(jnp.where(lo, last, 0.0), axis=-1, keepdims=True) * (1.0 / IDX_DIM)
    xc = jnp.where(lo, last - mu, 0.0)
    var = jnp.sum(xc * xc, axis=-1, keepdims=True) * (1.0 / IDX_DIM)
    kin = xc * lax.rsqrt(var + LN_EPS) * g_ref[...]
    kin = _rope_slab(kin, jnp.where(lo, c, 1.0), jnp.where(lo, s, 0.0))
    ki_ref[...] = kin[:, :IDX_DIM].astype(BF16)
    wi = pltpu.roll(last, LANES - IDX_DIM, axis=1) * (IDX_HEADS ** -0.5 * IDX_DIM ** -0.5)
    for qb in range(n_blk):
        wi_ref[qb] = wi[qb * Q_BLOCK:(qb + 1) * Q_BLOCK, :].T[0:8, :]


def _proj_a(h, w_a, cos_t, sin_t, ikg, tm=512):
    n = h.shape[0]
    row = lambda w: pl.BlockSpec((tm, w), lambda i: (i, 0))
    stacked = lambda heads: pl.BlockSpec((tm // Q_BLOCK, heads * Q_BLOCK, A_HEAD_DIM), lambda i: (i, 0, 0))
    return pl.pallas_call(
        _proj_a_kernel,
        grid=(n // tm,),
        in_specs=[row(D_MODEL),
                  pl.BlockSpec((D_MODEL, 1024), lambda i: (0, 0)),
                  row(LANES), row(LANES),
                  pl.BlockSpec((1, LANES), lambda i: (0, 0))],
        out_specs=[stacked(A_HEADS), row(A_HEAD_DIM), row(A_HEAD_DIM),
                   stacked(IDX_HEADS), row(IDX_DIM),
                   pl.BlockSpec((tm // Q_BLOCK, 8, Q_BLOCK), lambda i: (i, 0, 0))],
        out_shape=[jax.ShapeDtypeStruct((n // Q_BLOCK, A_HEADS * Q_BLOCK, A_HEAD_DIM), BF16),
                   jax.ShapeDtypeStruct((n, A_HEAD_DIM), BF16),
                   jax.ShapeDtypeStruct((n, A_HEAD_DIM), BF16),
                   jax.ShapeDtypeStruct((n // Q_BLOCK, IDX_HEADS * Q_BLOCK, IDX_DIM), BF16),
                   jax.ShapeDtypeStruct((n, IDX_DIM), BF16),
                   jax.ShapeDtypeStruct((n // Q_BLOCK, 8, Q_BLOCK), F32)],
        compiler_params=_cparams(("parallel",)),
        name="proj_a",
    )(h, w_a, cos_t, sin_t, ikg)


KEY_CHUNK = 512


def _dsa_kernel(q_ref, iq_ref, wi_ref, k_ref, v_ref, ki_ref, o_ref, key_ref, bias_ref, *, seq):
    j = pl.program_id(1)
    n_sel = min(TOPK_MAX, seq // 4)
    n_kc = (j * Q_BLOCK + Q_BLOCK + KEY_CHUNK - 1) // KEY_CHUNK
    qpos = j * Q_BLOCK + lax.broadcasted_iota(jnp.int32, (1, Q_BLOCK), 1)
    kk = jnp.minimum(n_sel, qpos + 1).astype(F32)
    row_pos = lax.broadcasted_iota(jnp.int32, (KEY_CHUNK, 1), 0)

    def key_rows(c):
        return pl.ds(pl.multiple_of(c * KEY_CHUNK, KEY_CHUNK), KEY_CHUNK)

    iq = iq_ref[...]
    wi = wi_ref[...]

    def score_chunk(c, carry):
        d = jnp.maximum(_dot_nt(ki_ref[key_rows(c), :], iq), 0.0)
        score = wi[0:1, :] * d[:, 0:Q_BLOCK]
        for h in range(1, IDX_HEADS):
            score = score + wi[h:h + 1, :] * d[:, h * Q_BLOCK:(h + 1) * Q_BLOCK]
        bits = pltpu.bitcast(score + 0.0, jnp.int32)
        key = bits ^ ((bits >> 31) & 0x7FFFFFFF)
        key_ref[c] = jnp.where(c * KEY_CHUNK + row_pos <= qpos, key, INT_MIN)
        return carry

    lax.fori_loop(0, n_kc, score_chunk, 0)

    def count(pred):
        def body(c, acc):
            m = jnp.where(pred(key_ref[c]), 1.0, 0.0)
            return acc + jnp.sum(m.reshape(KEY_CHUNK // 64, 64, Q_BLOCK), axis=0)
        acc = lax.fori_loop(0, n_kc, body, jnp.zeros((64, Q_BLOCK), F32))
        return jnp.sum(acc, axis=0, keepdims=True)

    def search(i, t_u):
        cand_u = t_u | jnp.left_shift(jnp.int32(1), 31 - i)
        cand_s = cand_u ^ INT_MIN
        return jnp.where(count(lambda kc: kc >= cand_s) >= kk, cand_u, t_u)

    t_u = lax.fori_loop(0, 32, search, jnp.zeros((1, Q_BLOCK), jnp.int32))
    thr = t_u ^ INT_MIN
    need = kk - count(lambda kc: kc > thr)
    r_i = lax.broadcasted_iota(jnp.int32, (KEY_CHUNK, KEY_CHUNK), 0)
    c_i = lax.broadcasted_iota(jnp.int32, (KEY_CHUNK, KEY_CHUNK), 1)
    lower = jnp.where(c_i < r_i, 1.0, 0.0).astype(BF16)

    def bias_chunk(c, carry):
        kc = key_ref[c]
        eq = kc == thr
        eq_f = jnp.where(eq, 1.0, 0.0)
        pre = _dot(lower, eq_f.astype(BF16)) + carry
        sel = (kc > thr) | (eq & (pre < need))
        bias_ref[c] = jnp.where(sel, 0.0, NEG_BIG)
        return carry + jnp.sum(eq_f, axis=0, keepdims=True)

    lax.fori_loop(0, n_kc, bias_chunk, jnp.zeros((1, Q_BLOCK), F32))

    q = q_ref[...]
    cols = A_HEADS * Q_BLOCK

    def attn_chunk(c, carry):
        m, l, acc = carry
        ks = key_rows(c)
        b = bias_ref[c]
        lg = _dot_nt(k_ref[ks, :], q) + jnp.concatenate([b] * A_HEADS, axis=1)
        m_new = jnp.maximum(m, jnp.max(lg, axis=0, keepdims=True))
        alpha = jnp.exp(m - m_new)
        p = jnp.exp(lg - m_new)
        l = alpha * l + jnp.sum(p, axis=0, keepdims=True)
        pv = lax.dot_general(v_ref[ks, :], p.astype(BF16), (((0,), (0,)), ((), ())),
                             preferred_element_type=F32)
        return m_new, l, alpha * acc + pv

    init = (jnp.full((1, cols), -3e38, F32), jnp.zeros((1, cols), F32),
            jnp.zeros((A_HEAD_DIM, cols), F32))
    _, l, acc = lax.fori_loop(0, n_kc, attn_chunk, init)
    o = acc / l
    for h in range(A_HEADS):
        o_ref[:, h * A_HEAD_DIM:(h + 1) * A_HEAD_DIM] = (
            o[:, h * Q_BLOCK:(h + 1) * Q_BLOCK].T.astype(BF16))


def _dsa(q, iq, wi, k, v, ki, bsz, seq):
    nq = seq // Q_BLOCK
    n_kc = seq // KEY_CHUNK
    stacked = lambda heads: pl.BlockSpec((None, heads * Q_BLOCK, A_HEAD_DIM),
                                         lambda b, j: (b * nq + j, 0, 0))
    kspec = lambda w: pl.BlockSpec((seq, w), lambda b, j: (b, 0))
    return pl.pallas_call(
        functools.partial(_dsa_kernel, seq=seq),
        grid=(bsz, nq),
        in_specs=[stacked(A_HEADS), stacked(IDX_HEADS),
                  pl.BlockSpec((None, 8, Q_BLOCK), lambda b, j: (b * nq + j, 0, 0)),
                  kspec(A_HEAD_DIM), kspec(A_HEAD_DIM), kspec(IDX_DIM)],
        out_specs=pl.BlockSpec((Q_BLOCK, A_OUT), lambda b, j: (b * nq + j, 0)),
        out_shape=jax.ShapeDtypeStruct((bsz * seq, A_OUT), BF16),
        scratch_shapes=[pltpu.VMEM((n_kc, KEY_CHUNK, Q_BLOCK), jnp.int32),
                        pltpu.VMEM((n_kc, KEY_CHUNK, Q_BLOCK), F32)],
        compiler_params=_cparams(("parallel", "parallel")),
        name="dsa",
    )(q, iq, wi, k, v, ki)


GLA_REP = B_HEADS * N_SUB * B_KEY_DIM
GLA_COLS = 2 * GLA_REP + 2 * B_OUT + LANES


def _gla_kernel(h_ref, w_ref, wa_ref, ba_ref, ng_ref, o_ref, p_ref, g_ref, acc_ref, st_ref, *, ts):
    @pl.when(pl.program_id(1) == 0)
    def _():
        st_ref[...] = jnp.zeros_like(st_ref)

    p_ref[...] = _dot(h_ref[...].astype(BF16), w_ref[...])
    x = _dot(p_ref[:, 2 * GLA_REP + 2 * B_OUT:].astype(BF16), wa_ref[...]) + ba_ref[...]
    g_ref[...] = (jnp.minimum(x, 0.0) - jnp.log1p(jnp.exp(-jnp.abs(x)))) * (1.0 / B_GATE_TAU)

    t_i = lax.broadcasted_iota(jnp.int32, (B_CHUNK, B_CHUNK), 0)
    s_i = lax.broadcasted_iota(jnp.int32, (B_CHUNK, B_CHUNK), 1)
    tri = t_i >= s_i
    tri_f = jnp.where(tri, 1.0, 0.0)
    lane_sub = (lax.broadcasted_iota(jnp.int32, (1, GLA_REP), 1) // B_KEY_DIM) % N_SUB
    row_sub = lax.broadcasted_iota(jnp.int32, (B_CHUNK, 1), 0) // B_SUB
    q_mask = lane_sub <= row_sub
    k_mask = lane_sub == row_sub

    def chunk(c, carry):
        rows = pl.ds(pl.multiple_of(c * B_CHUNK, B_CHUNK), B_CHUNK)
        b = jnp.dot(tri_f, g_ref[rows, :], precision=HIGHEST, preferred_element_type=F32)
        ref = jnp.zeros((1, GLA_REP), F32)
        for jj in range(1, N_SUB):
            ref = ref + jnp.where(lane_sub == jj, b[jj * B_SUB - 1:jj * B_SUB, :], 0.0)
        d = b - ref
        e_q = jnp.where(q_mask, jnp.exp(jnp.minimum(d, 0.0)), 0.0)
        e_k = jnp.where(k_mask, jnp.exp(jnp.where(k_mask, -d, 0.0)), 0.0)
        b_last = b[B_CHUNK - 1:B_CHUNK, :]
        q_c = (p_ref[rows, 0:GLA_REP] * e_q).astype(BF16)
        k_raw = p_ref[rows, GLA_REP:2 * GLA_REP]
        k_c = (k_raw * e_k).astype(BF16)
        k_st = (k_raw * jnp.exp(b_last - b)).astype(BF16)
        e_last = jnp.exp(b_last)
        for hh in range(B_HEADS):
            base = hh * N_SUB * B_KEY_DIM
            q_h = q_c[:, base:base + N_SUB * B_KEY_DIM]
            k_h = k_c[:, base:base + N_SUB * B_KEY_DIM]
            att = jnp.where(tri, _dot_nt(q_h, k_h), 0.0)
            v_h = p_ref[rows, 2 * GLA_REP + hh * B_VAL_DIM:2 * GLA_REP + (hh + 1) * B_VAL_DIM].astype(BF16)
            st = st_ref[hh]
            o = _dot(att.astype(BF16), v_h) + _dot_nt(q_h[:, :B_KEY_DIM], st.astype(BF16))
            acc_ref[rows, hh * B_VAL_DIM:(hh + 1) * B_VAL_DIM] = o
            upd = _dot(v_h.T, k_st[:, base:base + B_KEY_DIM])
            st_ref[hh] = st * e_last[:, base:base + B_KEY_DIM] + upd
        return carry

    lax.fori_loop(0, ts // B_CHUNK, chunk, 0)

    r = p_ref[:, 2 * GLA_REP + B_OUT:2 * GLA_REP + 2 * B_OUT]
    gate = r * (1.0 / (1.0 + jnp.exp(-r)))
    ng = ng_ref[...]
    for hh in range(B_HEADS):
        sl = slice(hh * B_VAL_DIM, (hh + 1) * B_VAL_DIM)
        o = acc_ref[:, sl]
        y = o * lax.rsqrt(jnp.mean(o * o, axis=-1, keepdims=True) + RMS_EPS) * ng
        o_ref[:, sl] = (gate[:, sl] * y).astype(BF16)


def _gla(h, w_b, wa_rep, ba_rep, norm_g, bsz, seq, ts=512):
    ns = seq // ts
    const = lambda shape: pl.BlockSpec(shape, lambda b, i: (0, 0))
    return pl.pallas_call(
        functools.partial(_gla_kernel, ts=ts),
        grid=(bsz, ns),
        in_specs=[pl.BlockSpec((ts, D_MODEL), lambda b, i: (b * ns + i, 0)),
                  const((D_MODEL, GLA_COLS)), const((LANES, GLA_REP)),
                  const((1, GLA_REP)), const((1, B_VAL_DIM))],
        out_specs=pl.BlockSpec((ts, B_OUT), lambda b, i: (b * ns + i, 0)),
        out_shape=jax.ShapeDtypeStruct((bsz * seq, B_OUT), BF16),
        scratch_shapes=[pltpu.VMEM((ts, GLA_COLS), F32),
                        pltpu.VMEM((ts, GLA_REP), F32),
                        pltpu.VMEM((ts, B_OUT), F32),
                        pltpu.VMEM((B_HEADS, B_VAL_DIM, B_KEY_DIM), F32)],
        compiler_params=_cparams(("parallel", "arbitrary")),
        name="gla",
    )(h, w_b, wa_rep, ba_rep, norm_g)


def _gmlp_kernel(h_ref, w_ref, lg_ref, lb_ref, ws_ref, bs_ref, o_ref, *, tm):
    z = _dot(h_ref[...].astype(BF16), w_ref[...])
    z = z * (0.5 * (1.0 + jnp.tanh(np.sqrt(2.0 / np.pi) * (z + 0.044715 * (z * z * z)))))
    u = z[:, :C_OUT]
    v = _layer_norm(z[:, C_OUT:], lg_ref[...], lb_ref[...]).astype(BF16)
    t_i = lax.broadcasted_iota(jnp.int32, (C_CHUNK, C_CHUNK), 0)
    s_i = lax.broadcasted_iota(jnp.int32, (C_CHUNK, C_CHUNK), 1)
    bs = bs_ref[...]
    for g in range(C_GROUPS):
        w = jnp.where(t_i >= s_i, ws_ref[g], 0.0).astype(BF16)
        sl = slice(g * C_GROUP_DIM, (g + 1) * C_GROUP_DIM)
        for c in range(tm // C_CHUNK):
            rows = slice(c * C_CHUNK, (c + 1) * C_CHUNK)
            mixed = _dot(w, v[rows, sl]) + bs[:, g:g + 1]
            o_ref[rows, sl] = (u[rows, sl] * mixed).astype(BF16)


def _gmlp(h, w_c, ln_g, ln_b, ws, bs_t, tm=512):
    n = h.shape[0]
    return pl.pallas_call(
        functools.partial(_gmlp_kernel, tm=tm),
        grid=(n // tm,),
        in_specs=[pl.BlockSpec((tm, D_MODEL), lambda i: (i, 0)),
                  pl.BlockSpec((D_MODEL, 2 * C_OUT), lambda i: (0, 0)),
                  pl.BlockSpec((1, C_OUT), lambda i: (0, 0)),
                  pl.BlockSpec((1, C_OUT), lambda i: (0, 0)),
                  pl.BlockSpec((C_GROUPS, C_CHUNK, C_CHUNK), lambda i: (0, 0, 0)),
                  pl.BlockSpec((C_CHUNK, LANES), lambda i: (0, 0))],
        out_specs=pl.BlockSpec((tm, C_OUT), lambda i: (i, 0)),
        out_shape=jax.ShapeDtypeStruct((n, C_OUT), BF16),
        compiler_params=_cparams(("parallel",)),
        name="gmlp",
    )(h, w_c, ln_g, ln_b, ws, bs_t)


def _sigmoid(x):
    return 1.0 / (1.0 + jnp.exp(-x))


def _merge_kernel(h_ref, ya_ref, yb_ref, yc_ref, wg_ref, wa_ref, wb_ref, wc_ref, wo_ref,
                  lg_ref, lb_ref, wr_ref, br_ref, x_ref, xb_ref, comb_ref):
    h = h_ref[...]
    hb = h.astype(BF16)
    acc = None
    for i, (y_ref, wbr_ref) in enumerate(((ya_ref, wa_ref), (yb_ref, wb_ref), (yc_ref, wc_ref))):
        gate = _sigmoid(_dot(hb, wg_ref[:, i * D_MODEL:(i + 1) * D_MODEL]))
        term = gate * _dot(y_ref[...], wbr_ref[...])
        acc = term if acc is None else acc + term
    mix = _dot(acc.astype(BF16), wo_ref[...])
    x1 = _layer_norm(DN_ALPHA * h + mix, lg_ref[...], lb_ref[...])
    x_ref[...] = x1
    xb_ref[...] = x1.astype(BF16)

    logit = jnp.dot(x1, wr_ref[...], precision=HIGHEST, preferred_element_type=F32) + br_ref[...]
    lane = lax.broadcasted_iota(jnp.int32, logit.shape, 1)
    big = jnp.int32(LANES)
    is_g = lane < N_GROUPS
    gl = jnp.where(is_g, logit, -jnp.inf)
    g_max = jnp.max(gl, axis=-1, keepdims=True)
    g_idx = jnp.min(jnp.where(is_g & (gl == g_max), lane, big), axis=-1, keepdims=True)
    g_p = 1.0 / jnp.sum(jnp.where(is_g, jnp.exp(gl - g_max), 0.0), axis=-1, keepdims=True)
    lo = N_GROUPS + g_idx * EXPERTS_PER_GROUP
    in_g = (lane >= lo) & (lane < lo + EXPERTS_PER_GROUP)
    el = jnp.where(in_g, logit, -jnp.inf)
    e1 = jnp.max(el, axis=-1, keepdims=True)
    i1 = jnp.min(jnp.where(in_g & (el == e1), lane, big), axis=-1, keepdims=True)
    el2 = jnp.where(lane == i1, -jnp.inf, el)
    e2 = jnp.max(el2, axis=-1, keepdims=True)
    i2 = jnp.min(jnp.where(in_g & (lane != i1) & (el2 == e2), lane, big), axis=-1, keepdims=True)
    z2 = jnp.exp(e2 - e1)
    den = 1.0 + z2
    comb_ref[...] = jnp.where(lane == i1, (1.0 / den) * g_p,
                              jnp.where(lane == i2, (z2 / den) * g_p, 0.0))


def _merge(h, ya, yb, yc, wg, wa, wb, wc, wo, ln_g, ln_b, wr, br, tm=512):
    n = h.shape[0]
    row = lambda w: pl.BlockSpec((tm, w), lambda i: (i, 0))
    const = lambda r, c: pl.BlockSpec((r, c), lambda i: (0, 0))
    return pl.pallas_call(
        _merge_kernel,
        grid=(n // tm,),
        in_specs=[row(D_MODEL), row(A_OUT), row(B_OUT), row(C_OUT),
                  const(D_MODEL, 3 * D_MODEL), const(A_OUT, D_MODEL), const(B_OUT, D_MODEL),
                  const(C_OUT, D_MODEL), const(D_MODEL, D_MODEL),
                  const(1, D_MODEL), const(1, D_MODEL), const(D_MODEL, LANES), const(1, LANES)],
        out_specs=[row(D_MODEL), row(D_MODEL), row(LANES)],
        out_shape=[jax.ShapeDtypeStruct((n, D_MODEL), F32),
                   jax.ShapeDtypeStruct((n, D_MODEL), BF16),
                   jax.ShapeDtypeStruct((n, LANES), F32)],
        compiler_params=_cparams(("parallel",)),
        name="merge",
    )(h, ya, yb, yc, wg, wa, wb, wc, wo, ln_g, ln_b, wr, br)


def _moe_kernel(x_ref, xb_ref, comb_ref, wg_ref, wu_ref, wd_ref, lg_ref, lb_ref, o_ref, acc_ref):
    e = pl.program_id(1)

    @pl.when(e == 0)
    def _():
        acc_ref[...] = jnp.zeros_like(acc_ref)

    xb = xb_ref[...]
    comb = comb_ref[...]
    lane = lax.broadcasted_iota(jnp.int32, comb.shape, 1)
    c_e = jnp.sum(jnp.where(lane == e + N_GROUPS, comb, 0.0), axis=-1, keepdims=True)
    gte = _dot(xb, wg_ref[...])
    up = _dot(xb, wu_ref[...])
    hid = gte * _sigmoid(gte) * up * c_e
    acc_ref[...] += _dot(hid.astype(BF16), wd_ref[...])

    @pl.when(e == N_EXPERTS - 1)
    def _():
        o_ref[...] = _layer_norm(DN_ALPHA * x_ref[...] + acc_ref[...], lg_ref[...], lb_ref[...])


def _moe(x1, x1b, comb, w_gate, w_up, w_down, ln_g, ln_b, tm=1024):
    n = x1.shape[0]
    row = lambda w: pl.BlockSpec((tm, w), lambda i, e: (i, 0))
    return pl.pallas_call(
        _moe_kernel,
        grid=(n // tm, N_EXPERTS),
        in_specs=[row(D_MODEL), row(D_MODEL), row(LANES),
                  pl.BlockSpec((None, D_MODEL, D_EXPERT), lambda i, e: (e, 0, 0)),
                  pl.BlockSpec((None, D_MODEL, D_EXPERT), lambda i, e: (e, 0, 0)),
                  pl.BlockSpec((None, D_EXPERT, D_MODEL), lambda i, e: (e, 0, 0)),
                  pl.BlockSpec((1, D_MODEL), lambda i, e: (0, 0)),
                  pl.BlockSpec((1, D_MODEL), lambda i, e: (0, 0))],
        out_specs=row(D_MODEL),
        out_shape=jax.ShapeDtypeStruct((n, D_MODEL), F32),
        scratch_shapes=[pltpu.VMEM((tm, D_MODEL), F32)],
        compiler_params=_cparams(("parallel", "arbitrary")),
        name="moe",
    )(x1, x1b, comb, w_gate, w_up, w_down, ln_g, ln_b)


def _rope_tables(positions):
    inv_freq = ROPE_THETA ** (-jnp.arange(0, ROT_DIM, 2, dtype=F32) / ROT_DIM)
    ang = positions.astype(F32).reshape(-1, 1) * inv_freq
    cos, sin = jnp.cos(ang), jnp.sin(ang)
    n = ang.shape[0]
    half = ROT_DIM // 2
    c64 = jnp.concatenate([cos, cos, jnp.ones((n, A_HEAD_DIM - ROT_DIM), F32)], axis=1)
    s64 = jnp.concatenate([-sin, sin, jnp.zeros((n, A_HEAD_DIM - ROT_DIM), F32)], axis=1)
    del half
    return jnp.tile(c64, (1, 2)), jnp.tile(s64, (1, 2))


def _rep_cols(w):
    r = w.shape[0]
    w4 = w.reshape(r, B_HEADS, 1, B_KEY_DIM)
    return jnp.broadcast_to(w4, (r, B_HEADS, N_SUB, B_KEY_DIM)).reshape(r, GLA_REP)


def _split_w_in(w):
    widths = (A_OUT, A_HEAD_DIM, A_HEAD_DIM, IDX_HEADS * IDX_DIM, IDX_DIM, IDX_HEADS,
              B_HEADS * B_KEY_DIM, B_HEADS * B_KEY_DIM, B_OUT, B_GATE_RANK, B_OUT,
              2 * C_OUT, D_MODEL, D_MODEL, D_MODEL)
    pts = np.cumsum((0,) + widths)
    return [w[:, int(pts[i]):int(pts[i + 1])] for i in range(len(widths))]


def kernel(x, positions, ln_in_g, ln_in_b, w_in, idx_k_g, gla_wa2, gla_ba, gla_norm_g, gm_ln_g, gm_ln_b, gm_ws, gm_bs, w_branch_a, w_branch_b, w_branch_c, w_out, ln1_g, ln1_b, w_rg, b_rg, w_re, b_re, w_gate, w_up, w_down, ln2_g, ln2_b):
    bsz, seq, d = x.shape
    n = bsz * seq
    cos_t, sin_t = _rope_tables(positions)
    h = _entry_ln(x.reshape(n, d), ln_in_g, ln_in_b)
    for l in range(DEPTH):
        (a_q, a_k, a_v, i_q, i_k, i_w, b_q, b_k, b_v, b_glr, b_r, c_uv,
         g_a, g_b, g_c) = _split_w_in(w_in[l])
        zpad = lambda c: jnp.zeros((d, c), F32)
        w_a = jnp.concatenate([a_q, a_k, a_v, i_q, i_k, i_w, zpad(LANES - IDX_DIM - IDX_HEADS)],
                              axis=1).astype(BF16)
        w_b = jnp.concatenate([_rep_cols(b_q * (B_KEY_DIM ** -0.5)), _rep_cols(b_k), b_v, b_r,
                               b_glr, zpad(LANES - B_GATE_RANK)], axis=1).astype(BF16)
        wa_rep = jnp.concatenate([_rep_cols(gla_wa2[l]),
                                  jnp.zeros((LANES - B_GATE_RANK, GLA_REP), F32)], axis=0).astype(BF16)
        ba_rep = _rep_cols(gla_ba[l].reshape(1, -1))
        w_g = jnp.concatenate([g_a, g_b, g_c], axis=1).astype(BF16)
        ikg = jnp.concatenate([idx_k_g[l], jnp.zeros((LANES - IDX_DIM,), F32)]).reshape(1, LANES)
        bs_t = jnp.concatenate([gm_bs[l].T, jnp.zeros((C_CHUNK, LANES - C_GROUPS), F32)], axis=1)
        w_r = jnp.concatenate([w_rg[l], w_re[l], zpad(LANES - N_GROUPS - N_EXPERTS)], axis=1)
        b_r_all = jnp.concatenate([b_rg[l], b_re[l],
                                   jnp.zeros((LANES - N_GROUPS - N_EXPERTS,), F32)]).reshape(1, LANES)

        q, k, v, iq, ki, wi = _proj_a(h, w_a, cos_t, sin_t, ikg)
        y_a = _dsa(q, iq, wi, k, v, ki, bsz, seq)
        y_b = _gla(h, w_b, wa_rep, ba_rep, gla_norm_g[l].reshape(1, -1), bsz, seq)
        y_c = _gmlp(h, c_uv.astype(BF16), gm_ln_g[l].reshape(1, -1), gm_ln_b[l].reshape(1, -1),
                    gm_ws[l], bs_t)
        x1, x1b, comb = _merge(h, y_a, y_b, y_c, w_g,
                               w_branch_a[l].astype(BF16), w_branch_b[l].astype(BF16),
                               w_branch_c[l].astype(BF16), w_out[l].astype(BF16),
                               ln1_g[l].reshape(1, -1), ln1_b[l].reshape(1, -1), w_r, b_r_all)
        h = _moe(x1, x1b, comb, w_gate[l].astype(BF16), w_up[l].astype(BF16),
                 w_down[l].astype(BF16), ln2_g[l].reshape(1, -1), ln2_b[l].reshape(1, -1))
    return h.reshape(bsz, seq, d)
```

```python
import functools

import numpy as np
import jax
import jax.numpy as jnp
from jax import lax
from jax.experimental import pallas as pl
from jax.experimental.pallas import tpu as pltpu

F32 = jnp.float32
BF16 = jnp.bfloat16

D_MODEL = 1024
DEPTH = 2
A_HEADS = 8
A_HEAD_DIM = 64
A_OUT = 512
IDX_HEADS = 4
IDX_DIM = 64
TOPK_MAX = 256
Q_BLOCK = 128
ROPE_THETA = 500000.0
ROT_DIM = 16
B_HEADS = 4
B_KEY_DIM = 64
B_VAL_DIM = 128
B_OUT = 512
B_GATE_RANK = 16
B_GATE_TAU = 16.0
B_CHUNK = 64
B_SUB = 16
N_SUB = B_CHUNK // B_SUB
C_GROUPS = 4
C_GROUP_DIM = 128
C_OUT = 512
C_CHUNK = 128
N_GROUPS = 4
EXPERTS_PER_GROUP = 8
N_EXPERTS = 32
D_EXPERT = 256
TOP_K_INNER = 2
DN_ALPHA = (2 * DEPTH) ** 0.25
LN_EPS = 1e-5
RMS_EPS = 1e-6

LANES = 128
NEG_BIG = -1e30
INT_MIN = -2 ** 31
VMEM_LIMIT = 56 * 1024 * 1024

HIGHEST = lax.Precision.HIGHEST


def _cparams(sem):
    return pltpu.CompilerParams(dimension_semantics=sem, vmem_limit_bytes=VMEM_LIMIT)


def _layer_norm(x, g, b):
    mu = jnp.mean(x, axis=-1, keepdims=True)
    xc = x - mu
    var = jnp.mean(xc * xc, axis=-1, keepdims=True)
    return xc * lax.rsqrt(var + LN_EPS) * g + b


def _dot(a, b):
    return jnp.dot(a, b, preferred_element_type=F32)


def _dot_nt(a, b):
    return lax.dot_general(a, b, (((1,), (1,)), ((), ())), preferred_element_type=F32)


def _ln_kernel(x_ref, g_ref, b_ref, o_ref):
    o_ref[...] = _layer_norm(x_ref[...], g_ref[...], b_ref[...])


def _entry_ln(x, g, b, tm=1024):
    n = x.shape[0]
    return pl.pallas_call(
        _ln_kernel,
        grid=(n // tm,),
        in_specs=[pl.BlockSpec((tm, D_MODEL), lambda i: (i, 0)),
                  pl.BlockSpec((1, D_MODEL), lambda i: (0, 0)),
                  pl.BlockSpec((1, D_MODEL), lambda i: (0, 0))],
        out_specs=pl.BlockSpec((tm, D_MODEL), lambda i: (i, 0)),
        out_shape=jax.ShapeDtypeStruct((n, D_MODEL), F32),
        compiler_params=_cparams(("parallel",)),
        name="entry_ln",
    )(x, g.reshape(1, -1), b.reshape(1, -1))


def _rope_slab(xs, c, s):
    lane = lax.broadcasted_iota(jnp.int32, xs.shape, 1) % A_HEAD_DIM
    fwd = pltpu.roll(xs, LANES - ROT_DIM // 2, axis=1)
    bwd = pltpu.roll(xs, ROT_DIM // 2, axis=1)
    partner = jnp.where(lane < ROT_DIM // 2, fwd, bwd)
    return xs * c + partner * s


def _proj_a_kernel(h_ref, w_ref, c_ref, s_ref, g_ref,
                   q_ref, k_ref, v_ref, iq_ref, ki_ref, wi_ref):
    p = _dot(h_ref[...].astype(BF16), w_ref[...])
    c = c_ref[...]
    s = s_ref[...]
    lane = lax.broadcasted_iota(jnp.int32, c.shape, 1)
    lo = lane < A_HEAD_DIM
    n_blk = p.shape[0] // Q_BLOCK

    def put_heads(dst_ref, slab, first_head):
        for half in range(2):
            hd = first_head + half
            cols = slab[:, half * A_HEAD_DIM:(half + 1) * A_HEAD_DIM].astype(BF16)
            for qb in range(n_blk):
                dst_ref[qb, hd * Q_BLOCK:(hd + 1) * Q_BLOCK, :] = cols[qb * Q_BLOCK:(qb + 1) * Q_BLOCK, :]

    for i in range(4):
        sl = slice(i * LANES, (i + 1) * LANES)
        put_heads(q_ref, _rope_slab(p[:, sl], c, s) * (A_HEAD_DIM ** -0.5), 2 * i)
    kv = p[:, 512:640]
    kv = _rope_slab(kv, jnp.where(lo, c, 1.0), jnp.where(lo, s, 0.0))
    k_ref[...] = kv[:, :A_HEAD_DIM].astype(BF16)
    v_ref[...] = kv[:, A_HEAD_DIM:].astype(BF16)
    for i in range(2):
        sl = slice(640 + i * LANES, 640 + (i + 1) * LANES)
        put_heads(iq_ref, _rope_slab(p[:, sl], c, s), 2 * i)
    last = p[:, 896:1024]
    mu = jnp.sum(jnp.where(lo, last, 0.0), axis=-1, keepdims=True) * (1.0 / IDX_DIM)
    xc = jnp.where(lo, last - mu, 0.0)
    var = jnp.sum(xc * xc, axis=-1, keepdims=True) * (1.0 / IDX_DIM)
    kin = xc * lax.rsqrt(var + LN_EPS) * g_ref[...]
    kin = _rope_slab(kin, jnp.where(lo, c, 1.0), jnp.where(lo, s, 0.0))
    ki_ref[...] = kin[:, :IDX_DIM].astype(BF16)
    wi = pltpu.roll(last, LANES - IDX_DIM, axis=1) * (IDX_HEADS ** -0.5 * IDX_DIM ** -0.5)
    for qb in range(n_blk):
        wi_ref[qb] = wi[qb * Q_BLOCK:(qb + 1) * Q_BLOCK, :].T[0:8, :]


def _proj_a(h, w_a, cos_t, sin_t, ikg, tm=512):
    n = h.shape[0]
    row = lambda w: pl.BlockSpec((tm, w), lambda i: (i, 0))
    stacked = lambda heads: pl.BlockSpec((tm // Q_BLOCK, heads * Q_BLOCK, A_HEAD_DIM), lambda i: (i, 0, 0))
    return pl.pallas_call(
        _proj_a_kernel,
        grid=(n // tm,),
        in_specs=[row(D_MODEL),
                  pl.BlockSpec((D_MODEL, 1024), lambda i: (0, 0)),
                  row(LANES), row(LANES),
                  pl.BlockSpec((1, LANES), lambda i: (0, 0))],
        out_specs=[stacked(A_HEADS), row(A_HEAD_DIM), row(A_HEAD_DIM),
                   stacked(IDX_HEADS), row(IDX_DIM),
                   pl.BlockSpec((tm // Q_BLOCK, 8, Q_BLOCK), lambda i: (i, 0, 0))],
        out_shape=[jax.ShapeDtypeStruct((n // Q_BLOCK, A_HEADS * Q_BLOCK, A_HEAD_DIM), BF16),
                   jax.ShapeDtypeStruct((n, A_HEAD_DIM), BF16),
                   jax.ShapeDtypeStruct((n, A_HEAD_DIM), BF16),
                   jax.ShapeDtypeStruct((n // Q_BLOCK, IDX_HEADS * Q_BLOCK, IDX_DIM), BF16),
                   jax.ShapeDtypeStruct((n, IDX_DIM), BF16),
                   jax.ShapeDtypeStruct((n // Q_BLOCK, 8, Q_BLOCK), F32)],
        compiler_params=_cparams(("parallel",)),
        name="proj_a",
    )(h, w_a, cos_t, sin_t, ikg)


KEY_CHUNK = 512


def _dsa_kernel(q_ref, iq_ref, wi_ref, k_ref, v_ref, ki_ref, o_ref, key_ref, bias_ref, *, seq):
    j = pl.program_id(1)
    n_sel = min(TOPK_MAX, seq // 4)
    n_kc = (j * Q_BLOCK + Q_BLOCK + KEY_CHUNK - 1) // KEY_CHUNK
    qpos = j * Q_BLOCK + lax.broadcasted_iota(jnp.int32, (1, Q_BLOCK), 1)
    kk = jnp.minimum(n_sel, qpos + 1).astype(F32)
    row_pos = lax.broadcasted_iota(jnp.int32, (KEY_CHUNK, 1), 0)

    def key_rows(c):
        return pl.ds(pl.multiple_of(c * KEY_CHUNK, KEY_CHUNK), KEY_CHUNK)

    iq = iq_ref[...]
    wi = wi_ref[...]

    def score_chunk(c, carry):
        d = jnp.maximum(_dot_nt(ki_ref[key_rows(c), :], iq), 0.0)
        score = wi[0:1, :] * d[:, 0:Q_BLOCK]
        for h in range(1, IDX_HEADS):
            score = score + wi[h:h + 1, :] * d[:, h * Q_BLOCK:(h + 1) * Q_BLOCK]
        bits = pltpu.bitcast(score + 0.0, jnp.int32)
        key = bits ^ ((bits >> 31) & 0x7FFFFFFF)
        key_ref[c] = jnp.where(c * KEY_CHUNK + row_pos <= qpos, key, INT_MIN)
        return carry

    lax.fori_loop(0, n_kc, score_chunk, 0)

    def count(pred):
        def body(c, acc):
            m = jnp.where(pred(key_ref[c]), 1.0, 0.0)
            return acc + jnp.sum(m.reshape(KEY_CHUNK // 64, 64, Q_BLOCK), axis=0)
        acc = lax.fori_loop(0, n_kc, body, jnp.zeros((64, Q_BLOCK), F32))
        return jnp.sum(acc, axis=0, keepdims=True)

    def search(i, t_u):
        cand_u = t_u | jnp.left_shift(jnp.int32(1), 31 - i)
        cand_s = cand_u ^ INT_MIN
        return jnp.where(count(lambda kc: kc >= cand_s) >= kk, cand_u, t_u)

    t_u = lax.fori_loop(0, 32, search, jnp.zeros((1, Q_BLOCK), jnp.int32))
    thr = t_u ^ INT_MIN
    need = kk - count(lambda kc: kc > thr)
    r_i = lax.broadcasted_iota(jnp.int32, (KEY_CHUNK, KEY_CHUNK), 0)
    c_i = lax.broadcasted_iota(jnp.int32, (KEY_CHUNK, KEY_CHUNK), 1)
    lower = jnp.where(c_i < r_i, 1.0, 0.0).astype(BF16)

    def bias_chunk(c, carry):
        kc = key_ref[c]
        eq = kc == thr
        eq_f = jnp.where(eq, 1.0, 0.0)
        pre = _dot(lower, eq_f.astype(BF16)) + carry
        sel = (kc > thr) | (eq & (pre < need))
        bias_ref[c] = jnp.where(sel, 0.0, NEG_BIG)
        return carry + jnp.sum(eq_f, axis=0, keepdims=True)

    lax.fori_loop(0, n_kc, bias_chunk, jnp.zeros((1, Q_BLOCK), F32))

    q = q_ref[...]
    cols = A_HEADS * Q_BLOCK

    def attn_chunk(c, carry):
        m, l, acc = carry
        ks = key_rows(c)
        b = bias_ref[c]
        lg = _dot_nt(k_ref[ks, :], q) + jnp.concatenate([b] * A_HEADS, axis=1)
        m_new = jnp.maximum(m, jnp.max(lg, axis=0, keepdims=True))
        alpha = jnp.exp(m - m_new)
        p = jnp.exp(lg - m_new)
        l = alpha * l + jnp.sum(p, axis=0, keepdims=True)
        pv = lax.dot_general(v_ref[ks, :], p.astype(BF16), (((0,), (0,)), ((), ())),
                             preferred_element_type=F32)
        return m_new, l, alpha * acc + pv

    init = (jnp.full((1, cols), -3e38, F32), jnp.zeros((1, cols), F32),
            jnp.zeros((A_HEAD_DIM, cols), F32))
    _, l, acc = lax.fori_loop(0, n_kc, attn_chunk, init)
    o = acc / l
    for h in range(A_HEADS):
        o_ref[:, h * A_HEAD_DIM:(h + 1) * A_HEAD_DIM] = (
            o[:, h * Q_BLOCK:(h + 1) * Q_BLOCK].T.astype(BF16))


def _dsa(q, iq, wi, k, v, ki, bsz, seq):
    nq = seq // Q_BLOCK
    n_kc = seq // KEY_CHUNK
    stacked = lambda heads: pl.BlockSpec((None, heads * Q_BLOCK, A_HEAD_DIM),
                                         lambda b, j: (b * nq + j, 0, 0))
    kspec = lambda w: pl.BlockSpec((seq, w), lambda b, j: (b, 0))
    return pl.pallas_call(
        functools.partial(_dsa_kernel, seq=seq),
        grid=(bsz, nq),
        in_specs=[stacked(A_HEADS), stacked(IDX_HEADS),
                  pl.BlockSpec((None, 8, Q_BLOCK), lambda b, j: (b * nq + j, 0, 0)),
                  kspec(A_HEAD_DIM), kspec(A_HEAD_DIM), kspec(IDX_DIM)],
        out_specs=pl.BlockSpec((Q_BLOCK, A_OUT), lambda b, j: (b * nq + j, 0)),
        out_shape=jax.ShapeDtypeStruct((bsz * seq, A_OUT), BF16),
        scratch_shapes=[pltpu.VMEM((n_kc, KEY_CHUNK, Q_BLOCK), jnp.int32),
                        pltpu.VMEM((n_kc, KEY_CHUNK, Q_BLOCK), F32)],
        compiler_params=_cparams(("parallel", "parallel")),
        name="dsa",
    )(q, iq, wi, k, v, ki)


GLA_REP = B_HEADS * N_SUB * B_KEY_DIM
GLA_COLS = 2 * GLA_REP + 2 * B_OUT + LANES


def _gla_kernel(h_ref, w_ref, wa_ref, ba_ref, ng_ref, o_ref, p_ref, g_ref, acc_ref, st_ref, *, ts):
    @pl.when(pl.program_id(1) == 0)
    def _():
        st_ref[...] = jnp.zeros_like(st_ref)

    p_ref[...] = _dot(h_ref[...].astype(BF16), w_ref[...])
    x = _dot(p_ref[:, 2 * GLA_REP + 2 * B_OUT:].astype(BF16), wa_ref[...]) + ba_ref[...]
    g_ref[...] = (jnp.minimum(x, 0.0) - jnp.log1p(jnp.exp(-jnp.abs(x)))) * (1.0 / B_GATE_TAU)

    t_i = lax.broadcasted_iota(jnp.int32, (B_CHUNK, B_CHUNK), 0)
    s_i = lax.broadcasted_iota(jnp.int32, (B_CHUNK, B_CHUNK), 1)
    tri = t_i >= s_i
    tri_f = jnp.where(tri, 1.0, 0.0)
    lane_sub = (lax.broadcasted_iota(jnp.int32, (1, GLA_REP), 1) // B_KEY_DIM) % N_SUB
    row_sub = lax.broadcasted_iota(jnp.int32, (B_CHUNK, 1), 0) // B_SUB
    q_mask = lane_sub <= row_sub
    k_mask = lane_sub == row_sub

    def chunk(c, carry):
        rows = pl.ds(pl.multiple_of(c * B_CHUNK, B_CHUNK), B_CHUNK)
        b = jnp.dot(tri_f, g_ref[rows, :], precision=HIGHEST, preferred_element_type=F32)
        ref = jnp.zeros((1, GLA_REP), F32)
        for jj in range(1, N_SUB):
            ref = ref + jnp.where(lane_sub == jj, b[jj * B_SUB - 1:jj * B_SUB, :], 0.0)
        d = b - ref
        e_q = jnp.where(q_mask, jnp.exp(jnp.minimum(d, 0.0)), 0.0)
        e_k = jnp.where(k_mask, jnp.exp(jnp.where(k_mask, -d, 0.0)), 0.0)
        b_last = b[B_CHUNK - 1:B_CHUNK, :]
        q_c = (p_ref[rows, 0:GLA_REP] * e_q).astype(BF16)
        k_raw = p_ref[rows, GLA_REP:2 * GLA_REP]
        k_c = (k_raw * e_k).astype(BF16)
        k_st = (k_raw * jnp.exp(b_last - b)).astype(BF16)
        e_last = jnp.exp(b_last)
        for hh in range(B_HEADS):
            base = hh * N_SUB * B_KEY_DIM
            q_h = q_c[:, base:base + N_SUB * B_KEY_DIM]
            k_h = k_c[:, base:base + N_SUB * B_KEY_DIM]
            att = jnp.where(tri, _dot_nt(q_h, k_h), 0.0)
            v_h = p_ref[rows, 2 * GLA_REP + hh * B_VAL_DIM:2 * GLA_REP + (hh + 1) * B_VAL_DIM].astype(BF16)
            st = st_ref[hh]
            o = _dot(att.astype(BF16), v_h) + _dot_nt(q_h[:, :B_KEY_DIM], st.astype(BF16))
            acc_ref[rows, hh * B_VAL_DIM:(hh + 1) * B_VAL_DIM] = o
            upd = _dot(v_h.T, k_st[:, base:base + B_KEY_DIM])
            st_ref[hh] = st * e_last[:, base:base + B_KEY_DIM] + upd
        return carry

    lax.fori_loop(0, ts // B_CHUNK, chunk, 0)

    r = p_ref[:, 2 * GLA_REP + B_OUT:2 * GLA_REP + 2 * B_OUT]
    gate = r * (1.0 / (1.0 + jnp.exp(-r)))
    ng = ng_ref[...]
    for hh in range(B_HEADS):
        sl = slice(hh * B_VAL_DIM, (hh + 1) * B_VAL_DIM)
        o = acc_ref[:, sl]
        y = o * lax.rsqrt(jnp.mean(o * o, axis=-1, keepdims=True) + RMS_EPS) * ng
        o_ref[:, sl] = (gate[:, sl] * y).astype(BF16)


def _gla(h, w_b, wa_rep, ba_rep, norm_g, bsz, seq, ts=512):
    ns = seq // ts
    const = lambda shape: pl.BlockSpec(shape, lambda b, i: (0, 0))
    return pl.pallas_call(
        functools.partial(_gla_kernel, ts=ts),
        grid=(bsz, ns),
        in_specs=[pl.BlockSpec((ts, D_MODEL), lambda b, i: (b * ns + i, 0)),
                  const((D_MODEL, GLA_COLS)), const((LANES, GLA_REP)),
                  const((1, GLA_REP)), const((1, B_VAL_DIM))],
        out_specs=pl.BlockSpec((ts, B_OUT), lambda b, i: (b * ns + i, 0)),
        out_shape=jax.ShapeDtypeStruct((bsz * seq, B_OUT), BF16),
        scratch_shapes=[pltpu.VMEM((ts, GLA_COLS), F32),
                        pltpu.VMEM((ts, GLA_REP), F32),
                        pltpu.VMEM((ts, B_OUT), F32),
                        pltpu.VMEM((B_HEADS, B_VAL_DIM, B_KEY_DIM), F32)],
        compiler_params=_cparams(("parallel", "arbitrary")),
        name="gla",
    )(h, w_b, wa_rep, ba_rep, norm_g)


def _gmlp_kernel(h_ref, w_ref, lg_ref, lb_ref, ws_ref, bs_ref, o_ref, *, tm):
    z = _dot(h_ref[...].astype(BF16), w_ref[...])
    z = z * (0.5 * (1.0 + jnp.tanh(np.sqrt(2.0 / np.pi) * (z + 0.044715 * (z * z * z)))))
    u = z[:, :C_OUT]
    v = _layer_norm(z[:, C_OUT:], lg_ref[...], lb_ref[...]).astype(BF16)
    t_i = lax.broadcasted_iota(jnp.int32, (C_CHUNK, C_CHUNK), 0)
    s_i = lax.broadcasted_iota(jnp.int32, (C_CHUNK, C_CHUNK), 1)
    bs = bs_ref[...]
    for g in range(C_GROUPS):
        w = jnp.where(t_i >= s_i, ws_ref[g], 0.0).astype(BF16)
        sl = slice(g * C_GROUP_DIM, (g + 1) * C_GROUP_DIM)
        for c in range(tm // C_CHUNK):
            rows = slice(c * C_CHUNK, (c + 1) * C_CHUNK)
            mixed = _dot(w, v[rows, sl]) + bs[:, g:g + 1]
            o_ref[rows, sl] = (u[rows, sl] * mixed).astype(BF16)


def _gmlp(h, w_c, ln_g, ln_b, ws, bs_t, tm=512):
    n = h.shape[0]
    return pl.pallas_call(
        functools.partial(_gmlp_kernel, tm=tm),
        grid=(n // tm,),
        in_specs=[pl.BlockSpec((tm, D_MODEL), lambda i: (i, 0)),
                  pl.BlockSpec((D_MODEL, 2 * C_OUT), lambda i: (0, 0)),
                  pl.BlockSpec((1, C_OUT), lambda i: (0, 0)),
                  pl.BlockSpec((1, C_OUT), lambda i: (0, 0)),
                  pl.BlockSpec((C_GROUPS, C_CHUNK, C_CHUNK), lambda i: (0, 0, 0)),
                  pl.BlockSpec((C_CHUNK, LANES), lambda i: (0, 0))],
        out_specs=pl.BlockSpec((tm, C_OUT), lambda i: (i, 0)),
        out_shape=jax.ShapeDtypeStruct((n, C_OUT), BF16),
        compiler_params=_cparams(("parallel",)),
        name="gmlp",
    )(h, w_c, ln_g, ln_b, ws, bs_t)


def _sigmoid(x):
    return 1.0 / (1.0 + jnp.exp(-x))


def _merge_kernel(h_ref, ya_ref, yb_ref, yc_ref, wg_ref, wa_ref, wb_ref, wc_ref, wo_ref,
                  lg_ref, lb_ref, wr_ref, br_ref, x_ref, oh_ref, route_ref):
    h = h_ref[...]
    hb = h.astype(BF16)
    acc = None
    for i, (y_ref, wbr_ref) in enumerate(((ya_ref, wa_ref), (yb_ref, wb_ref), (yc_ref, wc_ref))):
        gate = _sigmoid(_dot(hb, wg_ref[:, i * D_MODEL:(i + 1) * D_MODEL]))
        term = gate * _dot(y_ref[...], wbr_ref[...])
        acc = term if acc is None else acc + term
    mix = _dot(acc.astype(BF16), wo_ref[...])
    x1 = _layer_norm(DN_ALPHA * h + mix, lg_ref[...], lb_ref[...])
    x_ref[...] = x1

    logit = jnp.dot(x1, wr_ref[...], precision=HIGHEST, preferred_element_type=F32) + br_ref[...]
    lane = lax.broadcasted_iota(jnp.int32, logit.shape, 1)
    big = jnp.int32(LANES)
    is_g = (lane >= N_EXPERTS) & (lane < N_EXPERTS + N_GROUPS)
    gl = jnp.where(is_g, logit, -jnp.inf)
    g_max = jnp.max(gl, axis=-1, keepdims=True)
    g_idx = jnp.min(jnp.where(is_g & (gl == g_max), lane, big), axis=-1, keepdims=True) - N_EXPERTS
    g_p = 1.0 / jnp.sum(jnp.where(is_g, jnp.exp(gl - g_max), 0.0), axis=-1, keepdims=True)
    lo = g_idx * EXPERTS_PER_GROUP
    in_g = (lane >= lo) & (lane < lo + EXPERTS_PER_GROUP)
    el = jnp.where(in_g, logit, -jnp.inf)
    e1 = jnp.max(el, axis=-1, keepdims=True)
    i1 = jnp.min(jnp.where(in_g & (el == e1), lane, big), axis=-1, keepdims=True)
    el2 = jnp.where(lane == i1, -jnp.inf, el)
    e2 = jnp.max(el2, axis=-1, keepdims=True)
    i2 = jnp.min(jnp.where(in_g & (lane != i1) & (el2 == e2), lane, big), axis=-1, keepdims=True)
    z2 = jnp.exp(e2 - e1)
    den = 1.0 + z2
    oh_ref[...] = jnp.where((lane == i1) | (lane == i2), 1.0, 0.0).astype(BF16)
    route_ref[...] = jnp.where(lane == 0, i1.astype(F32),
                               jnp.where(lane == 1, i2.astype(F32),
                                         jnp.where(lane == 2, (1.0 / den) * g_p,
                                                   jnp.where(lane == 3, (z2 / den) * g_p, 0.0))))


def _merge(h, ya, yb, yc, wg, wa, wb, wc, wo, ln_g, ln_b, wr, br, tm=512):
    n = h.shape[0]
    row = lambda w: pl.BlockSpec((tm, w), lambda i: (i, 0))
    const = lambda r, c: pl.BlockSpec((r, c), lambda i: (0, 0))
    return pl.pallas_call(
        _merge_kernel,
        grid=(n // tm,),
        in_specs=[row(D_MODEL), row(A_OUT), row(B_OUT), row(C_OUT),
                  const(D_MODEL, 3 * D_MODEL), const(A_OUT, D_MODEL), const(B_OUT, D_MODEL),
                  const(C_OUT, D_MODEL), const(D_MODEL, D_MODEL),
                  const(1, D_MODEL), const(1, D_MODEL), const(D_MODEL, LANES), const(1, LANES)],
        out_specs=[row(D_MODEL), row(LANES), row(LANES)],
        out_shape=[jax.ShapeDtypeStruct((n, D_MODEL), F32),
                   jax.ShapeDtypeStruct((n, LANES), BF16),
                   jax.ShapeDtypeStruct((n, LANES), F32)],
        compiler_params=_cparams(("parallel",)),
        name="merge",
    )(h, ya, yb, yc, wg, wa, wb, wc, wo, ln_g, ln_b, wr, br)


FFN_TILE = 256
MOE_TOK = 256


def _rank_kernel(oh_ref, route_ref, r_ref, cnt_ref, carry_ref):
    @pl.when(pl.program_id(0) == 0)
    def _():
        carry_ref[...] = jnp.zeros_like(carry_ref)

    oh = oh_ref[...]
    tt = oh.shape[0]
    r_i = lax.broadcasted_iota(jnp.int32, (tt, tt), 0)
    c_i = lax.broadcasted_iota(jnp.int32, (tt, tt), 1)
    lower = jnp.where(c_i < r_i, 1.0, 0.0).astype(BF16)
    rank = _dot(lower, oh) + carry_ref[...]
    route = route_ref[...]
    lane = lax.broadcasted_iota(jnp.int32, rank.shape, 1)
    i1 = route[:, 0:1].astype(jnp.int32)
    i2 = route[:, 1:2].astype(jnp.int32)
    r1 = jnp.sum(jnp.where(lane == i1, rank, 0.0), axis=-1, keepdims=True)
    r2 = jnp.sum(jnp.where(lane == i2, rank, 0.0), axis=-1, keepdims=True)
    r_ref[...] = jnp.where(lane == 0, r1, jnp.where(lane == 1, r2, 0.0))
    carry_ref[...] += jnp.sum(oh.astype(F32), axis=0, keepdims=True)
    cnt_ref[...] = carry_ref[...]


def _rank(oh, route, tt=512):
    n = oh.shape[0]
    return pl.pallas_call(
        _rank_kernel,
        grid=(n // tt,),
        in_specs=[pl.BlockSpec((tt, LANES), lambda i: (i, 0)),
                  pl.BlockSpec((tt, LANES), lambda i: (i, 0))],
        out_specs=[pl.BlockSpec((tt, LANES), lambda i: (i, 0)),
                   pl.BlockSpec((1, LANES), lambda i: (0, 0))],
        out_shape=[jax.ShapeDtypeStruct((n, LANES), F32),
                   jax.ShapeDtypeStruct((1, LANES), F32)],
        scratch_shapes=[pltpu.VMEM((1, LANES), F32)],
        compiler_params=_cparams(("arbitrary",)),
        name="moe_rank",
    )(oh, route)


def _row_copy(src_ref, src_row, dst_ref, dst_row, sem):
    return pltpu.make_async_copy(src_ref.at[pl.ds(src_row, 1), :], dst_ref.at[pl.ds(dst_row, 1), :], sem)


def _dispatch_kernel(pos1_ref, pos2_ref, x_ref, init_ref, xs_ref, sem):
    del init_ref
    base = pl.program_id(0) * MOE_TOK

    def issue(t, carry):
        _row_copy(x_ref, t, xs_ref, pos1_ref[base + t], sem).start()
        _row_copy(x_ref, t, xs_ref, pos2_ref[base + t], sem).start()
        return carry

    lax.fori_loop(0, MOE_TOK, issue, 0, unroll=8)
    for _ in range(2):
        pltpu.make_async_copy(x_ref, xs_ref.at[pl.ds(0, MOE_TOK), :], sem).wait()


def _dispatch(pos1, pos2, x1, n_slots):
    n = x1.shape[0]
    init = jnp.zeros((n_slots, D_MODEL), F32)
    return pl.pallas_call(
        _dispatch_kernel,
        grid_spec=pltpu.PrefetchScalarGridSpec(
            num_scalar_prefetch=2,
            grid=(n // MOE_TOK,),
            in_specs=[pl.BlockSpec((MOE_TOK, D_MODEL), lambda i, p1, p2: (i, 0)),
                      pl.BlockSpec(memory_space=pl.ANY)],
            out_specs=pl.BlockSpec(memory_space=pl.ANY),
            scratch_shapes=[pltpu.SemaphoreType.DMA(())]),
        out_shape=jax.ShapeDtypeStruct((n_slots, D_MODEL), F32),
        input_output_aliases={3: 0},
        compiler_params=_cparams(("arbitrary",)),
        name="moe_dispatch",
    )(pos1, pos2, x1, init)


def _ffn_kernel(te_ref, nu_ref, xs_ref, wg_ref, wu_ref, wd_ref, ys_ref):
    del te_ref
    i = pl.program_id(0)

    @pl.when(i < nu_ref[0])
    def _():
        xb = xs_ref[...].astype(BF16)
        gte = _dot(xb, wg_ref[...])
        hid = gte * _sigmoid(gte) * _dot(xb, wu_ref[...])
        ys_ref[...] = _dot(hid.astype(BF16), wd_ref[...])

    @pl.when(i >= nu_ref[0])
    def _():
        ys_ref[...] = jnp.zeros_like(ys_ref)


def _ffn(tile_expert, n_used, xs, w_gate, w_up, w_down):
    n_slots = xs.shape[0]
    wspec = lambda r, c: pl.BlockSpec((None, r, c), lambda i, te, nu: (te[i], 0, 0))
    return pl.pallas_call(
        _ffn_kernel,
        grid_spec=pltpu.PrefetchScalarGridSpec(
            num_scalar_prefetch=2,
            grid=(n_slots // FFN_TILE,),
            in_specs=[pl.BlockSpec((FFN_TILE, D_MODEL), lambda i, te, nu: (i, 0)),
                      wspec(D_MODEL, D_EXPERT), wspec(D_MODEL, D_EXPERT), wspec(D_EXPERT, D_MODEL)],
            out_specs=pl.BlockSpec((FFN_TILE, D_MODEL), lambda i, te, nu: (i, 0))),
        out_shape=jax.ShapeDtypeStruct((n_slots, D_MODEL), F32),
        compiler_params=_cparams(("arbitrary",)),
        name="moe_ffn",
    )(tile_expert, n_used, xs, w_gate, w_up, w_down)


def _combine_kernel(pos1_ref, pos2_ref, x_ref, route_ref, lg_ref, lb_ref, ys_ref, o_ref, buf_ref, sem):
    base = pl.program_id(0) * MOE_TOK

    def issue(t, carry):
        _row_copy(ys_ref, pos1_ref[base + t], buf_ref.at[0], t, sem).start()
        _row_copy(ys_ref, pos2_ref[base + t], buf_ref.at[1], t, sem).start()
        return carry

    lax.fori_loop(0, MOE_TOK, issue, 0, unroll=8)
    for s in range(2):
        pltpu.make_async_copy(ys_ref.at[pl.ds(0, MOE_TOK), :], buf_ref.at[s], sem).wait()
    route = route_ref[...]
    ffn = route[:, 2:3] * buf_ref[0] + route[:, 3:4] * buf_ref[1]
    o_ref[...] = _layer_norm(DN_ALPHA * x_ref[...] + ffn, lg_ref[...], lb_ref[...])


def _combine(pos1, pos2, x1, route, ln_g, ln_b, ys):
    n = x1.shape[0]
    row = lambda w: pl.BlockSpec((MOE_TOK, w), lambda i, p1, p2: (i, 0))
    const = lambda w: pl.BlockSpec((1, w), lambda i, p1, p2: (0, 0))
    return pl.pallas_call(
        _combine_kernel,
        grid_spec=pltpu.PrefetchScalarGridSpec(
            num_scalar_prefetch=2,
            grid=(n // MOE_TOK,),
            in_specs=[row(D_MODEL), row(LANES), const(D_MODEL), const(D_MODEL),
                      pl.BlockSpec(memory_space=pl.ANY)],
            out_specs=row(D_MODEL),
            scratch_shapes=[pltpu.VMEM((2, MOE_TOK, D_MODEL), F32),
                            pltpu.SemaphoreType.DMA(())]),
        out_shape=jax.ShapeDtypeStruct((n, D_MODEL), F32),
        compiler_params=_cparams(("arbitrary",)),
        name="moe_combine",
    )(pos1, pos2, x1, route, ln_g, ln_b, ys)


def _moe(x1, oh, route, w_gate, w_up, w_down, ln_g, ln_b):
    n = x1.shape[0]
    n_slots = TOP_K_INNER * n + N_EXPERTS * FFN_TILE
    rank, cnt = _rank(oh, route)
    cnt = cnt[0, :N_EXPERTS].astype(jnp.int32)
    padded = (cnt + FFN_TILE - 1) // FFN_TILE * FFN_TILE
    ends = jnp.cumsum(padded)
    offs = ends - padded
    experts = jnp.arange(N_EXPERTS, dtype=jnp.int32)
    i1 = route[:, 0].astype(jnp.int32)
    i2 = route[:, 1].astype(jnp.int32)
    off_of = lambda idx: jnp.sum(jnp.where(idx[:, None] == experts[None, :], offs[None, :], 0), axis=1)
    pos1 = off_of(i1) + rank[:, 0].astype(jnp.int32)
    pos2 = off_of(i2) + rank[:, 1].astype(jnp.int32)
    tile_start = jnp.arange(n_slots // FFN_TILE, dtype=jnp.int32) * FFN_TILE
    tile_expert = jnp.minimum(jnp.sum(tile_start[:, None] >= ends[None, :], axis=1), N_EXPERTS - 1)
    n_used = (ends[-1] // FFN_TILE).reshape(1)

    xs = _dispatch(pos1, pos2, x1, n_slots)
    ys = _ffn(tile_expert.astype(jnp.int32), n_used.astype(jnp.int32), xs, w_gate, w_up, w_down)
    return _combine(pos1, pos2, x1, route, ln_g, ln_b, ys)


def _rope_tables(positions):
    inv_freq = ROPE_THETA ** (-jnp.arange(0, ROT_DIM, 2, dtype=F32) / ROT_DIM)
    ang = positions.astype(F32).reshape(-1, 1) * inv_freq
    cos, sin = jnp.cos(ang), jnp.sin(ang)
    n = ang.shape[0]
    half = ROT_DIM // 2
    c64 = jnp.concatenate([cos, cos, jnp.ones((n, A_HEAD_DIM - ROT_DIM), F32)], axis=1)
    s64 = jnp.concatenate([-sin, sin, jnp.zeros((n, A_HEAD_DIM - ROT_DIM), F32)], axis=1)
    del half
    return jnp.tile(c64, (1, 2)), jnp.tile(s64, (1, 2))


def _rep_cols(w):
    r = w.shape[0]
    w4 = w.reshape(r, B_HEADS, 1, B_KEY_DIM)
    return jnp.broadcast_to(w4, (r, B_HEADS, N_SUB, B_KEY_DIM)).reshape(r, GLA_REP)


def _split_w_in(w):
    widths = (A_OUT, A_HEAD_DIM, A_HEAD_DIM, IDX_HEADS * IDX_DIM, IDX_DIM, IDX_HEADS,
              B_HEADS * B_KEY_DIM, B_HEADS * B_KEY_DIM, B_OUT, B_GATE_RANK, B_OUT,
              2 * C_OUT, D_MODEL, D_MODEL, D_MODEL)
    pts = np.cumsum((0,) + widths)
    return [w[:, int(pts[i]):int(pts[i + 1])] for i in range(len(widths))]


def kernel(x, positions, ln_in_g, ln_in_b, w_in, idx_k_g, gla_wa2, gla_ba, gla_norm_g, gm_ln_g, gm_ln_b, gm_ws, gm_bs, w_branch_a, w_branch_b, w_branch_c, w_out, ln1_g, ln1_b, w_rg, b_rg, w_re, b_re, w_gate, w_up, w_down, ln2_g, ln2_b):
    bsz, seq, d = x.shape
    n = bsz * seq
    cos_t, sin_t = _rope_tables(positions)
    h = _entry_ln(x.reshape(n, d), ln_in_g, ln_in_b)
    for l in range(DEPTH):
        (a_q, a_k, a_v, i_q, i_k, i_w, b_q, b_k, b_v, b_glr, b_r, c_uv,
         g_a, g_b, g_c) = _split_w_in(w_in[l])
        zpad = lambda c: jnp.zeros((d, c), F32)
        w_a = jnp.concatenate([a_q, a_k, a_v, i_q, i_k, i_w, zpad(LANES - IDX_DIM - IDX_HEADS)],
                              axis=1).astype(BF16)
        w_b = jnp.concatenate([_rep_cols(b_q * (B_KEY_DIM ** -0.5)), _rep_cols(b_k), b_v, b_r,
                               b_glr, zpad(LANES - B_GATE_RANK)], axis=1).astype(BF16)
        wa_rep = jnp.concatenate([_rep_cols(gla_wa2[l]),
                                  jnp.zeros((LANES - B_GATE_RANK, GLA_REP), F32)], axis=0).astype(BF16)
        ba_rep = _rep_cols(gla_ba[l].reshape(1, -1))
        w_g = jnp.concatenate([g_a, g_b, g_c], axis=1).astype(BF16)
        ikg = jnp.concatenate([idx_k_g[l], jnp.zeros((LANES - IDX_DIM,), F32)]).reshape(1, LANES)
        bs_t = jnp.concatenate([gm_bs[l].T, jnp.zeros((C_CHUNK, LANES - C_GROUPS), F32)], axis=1)
        w_r = jnp.concatenate([w_re[l], w_rg[l], zpad(LANES - N_GROUPS - N_EXPERTS)], axis=1)
        b_r_all = jnp.concatenate([b_re[l], b_rg[l],
                                   jnp.zeros((LANES - N_GROUPS - N_EXPERTS,), F32)]).reshape(1, LANES)

        q, k, v, iq, ki, wi = _proj_a(h, w_a, cos_t, sin_t, ikg)
        y_a = _dsa(q, iq, wi, k, v, ki, bsz, seq)
        y_b = _gla(h, w_b, wa_rep, ba_rep, gla_norm_g[l].reshape(1, -1), bsz, seq)
        y_c = _gmlp(h, c_uv.astype(BF16), gm_ln_g[l].reshape(1, -1), gm_ln_b[l].reshape(1, -1),
                    gm_ws[l], bs_t)
        x1, oh, route = _merge(h, y_a, y_b, y_c, w_g,
                               w_branch_a[l].astype(BF16), w_branch_b[l].astype(BF16),
                               w_branch_c[l].astype(BF16), w_out[l].astype(BF16),
                               ln1_g[l].reshape(1, -1), ln1_b[l].reshape(1, -1), w_r, b_r_all)
        h = _moe(x1, oh, route, w_gate[l].astype(BF16), w_up[l].astype(BF16),
                 w_down[l].astype(BF16), ln2_g[l].reshape(1, -1), ln2_b[l].reshape(1, -1))
    return h.reshape(bsz, seq, d)
```

```python
import functools

import numpy as np
import jax
import jax.numpy as jnp
from jax import lax
from jax.experimental import pallas as pl
from jax.experimental.pallas import tpu as pltpu

F32 = jnp.float32
BF16 = jnp.bfloat16

D_MODEL = 1024
DEPTH = 2
A_HEADS = 8
A_HEAD_DIM = 64
A_OUT = 512
IDX_HEADS = 4
IDX_DIM = 64
TOPK_MAX = 256
Q_BLOCK = 128
ROPE_THETA = 500000.0
ROT_DIM = 16
B_HEADS = 4
B_KEY_DIM = 64
B_VAL_DIM = 128
B_OUT = 512
B_GATE_RANK = 16
B_GATE_TAU = 16.0
B_CHUNK = 64
B_SUB = 16
N_SUB = B_CHUNK // B_SUB
C_GROUPS = 4
C_GROUP_DIM = 128
C_OUT = 512
C_CHUNK = 128
N_GROUPS = 4
EXPERTS_PER_GROUP = 8
N_EXPERTS = 32
D_EXPERT = 256
TOP_K_INNER = 2
DN_ALPHA = (2 * DEPTH) ** 0.25
LN_EPS = 1e-5
RMS_EPS = 1e-6

LANES = 128
NEG_BIG = -1e30
INT_MIN = -2 ** 31
VMEM_LIMIT = 56 * 1024 * 1024

HIGHEST = lax.Precision.HIGHEST


def _cparams(sem):
    return pltpu.CompilerParams(dimension_semantics=sem, vmem_limit_bytes=VMEM_LIMIT)


def _layer_norm(x, g, b):
    mu = jnp.mean(x, axis=-1, keepdims=True)
    xc = x - mu
    var = jnp.mean(xc * xc, axis=-1, keepdims=True)
    return xc * lax.rsqrt(var + LN_EPS) * g + b


def _dot(a, b):
    return jnp.dot(a, b, preferred_element_type=F32)


def _dot_nt(a, b):
    return lax.dot_general(a, b, (((1,), (1,)), ((), ())), preferred_element_type=F32)


def _ln_kernel(x_ref, g_ref, b_ref, o_ref):
    o_ref[...] = _layer_norm(x_ref[...], g_ref[...], b_ref[...])


def _entry_ln(x, g, b, tm=1024):
    n = x.shape[0]
    return pl.pallas_call(
        _ln_kernel,
        grid=(n // tm,),
        in_specs=[pl.BlockSpec((tm, D_MODEL), lambda i: (i, 0)),
                  pl.BlockSpec((1, D_MODEL), lambda i: (0, 0)),
                  pl.BlockSpec((1, D_MODEL), lambda i: (0, 0))],
        out_specs=pl.BlockSpec((tm, D_MODEL), lambda i: (i, 0)),
        out_shape=jax.ShapeDtypeStruct((n, D_MODEL), F32),
        compiler_params=_cparams(("parallel",)),
        name="entry_ln",
    )(x, g.reshape(1, -1), b.reshape(1, -1))


def _rope_slab(xs, c, s):
    lane = lax.broadcasted_iota(jnp.int32, xs.shape, 1) % A_HEAD_DIM
    fwd = pltpu.roll(xs, LANES - ROT_DIM // 2, axis=1)
    bwd = pltpu.roll(xs, ROT_DIM // 2, axis=1)
    partner = jnp.where(lane < ROT_DIM // 2, fwd, bwd)
    return xs * c + partner * s


def _proj_a_kernel(h_ref, w_ref, c_ref, s_ref, g_ref,
                   q_ref, k_ref, v_ref, iq_ref, ki_ref, wi_ref):
    p = _dot(h_ref[...].astype(BF16), w_ref[...])
    c = c_ref[...]
    s = s_ref[...]
    lane = lax.broadcasted_iota(jnp.int32, c.shape, 1)
    lo = lane < A_HEAD_DIM
    n_blk = p.shape[0] // Q_BLOCK

    def put_heads(dst_ref, slab, first_head):
        for half in range(2):
            hd = first_head + half
            cols = slab[:, half * A_HEAD_DIM:(half + 1) * A_HEAD_DIM].astype(BF16)
            for qb in range(n_blk):
                dst_ref[qb, hd * Q_BLOCK:(hd + 1) * Q_BLOCK, :] = cols[qb * Q_BLOCK:(qb + 1) * Q_BLOCK, :]

    for i in range(4):
        sl = slice(i * LANES, (i + 1) * LANES)
        put_heads(q_ref, _rope_slab(p[:, sl], c, s) * (A_HEAD_DIM ** -0.5), 2 * i)
    kv = p[:, 512:640]
    kv = _rope_slab(kv, jnp.where(lo, c, 1.0), jnp.where(lo, s, 0.0))
    k_ref[...] = kv[:, :A_HEAD_DIM].astype(BF16)
    v_ones = jnp.where(lo, pltpu.roll(kv, A_HEAD_DIM, axis=1), jnp.where(lane == A_HEAD_DIM, 1.0, 0.0))
    v_ref[...] = v_ones.astype(BF16)
    for i in range(2):
        sl = slice(640 + i * LANES, 640 + (i + 1) * LANES)
        put_heads(iq_ref, _rope_slab(p[:, sl], c, s), 2 * i)
    last = p[:, 896:1024]
    mu = jnp.sum(jnp.where(lo, last, 0.0), axis=-1, keepdims=True) * (1.0 / IDX_DIM)
    xc = jnp.where(lo, last - mu, 0.0)
    var = jnp.sum(xc * xc, axis=-1, keepdims=True) * (1.0 / IDX_DIM)
    kin = xc * lax.rsqrt(var + LN_EPS) * g_ref[...]
    kin = _rope_slab(kin, jnp.where(lo, c, 1.0), jnp.where(lo, s, 0.0))
    ki_ref[...] = kin[:, :IDX_DIM].astype(BF16)
    wi = pltpu.roll(last, LANES - IDX_DIM, axis=1) * (IDX_HEADS ** -0.5 * IDX_DIM ** -0.5)
    for qb in range(n_blk):
        wi_ref[qb] = wi[qb * Q_BLOCK:(qb + 1) * Q_BLOCK, :].T[0:8, :]


def _proj_a(h, w_a, cos_t, sin_t, ikg, tm=512):
    n = h.shape[0]
    row = lambda w: pl.BlockSpec((tm, w), lambda i: (i, 0))
    stacked = lambda heads: pl.BlockSpec((tm // Q_BLOCK, heads * Q_BLOCK, A_HEAD_DIM), lambda i: (i, 0, 0))
    return pl.pallas_call(
        _proj_a_kernel,
        grid=(n // tm,),
        in_specs=[row(D_MODEL),
                  pl.BlockSpec((D_MODEL, 1024), lambda i: (0, 0)),
                  row(LANES), row(LANES),
                  pl.BlockSpec((1, LANES), lambda i: (0, 0))],
        out_specs=[stacked(A_HEADS), row(A_HEAD_DIM), row(LANES),
                   stacked(IDX_HEADS), row(IDX_DIM),
                   pl.BlockSpec((tm // Q_BLOCK, 8, Q_BLOCK), lambda i: (i, 0, 0))],
        out_shape=[jax.ShapeDtypeStruct((n // Q_BLOCK, A_HEADS * Q_BLOCK, A_HEAD_DIM), BF16),
                   jax.ShapeDtypeStruct((n, A_HEAD_DIM), BF16),
                   jax.ShapeDtypeStruct((n, LANES), BF16),
                   jax.ShapeDtypeStruct((n // Q_BLOCK, IDX_HEADS * Q_BLOCK, IDX_DIM), BF16),
                   jax.ShapeDtypeStruct((n, IDX_DIM), BF16),
                   jax.ShapeDtypeStruct((n // Q_BLOCK, 8, Q_BLOCK), F32)],
        compiler_params=_cparams(("parallel",)),
        name="proj_a",
    )(h, w_a, cos_t, sin_t, ikg)


KEY_CHUNK = 512
HALF16 = 2 ** 15


def _dsa_kernel(q_ref, iq_ref, wi_ref, k_ref, v_ref, ki_ref, o_ref,
                key_ref, hi_ref, lo_ref, bias_ref, lg0_ref, lg1_ref, p0_ref, p1_ref, *, seq):
    j = pl.program_id(1)
    n_sel = min(TOPK_MAX, seq // 4)
    n_kc = (j * Q_BLOCK + Q_BLOCK + KEY_CHUNK - 1) // KEY_CHUNK
    qpos = j * Q_BLOCK + lax.broadcasted_iota(jnp.int32, (1, Q_BLOCK), 1)
    kk = jnp.minimum(n_sel, qpos + 1).astype(F32)
    row_pos = lax.broadcasted_iota(jnp.int32, (KEY_CHUNK, 1), 0)

    def key_rows(c):
        return pl.ds(pl.multiple_of(c * KEY_CHUNK, KEY_CHUNK), KEY_CHUNK)

    iq = iq_ref[...]
    wi = wi_ref[...]

    def score_chunk(c, carry):
        d = jnp.maximum(_dot_nt(ki_ref[key_rows(c), :], iq), 0.0)
        score = wi[0:1, :] * d[:, 0:Q_BLOCK]
        for h in range(1, IDX_HEADS):
            score = score + wi[h:h + 1, :] * d[:, h * Q_BLOCK:(h + 1) * Q_BLOCK]
        bits = pltpu.bitcast(score + 0.0, jnp.int32)
        key = bits ^ ((bits >> 31) & 0x7FFFFFFF)
        key = jnp.where(c * KEY_CHUNK + row_pos <= qpos, key, INT_MIN)
        key_ref[c] = key
        hi_ref[c] = (key >> 16).astype(jnp.int16)
        lo_ref[c] = ((key & 0xFFFF) - HALF16).astype(jnp.int16)
        return carry

    lax.fori_loop(0, n_kc, score_chunk, 0)

    def count(pred):
        def body(c, acc):
            m = jnp.where(pred(key_ref[c]), 1.0, 0.0)
            return acc + jnp.sum(m.reshape(KEY_CHUNK // 64, 64, Q_BLOCK), axis=0)
        acc = lax.fori_loop(0, n_kc, body, jnp.zeros((64, Q_BLOCK), F32))
        return jnp.sum(acc, axis=0, keepdims=True)

    one16, zero16 = jnp.int16(1), jnp.int16(0)

    def count16(ref, pred):
        def body(c, acc):
            m = jnp.where(pred(ref[c]), one16, zero16).reshape(KEY_CHUNK // 64, 64, Q_BLOCK)
            parts = [m[t] for t in range(KEY_CHUNK // 64)]
            while len(parts) > 1:
                parts = [a + b for a, b in zip(parts[::2], parts[1::2])]
            return acc + parts[0]
        acc = lax.fori_loop(0, n_kc, body, jnp.zeros((64, Q_BLOCK), jnp.int16))
        return jnp.sum(acc.astype(F32), axis=0, keepdims=True)

    def search16(ref, target):
        def step(i, t_u):
            cand_u = t_u | jnp.left_shift(jnp.int32(1), 15 - i)
            cand = (cand_u - HALF16).astype(jnp.int16)
            return jnp.where(count16(ref, lambda kc: kc >= cand) >= target, cand_u, t_u)
        return lax.fori_loop(0, 16, step, jnp.zeros((1, Q_BLOCK), jnp.int32))

    t_hi = search16(hi_ref, kk) - HALF16
    t_hi16 = t_hi.astype(jnp.int16)
    need_lo = kk - count16(hi_ref, lambda kc: kc > t_hi16)

    def fill_lo(c, carry):
        lo_ref[c] = jnp.where(hi_ref[c] == t_hi16, lo_ref[c], jnp.int16(-HALF16))
        return carry

    lax.fori_loop(0, n_kc, fill_lo, 0)
    thr = (t_hi << 16) | search16(lo_ref, need_lo)
    c_gt = count(lambda kc: kc > thr)
    c_ge = count(lambda kc: kc >= thr)
    need = kk - c_gt
    has_ties = jnp.max(c_ge - kk) > 0.0

    @pl.when(jnp.logical_not(has_ties))
    def _():
        def fill(c, carry):
            bias_ref[c] = jnp.where(key_ref[c] >= thr, 0.0, NEG_BIG)
            return carry
        lax.fori_loop(0, n_kc, fill, 0)

    @pl.when(has_ties)
    def _():
        r_i = lax.broadcasted_iota(jnp.int32, (KEY_CHUNK, KEY_CHUNK), 0)
        c_i = lax.broadcasted_iota(jnp.int32, (KEY_CHUNK, KEY_CHUNK), 1)
        lower = jnp.where(c_i < r_i, 1.0, 0.0).astype(BF16)

        def fill(c, carry):
            kc = key_ref[c]
            eq = kc == thr
            eq_f = jnp.where(eq, 1.0, 0.0)
            pre = _dot(lower, eq_f.astype(BF16)) + carry
            sel = (kc > thr) | (eq & (pre < need))
            bias_ref[c] = jnp.where(sel, 0.0, NEG_BIG)
            return carry + jnp.sum(eq_f, axis=0, keepdims=True)

        lax.fori_loop(0, n_kc, fill, jnp.zeros((1, Q_BLOCK), F32))

    q = q_ref[...]
    cols = A_HEADS * Q_BLOCK

    half = KEY_CHUNK // 2

    def half_rows(c, hf):
        return pl.ds(pl.multiple_of(c * KEY_CHUNK + hf * half, half), half)

    def logits(c, hf):
        b = bias_ref[c, hf * half:(hf + 1) * half, :]
        return _dot_nt(k_ref[half_rows(c, hf), :], q) + jnp.concatenate([b] * A_HEADS, axis=1)

    def weighted_values(c, hf, p_in):
        return lax.dot_general(v_ref[half_rows(c, hf), :], p_in[...], (((0,), (0,)), ((), ())),
                               preferred_element_type=F32)

    def step(nxt, cur_lg, nxt_lg, prv, prv_p, cur_p, carry):
        m, alpha, acc = carry
        nxt_lg[...] = logits(*nxt)
        lg = cur_lg[...]
        m_new = jnp.maximum(m, jnp.max(lg, axis=0, keepdims=True))
        acc = alpha * acc + weighted_values(*prv, prv_p)
        cur_p[...] = jnp.exp(lg - m_new).astype(BF16)
        return m_new, jnp.exp(m - m_new), acc

    def attn_chunk(c, carry):
        carry = step((c, 1), lg0_ref, lg1_ref, (jnp.maximum(c - 1, 0), 1), p1_ref, p0_ref, carry)
        return step((jnp.minimum(c + 1, n_kc - 1), 0), lg1_ref, lg0_ref, (c, 0), p0_ref, p1_ref, carry)

    lg0_ref[...] = logits(0, 0)
    p1_ref[...] = jnp.zeros(p1_ref.shape, BF16)
    init = (jnp.full((1, cols), -3e38, F32), jnp.ones((1, cols), F32), jnp.zeros((LANES, cols), F32))
    _, alpha, acc = lax.fori_loop(0, n_kc, attn_chunk, init)
    acc = alpha * acc + weighted_values(n_kc - 1, 1, p1_ref)
    o = acc[:A_HEAD_DIM] / acc[A_HEAD_DIM:A_HEAD_DIM + 1]
    for h in range(A_HEADS):
        o_ref[:, h * A_HEAD_DIM:(h + 1) * A_HEAD_DIM] = (
            o[:, h * Q_BLOCK:(h + 1) * Q_BLOCK].T.astype(BF16))


def _dsa(q, iq, wi, k, v, ki, bsz, seq):
    nq = seq // Q_BLOCK
    n_kc = seq // KEY_CHUNK
    stacked = lambda heads: pl.BlockSpec((None, heads * Q_BLOCK, A_HEAD_DIM),
                                         lambda b, j: (b * nq + j, 0, 0))
    kspec = lambda w: pl.BlockSpec((seq, w), lambda b, j: (b, 0))
    return pl.pallas_call(
        functools.partial(_dsa_kernel, seq=seq),
        grid=(bsz, nq),
        in_specs=[stacked(A_HEADS), stacked(IDX_HEADS),
                  pl.BlockSpec((None, 8, Q_BLOCK), lambda b, j: (b * nq + j, 0, 0)),
                  kspec(A_HEAD_DIM), kspec(LANES), kspec(IDX_DIM)],
        out_specs=pl.BlockSpec((Q_BLOCK, A_OUT), lambda b, j: (b * nq + j, 0)),
        out_shape=jax.ShapeDtypeStruct((bsz * seq, A_OUT), BF16),
        scratch_shapes=[pltpu.VMEM((n_kc, KEY_CHUNK, Q_BLOCK), jnp.int32),
                        pltpu.VMEM((n_kc, KEY_CHUNK, Q_BLOCK), jnp.int16),
                        pltpu.VMEM((n_kc, KEY_CHUNK, Q_BLOCK), jnp.int16),
                        pltpu.VMEM((n_kc, KEY_CHUNK, Q_BLOCK), F32),
                        pltpu.VMEM((KEY_CHUNK // 2, A_HEADS * Q_BLOCK), F32),
                        pltpu.VMEM((KEY_CHUNK // 2, A_HEADS * Q_BLOCK), F32),
                        pltpu.VMEM((KEY_CHUNK // 2, A_HEADS * Q_BLOCK), BF16),
                        pltpu.VMEM((KEY_CHUNK // 2, A_HEADS * Q_BLOCK), BF16)],
        compiler_params=_cparams(("parallel", "parallel")),
        name="dsa",
    )(q, iq, wi, k, v, ki)


GLA_REP = B_HEADS * N_SUB * B_KEY_DIM
GLA_COLS = 2 * GLA_REP + 2 * B_OUT + LANES


def _gla_kernel(h_ref, w_ref, wa_ref, ba_ref, ng_ref, o_ref, p_ref, g_ref, acc_ref, st_ref, *, ts):
    @pl.when(pl.program_id(1) == 0)
    def _():
        st_ref[...] = jnp.zeros_like(st_ref)

    p_ref[...] = _dot(h_ref[...].astype(BF16), w_ref[...])
    x = _dot(p_ref[:, 2 * GLA_REP + 2 * B_OUT:].astype(BF16), wa_ref[...]) + ba_ref[...]
    g_ref[...] = (jnp.minimum(x, 0.0) - jnp.log1p(jnp.exp(-jnp.abs(x)))) * (1.0 / B_GATE_TAU)

    t_i = lax.broadcasted_iota(jnp.int32, (B_CHUNK, B_CHUNK), 0)
    s_i = lax.broadcasted_iota(jnp.int32, (B_CHUNK, B_CHUNK), 1)
    tri = t_i >= s_i
    tri_f = jnp.where(tri, 1.0, 0.0)
    lane_sub = (lax.broadcasted_iota(jnp.int32, (1, GLA_REP), 1) // B_KEY_DIM) % N_SUB
    row_sub = lax.broadcasted_iota(jnp.int32, (B_CHUNK, 1), 0) // B_SUB
    q_mask = lane_sub <= row_sub
    k_mask = lane_sub == row_sub

    def chunk(c, carry):
        rows = pl.ds(pl.multiple_of(c * B_CHUNK, B_CHUNK), B_CHUNK)
        b = jnp.dot(tri_f, g_ref[rows, :], precision=HIGHEST, preferred_element_type=F32)
        ref = jnp.zeros((1, GLA_REP), F32)
        for jj in range(1, N_SUB):
            ref = ref + jnp.where(lane_sub == jj, b[jj * B_SUB - 1:jj * B_SUB, :], 0.0)
        d = b - ref
        e_q = jnp.where(q_mask, jnp.exp(jnp.minimum(d, 0.0)), 0.0)
        e_k = jnp.where(k_mask, jnp.exp(jnp.where(k_mask, -d, 0.0)), 0.0)
        b_last = b[B_CHUNK - 1:B_CHUNK, :]
        q_c = (p_ref[rows, 0:GLA_REP] * e_q).astype(BF16)
        k_raw = p_ref[rows, GLA_REP:2 * GLA_REP]
        k_c = (k_raw * e_k).astype(BF16)
        k_st = (k_raw * jnp.exp(b_last - b)).astype(BF16)
        e_last = jnp.exp(b_last)
        for hh in range(B_HEADS):
            base = hh * N_SUB * B_KEY_DIM
            q_h = q_c[:, base:base + N_SUB * B_KEY_DIM]
            k_h = k_c[:, base:base + N_SUB * B_KEY_DIM]
            att = jnp.where(tri, _dot_nt(q_h, k_h), 0.0)
            v_h = p_ref[rows, 2 * GLA_REP + hh * B_VAL_DIM:2 * GLA_REP + (hh + 1) * B_VAL_DIM].astype(BF16)
            st = st_ref[hh]
            o = _dot(att.astype(BF16), v_h) + _dot_nt(q_h[:, :B_KEY_DIM], st.astype(BF16))
            acc_ref[rows, hh * B_VAL_DIM:(hh + 1) * B_VAL_DIM] = o
            upd = _dot(v_h.T, k_st[:, base:base + B_KEY_DIM])
            st_ref[hh] = st * e_last[:, base:base + B_KEY_DIM] + upd
        return carry

    lax.fori_loop(0, ts // B_CHUNK, chunk, 0)

    r = p_ref[:, 2 * GLA_REP + B_OUT:2 * GLA_REP + 2 * B_OUT]
    gate = r * (1.0 / (1.0 + jnp.exp(-r)))
    ng = ng_ref[...]
    for hh in range(B_HEADS):
        sl = slice(hh * B_VAL_DIM, (hh + 1) * B_VAL_DIM)
        o = acc_ref[:, sl]
        y = o * lax.rsqrt(jnp.mean(o * o, axis=-1, keepdims=True) + RMS_EPS) * ng
        o_ref[:, sl] = (gate[:, sl] * y).astype(BF16)


def _gla(h, w_b, wa_rep, ba_rep, norm_g, bsz, seq, ts=512):
    ns = seq // ts
    const = lambda shape: pl.BlockSpec(shape, lambda b, i: (0, 0))
    return pl.pallas_call(
        functools.partial(_gla_kernel, ts=ts),
        grid=(bsz, ns),
        in_specs=[pl.BlockSpec((ts, D_MODEL), lambda b, i: (b * ns + i, 0)),
                  const((D_MODEL, GLA_COLS)), const((LANES, GLA_REP)),
                  const((1, GLA_REP)), const((1, B_VAL_DIM))],
        out_specs=pl.BlockSpec((ts, B_OUT), lambda b, i: (b * ns + i, 0)),
        out_shape=jax.ShapeDtypeStruct((bsz * seq, B_OUT), BF16),
        scratch_shapes=[pltpu.VMEM((ts, GLA_COLS), F32),
                        pltpu.VMEM((ts, GLA_REP), F32),
                        pltpu.VMEM((ts, B_OUT), F32),
                        pltpu.VMEM((B_HEADS, B_VAL_DIM, B_KEY_DIM), F32)],
        compiler_params=_cparams(("parallel", "arbitrary")),
        name="gla",
    )(h, w_b, wa_rep, ba_rep, norm_g)


def _gmlp_kernel(h_ref, w_ref, lg_ref, lb_ref, ws_ref, bs_ref, o_ref, *, tm):
    z = _dot(h_ref[...].astype(BF16), w_ref[...])
    z = z * (0.5 * (1.0 + jnp.tanh(np.sqrt(2.0 / np.pi) * (z + 0.044715 * (z * z * z)))))
    u = z[:, :C_OUT]
    v = _layer_norm(z[:, C_OUT:], lg_ref[...], lb_ref[...]).astype(BF16)
    t_i = lax.broadcasted_iota(jnp.int32, (C_CHUNK, C_CHUNK), 0)
    s_i = lax.broadcasted_iota(jnp.int32, (C_CHUNK, C_CHUNK), 1)
    bs = bs_ref[...]
    for g in range(C_GROUPS):
        w = jnp.where(t_i >= s_i, ws_ref[g], 0.0).astype(BF16)
        sl = slice(g * C_GROUP_DIM, (g + 1) * C_GROUP_DIM)
        for c in range(tm // C_CHUNK):
            rows = slice(c * C_CHUNK, (c + 1) * C_CHUNK)
            mixed = _dot(w, v[rows, sl]) + bs[:, g:g + 1]
            o_ref[rows, sl] = (u[rows, sl] * mixed).astype(BF16)


def _gmlp(h, w_c, ln_g, ln_b, ws, bs_t, tm=512):
    n = h.shape[0]
    return pl.pallas_call(
        functools.partial(_gmlp_kernel, tm=tm),
        grid=(n // tm,),
        in_specs=[pl.BlockSpec((tm, D_MODEL), lambda i: (i, 0)),
                  pl.BlockSpec((D_MODEL, 2 * C_OUT), lambda i: (0, 0)),
                  pl.BlockSpec((1, C_OUT), lambda i: (0, 0)),
                  pl.BlockSpec((1, C_OUT), lambda i: (0, 0)),
                  pl.BlockSpec((C_GROUPS, C_CHUNK, C_CHUNK), lambda i: (0, 0, 0)),
                  pl.BlockSpec((C_CHUNK, LANES), lambda i: (0, 0))],
        out_specs=pl.BlockSpec((tm, C_OUT), lambda i: (i, 0)),
        out_shape=jax.ShapeDtypeStruct((n, C_OUT), BF16),
        compiler_params=_cparams(("parallel",)),
        name="gmlp",
    )(h, w_c, ln_g, ln_b, ws, bs_t)


def _sigmoid(x):
    return 1.0 / (1.0 + jnp.exp(-x))


def _merge_kernel(h_ref, ya_ref, yb_ref, yc_ref, wg_ref, wa_ref, wb_ref, wc_ref, wo_ref,
                  lg_ref, lb_ref, wr_ref, br_ref, x_ref, oh_ref, route_ref):
    h = h_ref[...]
    hb = h.astype(BF16)
    acc = None
    for i, (y_ref, wbr_ref) in enumerate(((ya_ref, wa_ref), (yb_ref, wb_ref), (yc_ref, wc_ref))):
        gate = _sigmoid(_dot(hb, wg_ref[:, i * D_MODEL:(i + 1) * D_MODEL]))
        term = gate * _dot(y_ref[...], wbr_ref[...])
        acc = term if acc is None else acc + term
    mix = _dot(acc.astype(BF16), wo_ref[...])
    x1 = _layer_norm(DN_ALPHA * h + mix, lg_ref[...], lb_ref[...])
    x_ref[...] = x1

    logit = jnp.dot(x1, wr_ref[...], precision=HIGHEST, preferred_element_type=F32) + br_ref[...]
    lane = lax.broadcasted_iota(jnp.int32, logit.shape, 1)
    big = jnp.int32(LANES)
    is_g = (lane >= N_EXPERTS) & (lane < N_EXPERTS + N_GROUPS)
    gl = jnp.where(is_g, logit, -jnp.inf)
    g_max = jnp.max(gl, axis=-1, keepdims=True)
    g_idx = jnp.min(jnp.where(is_g & (gl == g_max), lane, big), axis=-1, keepdims=True) - N_EXPERTS
    g_p = 1.0 / jnp.sum(jnp.where(is_g, jnp.exp(gl - g_max), 0.0), axis=-1, keepdims=True)
    lo = g_idx * EXPERTS_PER_GROUP
    in_g = (lane >= lo) & (lane < lo + EXPERTS_PER_GROUP)
    el = jnp.where(in_g, logit, -jnp.inf)
    e1 = jnp.max(el, axis=-1, keepdims=True)
    i1 = jnp.min(jnp.where(in_g & (el == e1), lane, big), axis=-1, keepdims=True)
    el2 = jnp.where(lane == i1, -jnp.inf, el)
    e2 = jnp.max(el2, axis=-1, keepdims=True)
    i2 = jnp.min(jnp.where(in_g & (lane != i1) & (el2 == e2), lane, big), axis=-1, keepdims=True)
    z2 = jnp.exp(e2 - e1)
    den = 1.0 + z2
    oh_ref[...] = jnp.where((lane == i1) | (lane == i2), 1.0, 0.0).astype(BF16)
    route_ref[...] = jnp.where(lane == 0, i1.astype(F32),
                               jnp.where(lane == 1, i2.astype(F32),
                                         jnp.where(lane == 2, (1.0 / den) * g_p,
                                                   jnp.where(lane == 3, (z2 / den) * g_p, 0.0))))


def _merge(h, ya, yb, yc, wg, wa, wb, wc, wo, ln_g, ln_b, wr, br, tm=512):
    n = h.shape[0]
    row = lambda w: pl.BlockSpec((tm, w), lambda i: (i, 0))
    const = lambda r, c: pl.BlockSpec((r, c), lambda i: (0, 0))
    return pl.pallas_call(
        _merge_kernel,
        grid=(n // tm,),
        in_specs=[row(D_MODEL), row(A_OUT), row(B_OUT), row(C_OUT),
                  const(D_MODEL, 3 * D_MODEL), const(A_OUT, D_MODEL), const(B_OUT, D_MODEL),
                  const(C_OUT, D_MODEL), const(D_MODEL, D_MODEL),
                  const(1, D_MODEL), const(1, D_MODEL), const(D_MODEL, LANES), const(1, LANES)],
        out_specs=[row(D_MODEL), row(LANES), row(LANES)],
        out_shape=[jax.ShapeDtypeStruct((n, D_MODEL), F32),
                   jax.ShapeDtypeStruct((n, LANES), BF16),
                   jax.ShapeDtypeStruct((n, LANES), F32)],
        compiler_params=_cparams(("parallel",)),
        name="merge",
    )(h, ya, yb, yc, wg, wa, wb, wc, wo, ln_g, ln_b, wr, br)


FFN_TILE = 256
MOE_TOK = 256


def _rank_kernel(oh_ref, route_ref, r_ref, cnt_ref, carry_ref):
    @pl.when(pl.program_id(0) == 0)
    def _():
        carry_ref[...] = jnp.zeros_like(carry_ref)

    oh = oh_ref[...]
    tt = oh.shape[0]
    r_i = lax.broadcasted_iota(jnp.int32, (tt, tt), 0)
    c_i = lax.broadcasted_iota(jnp.int32, (tt, tt), 1)
    lower = jnp.where(c_i < r_i, 1.0, 0.0).astype(BF16)
    rank = _dot(lower, oh) + carry_ref[...]
    route = route_ref[...]
    lane = lax.broadcasted_iota(jnp.int32, rank.shape, 1)
    i1 = route[:, 0:1].astype(jnp.int32)
    i2 = route[:, 1:2].astype(jnp.int32)
    r1 = jnp.sum(jnp.where(lane == i1, rank, 0.0), axis=-1, keepdims=True)
    r2 = jnp.sum(jnp.where(lane == i2, rank, 0.0), axis=-1, keepdims=True)
    r_ref[...] = jnp.where(lane == 0, r1, jnp.where(lane == 1, r2, 0.0))
    carry_ref[...] += jnp.sum(oh.astype(F32), axis=0, keepdims=True)
    cnt_ref[...] = carry_ref[...]


def _rank(oh, route, tt=512):
    n = oh.shape[0]
    return pl.pallas_call(
        _rank_kernel,
        grid=(n // tt,),
        in_specs=[pl.BlockSpec((tt, LANES), lambda i: (i, 0)),
                  pl.BlockSpec((tt, LANES), lambda i: (i, 0))],
        out_specs=[pl.BlockSpec((tt, LANES), lambda i: (i, 0)),
                   pl.BlockSpec((1, LANES), lambda i: (0, 0))],
        out_shape=[jax.ShapeDtypeStruct((n, LANES), F32),
                   jax.ShapeDtypeStruct((1, LANES), F32)],
        scratch_shapes=[pltpu.VMEM((1, LANES), F32)],
        compiler_params=_cparams(("arbitrary",)),
        name="moe_rank",
    )(oh, route)


def _row_copy(src_ref, src_row, dst_ref, dst_row, sem):
    return pltpu.make_async_copy(src_ref.at[pl.ds(src_row, 1), :], dst_ref.at[pl.ds(dst_row, 1), :], sem)


def _dispatch_kernel(pos1_ref, pos2_ref, x_ref, init_ref, xs_ref, sem):
    del init_ref
    base = pl.program_id(0) * MOE_TOK

    def issue(t, carry):
        _row_copy(x_ref, t, xs_ref, pos1_ref[base + t], sem).start()
        _row_copy(x_ref, t, xs_ref, pos2_ref[base + t], sem).start()
        return carry

    lax.fori_loop(0, MOE_TOK, issue, 0, unroll=8)
    for _ in range(2):
        pltpu.make_async_copy(x_ref, xs_ref.at[pl.ds(0, MOE_TOK), :], sem).wait()


def _dispatch(pos1, pos2, x1, n_slots):
    n = x1.shape[0]
    init = jnp.zeros((n_slots, D_MODEL), F32)
    return pl.pallas_call(
        _dispatch_kernel,
        grid_spec=pltpu.PrefetchScalarGridSpec(
            num_scalar_prefetch=2,
            grid=(n // MOE_TOK,),
            in_specs=[pl.BlockSpec((MOE_TOK, D_MODEL), lambda i, p1, p2: (i, 0)),
                      pl.BlockSpec(memory_space=pl.ANY)],
            out_specs=pl.BlockSpec(memory_space=pl.ANY),
            scratch_shapes=[pltpu.SemaphoreType.DMA(())]),
        out_shape=jax.ShapeDtypeStruct((n_slots, D_MODEL), F32),
        input_output_aliases={3: 0},
        compiler_params=_cparams(("arbitrary",)),
        name="moe_dispatch",
    )(pos1, pos2, x1, init)


def _ffn_kernel(te_ref, nu_ref, xs_ref, wg_ref, wu_ref, wd_ref, ys_ref):
    del te_ref
    i = pl.program_id(0)

    @pl.when(i < nu_ref[0])
    def _():
        xb = xs_ref[...].astype(BF16)
        gte = _dot(xb, wg_ref[...])
        hid = gte * _sigmoid(gte) * _dot(xb, wu_ref[...])
        ys_ref[...] = _dot(hid.astype(BF16), wd_ref[...])

    @pl.when(i >= nu_ref[0])
    def _():
        ys_ref[...] = jnp.zeros_like(ys_ref)


def _ffn(tile_expert, n_used, xs, w_gate, w_up, w_down):
    n_slots = xs.shape[0]
    wspec = lambda r, c: pl.BlockSpec((None, r, c), lambda i, te, nu: (te[i], 0, 0))
    return pl.pallas_call(
        _ffn_kernel,
        grid_spec=pltpu.PrefetchScalarGridSpec(
            num_scalar_prefetch=2,
            grid=(n_slots // FFN_TILE,),
            in_specs=[pl.BlockSpec((FFN_TILE, D_MODEL), lambda i, te, nu: (i, 0)),
                      wspec(D_MODEL, D_EXPERT), wspec(D_MODEL, D_EXPERT), wspec(D_EXPERT, D_MODEL)],
            out_specs=pl.BlockSpec((FFN_TILE, D_MODEL), lambda i, te, nu: (i, 0))),
        out_shape=jax.ShapeDtypeStruct((n_slots, D_MODEL), F32),
        compiler_params=_cparams(("arbitrary",)),
        name="moe_ffn",
    )(tile_expert, n_used, xs, w_gate, w_up, w_down)


def _combine_kernel(pos1_ref, pos2_ref, x_ref, route_ref, lg_ref, lb_ref, ys_ref, o_ref, buf_ref, sem):
    base = pl.program_id(0) * MOE_TOK

    def issue(t, carry):
        _row_copy(ys_ref, pos1_ref[base + t], buf_ref.at[0], t, sem).start()
        _row_copy(ys_ref, pos2_ref[base + t], buf_ref.at[1], t, sem).start()
        return carry

    lax.fori_loop(0, MOE_TOK, issue, 0, unroll=8)
    for s in range(2):
        pltpu.make_async_copy(ys_ref.at[pl.ds(0, MOE_TOK), :], buf_ref.at[s], sem).wait()
    route = route_ref[...]
    ffn = route[:, 2:3] * buf_ref[0] + route[:, 3:4] * buf_ref[1]
    o_ref[...] = _layer_norm(DN_ALPHA * x_ref[...] + ffn, lg_ref[...], lb_ref[...])


def _combine(pos1, pos2, x1, route, ln_g, ln_b, ys):
    n = x1.shape[0]
    row = lambda w: pl.BlockSpec((MOE_TOK, w), lambda i, p1, p2: (i, 0))
    const = lambda w: pl.BlockSpec((1, w), lambda i, p1, p2: (0, 0))
    return pl.pallas_call(
        _combine_kernel,
        grid_spec=pltpu.PrefetchScalarGridSpec(
            num_scalar_prefetch=2,
            grid=(n // MOE_TOK,),
            in_specs=[row(D_MODEL), row(LANES), const(D_MODEL), const(D_MODEL),
                      pl.BlockSpec(memory_space=pl.ANY)],
            out_specs=row(D_MODEL),
            scratch_shapes=[pltpu.VMEM((2, MOE_TOK, D_MODEL), F32),
                            pltpu.SemaphoreType.DMA(())]),
        out_shape=jax.ShapeDtypeStruct((n, D_MODEL), F32),
        compiler_params=_cparams(("arbitrary",)),
        name="moe_combine",
    )(pos1, pos2, x1, route, ln_g, ln_b, ys)


def _moe(x1, oh, route, w_gate, w_up, w_down, ln_g, ln_b):
    n = x1.shape[0]
    n_slots = TOP_K_INNER * n + N_EXPERTS * FFN_TILE
    rank, cnt = _rank(oh, route)
    cnt = cnt[0, :N_EXPERTS].astype(jnp.int32)
    padded = (cnt + FFN_TILE - 1) // FFN_TILE * FFN_TILE
    ends = jnp.cumsum(padded)
    offs = ends - padded
    experts = jnp.arange(N_EXPERTS, dtype=jnp.int32)
    i1 = route[:, 0].astype(jnp.int32)
    i2 = route[:, 1].astype(jnp.int32)
    off_of = lambda idx: jnp.sum(jnp.where(idx[:, None] == experts[None, :], offs[None, :], 0), axis=1)
    pos1 = off_of(i1) + rank[:, 0].astype(jnp.int32)
    pos2 = off_of(i2) + rank[:, 1].astype(jnp.int32)
    tile_start = jnp.arange(n_slots // FFN_TILE, dtype=jnp.int32) * FFN_TILE
    tile_expert = jnp.minimum(jnp.sum(tile_start[:, None] >= ends[None, :], axis=1), N_EXPERTS - 1)
    n_used = (ends[-1] // FFN_TILE).reshape(1)

    xs = _dispatch(pos1, pos2, x1, n_slots)
    ys = _ffn(tile_expert.astype(jnp.int32), n_used.astype(jnp.int32), xs, w_gate, w_up, w_down)
    return _combine(pos1, pos2, x1, route, ln_g, ln_b, ys)


def _rope_tables(positions):
    inv_freq = ROPE_THETA ** (-jnp.arange(0, ROT_DIM, 2, dtype=F32) / ROT_DIM)
    ang = positions.astype(F32).reshape(-1, 1) * inv_freq
    cos, sin = jnp.cos(ang), jnp.sin(ang)
    n = ang.shape[0]
    half = ROT_DIM // 2
    c64 = jnp.concatenate([cos, cos, jnp.ones((n, A_HEAD_DIM - ROT_DIM), F32)], axis=1)
    s64 = jnp.concatenate([-sin, sin, jnp.zeros((n, A_HEAD_DIM - ROT_DIM), F32)], axis=1)
    del half
    return jnp.tile(c64, (1, 2)), jnp.tile(s64, (1, 2))


def _rep_cols(w):
    r = w.shape[0]
    w4 = w.reshape(r, B_HEADS, 1, B_KEY_DIM)
    return jnp.broadcast_to(w4, (r, B_HEADS, N_SUB, B_KEY_DIM)).reshape(r, GLA_REP)


def _split_w_in(w):
    widths = (A_OUT, A_HEAD_DIM, A_HEAD_DIM, IDX_HEADS * IDX_DIM, IDX_DIM, IDX_HEADS,
              B_HEADS * B_KEY_DIM, B_HEADS * B_KEY_DIM, B_OUT, B_GATE_RANK, B_OUT,
              2 * C_OUT, D_MODEL, D_MODEL, D_MODEL)
    pts = np.cumsum((0,) + widths)
    return [w[:, int(pts[i]):int(pts[i + 1])] for i in range(len(widths))]


def kernel(x, positions, ln_in_g, ln_in_b, w_in, idx_k_g, gla_wa2, gla_ba, gla_norm_g, gm_ln_g, gm_ln_b, gm_ws, gm_bs, w_branch_a, w_branch_b, w_branch_c, w_out, ln1_g, ln1_b, w_rg, b_rg, w_re, b_re, w_gate, w_up, w_down, ln2_g, ln2_b):
    bsz, seq, d = x.shape
    n = bsz * seq
    cos_t, sin_t = _rope_tables(positions)
    h = _entry_ln(x.reshape(n, d), ln_in_g, ln_in_b)
    for l in range(DEPTH):
        (a_q, a_k, a_v, i_q, i_k, i_w, b_q, b_k, b_v, b_glr, b_r, c_uv,
         g_a, g_b, g_c) = _split_w_in(w_in[l])
        zpad = lambda c: jnp.zeros((d, c), F32)
        w_a = jnp.concatenate([a_q, a_k, a_v, i_q, i_k, i_w, zpad(LANES - IDX_DIM - IDX_HEADS)],
                              axis=1).astype(BF16)
        w_b = jnp.concatenate([_rep_cols(b_q * (B_KEY_DIM ** -0.5)), _rep_cols(b_k), b_v, b_r,
                               b_glr, zpad(LANES - B_GATE_RANK)], axis=1).astype(BF16)
        wa_rep = jnp.concatenate([_rep_cols(gla_wa2[l]),
                                  jnp.zeros((LANES - B_GATE_RANK, GLA_REP), F32)], axis=0).astype(BF16)
        ba_rep = _rep_cols(gla_ba[l].reshape(1, -1))
        w_g = jnp.concatenate([g_a, g_b, g_c], axis=1).astype(BF16)
        ikg = jnp.concatenate([idx_k_g[l], jnp.zeros((LANES - IDX_DIM,), F32)]).reshape(1, LANES)
        bs_t = jnp.concatenate([gm_bs[l].T, jnp.zeros((C_CHUNK, LANES - C_GROUPS), F32)], axis=1)
        w_r = jnp.concatenate([w_re[l], w_rg[l], zpad(LANES - N_GROUPS - N_EXPERTS)], axis=1)
        b_r_all = jnp.concatenate([b_re[l], b_rg[l],
                                   jnp.zeros((LANES - N_GROUPS - N_EXPERTS,), F32)]).reshape(1, LANES)

        q, k, v, iq, ki, wi = _proj_a(h, w_a, cos_t, sin_t, ikg)
        y_a = _dsa(q, iq, wi, k, v, ki, bsz, seq)
        y_b = _gla(h, w_b, wa_rep, ba_rep, gla_norm_g[l].reshape(1, -1), bsz, seq)
        y_c = _gmlp(h, c_uv.astype(BF16), gm_ln_g[l].reshape(1, -1), gm_ln_b[l].reshape(1, -1),
                    gm_ws[l], bs_t)
        x1, oh, route = _merge(h, y_a, y_b, y_c, w_g,
                               w_branch_a[l].astype(BF16), w_branch_b[l].astype(BF16),
                               w_branch_c[l].astype(BF16), w_out[l].astype(BF16),
                               ln1_g[l].reshape(1, -1), ln1_b[l].reshape(1, -1), w_r, b_r_all)
        h = _moe(x1, oh, route, w_gate[l].astype(BF16), w_up[l].astype(BF16),
                 w_down[l].astype(BF16), ln2_g[l].reshape(1, -1), ln2_b[l].reshape(1, -1))
    return h.reshape(bsz, seq, d)
```

```python
import functools

import numpy as np
import jax
import jax.numpy as jnp
from jax import lax
from jax.experimental import pallas as pl
from jax.experimental.pallas import tpu as pltpu

F32 = jnp.float32
BF16 = jnp.bfloat16

D_MODEL = 1024
DEPTH = 2
A_HEADS = 8
A_HEAD_DIM = 64
A_OUT = 512
IDX_HEADS = 4
IDX_DIM = 64
TOPK_MAX = 256
Q_BLOCK = 128
ROPE_THETA = 500000.0
ROT_DIM = 16
B_HEADS = 4
B_KEY_DIM = 64
B_VAL_DIM = 128
B_OUT = 512
B_GATE_RANK = 16
B_GATE_TAU = 16.0
B_CHUNK = 64
B_SUB = 16
N_SUB = B_CHUNK // B_SUB
C_GROUPS = 4
C_GROUP_DIM = 128
C_OUT = 512
C_CHUNK = 128
N_GROUPS = 4
EXPERTS_PER_GROUP = 8
N_EXPERTS = 32
D_EXPERT = 256
TOP_K_INNER = 2
DN_ALPHA = (2 * DEPTH) ** 0.25
LN_EPS = 1e-5
RMS_EPS = 1e-6

LANES = 128
NEG_BIG = -1e30
INT_MIN = -2 ** 31
VMEM_LIMIT = 56 * 1024 * 1024

HIGHEST = lax.Precision.HIGHEST


def _cparams(sem):
    return pltpu.CompilerParams(dimension_semantics=sem, vmem_limit_bytes=VMEM_LIMIT)


def _layer_norm(x, g, b):
    mu = jnp.mean(x, axis=-1, keepdims=True)
    xc = x - mu
    var = jnp.mean(xc * xc, axis=-1, keepdims=True)
    return xc * lax.rsqrt(var + LN_EPS) * g + b


def _dot(a, b):
    return jnp.dot(a, b, preferred_element_type=F32)


def _dot_nt(a, b):
    return lax.dot_general(a, b, (((1,), (1,)), ((), ())), preferred_element_type=F32)


def _ln_kernel(x_ref, g_ref, b_ref, o_ref):
    o_ref[...] = _layer_norm(x_ref[...], g_ref[...], b_ref[...])


def _entry_ln(x, g, b, tm=1024):
    n = x.shape[0]
    return pl.pallas_call(
        _ln_kernel,
        grid=(n // tm,),
        in_specs=[pl.BlockSpec((tm, D_MODEL), lambda i: (i, 0)),
                  pl.BlockSpec((1, D_MODEL), lambda i: (0, 0)),
                  pl.BlockSpec((1, D_MODEL), lambda i: (0, 0))],
        out_specs=pl.BlockSpec((tm, D_MODEL), lambda i: (i, 0)),
        out_shape=jax.ShapeDtypeStruct((n, D_MODEL), F32),
        compiler_params=_cparams(("parallel",)),
        name="entry_ln",
    )(x, g.reshape(1, -1), b.reshape(1, -1))


def _rope_slab(xs, c, s):
    lane = lax.broadcasted_iota(jnp.int32, xs.shape, 1) % A_HEAD_DIM
    fwd = pltpu.roll(xs, LANES - ROT_DIM // 2, axis=1)
    bwd = pltpu.roll(xs, ROT_DIM // 2, axis=1)
    partner = jnp.where(lane < ROT_DIM // 2, fwd, bwd)
    return xs * c + partner * s


def _proj_a_kernel(h_ref, w_ref, c_ref, s_ref, g_ref,
                   q_ref, k_ref, v_ref, iq_ref, ki_ref, wi_ref):
    p = _dot(h_ref[...].astype(BF16), w_ref[...])
    c = c_ref[...]
    s = s_ref[...]
    lane = lax.broadcasted_iota(jnp.int32, c.shape, 1)
    lo = lane < A_HEAD_DIM
    n_blk = p.shape[0] // Q_BLOCK

    def put_heads(dst_ref, slab, first_head):
        for qb in range(n_blk):
            t = slab[qb * Q_BLOCK:(qb + 1) * Q_BLOCK, :].T
            for half in range(2):
                hd = first_head + half
                dst_ref[qb, :, hd * Q_BLOCK:(hd + 1) * Q_BLOCK] = (
                    t[half * A_HEAD_DIM:(half + 1) * A_HEAD_DIM, :].astype(BF16))

    for i in range(4):
        sl = slice(i * LANES, (i + 1) * LANES)
        put_heads(q_ref, _rope_slab(p[:, sl], c, s) * (A_HEAD_DIM ** -0.5), 2 * i)
    kv = p[:, 512:640]
    kv = _rope_slab(kv, jnp.where(lo, c, 1.0), jnp.where(lo, s, 0.0))
    k_ref[...] = kv[:, :A_HEAD_DIM].astype(BF16)
    v_ones = jnp.where(lo, pltpu.roll(kv, A_HEAD_DIM, axis=1), jnp.where(lane == A_HEAD_DIM, 1.0, 0.0))
    for kb in range(p.shape[0] // V_BLOCK):
        v_ref[kb] = v_ones[kb * V_BLOCK:(kb + 1) * V_BLOCK, :].T.astype(BF16)
    for i in range(2):
        sl = slice(640 + i * LANES, 640 + (i + 1) * LANES)
        put_heads(iq_ref, _rope_slab(p[:, sl], c, s), 2 * i)
    last = p[:, 896:1024]
    mu = jnp.sum(jnp.where(lo, last, 0.0), axis=-1, keepdims=True) * (1.0 / IDX_DIM)
    xc = jnp.where(lo, last - mu, 0.0)
    var = jnp.sum(xc * xc, axis=-1, keepdims=True) * (1.0 / IDX_DIM)
    kin = xc * lax.rsqrt(var + LN_EPS) * g_ref[...]
    kin = _rope_slab(kin, jnp.where(lo, c, 1.0), jnp.where(lo, s, 0.0))
    ki_ref[...] = kin[:, :IDX_DIM].astype(BF16)
    wi = pltpu.roll(last, LANES - IDX_DIM, axis=1) * (IDX_HEADS ** -0.5 * IDX_DIM ** -0.5)
    for qb in range(n_blk):
        wi_ref[qb] = wi[qb * Q_BLOCK:(qb + 1) * Q_BLOCK, :].T[0:8, :]


def _proj_a(h, w_a, cos_t, sin_t, ikg, tm=512):
    n = h.shape[0]
    row = lambda w: pl.BlockSpec((tm, w), lambda i: (i, 0))
    stacked = lambda heads: pl.BlockSpec((tm // Q_BLOCK, A_HEAD_DIM, heads * Q_BLOCK), lambda i: (i, 0, 0))
    return pl.pallas_call(
        _proj_a_kernel,
        grid=(n // tm,),
        in_specs=[row(D_MODEL),
                  pl.BlockSpec((D_MODEL, 1024), lambda i: (0, 0)),
                  row(LANES), row(LANES),
                  pl.BlockSpec((1, LANES), lambda i: (0, 0))],
        out_specs=[stacked(A_HEADS), row(A_HEAD_DIM),
                   pl.BlockSpec((tm // V_BLOCK, LANES, V_BLOCK), lambda i: (i, 0, 0)),
                   stacked(IDX_HEADS), row(IDX_DIM),
                   pl.BlockSpec((tm // Q_BLOCK, 8, Q_BLOCK), lambda i: (i, 0, 0))],
        out_shape=[jax.ShapeDtypeStruct((n // Q_BLOCK, A_HEAD_DIM, A_HEADS * Q_BLOCK), BF16),
                   jax.ShapeDtypeStruct((n, A_HEAD_DIM), BF16),
                   jax.ShapeDtypeStruct((n // V_BLOCK, LANES, V_BLOCK), BF16),
                   jax.ShapeDtypeStruct((n // Q_BLOCK, IDX_DIM, IDX_HEADS * Q_BLOCK), BF16),
                   jax.ShapeDtypeStruct((n, IDX_DIM), BF16),
                   jax.ShapeDtypeStruct((n // Q_BLOCK, 8, Q_BLOCK), F32)],
        compiler_params=_cparams(("parallel",)),
        name="proj_a",
    )(h, w_a, cos_t, sin_t, ikg)


KEY_CHUNK = 512
HALF16 = 2 ** 15
V_BLOCK = KEY_CHUNK // 2


def _dsa_kernel(q_ref, iq_ref, wi_ref, k_ref, v_ref, ki_ref, o_ref,
                key_ref, hi_ref, lo_ref, bias_ref, lg0_ref, lg1_ref, p0_ref, p1_ref, *, seq):
    j = pl.program_id(1)
    n_sel = min(TOPK_MAX, seq // 4)
    n_kc = (j * Q_BLOCK + Q_BLOCK + KEY_CHUNK - 1) // KEY_CHUNK
    qpos = j * Q_BLOCK + lax.broadcasted_iota(jnp.int32, (1, Q_BLOCK), 1)
    kk = jnp.minimum(n_sel, qpos + 1).astype(F32)
    row_pos = lax.broadcasted_iota(jnp.int32, (KEY_CHUNK, 1), 0)

    def key_rows(c):
        return pl.ds(pl.multiple_of(c * KEY_CHUNK, KEY_CHUNK), KEY_CHUNK)

    iq = iq_ref[...]
    wi = wi_ref[...]

    def score_chunk(c, carry):
        d = jnp.maximum(_dot(ki_ref[key_rows(c), :], iq), 0.0)
        score = wi[0:1, :] * d[:, 0:Q_BLOCK]
        for h in range(1, IDX_HEADS):
            score = score + wi[h:h + 1, :] * d[:, h * Q_BLOCK:(h + 1) * Q_BLOCK]
        bits = pltpu.bitcast(score + 0.0, jnp.int32)
        key = bits ^ ((bits >> 31) & 0x7FFFFFFF)
        key = jnp.where(c * KEY_CHUNK + row_pos <= qpos, key, INT_MIN)
        key_ref[c] = key
        hi_ref[c] = (key >> 16).astype(jnp.int16)
        lo_ref[c] = ((key & 0xFFFF) - HALF16).astype(jnp.int16)
        return carry

    lax.fori_loop(0, n_kc, score_chunk, 0)

    def count(pred):
        def body(c, acc):
            m = jnp.where(pred(key_ref[c]), 1.0, 0.0)
            return acc + jnp.sum(m.reshape(KEY_CHUNK // 64, 64, Q_BLOCK), axis=0)
        acc = lax.fori_loop(0, n_kc, body, jnp.zeros((64, Q_BLOCK), F32))
        return jnp.sum(acc, axis=0, keepdims=True)

    one16, zero16 = jnp.int16(1), jnp.int16(0)

    def count16(ref, pred):
        def body(c, acc):
            m = jnp.where(pred(ref[c]), one16, zero16).reshape(KEY_CHUNK // 64, 64, Q_BLOCK)
            parts = [m[t] for t in range(KEY_CHUNK // 64)]
            while len(parts) > 1:
                parts = [a + b for a, b in zip(parts[::2], parts[1::2])]
            return acc + parts[0]
        acc = lax.fori_loop(0, n_kc, body, jnp.zeros((64, Q_BLOCK), jnp.int16))
        return jnp.sum(acc.astype(F32), axis=0, keepdims=True)

    def search16(ref, target):
        def step(i, t_u):
            cand_u = t_u | jnp.left_shift(jnp.int32(1), 15 - i)
            cand = (cand_u - HALF16).astype(jnp.int16)
            return jnp.where(count16(ref, lambda kc: kc >= cand) >= target, cand_u, t_u)
        return lax.fori_loop(0, 16, step, jnp.zeros((1, Q_BLOCK), jnp.int32))

    t_hi = search16(hi_ref, kk) - HALF16
    t_hi16 = t_hi.astype(jnp.int16)
    need_lo = kk - count16(hi_ref, lambda kc: kc > t_hi16)

    def fill_lo(c, carry):
        lo_ref[c] = jnp.where(hi_ref[c] == t_hi16, lo_ref[c], jnp.int16(-HALF16))
        return carry

    lax.fori_loop(0, n_kc, fill_lo, 0)
    thr = (t_hi << 16) | search16(lo_ref, need_lo)
    c_gt = count(lambda kc: kc > thr)
    c_ge = count(lambda kc: kc >= thr)
    need = kk - c_gt
    has_ties = jnp.max(c_ge - kk) > 0.0

    @pl.when(jnp.logical_not(has_ties))
    def _():
        def fill(c, carry):
            bias_ref[c] = jnp.where(key_ref[c] >= thr, 0.0, NEG_BIG)
            return carry
        lax.fori_loop(0, n_kc, fill, 0)

    @pl.when(has_ties)
    def _():
        r_i = lax.broadcasted_iota(jnp.int32, (KEY_CHUNK, KEY_CHUNK), 0)
        c_i = lax.broadcasted_iota(jnp.int32, (KEY_CHUNK, KEY_CHUNK), 1)
        lower = jnp.where(c_i < r_i, 1.0, 0.0).astype(BF16)

        def fill(c, carry):
            kc = key_ref[c]
            eq = kc == thr
            eq_f = jnp.where(eq, 1.0, 0.0)
            pre = _dot(lower, eq_f.astype(BF16)) + carry
            sel = (kc > thr) | (eq & (pre < need))
            bias_ref[c] = jnp.where(sel, 0.0, NEG_BIG)
            return carry + jnp.sum(eq_f, axis=0, keepdims=True)

        lax.fori_loop(0, n_kc, fill, jnp.zeros((1, Q_BLOCK), F32))

    q = q_ref[...]
    cols = A_HEADS * Q_BLOCK

    half = V_BLOCK

    def half_rows(c, hf):
        return pl.ds(pl.multiple_of(c * KEY_CHUNK + hf * half, half), half)

    def logits(c, hf):
        b = bias_ref[c, hf * half:(hf + 1) * half, :]
        return _dot(k_ref[half_rows(c, hf), :], q) + jnp.concatenate([b] * A_HEADS, axis=1)

    def weighted_values(c, hf, p_in):
        return _dot(v_ref[c * (KEY_CHUNK // V_BLOCK) + hf], p_in[...])

    def step(nxt, cur_lg, nxt_lg, prv, prv_p, cur_p, carry):
        m, alpha, acc = carry
        nxt_lg[...] = logits(*nxt)
        lg = cur_lg[...]
        m_new = jnp.maximum(m, jnp.max(lg, axis=0, keepdims=True))
        acc = alpha * acc + weighted_values(*prv, prv_p)
        cur_p[...] = jnp.exp(lg - m_new).astype(BF16)
        return m_new, jnp.exp(m - m_new), acc

    def attn_chunk(c, carry):
        carry = step((c, 1), lg0_ref, lg1_ref, (jnp.maximum(c - 1, 0), 1), p1_ref, p0_ref, carry)
        return step((jnp.minimum(c + 1, n_kc - 1), 0), lg1_ref, lg0_ref, (c, 0), p0_ref, p1_ref, carry)

    lg0_ref[...] = logits(0, 0)
    p1_ref[...] = jnp.zeros(p1_ref.shape, BF16)
    init = (jnp.full((1, cols), -3e38, F32), jnp.ones((1, cols), F32), jnp.zeros((LANES, cols), F32))
    _, alpha, acc = lax.fori_loop(0, n_kc, attn_chunk, init)
    acc = alpha * acc + weighted_values(n_kc - 1, 1, p1_ref)
    o = acc[:A_HEAD_DIM] / acc[A_HEAD_DIM:A_HEAD_DIM + 1]
    for h in range(A_HEADS):
        o_ref[:, h * A_HEAD_DIM:(h + 1) * A_HEAD_DIM] = (
            o[:, h * Q_BLOCK:(h + 1) * Q_BLOCK].T.astype(BF16))


def _dsa(q, iq, wi, k, v, ki, bsz, seq):
    nq = seq // Q_BLOCK
    n_kc = seq // KEY_CHUNK
    stacked = lambda heads: pl.BlockSpec((None, A_HEAD_DIM, heads * Q_BLOCK),
                                         lambda b, j: (b * nq + j, 0, 0))
    kspec = lambda w: pl.BlockSpec((seq, w), lambda b, j: (b, 0))
    return pl.pallas_call(
        functools.partial(_dsa_kernel, seq=seq),
        grid=(bsz, nq),
        in_specs=[stacked(A_HEADS), stacked(IDX_HEADS),
                  pl.BlockSpec((None, 8, Q_BLOCK), lambda b, j: (b * nq + j, 0, 0)),
                  kspec(A_HEAD_DIM),
                  pl.BlockSpec((seq // V_BLOCK, LANES, V_BLOCK), lambda b, j: (b, 0, 0)),
                  kspec(IDX_DIM)],
        out_specs=pl.BlockSpec((Q_BLOCK, A_OUT), lambda b, j: (b * nq + j, 0)),
        out_shape=jax.ShapeDtypeStruct((bsz * seq, A_OUT), BF16),
        scratch_shapes=[pltpu.VMEM((n_kc, KEY_CHUNK, Q_BLOCK), jnp.int32),
                        pltpu.VMEM((n_kc, KEY_CHUNK, Q_BLOCK), jnp.int16),
                        pltpu.VMEM((n_kc, KEY_CHUNK, Q_BLOCK), jnp.int16),
                        pltpu.VMEM((n_kc, KEY_CHUNK, Q_BLOCK), F32),
                        pltpu.VMEM((KEY_CHUNK // 2, A_HEADS * Q_BLOCK), F32),
                        pltpu.VMEM((KEY_CHUNK // 2, A_HEADS * Q_BLOCK), F32),
                        pltpu.VMEM((KEY_CHUNK // 2, A_HEADS * Q_BLOCK), BF16),
                        pltpu.VMEM((KEY_CHUNK // 2, A_HEADS * Q_BLOCK), BF16)],
        compiler_params=_cparams(("parallel", "parallel")),
        name="dsa",
    )(q, iq, wi, k, v, ki)


GLA_REP = B_HEADS * N_SUB * B_KEY_DIM
GLA_COLS = 2 * GLA_REP + 2 * B_OUT + LANES


def _gla_kernel(h_ref, w_ref, wa_ref, ba_ref, ng_ref, o_ref, p_ref, g_ref, acc_ref, st_ref, *, ts):
    @pl.when(pl.program_id(1) == 0)
    def _():
        st_ref[...] = jnp.zeros_like(st_ref)

    p_ref[...] = _dot(h_ref[...].astype(BF16), w_ref[...])
    x = _dot(p_ref[:, 2 * GLA_REP + 2 * B_OUT:].astype(BF16), wa_ref[...]) + ba_ref[...]
    g_ref[...] = (jnp.minimum(x, 0.0) - jnp.log1p(jnp.exp(-jnp.abs(x)))) * (1.0 / B_GATE_TAU)

    t_i = lax.broadcasted_iota(jnp.int32, (B_CHUNK, B_CHUNK), 0)
    s_i = lax.broadcasted_iota(jnp.int32, (B_CHUNK, B_CHUNK), 1)
    tri = t_i >= s_i
    tri_f = jnp.where(tri, 1.0, 0.0)
    lane_sub = (lax.broadcasted_iota(jnp.int32, (1, GLA_REP), 1) // B_KEY_DIM) % N_SUB
    row_sub = lax.broadcasted_iota(jnp.int32, (B_CHUNK, 1), 0) // B_SUB
    q_mask = lane_sub <= row_sub
    k_mask = lane_sub == row_sub

    def chunk(c, carry):
        rows = pl.ds(pl.multiple_of(c * B_CHUNK, B_CHUNK), B_CHUNK)
        b = jnp.dot(tri_f, g_ref[rows, :], precision=HIGHEST, preferred_element_type=F32)
        ref = jnp.zeros((1, GLA_REP), F32)
        for jj in range(1, N_SUB):
            ref = ref + jnp.where(lane_sub == jj, b[jj * B_SUB - 1:jj * B_SUB, :], 0.0)
        d = b - ref
        e_q = jnp.where(q_mask, jnp.exp(jnp.minimum(d, 0.0)), 0.0)
        e_k = jnp.where(k_mask, jnp.exp(jnp.where(k_mask, -d, 0.0)), 0.0)
        b_last = b[B_CHUNK - 1:B_CHUNK, :]
        q_c = (p_ref[rows, 0:GLA_REP] * e_q).astype(BF16)
        k_raw = p_ref[rows, GLA_REP:2 * GLA_REP]
        k_c = (k_raw * e_k).astype(BF16)
        k_st = (k_raw * jnp.exp(b_last - b)).astype(BF16)
        e_last = jnp.exp(b_last)
        for hh in range(B_HEADS):
            base = hh * N_SUB * B_KEY_DIM
            q_h = q_c[:, base:base + N_SUB * B_KEY_DIM]
            k_h = k_c[:, base:base + N_SUB * B_KEY_DIM]
            att = jnp.where(tri, _dot_nt(q_h, k_h), 0.0)
            v_h = p_ref[rows, 2 * GLA_REP + hh * B_VAL_DIM:2 * GLA_REP + (hh + 1) * B_VAL_DIM].astype(BF16)
            st = st_ref[hh]
            o = _dot(att.astype(BF16), v_h) + _dot_nt(q_h[:, :B_KEY_DIM], st.astype(BF16))
            acc_ref[rows, hh * B_VAL_DIM:(hh + 1) * B_VAL_DIM] = o
            upd = _dot(v_h.T, k_st[:, base:base + B_KEY_DIM])
            st_ref[hh] = st * e_last[:, base:base + B_KEY_DIM] + upd
        return carry

    lax.fori_loop(0, ts // B_CHUNK, chunk, 0)

    r = p_ref[:, 2 * GLA_REP + B_OUT:2 * GLA_REP + 2 * B_OUT]
    gate = r * (1.0 / (1.0 + jnp.exp(-r)))
    ng = ng_ref[...]
    for hh in range(B_HEADS):
        sl = slice(hh * B_VAL_DIM, (hh + 1) * B_VAL_DIM)
        o = acc_ref[:, sl]
        y = o * lax.rsqrt(jnp.mean(o * o, axis=-1, keepdims=True) + RMS_EPS) * ng
        o_ref[:, sl] = (gate[:, sl] * y).astype(BF16)


def _gla(h, w_b, wa_rep, ba_rep, norm_g, bsz, seq, ts=512):
    ns = seq // ts
    const = lambda shape: pl.BlockSpec(shape, lambda b, i: (0, 0))
    return pl.pallas_call(
        functools.partial(_gla_kernel, ts=ts),
        grid=(bsz, ns),
        in_specs=[pl.BlockSpec((ts, D_MODEL), lambda b, i: (b * ns + i, 0)),
                  const((D_MODEL, GLA_COLS)), const((LANES, GLA_REP)),
                  const((1, GLA_REP)), const((1, B_VAL_DIM))],
        out_specs=pl.BlockSpec((ts, B_OUT), lambda b, i: (b * ns + i, 0)),
        out_shape=jax.ShapeDtypeStruct((bsz * seq, B_OUT), BF16),
        scratch_shapes=[pltpu.VMEM((ts, GLA_COLS), F32),
                        pltpu.VMEM((ts, GLA_REP), F32),
                        pltpu.VMEM((ts, B_OUT), F32),
                        pltpu.VMEM((B_HEADS, B_VAL_DIM, B_KEY_DIM), F32)],
        compiler_params=_cparams(("parallel", "arbitrary")),
        name="gla",
    )(h, w_b, wa_rep, ba_rep, norm_g)


def _gmlp_kernel(h_ref, w_ref, lg_ref, lb_ref, ws_ref, bs_ref, o_ref, *, tm):
    z = _dot(h_ref[...].astype(BF16), w_ref[...])
    z = z * (0.5 * (1.0 + jnp.tanh(np.sqrt(2.0 / np.pi) * (z + 0.044715 * (z * z * z)))))
    u = z[:, :C_OUT]
    v = _layer_norm(z[:, C_OUT:], lg_ref[...], lb_ref[...]).astype(BF16)
    t_i = lax.broadcasted_iota(jnp.int32, (C_CHUNK, C_CHUNK), 0)
    s_i = lax.broadcasted_iota(jnp.int32, (C_CHUNK, C_CHUNK), 1)
    bs = bs_ref[...]
    for g in range(C_GROUPS):
        w = jnp.where(t_i >= s_i, ws_ref[g], 0.0).astype(BF16)
        sl = slice(g * C_GROUP_DIM, (g + 1) * C_GROUP_DIM)
        for c in range(tm // C_CHUNK):
            rows = slice(c * C_CHUNK, (c + 1) * C_CHUNK)
            mixed = _dot(w, v[rows, sl]) + bs[:, g:g + 1]
            o_ref[rows, sl] = (u[rows, sl] * mixed).astype(BF16)


def _gmlp(h, w_c, ln_g, ln_b, ws, bs_t, tm=512):
    n = h.shape[0]
    return pl.pallas_call(
        functools.partial(_gmlp_kernel, tm=tm),
        grid=(n // tm,),
        in_specs=[pl.BlockSpec((tm, D_MODEL), lambda i: (i, 0)),
                  pl.BlockSpec((D_MODEL, 2 * C_OUT), lambda i: (0, 0)),
                  pl.BlockSpec((1, C_OUT), lambda i: (0, 0)),
                  pl.BlockSpec((1, C_OUT), lambda i: (0, 0)),
                  pl.BlockSpec((C_GROUPS, C_CHUNK, C_CHUNK), lambda i: (0, 0, 0)),
                  pl.BlockSpec((C_CHUNK, LANES), lambda i: (0, 0))],
        out_specs=pl.BlockSpec((tm, C_OUT), lambda i: (i, 0)),
        out_shape=jax.ShapeDtypeStruct((n, C_OUT), BF16),
        compiler_params=_cparams(("parallel",)),
        name="gmlp",
    )(h, w_c, ln_g, ln_b, ws, bs_t)


def _sigmoid(x):
    return 1.0 / (1.0 + jnp.exp(-x))


def _merge_kernel(h_ref, ya_ref, yb_ref, yc_ref, wg_ref, wa_ref, wb_ref, wc_ref, wo_ref,
                  lg_ref, lb_ref, wr_ref, br_ref, x_ref, oh_ref, route_ref):
    h = h_ref[...]
    hb = h.astype(BF16)
    acc = None
    for i, (y_ref, wbr_ref) in enumerate(((ya_ref, wa_ref), (yb_ref, wb_ref), (yc_ref, wc_ref))):
        gate = _sigmoid(_dot(hb, wg_ref[:, i * D_MODEL:(i + 1) * D_MODEL]))
        term = gate * _dot(y_ref[...], wbr_ref[...])
        acc = term if acc is None else acc + term
    mix = _dot(acc.astype(BF16), wo_ref[...])
    x1 = _layer_norm(DN_ALPHA * h + mix, lg_ref[...], lb_ref[...])
    x_ref[...] = x1

    logit = jnp.dot(x1, wr_ref[...], precision=HIGHEST, preferred_element_type=F32) + br_ref[...]
    lane = lax.broadcasted_iota(jnp.int32, logit.shape, 1)
    big = jnp.int32(LANES)
    is_g = (lane >= N_EXPERTS) & (lane < N_EXPERTS + N_GROUPS)
    gl = jnp.where(is_g, logit, -jnp.inf)
    g_max = jnp.max(gl, axis=-1, keepdims=True)
    g_idx = jnp.min(jnp.where(is_g & (gl == g_max), lane, big), axis=-1, keepdims=True) - N_EXPERTS
    g_p = 1.0 / jnp.sum(jnp.where(is_g, jnp.exp(gl - g_max), 0.0), axis=-1, keepdims=True)
    lo = g_idx * EXPERTS_PER_GROUP
    in_g = (lane >= lo) & (lane < lo + EXPERTS_PER_GROUP)
    el = jnp.where(in_g, logit, -jnp.inf)
    e1 = jnp.max(el, axis=-1, keepdims=True)
    i1 = jnp.min(jnp.where(in_g & (el == e1), lane, big), axis=-1, keepdims=True)
    el2 = jnp.where(lane == i1, -jnp.inf, el)
    e2 = jnp.max(el2, axis=-1, keepdims=True)
    i2 = jnp.min(jnp.where(in_g & (lane != i1) & (el2 == e2), lane, big), axis=-1, keepdims=True)
    z2 = jnp.exp(e2 - e1)
    den = 1.0 + z2
    oh_ref[...] = jnp.where((lane == i1) | (lane == i2), 1.0, 0.0).astype(BF16)
    route_ref[...] = jnp.where(lane == 0, i1.astype(F32),
                               jnp.where(lane == 1, i2.astype(F32),
                                         jnp.where(lane == 2, (1.0 / den) * g_p,
                                                   jnp.where(lane == 3, (z2 / den) * g_p, 0.0))))


def _merge(h, ya, yb, yc, wg, wa, wb, wc, wo, ln_g, ln_b, wr, br, tm=512):
    n = h.shape[0]
    row = lambda w: pl.BlockSpec((tm, w), lambda i: (i, 0))
    const = lambda r, c: pl.BlockSpec((r, c), lambda i: (0, 0))
    return pl.pallas_call(
        _merge_kernel,
        grid=(n // tm,),
        in_specs=[row(D_MODEL), row(A_OUT), row(B_OUT), row(C_OUT),
                  const(D_MODEL, 3 * D_MODEL), const(A_OUT, D_MODEL), const(B_OUT, D_MODEL),
                  const(C_OUT, D_MODEL), const(D_MODEL, D_MODEL),
                  const(1, D_MODEL), const(1, D_MODEL), const(D_MODEL, LANES), const(1, LANES)],
        out_specs=[row(D_MODEL), row(LANES), row(LANES)],
        out_shape=[jax.ShapeDtypeStruct((n, D_MODEL), F32),
                   jax.ShapeDtypeStruct((n, LANES), BF16),
                   jax.ShapeDtypeStruct((n, LANES), F32)],
        compiler_params=_cparams(("parallel",)),
        name="merge",
    )(h, ya, yb, yc, wg, wa, wb, wc, wo, ln_g, ln_b, wr, br)


FFN_TILE = 256
MOE_TOK = 256


def _rank_kernel(oh_ref, route_ref, r_ref, cnt_ref, carry_ref):
    @pl.when(pl.program_id(0) == 0)
    def _():
        carry_ref[...] = jnp.zeros_like(carry_ref)

    oh = oh_ref[...]
    tt = oh.shape[0]
    r_i = lax.broadcasted_iota(jnp.int32, (tt, tt), 0)
    c_i = lax.broadcasted_iota(jnp.int32, (tt, tt), 1)
    lower = jnp.where(c_i < r_i, 1.0, 0.0).astype(BF16)
    rank = _dot(lower, oh) + carry_ref[...]
    route = route_ref[...]
    lane = lax.broadcasted_iota(jnp.int32, rank.shape, 1)
    i1 = route[:, 0:1].astype(jnp.int32)
    i2 = route[:, 1:2].astype(jnp.int32)
    r1 = jnp.sum(jnp.where(lane == i1, rank, 0.0), axis=-1, keepdims=True)
    r2 = jnp.sum(jnp.where(lane == i2, rank, 0.0), axis=-1, keepdims=True)
    r_ref[...] = jnp.where(lane == 0, r1, jnp.where(lane == 1, r2, 0.0))
    carry_ref[...] += jnp.sum(oh.astype(F32), axis=0, keepdims=True)
    cnt_ref[...] = carry_ref[...]


def _rank(oh, route, tt=512):
    n = oh.shape[0]
    return pl.pallas_call(
        _rank_kernel,
        grid=(n // tt,),
        in_specs=[pl.BlockSpec((tt, LANES), lambda i: (i, 0)),
                  pl.BlockSpec((tt, LANES), lambda i: (i, 0))],
        out_specs=[pl.BlockSpec((tt, LANES), lambda i: (i, 0)),
                   pl.BlockSpec((1, LANES), lambda i: (0, 0))],
        out_shape=[jax.ShapeDtypeStruct((n, LANES), F32),
                   jax.ShapeDtypeStruct((1, LANES), F32)],
        scratch_shapes=[pltpu.VMEM((1, LANES), F32)],
        compiler_params=_cparams(("arbitrary",)),
        name="moe_rank",
    )(oh, route)


def _row_copy(src_ref, src_row, dst_ref, dst_row, sem):
    return pltpu.make_async_copy(src_ref.at[pl.ds(src_row, 1), :], dst_ref.at[pl.ds(dst_row, 1), :], sem)


def _dispatch_kernel(pos1_ref, pos2_ref, x_ref, init_ref, xs_ref, sem):
    del init_ref
    base = pl.program_id(0) * MOE_TOK

    def issue(t, carry):
        _row_copy(x_ref, t, xs_ref, pos1_ref[base + t], sem).start()
        _row_copy(x_ref, t, xs_ref, pos2_ref[base + t], sem).start()
        return carry

    lax.fori_loop(0, MOE_TOK, issue, 0, unroll=8)
    for _ in range(2):
        pltpu.make_async_copy(x_ref, xs_ref.at[pl.ds(0, MOE_TOK), :], sem).wait()


def _dispatch(pos1, pos2, x1, n_slots):
    n = x1.shape[0]
    init = jnp.zeros((n_slots, D_MODEL), F32)
    return pl.pallas_call(
        _dispatch_kernel,
        grid_spec=pltpu.PrefetchScalarGridSpec(
            num_scalar_prefetch=2,
            grid=(n // MOE_TOK,),
            in_specs=[pl.BlockSpec((MOE_TOK, D_MODEL), lambda i, p1, p2: (i, 0)),
                      pl.BlockSpec(memory_space=pl.ANY)],
            out_specs=pl.BlockSpec(memory_space=pl.ANY),
            scratch_shapes=[pltpu.SemaphoreType.DMA(())]),
        out_shape=jax.ShapeDtypeStruct((n_slots, D_MODEL), F32),
        input_output_aliases={3: 0},
        compiler_params=_cparams(("arbitrary",)),
        name="moe_dispatch",
    )(pos1, pos2, x1, init)


def _ffn_kernel(te_ref, nu_ref, xs_ref, wg_ref, wu_ref, wd_ref, ys_ref):
    del te_ref
    i = pl.program_id(0)

    @pl.when(i < nu_ref[0])
    def _():
        xb = xs_ref[...].astype(BF16)
        gte = _dot(xb, wg_ref[...])
        hid = gte * _sigmoid(gte) * _dot(xb, wu_ref[...])
        ys_ref[...] = _dot(hid.astype(BF16), wd_ref[...])

    @pl.when(i >= nu_ref[0])
    def _():
        ys_ref[...] = jnp.zeros_like(ys_ref)


def _ffn(tile_expert, n_used, xs, w_gate, w_up, w_down):
    n_slots = xs.shape[0]
    wspec = lambda r, c: pl.BlockSpec((None, r, c), lambda i, te, nu: (te[i], 0, 0))
    return pl.pallas_call(
        _ffn_kernel,
        grid_spec=pltpu.PrefetchScalarGridSpec(
            num_scalar_prefetch=2,
            grid=(n_slots // FFN_TILE,),
            in_specs=[pl.BlockSpec((FFN_TILE, D_MODEL), lambda i, te, nu: (i, 0)),
                      wspec(D_MODEL, D_EXPERT), wspec(D_MODEL, D_EXPERT), wspec(D_EXPERT, D_MODEL)],
            out_specs=pl.BlockSpec((FFN_TILE, D_MODEL), lambda i, te, nu: (i, 0))),
        out_shape=jax.ShapeDtypeStruct((n_slots, D_MODEL), F32),
        compiler_params=_cparams(("arbitrary",)),
        name="moe_ffn",
    )(tile_expert, n_used, xs, w_gate, w_up, w_down)


def _combine_kernel(pos1_ref, pos2_ref, x_ref, route_ref, lg_ref, lb_ref, ys_ref, o_ref, buf_ref, sem):
    base = pl.program_id(0) * MOE_TOK

    def issue(t, carry):
        _row_copy(ys_ref, pos1_ref[base + t], buf_ref.at[0], t, sem).start()
        _row_copy(ys_ref, pos2_ref[base + t], buf_ref.at[1], t, sem).start()
        return carry

    lax.fori_loop(0, MOE_TOK, issue, 0, unroll=8)
    for s in range(2):
        pltpu.make_async_copy(ys_ref.at[pl.ds(0, MOE_TOK), :], buf_ref.at[s], sem).wait()
    route = route_ref[...]
    ffn = route[:, 2:3] * buf_ref[0] + route[:, 3:4] * buf_ref[1]
    o_ref[...] = _layer_norm(DN_ALPHA * x_ref[...] + ffn, lg_ref[...], lb_ref[...])


def _combine(pos1, pos2, x1, route, ln_g, ln_b, ys):
    n = x1.shape[0]
    row = lambda w: pl.BlockSpec((MOE_TOK, w), lambda i, p1, p2: (i, 0))
    const = lambda w: pl.BlockSpec((1, w), lambda i, p1, p2: (0, 0))
    return pl.pallas_call(
        _combine_kernel,
        grid_spec=pltpu.PrefetchScalarGridSpec(
            num_scalar_prefetch=2,
            grid=(n // MOE_TOK,),
            in_specs=[row(D_MODEL), row(LANES), const(D_MODEL), const(D_MODEL),
                      pl.BlockSpec(memory_space=pl.ANY)],
            out_specs=row(D_MODEL),
            scratch_shapes=[pltpu.VMEM((2, MOE_TOK, D_MODEL), F32),
                            pltpu.SemaphoreType.DMA(())]),
        out_shape=jax.ShapeDtypeStruct((n, D_MODEL), F32),
        compiler_params=_cparams(("arbitrary",)),
        name="moe_combine",
    )(pos1, pos2, x1, route, ln_g, ln_b, ys)


def _moe(x1, oh, route, w_gate, w_up, w_down, ln_g, ln_b):
    n = x1.shape[0]
    n_slots = TOP_K_INNER * n + N_EXPERTS * FFN_TILE
    rank, cnt = _rank(oh, route)
    cnt = cnt[0, :N_EXPERTS].astype(jnp.int32)
    padded = (cnt + FFN_TILE - 1) // FFN_TILE * FFN_TILE
    ends = jnp.cumsum(padded)
    offs = ends - padded
    experts = jnp.arange(N_EXPERTS, dtype=jnp.int32)
    i1 = route[:, 0].astype(jnp.int32)
    i2 = route[:, 1].astype(jnp.int32)
    off_of = lambda idx: jnp.sum(jnp.where(idx[:, None] == experts[None, :], offs[None, :], 0), axis=1)
    pos1 = off_of(i1) + rank[:, 0].astype(jnp.int32)
    pos2 = off_of(i2) + rank[:, 1].astype(jnp.int32)
    tile_start = jnp.arange(n_slots // FFN_TILE, dtype=jnp.int32) * FFN_TILE
    tile_expert = jnp.minimum(jnp.sum(tile_start[:, None] >= ends[None, :], axis=1), N_EXPERTS - 1)
    n_used = (ends[-1] // FFN_TILE).reshape(1)

    xs = _dispatch(pos1, pos2, x1, n_slots)
    ys = _ffn(tile_expert.astype(jnp.int32), n_used.astype(jnp.int32), xs, w_gate, w_up, w_down)
    return _combine(pos1, pos2, x1, route, ln_g, ln_b, ys)


def _rope_tables(positions):
    inv_freq = ROPE_THETA ** (-jnp.arange(0, ROT_DIM, 2, dtype=F32) / ROT_DIM)
    ang = positions.astype(F32).reshape(-1, 1) * inv_freq
    cos, sin = jnp.cos(ang), jnp.sin(ang)
    n = ang.shape[0]
    half = ROT_DIM // 2
    c64 = jnp.concatenate([cos, cos, jnp.ones((n, A_HEAD_DIM - ROT_DIM), F32)], axis=1)
    s64 = jnp.concatenate([-sin, sin, jnp.zeros((n, A_HEAD_DIM - ROT_DIM), F32)], axis=1)
    del half
    return jnp.tile(c64, (1, 2)), jnp.tile(s64, (1, 2))


def _rep_cols(w):
    r = w.shape[0]
    w4 = w.reshape(r, B_HEADS, 1, B_KEY_DIM)
    return jnp.broadcast_to(w4, (r, B_HEADS, N_SUB, B_KEY_DIM)).reshape(r, GLA_REP)


def _split_w_in(w):
    widths = (A_OUT, A_HEAD_DIM, A_HEAD_DIM, IDX_HEADS * IDX_DIM, IDX_DIM, IDX_HEADS,
              B_HEADS * B_KEY_DIM, B_HEADS * B_KEY_DIM, B_OUT, B_GATE_RANK, B_OUT,
              2 * C_OUT, D_MODEL, D_MODEL, D_MODEL)
    pts = np.cumsum((0,) + widths)
    return [w[:, int(pts[i]):int(pts[i + 1])] for i in range(len(widths))]


def kernel(x, positions, ln_in_g, ln_in_b, w_in, idx_k_g, gla_wa2, gla_ba, gla_norm_g, gm_ln_g, gm_ln_b, gm_ws, gm_bs, w_branch_a, w_branch_b, w_branch_c, w_out, ln1_g, ln1_b, w_rg, b_rg, w_re, b_re, w_gate, w_up, w_down, ln2_g, ln2_b):
    bsz, seq, d = x.shape
    n = bsz * seq
    cos_t, sin_t = _rope_tables(positions)
    h = _entry_ln(x.reshape(n, d), ln_in_g, ln_in_b)
    for l in range(DEPTH):
        (a_q, a_k, a_v, i_q, i_k, i_w, b_q, b_k, b_v, b_glr, b_r, c_uv,
         g_a, g_b, g_c) = _split_w_in(w_in[l])
        zpad = lambda c: jnp.zeros((d, c), F32)
        w_a = jnp.concatenate([a_q, a_k, a_v, i_q, i_k, i_w, zpad(LANES - IDX_DIM - IDX_HEADS)],
                              axis=1).astype(BF16)
        w_b = jnp.concatenate([_rep_cols(b_q * (B_KEY_DIM ** -0.5)), _rep_cols(b_k), b_v, b_r,
                               b_glr, zpad(LANES - B_GATE_RANK)], axis=1).astype(BF16)
        wa_rep = jnp.concatenate([_rep_cols(gla_wa2[l]),
                                  jnp.zeros((LANES - B_GATE_RANK, GLA_REP), F32)], axis=0).astype(BF16)
        ba_rep = _rep_cols(gla_ba[l].reshape(1, -1))
        w_g = jnp.concatenate([g_a, g_b, g_c], axis=1).astype(BF16)
        ikg = jnp.concatenate([idx_k_g[l], jnp.zeros((LANES - IDX_DIM,), F32)]).reshape(1, LANES)
        bs_t = jnp.concatenate([gm_bs[l].T, jnp.zeros((C_CHUNK, LANES - C_GROUPS), F32)], axis=1)
        w_r = jnp.concatenate([w_re[l], w_rg[l], zpad(LANES - N_GROUPS - N_EXPERTS)], axis=1)
        b_r_all = jnp.concatenate([b_re[l], b_rg[l],
                                   jnp.zeros((LANES - N_GROUPS - N_EXPERTS,), F32)]).reshape(1, LANES)

        q, k, v, iq, ki, wi = _proj_a(h, w_a, cos_t, sin_t, ikg)
        y_a = _dsa(q, iq, wi, k, v, ki, bsz, seq)
        y_b = _gla(h, w_b, wa_rep, ba_rep, gla_norm_g[l].reshape(1, -1), bsz, seq)
        y_c = _gmlp(h, c_uv.astype(BF16), gm_ln_g[l].reshape(1, -1), gm_ln_b[l].reshape(1, -1),
                    gm_ws[l], bs_t)
        x1, oh, route = _merge(h, y_a, y_b, y_c, w_g,
                               w_branch_a[l].astype(BF16), w_branch_b[l].astype(BF16),
                               w_branch_c[l].astype(BF16), w_out[l].astype(BF16),
                               ln1_g[l].reshape(1, -1), ln1_b[l].reshape(1, -1), w_r, b_r_all)
        h = _moe(x1, oh, route, w_gate[l].astype(BF16), w_up[l].astype(BF16),
                 w_down[l].astype(BF16), ln2_g[l].reshape(1, -1), ln2_b[l].reshape(1, -1))
    return h.reshape(bsz, seq, d)
```

```python
import functools

import numpy as np
import jax
import jax.numpy as jnp
from jax import lax
from jax.experimental import pallas as pl
from jax.experimental.pallas import tpu as pltpu

F32 = jnp.float32
BF16 = jnp.bfloat16

D_MODEL = 1024
DEPTH = 2
A_HEADS = 8
A_HEAD_DIM = 64
A_OUT = 512
IDX_HEADS = 4
IDX_DIM = 64
TOPK_MAX = 256
Q_BLOCK = 128
ROPE_THETA = 500000.0
ROT_DIM = 16
B_HEADS = 4
B_KEY_DIM = 64
B_VAL_DIM = 128
B_OUT = 512
B_GATE_RANK = 16
B_GATE_TAU = 16.0
B_CHUNK = 64
B_SUB = 16
N_SUB = B_CHUNK // B_SUB
C_GROUPS = 4
C_GROUP_DIM = 128
C_OUT = 512
C_CHUNK = 128
N_GROUPS = 4
EXPERTS_PER_GROUP = 8
N_EXPERTS = 32
D_EXPERT = 256
TOP_K_INNER = 2
DN_ALPHA = (2 * DEPTH) ** 0.25
LN_EPS = 1e-5
RMS_EPS = 1e-6

LANES = 128
NEG_BIG = -1e30
INT_MIN = -2 ** 31
VMEM_LIMIT = 56 * 1024 * 1024

HIGHEST = lax.Precision.HIGHEST


def _cparams(sem):
    return pltpu.CompilerParams(dimension_semantics=sem, vmem_limit_bytes=VMEM_LIMIT)


def _layer_norm(x, g, b):
    mu = jnp.mean(x, axis=-1, keepdims=True)
    xc = x - mu
    var = jnp.mean(xc * xc, axis=-1, keepdims=True)
    return xc * lax.rsqrt(var + LN_EPS) * g + b


def _dot(a, b):
    return jnp.dot(a, b, preferred_element_type=F32)


def _dot_nt(a, b):
    return lax.dot_general(a, b, (((1,), (1,)), ((), ())), preferred_element_type=F32)


def _ln_kernel(x_ref, g_ref, b_ref, o_ref):
    o_ref[...] = _layer_norm(x_ref[...], g_ref[...], b_ref[...])


def _entry_ln(x, g, b, tm=1024):
    n = x.shape[0]
    return pl.pallas_call(
        _ln_kernel,
        grid=(n // tm,),
        in_specs=[pl.BlockSpec((tm, D_MODEL), lambda i: (i, 0)),
                  pl.BlockSpec((1, D_MODEL), lambda i: (0, 0)),
                  pl.BlockSpec((1, D_MODEL), lambda i: (0, 0))],
        out_specs=pl.BlockSpec((tm, D_MODEL), lambda i: (i, 0)),
        out_shape=jax.ShapeDtypeStruct((n, D_MODEL), F32),
        compiler_params=_cparams(("parallel",)),
        name="entry_ln",
    )(x, g.reshape(1, -1), b.reshape(1, -1))


def _rope_slab(xs, c, s):
    lane = lax.broadcasted_iota(jnp.int32, xs.shape, 1) % A_HEAD_DIM
    fwd = pltpu.roll(xs, LANES - ROT_DIM // 2, axis=1)
    bwd = pltpu.roll(xs, ROT_DIM // 2, axis=1)
    partner = jnp.where(lane < ROT_DIM // 2, fwd, bwd)
    return xs * c + partner * s


def _proj_a_kernel(h_ref, w_ref, c_ref, s_ref, g_ref,
                   q_ref, k_ref, v_ref, iq_ref, ki_ref, wi_ref):
    p = _dot(h_ref[...].astype(BF16), w_ref[...])
    c = c_ref[...]
    s = s_ref[...]
    lane = lax.broadcasted_iota(jnp.int32, c.shape, 1)
    lo = lane < A_HEAD_DIM
    n_blk = p.shape[0] // Q_BLOCK

    def put_heads(dst_ref, slab, first_head):
        for qb in range(n_blk):
            t = slab[qb * Q_BLOCK:(qb + 1) * Q_BLOCK, :].T
            for half in range(2):
                hd = first_head + half
                dst_ref[qb, :, hd * Q_BLOCK:(hd + 1) * Q_BLOCK] = (
                    t[half * A_HEAD_DIM:(half + 1) * A_HEAD_DIM, :].astype(BF16))

    for i in range(4):
        sl = slice(i * LANES, (i + 1) * LANES)
        put_heads(q_ref, _rope_slab(p[:, sl], c, s) * (A_HEAD_DIM ** -0.5), 2 * i)
    kv = p[:, 512:640]
    kv = _rope_slab(kv, jnp.where(lo, c, 1.0), jnp.where(lo, s, 0.0))
    k_ref[...] = kv[:, :A_HEAD_DIM].astype(BF16)
    v_ones = jnp.where(lo, pltpu.roll(kv, A_HEAD_DIM, axis=1), jnp.where(lane == A_HEAD_DIM, 1.0, 0.0))
    for kb in range(p.shape[0] // V_BLOCK):
        v_ref[kb] = v_ones[kb * V_BLOCK:(kb + 1) * V_BLOCK, :].T.astype(BF16)
    for i in range(2):
        sl = slice(640 + i * LANES, 640 + (i + 1) * LANES)
        put_heads(iq_ref, _rope_slab(p[:, sl], c, s), 2 * i)
    last = p[:, 896:1024]
    mu = jnp.sum(jnp.where(lo, last, 0.0), axis=-1, keepdims=True) * (1.0 / IDX_DIM)
    xc = jnp.where(lo, last - mu, 0.0)
    var = jnp.sum(xc * xc, axis=-1, keepdims=True) * (1.0 / IDX_DIM)
    kin = xc * lax.rsqrt(var + LN_EPS) * g_ref[...]
    kin = _rope_slab(kin, jnp.where(lo, c, 1.0), jnp.where(lo, s, 0.0))
    ki_ref[...] = kin[:, :IDX_DIM].astype(BF16)
    wi = pltpu.roll(last, LANES - IDX_DIM, axis=1) * (IDX_HEADS ** -0.5 * IDX_DIM ** -0.5)
    for qb in range(n_blk):
        wi_ref[qb] = wi[qb * Q_BLOCK:(qb + 1) * Q_BLOCK, :].T[0:8, :]


def _proj_a(h, w_a, cos_t, sin_t, ikg, tm=512):
    n = h.shape[0]
    row = lambda w: pl.BlockSpec((tm, w), lambda i: (i, 0))
    stacked = lambda heads: pl.BlockSpec((tm // Q_BLOCK, A_HEAD_DIM, heads * Q_BLOCK), lambda i: (i, 0, 0))
    return pl.pallas_call(
        _proj_a_kernel,
        grid=(n // tm,),
        in_specs=[row(D_MODEL),
                  pl.BlockSpec((D_MODEL, 1024), lambda i: (0, 0)),
                  row(LANES), row(LANES),
                  pl.BlockSpec((1, LANES), lambda i: (0, 0))],
        out_specs=[stacked(A_HEADS), row(A_HEAD_DIM),
                   pl.BlockSpec((tm // V_BLOCK, LANES, V_BLOCK), lambda i: (i, 0, 0)),
                   stacked(IDX_HEADS), row(IDX_DIM),
                   pl.BlockSpec((tm // Q_BLOCK, 8, Q_BLOCK), lambda i: (i, 0, 0))],
        out_shape=[jax.ShapeDtypeStruct((n // Q_BLOCK, A_HEAD_DIM, A_HEADS * Q_BLOCK), BF16),
                   jax.ShapeDtypeStruct((n, A_HEAD_DIM), BF16),
                   jax.ShapeDtypeStruct((n // V_BLOCK, LANES, V_BLOCK), BF16),
                   jax.ShapeDtypeStruct((n // Q_BLOCK, IDX_DIM, IDX_HEADS * Q_BLOCK), BF16),
                   jax.ShapeDtypeStruct((n, IDX_DIM), BF16),
                   jax.ShapeDtypeStruct((n // Q_BLOCK, 8, Q_BLOCK), F32)],
        compiler_params=_cparams(("parallel",)),
        name="proj_a",
    )(h, w_a, cos_t, sin_t, ikg)


KEY_CHUNK = 512
HALF16 = 2 ** 15
V_BLOCK = KEY_CHUNK // 2


def _dsa_kernel(q_ref, iq_ref, wi_ref, k_ref, v_ref, ki_ref, o_ref,
                key_ref, hi_ref, lo_ref, bias_ref, lg0_ref, lg1_ref, p0_ref, p1_ref, *, seq):
    j = pl.program_id(1)
    n_sel = min(TOPK_MAX, seq // 4)
    n_kc = (j * Q_BLOCK + Q_BLOCK + KEY_CHUNK - 1) // KEY_CHUNK
    qpos = j * Q_BLOCK + lax.broadcasted_iota(jnp.int32, (1, Q_BLOCK), 1)
    kk = jnp.minimum(n_sel, qpos + 1).astype(F32)
    row_pos = lax.broadcasted_iota(jnp.int32, (KEY_CHUNK, 1), 0)

    def key_rows(c):
        return pl.ds(pl.multiple_of(c * KEY_CHUNK, KEY_CHUNK), KEY_CHUNK)

    iq = iq_ref[...]
    wi = wi_ref[...]

    def score_chunk(c, carry):
        d = jnp.maximum(_dot(ki_ref[key_rows(c), :], iq), 0.0)
        score = wi[0:1, :] * d[:, 0:Q_BLOCK]
        for h in range(1, IDX_HEADS):
            score = score + wi[h:h + 1, :] * d[:, h * Q_BLOCK:(h + 1) * Q_BLOCK]
        bits = pltpu.bitcast(score + 0.0, jnp.int32)
        key = bits ^ ((bits >> 31) & 0x7FFFFFFF)
        key = jnp.where(c * KEY_CHUNK + row_pos <= qpos, key, INT_MIN)
        key_ref[c] = key
        hi_ref[c] = (key >> 16).astype(jnp.int16)
        lo_ref[c] = ((key & 0xFFFF) - HALF16).astype(jnp.int16)
        return carry

    lax.fori_loop(0, n_kc, score_chunk, 0)

    def count(pred):
        def body(c, acc):
            m = jnp.where(pred(key_ref[c]), 1.0, 0.0)
            return acc + jnp.sum(m.reshape(KEY_CHUNK // 64, 64, Q_BLOCK), axis=0)
        acc = lax.fori_loop(0, n_kc, body, jnp.zeros((64, Q_BLOCK), F32))
        return jnp.sum(acc, axis=0, keepdims=True)

    one16, zero16 = jnp.int16(1), jnp.int16(0)

    def count16(ref, pred):
        def body(c, acc):
            m = jnp.where(pred(ref[c]), one16, zero16).reshape(KEY_CHUNK // 64, 64, Q_BLOCK)
            parts = [m[t] for t in range(KEY_CHUNK // 64)]
            while len(parts) > 1:
                parts = [a + b for a, b in zip(parts[::2], parts[1::2])]
            return acc + parts[0]
        acc = lax.fori_loop(0, n_kc, body, jnp.zeros((64, Q_BLOCK), jnp.int16))
        return jnp.sum(acc.astype(F32), axis=0, keepdims=True)

    def search16(ref, target):
        def step(i, t_u):
            cand_u = t_u | jnp.left_shift(jnp.int32(1), 15 - i)
            cand = (cand_u - HALF16).astype(jnp.int16)
            return jnp.where(count16(ref, lambda kc: kc >= cand) >= target, cand_u, t_u)
        return lax.fori_loop(0, 16, step, jnp.zeros((1, Q_BLOCK), jnp.int32))

    def search32(i, t_u):
        cand_u = t_u | jnp.left_shift(jnp.int32(1), 31 - i)
        cand_s = cand_u ^ INT_MIN
        return jnp.where(count(lambda kc: kc >= cand_s) >= kk, cand_u, t_u)

    thr = lax.fori_loop(0, 32, search32, jnp.zeros((1, Q_BLOCK), jnp.int32)) ^ INT_MIN
    c_gt = count(lambda kc: kc > thr)
    c_ge = count(lambda kc: kc >= thr)
    need = kk - c_gt
    has_ties = jnp.max(c_ge - kk) > 0.0

    @pl.when(jnp.logical_not(has_ties))
    def _():
        def fill(c, carry):
            bias_ref[c] = jnp.where(key_ref[c] >= thr, 0.0, NEG_BIG)
            return carry
        lax.fori_loop(0, n_kc, fill, 0)

    @pl.when(has_ties)
    def _():
        r_i = lax.broadcasted_iota(jnp.int32, (KEY_CHUNK, KEY_CHUNK), 0)
        c_i = lax.broadcasted_iota(jnp.int32, (KEY_CHUNK, KEY_CHUNK), 1)
        lower = jnp.where(c_i < r_i, 1.0, 0.0).astype(BF16)

        def fill(c, carry):
            kc = key_ref[c]
            eq = kc == thr
            eq_f = jnp.where(eq, 1.0, 0.0)
            pre = _dot(lower, eq_f.astype(BF16)) + carry
            sel = (kc > thr) | (eq & (pre < need))
            bias_ref[c] = jnp.where(sel, 0.0, NEG_BIG)
            return carry + jnp.sum(eq_f, axis=0, keepdims=True)

        lax.fori_loop(0, n_kc, fill, jnp.zeros((1, Q_BLOCK), F32))

    q = q_ref[...]
    cols = A_HEADS * Q_BLOCK

    half = V_BLOCK

    def half_rows(c, hf):
        return pl.ds(pl.multiple_of(c * KEY_CHUNK + hf * half, half), half)

    def logits(c, hf):
        b = bias_ref[c, hf * half:(hf + 1) * half, :]
        return _dot(k_ref[half_rows(c, hf), :], q) + jnp.concatenate([b] * A_HEADS, axis=1)

    def weighted_values(c, hf, p_in):
        return _dot(v_ref[c * (KEY_CHUNK // V_BLOCK) + hf], p_in[...])

    def step(nxt, cur_lg, nxt_lg, prv, prv_p, cur_p, carry):
        m, alpha, acc = carry
        nxt_lg[...] = logits(*nxt)
        lg = cur_lg[...]
        m_new = jnp.maximum(m, jnp.max(lg, axis=0, keepdims=True))
        acc = alpha * acc + weighted_values(*prv, prv_p)
        cur_p[...] = jnp.exp(lg - m_new).astype(BF16)
        return m_new, jnp.exp(m - m_new), acc

    def attn_chunk(c, carry):
        carry = step((c, 1), lg0_ref, lg1_ref, (jnp.maximum(c - 1, 0), 1), p1_ref, p0_ref, carry)
        return step((jnp.minimum(c + 1, n_kc - 1), 0), lg1_ref, lg0_ref, (c, 0), p0_ref, p1_ref, carry)

    lg0_ref[...] = logits(0, 0)
    p1_ref[...] = jnp.zeros(p1_ref.shape, BF16)
    init = (jnp.full((1, cols), -3e38, F32), jnp.ones((1, cols), F32), jnp.zeros((LANES, cols), F32))
    _, alpha, acc = lax.fori_loop(0, n_kc, attn_chunk, init)
    acc = alpha * acc + weighted_values(n_kc - 1, 1, p1_ref)
    o = acc[:A_HEAD_DIM] / acc[A_HEAD_DIM:A_HEAD_DIM + 1]
    for h in range(A_HEADS):
        o_ref[:, h * A_HEAD_DIM:(h + 1) * A_HEAD_DIM] = (
            o[:, h * Q_BLOCK:(h + 1) * Q_BLOCK].T.astype(BF16))


def _dsa(q, iq, wi, k, v, ki, bsz, seq):
    nq = seq // Q_BLOCK
    n_kc = seq // KEY_CHUNK
    stacked = lambda heads: pl.BlockSpec((None, A_HEAD_DIM, heads * Q_BLOCK),
                                         lambda b, j: (b * nq + j, 0, 0))
    kspec = lambda w: pl.BlockSpec((seq, w), lambda b, j: (b, 0))
    return pl.pallas_call(
        functools.partial(_dsa_kernel, seq=seq),
        grid=(bsz, nq),
        in_specs=[stacked(A_HEADS), stacked(IDX_HEADS),
                  pl.BlockSpec((None, 8, Q_BLOCK), lambda b, j: (b * nq + j, 0, 0)),
                  kspec(A_HEAD_DIM),
                  pl.BlockSpec((seq // V_BLOCK, LANES, V_BLOCK), lambda b, j: (b, 0, 0)),
                  kspec(IDX_DIM)],
        out_specs=pl.BlockSpec((Q_BLOCK, A_OUT), lambda b, j: (b * nq + j, 0)),
        out_shape=jax.ShapeDtypeStruct((bsz * seq, A_OUT), BF16),
        scratch_shapes=[pltpu.VMEM((n_kc, KEY_CHUNK, Q_BLOCK), jnp.int32),
                        pltpu.VMEM((n_kc, KEY_CHUNK, Q_BLOCK), jnp.int16),
                        pltpu.VMEM((n_kc, KEY_CHUNK, Q_BLOCK), jnp.int16),
                        pltpu.VMEM((n_kc, KEY_CHUNK, Q_BLOCK), F32),
                        pltpu.VMEM((KEY_CHUNK // 2, A_HEADS * Q_BLOCK), F32),
                        pltpu.VMEM((KEY_CHUNK // 2, A_HEADS * Q_BLOCK), F32),
                        pltpu.VMEM((KEY_CHUNK // 2, A_HEADS * Q_BLOCK), BF16),
                        pltpu.VMEM((KEY_CHUNK // 2, A_HEADS * Q_BLOCK), BF16)],
        compiler_params=_cparams(("parallel", "parallel")),
        name="dsa",
    )(q, iq, wi, k, v, ki)


GLA_REP = B_HEADS * N_SUB * B_KEY_DIM
GLA_COLS = 2 * GLA_REP + 2 * B_OUT + LANES


def _gla_kernel(h_ref, w_ref, wa_ref, ba_ref, ng_ref, o_ref, p_ref, g_ref, acc_ref, st_ref, *, ts):
    @pl.when(pl.program_id(1) == 0)
    def _():
        st_ref[...] = jnp.zeros_like(st_ref)

    p_ref[...] = _dot(h_ref[...].astype(BF16), w_ref[...])
    x = _dot(p_ref[:, 2 * GLA_REP + 2 * B_OUT:].astype(BF16), wa_ref[...]) + ba_ref[...]
    g_ref[...] = (jnp.minimum(x, 0.0) - jnp.log1p(jnp.exp(-jnp.abs(x)))) * (1.0 / B_GATE_TAU)

    t_i = lax.broadcasted_iota(jnp.int32, (B_CHUNK, B_CHUNK), 0)
    s_i = lax.broadcasted_iota(jnp.int32, (B_CHUNK, B_CHUNK), 1)
    tri = t_i >= s_i
    tri_f = jnp.where(tri, 1.0, 0.0)
    lane_sub = (lax.broadcasted_iota(jnp.int32, (1, GLA_REP), 1) // B_KEY_DIM) % N_SUB
    row_sub = lax.broadcasted_iota(jnp.int32, (B_CHUNK, 1), 0) // B_SUB
    q_mask = lane_sub <= row_sub
    k_mask = lane_sub == row_sub

    def chunk(c, carry):
        rows = pl.ds(pl.multiple_of(c * B_CHUNK, B_CHUNK), B_CHUNK)
        b = jnp.dot(tri_f, g_ref[rows, :], precision=HIGHEST, preferred_element_type=F32)
        ref = jnp.zeros((1, GLA_REP), F32)
        for jj in range(1, N_SUB):
            ref = ref + jnp.where(lane_sub == jj, b[jj * B_SUB - 1:jj * B_SUB, :], 0.0)
        d = b - ref
        e_q = jnp.where(q_mask, jnp.exp(jnp.minimum(d, 0.0)), 0.0)
        e_k = jnp.where(k_mask, jnp.exp(jnp.where(k_mask, -d, 0.0)), 0.0)
        b_last = b[B_CHUNK - 1:B_CHUNK, :]
        q_c = (p_ref[rows, 0:GLA_REP] * e_q).astype(BF16)
        k_raw = p_ref[rows, GLA_REP:2 * GLA_REP]
        k_c = (k_raw * e_k).astype(BF16)
        k_st = (k_raw * jnp.exp(b_last - b)).astype(BF16)
        e_last = jnp.exp(b_last)
        for hh in range(B_HEADS):
            base = hh * N_SUB * B_KEY_DIM
            q_h = q_c[:, base:base + N_SUB * B_KEY_DIM]
            k_h = k_c[:, base:base + N_SUB * B_KEY_DIM]
            att = jnp.where(tri, _dot_nt(q_h, k_h), 0.0)
            v_h = p_ref[rows, 2 * GLA_REP + hh * B_VAL_DIM:2 * GLA_REP + (hh + 1) * B_VAL_DIM].astype(BF16)
            st = st_ref[hh]
            o = _dot(att.astype(BF16), v_h) + _dot_nt(q_h[:, :B_KEY_DIM], st.astype(BF16))
            acc_ref[rows, hh * B_VAL_DIM:(hh + 1) * B_VAL_DIM] = o
            upd = _dot(v_h.T, k_st[:, base:base + B_KEY_DIM])
            st_ref[hh] = st * e_last[:, base:base + B_KEY_DIM] + upd
        return carry

    lax.fori_loop(0, ts // B_CHUNK, chunk, 0)

    r = p_ref[:, 2 * GLA_REP + B_OUT:2 * GLA_REP + 2 * B_OUT]
    gate = r * (1.0 / (1.0 + jnp.exp(-r)))
    ng = ng_ref[...]
    for hh in range(B_HEADS):
        sl = slice(hh * B_VAL_DIM, (hh + 1) * B_VAL_DIM)
        o = acc_ref[:, sl]
        y = o * lax.rsqrt(jnp.mean(o * o, axis=-1, keepdims=True) + RMS_EPS) * ng
        o_ref[:, sl] = (gate[:, sl] * y).astype(BF16)


def _gla(h, w_b, wa_rep, ba_rep, norm_g, bsz, seq, ts=512):
    ns = seq // ts
    const = lambda shape: pl.BlockSpec(shape, lambda b, i: (0, 0))
    return pl.pallas_call(
        functools.partial(_gla_kernel, ts=ts),
        grid=(bsz, ns),
        in_specs=[pl.BlockSpec((ts, D_MODEL), lambda b, i: (b * ns + i, 0)),
                  const((D_MODEL, GLA_COLS)), const((LANES, GLA_REP)),
                  const((1, GLA_REP)), const((1, B_VAL_DIM))],
        out_specs=pl.BlockSpec((ts, B_OUT), lambda b, i: (b * ns + i, 0)),
        out_shape=jax.ShapeDtypeStruct((bsz * seq, B_OUT), BF16),
        scratch_shapes=[pltpu.VMEM((ts, GLA_COLS), F32),
                        pltpu.VMEM((ts, GLA_REP), F32),
                        pltpu.VMEM((ts, B_OUT), F32),
                        pltpu.VMEM((B_HEADS, B_VAL_DIM, B_KEY_DIM), F32)],
        compiler_params=_cparams(("parallel", "arbitrary")),
        name="gla",
    )(h, w_b, wa_rep, ba_rep, norm_g)


def _gmlp_kernel(h_ref, w_ref, lg_ref, lb_ref, ws_ref, bs_ref, o_ref, *, tm):
    z = _dot(h_ref[...].astype(BF16), w_ref[...])
    z = z * (0.5 * (1.0 + jnp.tanh(np.sqrt(2.0 / np.pi) * (z + 0.044715 * (z * z * z)))))
    u = z[:, :C_OUT]
    v = _layer_norm(z[:, C_OUT:], lg_ref[...], lb_ref[...]).astype(BF16)
    t_i = lax.broadcasted_iota(jnp.int32, (C_CHUNK, C_CHUNK), 0)
    s_i = lax.broadcasted_iota(jnp.int32, (C_CHUNK, C_CHUNK), 1)
    bs = bs_ref[...]
    for g in range(C_GROUPS):
        w = jnp.where(t_i >= s_i, ws_ref[g], 0.0).astype(BF16)
        sl = slice(g * C_GROUP_DIM, (g + 1) * C_GROUP_DIM)
        for c in range(tm // C_CHUNK):
            rows = slice(c * C_CHUNK, (c + 1) * C_CHUNK)
            mixed = _dot(w, v[rows, sl]) + bs[:, g:g + 1]
            o_ref[rows, sl] = (u[rows, sl] * mixed).astype(BF16)


def _gmlp(h, w_c, ln_g, ln_b, ws, bs_t, tm=512):
    n = h.shape[0]
    return pl.pallas_call(
        functools.partial(_gmlp_kernel, tm=tm),
        grid=(n // tm,),
        in_specs=[pl.BlockSpec((tm, D_MODEL), lambda i: (i, 0)),
                  pl.BlockSpec((D_MODEL, 2 * C_OUT), lambda i: (0, 0)),
                  pl.BlockSpec((1, C_OUT), lambda i: (0, 0)),
                  pl.BlockSpec((1, C_OUT), lambda i: (0, 0)),
                  pl.BlockSpec((C_GROUPS, C_CHUNK, C_CHUNK), lambda i: (0, 0, 0)),
                  pl.BlockSpec((C_CHUNK, LANES), lambda i: (0, 0))],
        out_specs=pl.BlockSpec((tm, C_OUT), lambda i: (i, 0)),
        out_shape=jax.ShapeDtypeStruct((n, C_OUT), BF16),
        compiler_params=_cparams(("parallel",)),
        name="gmlp",
    )(h, w_c, ln_g, ln_b, ws, bs_t)


def _sigmoid(x):
    return 1.0 / (1.0 + jnp.exp(-x))


def _merge_kernel(h_ref, ya_ref, yb_ref, yc_ref, wg_ref, wa_ref, wb_ref, wc_ref, wo_ref,
                  lg_ref, lb_ref, wr_ref, br_ref, x_ref, oh_ref, route_ref):
    h = h_ref[...]
    hb = h.astype(BF16)
    acc = None
    for i, (y_ref, wbr_ref) in enumerate(((ya_ref, wa_ref), (yb_ref, wb_ref), (yc_ref, wc_ref))):
        gate = _sigmoid(_dot(hb, wg_ref[:, i * D_MODEL:(i + 1) * D_MODEL]))
        term = gate * _dot(y_ref[...], wbr_ref[...])
        acc = term if acc is None else acc + term
    mix = _dot(acc.astype(BF16), wo_ref[...])
    x1 = _layer_norm(DN_ALPHA * h + mix, lg_ref[...], lb_ref[...])
    x_ref[...] = x1

    logit = jnp.dot(x1, wr_ref[...], precision=HIGHEST, preferred_element_type=F32) + br_ref[...]
    lane = lax.broadcasted_iota(jnp.int32, logit.shape, 1)
    big = jnp.int32(LANES)
    is_g = (lane >= N_EXPERTS) & (lane < N_EXPERTS + N_GROUPS)
    gl = jnp.where(is_g, logit, -jnp.inf)
    g_max = jnp.max(gl, axis=-1, keepdims=True)
    g_idx = jnp.min(jnp.where(is_g & (gl == g_max), lane, big), axis=-1, keepdims=True) - N_EXPERTS
    g_p = 1.0 / jnp.sum(jnp.where(is_g, jnp.exp(gl - g_max), 0.0), axis=-1, keepdims=True)
    lo = g_idx * EXPERTS_PER_GROUP
    in_g = (lane >= lo) & (lane < lo + EXPERTS_PER_GROUP)
    el = jnp.where(in_g, logit, -jnp.inf)
    e1 = jnp.max(el, axis=-1, keepdims=True)
    i1 = jnp.min(jnp.where(in_g & (el == e1), lane, big), axis=-1, keepdims=True)
    el2 = jnp.where(lane == i1, -jnp.inf, el)
    e2 = jnp.max(el2, axis=-1, keepdims=True)
    i2 = jnp.min(jnp.where(in_g & (lane != i1) & (el2 == e2), lane, big), axis=-1, keepdims=True)
    z2 = jnp.exp(e2 - e1)
    den = 1.0 + z2
    oh_ref[...] = jnp.where((lane == i1) | (lane == i2), 1.0, 0.0).astype(BF16)
    route_ref[...] = jnp.where(lane == 0, i1.astype(F32),
                               jnp.where(lane == 1, i2.astype(F32),
                                         jnp.where(lane == 2, (1.0 / den) * g_p,
                                                   jnp.where(lane == 3, (z2 / den) * g_p, 0.0))))


def _merge(h, ya, yb, yc, wg, wa, wb, wc, wo, ln_g, ln_b, wr, br, tm=512):
    n = h.shape[0]
    row = lambda w: pl.BlockSpec((tm, w), lambda i: (i, 0))
    const = lambda r, c: pl.BlockSpec((r, c), lambda i: (0, 0))
    return pl.pallas_call(
        _merge_kernel,
        grid=(n // tm,),
        in_specs=[row(D_MODEL), row(A_OUT), row(B_OUT), row(C_OUT),
                  const(D_MODEL, 3 * D_MODEL), const(A_OUT, D_MODEL), const(B_OUT, D_MODEL),
                  const(C_OUT, D_MODEL), const(D_MODEL, D_MODEL),
                  const(1, D_MODEL), const(1, D_MODEL), const(D_MODEL, LANES), const(1, LANES)],
        out_specs=[row(D_MODEL), row(LANES), row(LANES)],
        out_shape=[jax.ShapeDtypeStruct((n, D_MODEL), F32),
                   jax.ShapeDtypeStruct((n, LANES), BF16),
                   jax.ShapeDtypeStruct((n, LANES), F32)],
        compiler_params=_cparams(("parallel",)),
        name="merge",
    )(h, ya, yb, yc, wg, wa, wb, wc, wo, ln_g, ln_b, wr, br)


FFN_TILE = 256
MOE_TOK = 256


def _rank_kernel(oh_ref, route_ref, r_ref, cnt_ref, carry_ref):
    @pl.when(pl.program_id(0) == 0)
    def _():
        carry_ref[...] = jnp.zeros_like(carry_ref)

    oh = oh_ref[...]
    tt = oh.shape[0]
    r_i = lax.broadcasted_iota(jnp.int32, (tt, tt), 0)
    c_i = lax.broadcasted_iota(jnp.int32, (tt, tt), 1)
    lower = jnp.where(c_i < r_i, 1.0, 0.0).astype(BF16)
    rank = _dot(lower, oh) + carry_ref[...]
    route = route_ref[...]
    lane = lax.broadcasted_iota(jnp.int32, rank.shape, 1)
    i1 = route[:, 0:1].astype(jnp.int32)
    i2 = route[:, 1:2].astype(jnp.int32)
    r1 = jnp.sum(jnp.where(lane == i1, rank, 0.0), axis=-1, keepdims=True)
    r2 = jnp.sum(jnp.where(lane == i2, rank, 0.0), axis=-1, keepdims=True)
    r_ref[...] = jnp.where(lane == 0, r1, jnp.where(lane == 1, r2, 0.0))
    carry_ref[...] += jnp.sum(oh.astype(F32), axis=0, keepdims=True)
    cnt_ref[...] = carry_ref[...]


def _rank(oh, route, tt=512):
    n = oh.shape[0]
    return pl.pallas_call(
        _rank_kernel,
        grid=(n // tt,),
        in_specs=[pl.BlockSpec((tt, LANES), lambda i: (i, 0)),
                  pl.BlockSpec((tt, LANES), lambda i: (i, 0))],
        out_specs=[pl.BlockSpec((tt, LANES), lambda i: (i, 0)),
                   pl.BlockSpec((1, LANES), lambda i: (0, 0))],
        out_shape=[jax.ShapeDtypeStruct((n, LANES), F32),
                   jax.ShapeDtypeStruct((1, LANES), F32)],
        scratch_shapes=[pltpu.VMEM((1, LANES), F32)],
        compiler_params=_cparams(("arbitrary",)),
        name="moe_rank",
    )(oh, route)


def _row_copy(src_ref, src_row, dst_ref, dst_row, sem):
    return pltpu.make_async_copy(src_ref.at[pl.ds(src_row, 1), :], dst_ref.at[pl.ds(dst_row, 1), :], sem)


def _dispatch_kernel(pos1_ref, pos2_ref, x_ref, init_ref, xs_ref, sem):
    del init_ref
    base = pl.program_id(0) * MOE_TOK

    def issue(t, carry):
        _row_copy(x_ref, t, xs_ref, pos1_ref[base + t], sem).start()
        _row_copy(x_ref, t, xs_ref, pos2_ref[base + t], sem).start()
        return carry

    lax.fori_loop(0, MOE_TOK, issue, 0, unroll=8)
    for _ in range(2):
        pltpu.make_async_copy(x_ref, xs_ref.at[pl.ds(0, MOE_TOK), :], sem).wait()


def _dispatch(pos1, pos2, x1, n_slots):
    n = x1.shape[0]
    init = jnp.zeros((n_slots, D_MODEL), F32)
    return pl.pallas_call(
        _dispatch_kernel,
        grid_spec=pltpu.PrefetchScalarGridSpec(
            num_scalar_prefetch=2,
            grid=(n // MOE_TOK,),
            in_specs=[pl.BlockSpec((MOE_TOK, D_MODEL), lambda i, p1, p2: (i, 0)),
                      pl.BlockSpec(memory_space=pl.ANY)],
            out_specs=pl.BlockSpec(memory_space=pl.ANY),
            scratch_shapes=[pltpu.SemaphoreType.DMA(())]),
        out_shape=jax.ShapeDtypeStruct((n_slots, D_MODEL), F32),
        input_output_aliases={3: 0},
        compiler_params=_cparams(("arbitrary",)),
        name="moe_dispatch",
    )(pos1, pos2, x1, init)


def _ffn_kernel(te_ref, nu_ref, xs_ref, wg_ref, wu_ref, wd_ref, ys_ref):
    del te_ref
    i = pl.program_id(0)

    @pl.when(i < nu_ref[0])
    def _():
        xb = xs_ref[...].astype(BF16)
        gte = _dot(xb, wg_ref[...])
        hid = gte * _sigmoid(gte) * _dot(xb, wu_ref[...])
        ys_ref[...] = _dot(hid.astype(BF16), wd_ref[...])

    @pl.when(i >= nu_ref[0])
    def _():
        ys_ref[...] = jnp.zeros_like(ys_ref)


def _ffn(tile_expert, n_used, xs, w_gate, w_up, w_down):
    n_slots = xs.shape[0]
    wspec = lambda r, c: pl.BlockSpec((None, r, c), lambda i, te, nu: (te[i], 0, 0))
    return pl.pallas_call(
        _ffn_kernel,
        grid_spec=pltpu.PrefetchScalarGridSpec(
            num_scalar_prefetch=2,
            grid=(n_slots // FFN_TILE,),
            in_specs=[pl.BlockSpec((FFN_TILE, D_MODEL), lambda i, te, nu: (i, 0)),
                      wspec(D_MODEL, D_EXPERT), wspec(D_MODEL, D_EXPERT), wspec(D_EXPERT, D_MODEL)],
            out_specs=pl.BlockSpec((FFN_TILE, D_MODEL), lambda i, te, nu: (i, 0))),
        out_shape=jax.ShapeDtypeStruct((n_slots, D_MODEL), F32),
        compiler_params=_cparams(("arbitrary",)),
        name="moe_ffn",
    )(tile_expert, n_used, xs, w_gate, w_up, w_down)


def _combine_kernel(pos1_ref, pos2_ref, x_ref, route_ref, lg_ref, lb_ref, ys_ref, o_ref, buf_ref, sem):
    base = pl.program_id(0) * MOE_TOK

    def issue(t, carry):
        _row_copy(ys_ref, pos1_ref[base + t], buf_ref.at[0], t, sem).start()
        _row_copy(ys_ref, pos2_ref[base + t], buf_ref.at[1], t, sem).start()
        return carry

    lax.fori_loop(0, MOE_TOK, issue, 0, unroll=8)
    for s in range(2):
        pltpu.make_async_copy(ys_ref.at[pl.ds(0, MOE_TOK), :], buf_ref.at[s], sem).wait()
    route = route_ref[...]
    ffn = route[:, 2:3] * buf_ref[0] + route[:, 3:4] * buf_ref[1]
    o_ref[...] = _layer_norm(DN_ALPHA * x_ref[...] + ffn, lg_ref[...], lb_ref[...])


def _combine(pos1, pos2, x1, route, ln_g, ln_b, ys):
    n = x1.shape[0]
    row = lambda w: pl.BlockSpec((MOE_TOK, w), lambda i, p1, p2: (i, 0))
    const = lambda w: pl.BlockSpec((1, w), lambda i, p1, p2: (0, 0))
    return pl.pallas_call(
        _combine_kernel,
        grid_spec=pltpu.PrefetchScalarGridSpec(
            num_scalar_prefetch=2,
            grid=(n // MOE_TOK,),
            in_specs=[row(D_MODEL), row(LANES), const(D_MODEL), const(D_MODEL),
                      pl.BlockSpec(memory_space=pl.ANY)],
            out_specs=row(D_MODEL),
            scratch_shapes=[pltpu.VMEM((2, MOE_TOK, D_MODEL), F32),
                            pltpu.SemaphoreType.DMA(())]),
        out_shape=jax.ShapeDtypeStruct((n, D_MODEL), F32),
        compiler_params=_cparams(("arbitrary",)),
        name="moe_combine",
    )(pos1, pos2, x1, route, ln_g, ln_b, ys)


def _moe(x1, oh, route, w_gate, w_up, w_down, ln_g, ln_b):
    n = x1.shape[0]
    n_slots = TOP_K_INNER * n + N_EXPERTS * FFN_TILE
    rank, cnt = _rank(oh, route)
    cnt = cnt[0, :N_EXPERTS].astype(jnp.int32)
    padded = (cnt + FFN_TILE - 1) // FFN_TILE * FFN_TILE
    ends = jnp.cumsum(padded)
    offs = ends - padded
    experts = jnp.arange(N_EXPERTS, dtype=jnp.int32)
    i1 = route[:, 0].astype(jnp.int32)
    i2 = route[:, 1].astype(jnp.int32)
    off_of = lambda idx: jnp.sum(jnp.where(idx[:, None] == experts[None, :], offs[None, :], 0), axis=1)
    pos1 = off_of(i1) + rank[:, 0].astype(jnp.int32)
    pos2 = off_of(i2) + rank[:, 1].astype(jnp.int32)
    tile_start = jnp.arange(n_slots // FFN_TILE, dtype=jnp.int32) * FFN_TILE
    tile_expert = jnp.minimum(jnp.sum(tile_start[:, None] >= ends[None, :], axis=1), N_EXPERTS - 1)
    n_used = (ends[-1] // FFN_TILE).reshape(1)

    xs = _dispatch(pos1, pos2, x1, n_slots)
    ys = _ffn(tile_expert.astype(jnp.int32), n_used.astype(jnp.int32), xs, w_gate, w_up, w_down)
    return _combine(pos1, pos2, x1, route, ln_g, ln_b, ys)


def _rope_tables(positions):
    inv_freq = ROPE_THETA ** (-jnp.arange(0, ROT_DIM, 2, dtype=F32) / ROT_DIM)
    ang = positions.astype(F32).reshape(-1, 1) * inv_freq
    cos, sin = jnp.cos(ang), jnp.sin(ang)
    n = ang.shape[0]
    half = ROT_DIM // 2
    c64 = jnp.concatenate([cos, cos, jnp.ones((n, A_HEAD_DIM - ROT_DIM), F32)], axis=1)
    s64 = jnp.concatenate([-sin, sin, jnp.zeros((n, A_HEAD_DIM - ROT_DIM), F32)], axis=1)
    del half
    return jnp.tile(c64, (1, 2)), jnp.tile(s64, (1, 2))


def _rep_cols(w):
    r = w.shape[0]
    w4 = w.reshape(r, B_HEADS, 1, B_KEY_DIM)
    return jnp.broadcast_to(w4, (r, B_HEADS, N_SUB, B_KEY_DIM)).reshape(r, GLA_REP)


def _split_w_in(w):
    widths = (A_OUT, A_HEAD_DIM, A_HEAD_DIM, IDX_HEADS * IDX_DIM, IDX_DIM, IDX_HEADS,
              B_HEADS * B_KEY_DIM, B_HEADS * B_KEY_DIM, B_OUT, B_GATE_RANK, B_OUT,
              2 * C_OUT, D_MODEL, D_MODEL, D_MODEL)
    pts = np.cumsum((0,) + widths)
    return [w[:, int(pts[i]):int(pts[i + 1])] for i in range(len(widths))]


def kernel(x, positions, ln_in_g, ln_in_b, w_in, idx_k_g, gla_wa2, gla_ba, gla_norm_g, gm_ln_g, gm_ln_b, gm_ws, gm_bs, w_branch_a, w_branch_b, w_branch_c, w_out, ln1_g, ln1_b, w_rg, b_rg, w_re, b_re, w_gate, w_up, w_down, ln2_g, ln2_b):
    bsz, seq, d = x.shape
    n = bsz * seq
    cos_t, sin_t = _rope_tables(positions)
    h = _entry_ln(x.reshape(n, d), ln_in_g, ln_in_b)
    for l in range(DEPTH):
        (a_q, a_k, a_v, i_q, i_k, i_w, b_q, b_k, b_v, b_glr, b_r, c_uv,
         g_a, g_b, g_c) = _split_w_in(w_in[l])
        zpad = lambda c: jnp.zeros((d, c), F32)
        w_a = jnp.concatenate([a_q, a_k, a_v, i_q, i_k, i_w, zpad(LANES - IDX_DIM - IDX_HEADS)],
                              axis=1).astype(BF16)
        w_b = jnp.concatenate([_rep_cols(b_q * (B_KEY_DIM ** -0.5)), _rep_cols(b_k), b_v, b_r,
                               b_glr, zpad(LANES - B_GATE_RANK)], axis=1).astype(BF16)
        wa_rep = jnp.concatenate([_rep_cols(gla_wa2[l]),
                                  jnp.zeros((LANES - B_GATE_RANK, GLA_REP), F32)], axis=0).astype(BF16)
        ba_rep = _rep_cols(gla_ba[l].reshape(1, -1))
        w_g = jnp.concatenate([g_a, g_b, g_c], axis=1).astype(BF16)
        ikg = jnp.concatenate([idx_k_g[l], jnp.zeros((LANES - IDX_DIM,), F32)]).reshape(1, LANES)
        bs_t = jnp.concatenate([gm_bs[l].T, jnp.zeros((C_CHUNK, LANES - C_GROUPS), F32)], axis=1)
        w_r = jnp.concatenate([w_re[l], w_rg[l], zpad(LANES - N_GROUPS - N_EXPERTS)], axis=1)
        b_r_all = jnp.concatenate([b_re[l], b_rg[l],
                                   jnp.zeros((LANES - N_GROUPS - N_EXPERTS,), F32)]).reshape(1, LANES)

        q, k, v, iq, ki, wi = _proj_a(h, w_a, cos_t, sin_t, ikg)
        y_a = _dsa(q, iq, wi, k, v, ki, bsz, seq)
        y_b = _gla(h, w_b, wa_rep, ba_rep, gla_norm_g[l].reshape(1, -1), bsz, seq)
        y_c = _gmlp(h, c_uv.astype(BF16), gm_ln_g[l].reshape(1, -1), gm_ln_b[l].reshape(1, -1),
                    gm_ws[l], bs_t)
        x1, oh, route = _merge(h, y_a, y_b, y_c, w_g,
                               w_branch_a[l].astype(BF16), w_branch_b[l].astype(BF16),
                               w_branch_c[l].astype(BF16), w_out[l].astype(BF16),
                               ln1_g[l].reshape(1, -1), ln1_b[l].reshape(1, -1), w_r, b_r_all)
        h = _moe(x1, oh, route, w_gate[l].astype(BF16), w_up[l].astype(BF16),
                 w_down[l].astype(BF16), ln2_g[l].reshape(1, -1), ln2_b[l].reshape(1, -1))
    return h.reshape(bsz, seq, d)
```

```python
import functools

import numpy as np
import jax
import jax.numpy as jnp
from jax import lax
from jax.experimental import pallas as pl
from jax.experimental.pallas import tpu as pltpu

F32 = jnp.float32
BF16 = jnp.bfloat16

D_MODEL = 1024
DEPTH = 2
A_HEADS = 8
A_HEAD_DIM = 64
A_OUT = 512
IDX_HEADS = 4
IDX_DIM = 64
TOPK_MAX = 256
Q_BLOCK = 128
ROPE_THETA = 500000.0
ROT_DIM = 16
B_HEADS = 4
B_KEY_DIM = 64
B_VAL_DIM = 128
B_OUT = 512
B_GATE_RANK = 16
B_GATE_TAU = 16.0
B_CHUNK = 64
B_SUB = 16
N_SUB = B_CHUNK // B_SUB
C_GROUPS = 4
C_GROUP_DIM = 128
C_OUT = 512
C_CHUNK = 128
N_GROUPS = 4
EXPERTS_PER_GROUP = 8
N_EXPERTS = 32
D_EXPERT = 256
TOP_K_INNER = 2
DN_ALPHA = (2 * DEPTH) ** 0.25
LN_EPS = 1e-5
RMS_EPS = 1e-6

LANES = 128
NEG_BIG = -1e30
INT_MIN = -2 ** 31
VMEM_LIMIT = 56 * 1024 * 1024

HIGHEST = lax.Precision.HIGHEST


def _cparams(sem):
    return pltpu.CompilerParams(dimension_semantics=sem, vmem_limit_bytes=VMEM_LIMIT)


def _layer_norm(x, g, b):
    mu = jnp.mean(x, axis=-1, keepdims=True)
    xc = x - mu
    var = jnp.mean(xc * xc, axis=-1, keepdims=True)
    return xc * lax.rsqrt(var + LN_EPS) * g + b


def _dot(a, b):
    return jnp.dot(a, b, preferred_element_type=F32)


def _dot_nt(a, b):
    return lax.dot_general(a, b, (((1,), (1,)), ((), ())), preferred_element_type=F32)


def _ln_kernel(x_ref, g_ref, b_ref, o_ref):
    o_ref[...] = _layer_norm(x_ref[...], g_ref[...], b_ref[...])


def _entry_ln(x, g, b, tm=1024):
    n = x.shape[0]
    return pl.pallas_call(
        _ln_kernel,
        grid=(n // tm,),
        in_specs=[pl.BlockSpec((tm, D_MODEL), lambda i: (i, 0)),
                  pl.BlockSpec((1, D_MODEL), lambda i: (0, 0)),
                  pl.BlockSpec((1, D_MODEL), lambda i: (0, 0))],
        out_specs=pl.BlockSpec((tm, D_MODEL), lambda i: (i, 0)),
        out_shape=jax.ShapeDtypeStruct((n, D_MODEL), F32),
        compiler_params=_cparams(("parallel",)),
        name="entry_ln",
    )(x, g.reshape(1, -1), b.reshape(1, -1))


def _rope_slab(xs, c, s):
    lane = lax.broadcasted_iota(jnp.int32, xs.shape, 1) % A_HEAD_DIM
    fwd = pltpu.roll(xs, LANES - ROT_DIM // 2, axis=1)
    bwd = pltpu.roll(xs, ROT_DIM // 2, axis=1)
    partner = jnp.where(lane < ROT_DIM // 2, fwd, bwd)
    return xs * c + partner * s


def _proj_a_kernel(h_ref, w_ref, c_ref, s_ref, g_ref,
                   q_ref, k_ref, v_ref, iq_ref, ki_ref, wi_ref):
    p = _dot(h_ref[...].astype(BF16), w_ref[...])
    c = c_ref[...]
    s = s_ref[...]
    lane = lax.broadcasted_iota(jnp.int32, c.shape, 1)
    lo = lane < A_HEAD_DIM
    n_blk = p.shape[0] // Q_BLOCK

    def put_heads(dst_ref, slab, first_head):
        for qb in range(n_blk):
            t = slab[qb * Q_BLOCK:(qb + 1) * Q_BLOCK, :].T
            for half in range(2):
                hd = first_head + half
                dst_ref[qb, :, hd * Q_BLOCK:(hd + 1) * Q_BLOCK] = (
                    t[half * A_HEAD_DIM:(half + 1) * A_HEAD_DIM, :].astype(BF16))

    for i in range(4):
        sl = slice(i * LANES, (i + 1) * LANES)
        put_heads(q_ref, _rope_slab(p[:, sl], c, s) * (A_HEAD_DIM ** -0.5), 2 * i)
    kv = p[:, 512:640]
    kv = _rope_slab(kv, jnp.where(lo, c, 1.0), jnp.where(lo, s, 0.0))
    k_ref[...] = kv[:, :A_HEAD_DIM].astype(BF16)
    v_ones = jnp.where(lo, pltpu.roll(kv, A_HEAD_DIM, axis=1), jnp.where(lane == A_HEAD_DIM, 1.0, 0.0))
    v_ref[...] = v_ones.astype(BF16)
    for i in range(2):
        sl = slice(640 + i * LANES, 640 + (i + 1) * LANES)
        put_heads(iq_ref, _rope_slab(p[:, sl], c, s), 2 * i)
    last = p[:, 896:1024]
    mu = jnp.sum(jnp.where(lo, last, 0.0), axis=-1, keepdims=True) * (1.0 / IDX_DIM)
    xc = jnp.where(lo, last - mu, 0.0)
    var = jnp.sum(xc * xc, axis=-1, keepdims=True) * (1.0 / IDX_DIM)
    kin = xc * lax.rsqrt(var + LN_EPS) * g_ref[...]
    kin = _rope_slab(kin, jnp.where(lo, c, 1.0), jnp.where(lo, s, 0.0))
    ki_ref[...] = kin[:, :IDX_DIM].astype(BF16)
    wi = pltpu.roll(last, LANES - IDX_DIM, axis=1) * (IDX_HEADS ** -0.5 * IDX_DIM ** -0.5)
    for qb in range(n_blk):
        wi_ref[qb] = wi[qb * Q_BLOCK:(qb + 1) * Q_BLOCK, :].T[0:8, :]


def _proj_a(h, w_a, cos_t, sin_t, ikg, tm=512):
    n = h.shape[0]
    row = lambda w: pl.BlockSpec((tm, w), lambda i: (i, 0))
    stacked = lambda heads: pl.BlockSpec((tm // Q_BLOCK, A_HEAD_DIM, heads * Q_BLOCK), lambda i: (i, 0, 0))
    return pl.pallas_call(
        _proj_a_kernel,
        grid=(n // tm,),
        in_specs=[row(D_MODEL),
                  pl.BlockSpec((D_MODEL, 1024), lambda i: (0, 0)),
                  row(LANES), row(LANES),
                  pl.BlockSpec((1, LANES), lambda i: (0, 0))],
        out_specs=[stacked(A_HEADS), row(A_HEAD_DIM),
                   row(LANES),
                   stacked(IDX_HEADS), row(IDX_DIM),
                   pl.BlockSpec((tm // Q_BLOCK, 8, Q_BLOCK), lambda i: (i, 0, 0))],
        out_shape=[jax.ShapeDtypeStruct((n // Q_BLOCK, A_HEAD_DIM, A_HEADS * Q_BLOCK), BF16),
                   jax.ShapeDtypeStruct((n, A_HEAD_DIM), BF16),
                   jax.ShapeDtypeStruct((n, LANES), BF16),
                   jax.ShapeDtypeStruct((n // Q_BLOCK, IDX_DIM, IDX_HEADS * Q_BLOCK), BF16),
                   jax.ShapeDtypeStruct((n, IDX_DIM), BF16),
                   jax.ShapeDtypeStruct((n // Q_BLOCK, 8, Q_BLOCK), F32)],
        compiler_params=_cparams(("parallel",)),
        name="proj_a",
    )(h, w_a, cos_t, sin_t, ikg)


KEY_CHUNK = 512
V_BLOCK = KEY_CHUNK // 2


def _dsa_kernel(q_ref, iq_ref, wi_ref, k_ref, v_ref, ki_ref, o_ref,
                key_ref, bias_ref, lg0_ref, lg1_ref, p0_ref, p1_ref, *, seq):
    j = pl.program_id(1)
    n_sel = min(TOPK_MAX, seq // 4)
    n_kc = (j * Q_BLOCK + Q_BLOCK + KEY_CHUNK - 1) // KEY_CHUNK
    qpos = j * Q_BLOCK + lax.broadcasted_iota(jnp.int32, (1, Q_BLOCK), 1)
    kk = jnp.minimum(n_sel, qpos + 1).astype(F32)
    row_pos = lax.broadcasted_iota(jnp.int32, (KEY_CHUNK, 1), 0)

    def key_rows(c):
        return pl.ds(pl.multiple_of(c * KEY_CHUNK, KEY_CHUNK), KEY_CHUNK)

    iq = iq_ref[...]
    wi = wi_ref[...]

    def score_chunk(c, carry):
        d = jnp.maximum(_dot(ki_ref[key_rows(c), :], iq), 0.0)
        score = wi[0:1, :] * d[:, 0:Q_BLOCK]
        for h in range(1, IDX_HEADS):
            score = score + wi[h:h + 1, :] * d[:, h * Q_BLOCK:(h + 1) * Q_BLOCK]
        bits = pltpu.bitcast(score + 0.0, jnp.int32)
        key = bits ^ ((bits >> 31) & 0x7FFFFFFF)
        key_ref[c] = jnp.where(c * KEY_CHUNK + row_pos <= qpos, key, INT_MIN)
        return carry

    lax.fori_loop(0, n_kc, score_chunk, 0)

    def count(pred):
        def body(c, acc):
            m = jnp.where(pred(key_ref[c]), 1.0, 0.0)
            return acc + jnp.sum(m.reshape(KEY_CHUNK // 64, 64, Q_BLOCK), axis=0)
        acc = lax.fori_loop(0, n_kc, body, jnp.zeros((64, Q_BLOCK), F32))
        return jnp.sum(acc, axis=0, keepdims=True)

    def search(i, t_u):
        cand_u = t_u | jnp.left_shift(jnp.int32(1), 31 - i)
        cand_s = cand_u ^ INT_MIN
        return jnp.where(count(lambda kc: kc >= cand_s) >= kk, cand_u, t_u)

    thr = lax.fori_loop(0, 32, search, jnp.zeros((1, Q_BLOCK), jnp.int32)) ^ INT_MIN
    c_gt = count(lambda kc: kc > thr)
    c_ge = count(lambda kc: kc >= thr)
    need = kk - c_gt
    has_ties = jnp.max(c_ge - kk) > 0.0

    @pl.when(jnp.logical_not(has_ties))
    def _():
        def fill(c, carry):
            bias_ref[c] = jnp.where(key_ref[c] >= thr, 0.0, NEG_BIG)
            return carry
        lax.fori_loop(0, n_kc, fill, 0)

    @pl.when(has_ties)
    def _():
        r_i = lax.broadcasted_iota(jnp.int32, (KEY_CHUNK, KEY_CHUNK), 0)
        c_i = lax.broadcasted_iota(jnp.int32, (KEY_CHUNK, KEY_CHUNK), 1)
        lower = jnp.where(c_i < r_i, 1.0, 0.0).astype(BF16)

        def fill(c, carry):
            kc = key_ref[c]
            eq = kc == thr
            eq_f = jnp.where(eq, 1.0, 0.0)
            pre = _dot(lower, eq_f.astype(BF16)) + carry
            sel = (kc > thr) | (eq & (pre < need))
            bias_ref[c] = jnp.where(sel, 0.0, NEG_BIG)
            return carry + jnp.sum(eq_f, axis=0, keepdims=True)

        lax.fori_loop(0, n_kc, fill, jnp.zeros((1, Q_BLOCK), F32))

    q = q_ref[...]
    cols = A_HEADS * Q_BLOCK

    half = V_BLOCK

    def half_rows(c, hf):
        return pl.ds(pl.multiple_of(c * KEY_CHUNK + hf * half, half), half)

    def logits(c, hf):
        b = bias_ref[c, hf * half:(hf + 1) * half, :]
        return _dot(k_ref[half_rows(c, hf), :], q) + jnp.concatenate([b] * A_HEADS, axis=1)

    def weighted_values(c, hf, p_in):
        return lax.dot_general(p_in[...], v_ref[half_rows(c, hf), :], (((0,), (0,)), ((), ())),
                               preferred_element_type=F32)

    pad_rows = jnp.zeros((LANES - A_HEADS, Q_BLOCK), F32)

    def rescale(acc, alpha):
        a = jnp.concatenate([alpha[:, h * Q_BLOCK:(h + 1) * Q_BLOCK] for h in range(A_HEADS)]
                            + [pad_rows], axis=0).T
        return jnp.concatenate([acc[h * Q_BLOCK:(h + 1) * Q_BLOCK] * a[:, h:h + 1]
                                for h in range(A_HEADS)], axis=0)

    def step(nxt, cur_lg, nxt_lg, prv, prv_p, cur_p, carry):
        m, alpha, acc = carry
        nxt_lg[...] = logits(*nxt)
        lg = cur_lg[...]
        m_new = jnp.maximum(m, jnp.max(lg, axis=0, keepdims=True))
        acc = rescale(acc, alpha) + weighted_values(*prv, prv_p)
        cur_p[...] = jnp.exp(lg - m_new).astype(BF16)
        return m_new, jnp.exp(m - m_new), acc

    def attn_chunk(c, carry):
        carry = step((c, 1), lg0_ref, lg1_ref, (jnp.maximum(c - 1, 0), 1), p1_ref, p0_ref, carry)
        return step((jnp.minimum(c + 1, n_kc - 1), 0), lg1_ref, lg0_ref, (c, 0), p0_ref, p1_ref, carry)

    lg0_ref[...] = logits(0, 0)
    p1_ref[...] = jnp.zeros(p1_ref.shape, BF16)
    init = (jnp.full((1, cols), -3e38, F32), jnp.ones((1, cols), F32), jnp.zeros((cols, LANES), F32))
    _, alpha, acc = lax.fori_loop(0, n_kc, attn_chunk, init)
    acc = rescale(acc, alpha) + weighted_values(n_kc - 1, 1, p1_ref)
    o = acc[:, :A_HEAD_DIM] / acc[:, A_HEAD_DIM:A_HEAD_DIM + 1]
    for h in range(A_HEADS):
        o_ref[:, h * A_HEAD_DIM:(h + 1) * A_HEAD_DIM] = o[h * Q_BLOCK:(h + 1) * Q_BLOCK].astype(BF16)


def _dsa(q, iq, wi, k, v, ki, bsz, seq):
    nq = seq // Q_BLOCK
    n_kc = seq // KEY_CHUNK
    stacked = lambda heads: pl.BlockSpec((None, A_HEAD_DIM, heads * Q_BLOCK),
                                         lambda b, j: (b * nq + j, 0, 0))
    kspec = lambda w: pl.BlockSpec((seq, w), lambda b, j: (b, 0))
    return pl.pallas_call(
        functools.partial(_dsa_kernel, seq=seq),
        grid=(bsz, nq),
        in_specs=[stacked(A_HEADS), stacked(IDX_HEADS),
                  pl.BlockSpec((None, 8, Q_BLOCK), lambda b, j: (b * nq + j, 0, 0)),
                  kspec(A_HEAD_DIM),
                  kspec(LANES),
                  kspec(IDX_DIM)],
        out_specs=pl.BlockSpec((Q_BLOCK, A_OUT), lambda b, j: (b * nq + j, 0)),
        out_shape=jax.ShapeDtypeStruct((bsz * seq, A_OUT), BF16),
        scratch_shapes=[pltpu.VMEM((n_kc, KEY_CHUNK, Q_BLOCK), jnp.int32),
                        pltpu.VMEM((n_kc, KEY_CHUNK, Q_BLOCK), F32),
                        pltpu.VMEM((KEY_CHUNK // 2, A_HEADS * Q_BLOCK), F32),
                        pltpu.VMEM((KEY_CHUNK // 2, A_HEADS * Q_BLOCK), F32),
                        pltpu.VMEM((KEY_CHUNK // 2, A_HEADS * Q_BLOCK), BF16),
                        pltpu.VMEM((KEY_CHUNK // 2, A_HEADS * Q_BLOCK), BF16)],
        compiler_params=_cparams(("parallel", "parallel")),
        name="dsa",
    )(q, iq, wi, k, v, ki)


GLA_REP = B_HEADS * N_SUB * B_KEY_DIM
GLA_COLS = 2 * GLA_REP + 2 * B_OUT + LANES


def _gla_kernel(h_ref, w_ref, wa_ref, ba_ref, ng_ref, o_ref, p_ref, g_ref, acc_ref, st_ref, *, ts):
    @pl.when(pl.program_id(1) == 0)
    def _():
        st_ref[...] = jnp.zeros_like(st_ref)

    p_ref[...] = _dot(h_ref[...].astype(BF16), w_ref[...])
    x = _dot(p_ref[:, 2 * GLA_REP + 2 * B_OUT:].astype(BF16), wa_ref[...]) + ba_ref[...]
    g_ref[...] = (jnp.minimum(x, 0.0) - jnp.log1p(jnp.exp(-jnp.abs(x)))) * (1.0 / B_GATE_TAU)

    t_i = lax.broadcasted_iota(jnp.int32, (B_CHUNK, B_CHUNK), 0)
    s_i = lax.broadcasted_iota(jnp.int32, (B_CHUNK, B_CHUNK), 1)
    tri = t_i >= s_i
    tri_f = jnp.where(tri, 1.0, 0.0)
    lane_sub = (lax.broadcasted_iota(jnp.int32, (1, GLA_REP), 1) // B_KEY_DIM) % N_SUB
    row_sub = lax.broadcasted_iota(jnp.int32, (B_CHUNK, 1), 0) // B_SUB
    q_mask = lane_sub <= row_sub
    k_mask = lane_sub == row_sub

    def chunk(c, carry):
        rows = pl.ds(pl.multiple_of(c * B_CHUNK, B_CHUNK), B_CHUNK)
        b = jnp.dot(tri_f, g_ref[rows, :], precision=HIGHEST, preferred_element_type=F32)
        ref = jnp.zeros((1, GLA_REP), F32)
        for jj in range(1, N_SUB):
            ref = ref + jnp.where(lane_sub == jj, b[jj * B_SUB - 1:jj * B_SUB, :], 0.0)
        d = b - ref
        e_q = jnp.where(q_mask, jnp.exp(jnp.minimum(d, 0.0)), 0.0)
        e_k = jnp.where(k_mask, jnp.exp(jnp.where(k_mask, -d, 0.0)), 0.0)
        b_last = b[B_CHUNK - 1:B_CHUNK, :]
        q_c = (p_ref[rows, 0:GLA_REP] * e_q).astype(BF16)
        k_raw = p_ref[rows, GLA_REP:2 * GLA_REP]
        k_c = (k_raw * e_k).astype(BF16)
        k_st = (k_raw * jnp.exp(b_last - b)).astype(BF16)
        e_last = jnp.exp(b_last)
        for hh in range(B_HEADS):
            base = hh * N_SUB * B_KEY_DIM
            q_h = q_c[:, base:base + N_SUB * B_KEY_DIM]
            k_h = k_c[:, base:base + N_SUB * B_KEY_DIM]
            att = jnp.where(tri, _dot_nt(q_h, k_h), 0.0)
            v_h = p_ref[rows, 2 * GLA_REP + hh * B_VAL_DIM:2 * GLA_REP + (hh + 1) * B_VAL_DIM].astype(BF16)
            st = st_ref[hh]
            o = _dot(att.astype(BF16), v_h) + _dot_nt(q_h[:, :B_KEY_DIM], st.astype(BF16))
            acc_ref[rows, hh * B_VAL_DIM:(hh + 1) * B_VAL_DIM] = o
            upd = _dot(v_h.T, k_st[:, base:base + B_KEY_DIM])
            st_ref[hh] = st * e_last[:, base:base + B_KEY_DIM] + upd
        return carry

    lax.fori_loop(0, ts // B_CHUNK, chunk, 0)

    r = p_ref[:, 2 * GLA_REP + B_OUT:2 * GLA_REP + 2 * B_OUT]
    gate = r * (1.0 / (1.0 + jnp.exp(-r)))
    ng = ng_ref[...]
    for hh in range(B_HEADS):
        sl = slice(hh * B_VAL_DIM, (hh + 1) * B_VAL_DIM)
        o = acc_ref[:, sl]
        y = o * lax.rsqrt(jnp.mean(o * o, axis=-1, keepdims=True) + RMS_EPS) * ng
        o_ref[:, sl] = (gate[:, sl] * y).astype(BF16)


def _gla(h, w_b, wa_rep, ba_rep, norm_g, bsz, seq, ts=512):
    ns = seq // ts
    const = lambda shape: pl.BlockSpec(shape, lambda b, i: (0, 0))
    return pl.pallas_call(
        functools.partial(_gla_kernel, ts=ts),
        grid=(bsz, ns),
        in_specs=[pl.BlockSpec((ts, D_MODEL), lambda b, i: (b * ns + i, 0)),
                  const((D_MODEL, GLA_COLS)), const((LANES, GLA_REP)),
                  const((1, GLA_REP)), const((1, B_VAL_DIM))],
        out_specs=pl.BlockSpec((ts, B_OUT), lambda b, i: (b * ns + i, 0)),
        out_shape=jax.ShapeDtypeStruct((bsz * seq, B_OUT), BF16),
        scratch_shapes=[pltpu.VMEM((ts, GLA_COLS), F32),
                        pltpu.VMEM((ts, GLA_REP), F32),
                        pltpu.VMEM((ts, B_OUT), F32),
                        pltpu.VMEM((B_HEADS, B_VAL_DIM, B_KEY_DIM), F32)],
        compiler_params=_cparams(("parallel", "arbitrary")),
        name="gla",
    )(h, w_b, wa_rep, ba_rep, norm_g)


def _gmlp_kernel(h_ref, w_ref, lg_ref, lb_ref, ws_ref, bs_ref, o_ref, *, tm):
    z = _dot(h_ref[...].astype(BF16), w_ref[...])
    z = z * (0.5 * (1.0 + jnp.tanh(np.sqrt(2.0 / np.pi) * (z + 0.044715 * (z * z * z)))))
    u = z[:, :C_OUT]
    v = _layer_norm(z[:, C_OUT:], lg_ref[...], lb_ref[...]).astype(BF16)
    t_i = lax.broadcasted_iota(jnp.int32, (C_CHUNK, C_CHUNK), 0)
    s_i = lax.broadcasted_iota(jnp.int32, (C_CHUNK, C_CHUNK), 1)
    bs = bs_ref[...]
    for g in range(C_GROUPS):
        w = jnp.where(t_i >= s_i, ws_ref[g], 0.0).astype(BF16)
        sl = slice(g * C_GROUP_DIM, (g + 1) * C_GROUP_DIM)
        for c in range(tm // C_CHUNK):
            rows = slice(c * C_CHUNK, (c + 1) * C_CHUNK)
            mixed = _dot(w, v[rows, sl]) + bs[:, g:g + 1]
            o_ref[rows, sl] = (u[rows, sl] * mixed).astype(BF16)


def _gmlp(h, w_c, ln_g, ln_b, ws, bs_t, tm=512):
    n = h.shape[0]
    return pl.pallas_call(
        functools.partial(_gmlp_kernel, tm=tm),
        grid=(n // tm,),
        in_specs=[pl.BlockSpec((tm, D_MODEL), lambda i: (i, 0)),
                  pl.BlockSpec((D_MODEL, 2 * C_OUT), lambda i: (0, 0)),
                  pl.BlockSpec((1, C_OUT), lambda i: (0, 0)),
                  pl.BlockSpec((1, C_OUT), lambda i: (0, 0)),
                  pl.BlockSpec((C_GROUPS, C_CHUNK, C_CHUNK), lambda i: (0, 0, 0)),
                  pl.BlockSpec((C_CHUNK, LANES), lambda i: (0, 0))],
        out_specs=pl.BlockSpec((tm, C_OUT), lambda i: (i, 0)),
        out_shape=jax.ShapeDtypeStruct((n, C_OUT), BF16),
        compiler_params=_cparams(("parallel",)),
        name="gmlp",
    )(h, w_c, ln_g, ln_b, ws, bs_t)


def _sigmoid(x):
    return 1.0 / (1.0 + jnp.exp(-x))


def _merge_kernel(h_ref, ya_ref, yb_ref, yc_ref, wg_ref, wa_ref, wb_ref, wc_ref, wo_ref,
                  lg_ref, lb_ref, wr_ref, br_ref, x_ref, oh_ref, route_ref):
    h = h_ref[...]
    hb = h.astype(BF16)
    acc = None
    for i, (y_ref, wbr_ref) in enumerate(((ya_ref, wa_ref), (yb_ref, wb_ref), (yc_ref, wc_ref))):
        gate = _sigmoid(_dot(hb, wg_ref[:, i * D_MODEL:(i + 1) * D_MODEL]))
        term = gate * _dot(y_ref[...], wbr_ref[...])
        acc = term if acc is None else acc + term
    mix = _dot(acc.astype(BF16), wo_ref[...])
    x1 = _layer_norm(DN_ALPHA * h + mix, lg_ref[...], lb_ref[...])
    x_ref[...] = x1

    logit = jnp.dot(x1, wr_ref[...], precision=HIGHEST, preferred_element_type=F32) + br_ref[...]
    lane = lax.broadcasted_iota(jnp.int32, logit.shape, 1)
    big = jnp.int32(LANES)
    is_g = (lane >= N_EXPERTS) & (lane < N_EXPERTS + N_GROUPS)
    gl = jnp.where(is_g, logit, -jnp.inf)
    g_max = jnp.max(gl, axis=-1, keepdims=True)
    g_idx = jnp.min(jnp.where(is_g & (gl == g_max), lane, big), axis=-1, keepdims=True) - N_EXPERTS
    g_p = 1.0 / jnp.sum(jnp.where(is_g, jnp.exp(gl - g_max), 0.0), axis=-1, keepdims=True)
    lo = g_idx * EXPERTS_PER_GROUP
    in_g = (lane >= lo) & (lane < lo + EXPERTS_PER_GROUP)
    el = jnp.where(in_g, logit, -jnp.inf)
    e1 = jnp.max(el, axis=-1, keepdims=True)
    i1 = jnp.min(jnp.where(in_g & (el == e1), lane, big), axis=-1, keepdims=True)
    el2 = jnp.where(lane == i1, -jnp.inf, el)
    e2 = jnp.max(el2, axis=-1, keepdims=True)
    i2 = jnp.min(jnp.where(in_g & (lane != i1) & (el2 == e2), lane, big), axis=-1, keepdims=True)
    z2 = jnp.exp(e2 - e1)
    den = 1.0 + z2
    oh_ref[...] = jnp.where((lane == i1) | (lane == i2), 1.0, 0.0).astype(BF16)
    route_ref[...] = jnp.where(lane == 0, i1.astype(F32),
                               jnp.where(lane == 1, i2.astype(F32),
                                         jnp.where(lane == 2, (1.0 / den) * g_p,
                                                   jnp.where(lane == 3, (z2 / den) * g_p, 0.0))))


def _merge(h, ya, yb, yc, wg, wa, wb, wc, wo, ln_g, ln_b, wr, br, tm=1024):
    n = h.shape[0]
    row = lambda w: pl.BlockSpec((tm, w), lambda i: (i, 0))
    const = lambda r, c: pl.BlockSpec((r, c), lambda i: (0, 0), pipeline_mode=pl.Buffered(1))
    return pl.pallas_call(
        _merge_kernel,
        grid=(n // tm,),
        in_specs=[row(D_MODEL), row(A_OUT), row(B_OUT), row(C_OUT),
                  const(D_MODEL, 3 * D_MODEL), const(A_OUT, D_MODEL), const(B_OUT, D_MODEL),
                  const(C_OUT, D_MODEL), const(D_MODEL, D_MODEL),
                  const(1, D_MODEL), const(1, D_MODEL), const(D_MODEL, LANES), const(1, LANES)],
        out_specs=[row(D_MODEL), row(LANES), row(LANES)],
        out_shape=[jax.ShapeDtypeStruct((n, D_MODEL), F32),
                   jax.ShapeDtypeStruct((n, LANES), BF16),
                   jax.ShapeDtypeStruct((n, LANES), F32)],
        compiler_params=_cparams(("parallel",)),
        name="merge",
    )(h, ya, yb, yc, wg, wa, wb, wc, wo, ln_g, ln_b, wr, br)


FFN_TILE = 512
MOE_TOK = 256


def _rank_kernel(oh_ref, route_ref, r_ref, cnt_ref, carry_ref):
    @pl.when(pl.program_id(0) == 0)
    def _():
        carry_ref[...] = jnp.zeros_like(carry_ref)

    oh = oh_ref[...]
    tt = oh.shape[0]
    r_i = lax.broadcasted_iota(jnp.int32, (tt, tt), 0)
    c_i = lax.broadcasted_iota(jnp.int32, (tt, tt), 1)
    lower = jnp.where(c_i < r_i, 1.0, 0.0).astype(BF16)
    rank = _dot(lower, oh) + carry_ref[...]
    route = route_ref[...]
    lane = lax.broadcasted_iota(jnp.int32, rank.shape, 1)
    i1 = route[:, 0:1].astype(jnp.int32)
    i2 = route[:, 1:2].astype(jnp.int32)
    r1 = jnp.sum(jnp.where(lane == i1, rank, 0.0), axis=-1, keepdims=True)
    r2 = jnp.sum(jnp.where(lane == i2, rank, 0.0), axis=-1, keepdims=True)
    r_ref[...] = jnp.where(lane == 0, r1, jnp.where(lane == 1, r2, 0.0))
    carry_ref[...] += jnp.sum(oh.astype(F32), axis=0, keepdims=True)
    cnt_ref[...] = carry_ref[...]


def _rank(oh, route, tt=512):
    n = oh.shape[0]
    return pl.pallas_call(
        _rank_kernel,
        grid=(n // tt,),
        in_specs=[pl.BlockSpec((tt, LANES), lambda i: (i, 0)),
                  pl.BlockSpec((tt, LANES), lambda i: (i, 0))],
        out_specs=[pl.BlockSpec((tt, LANES), lambda i: (i, 0)),
                   pl.BlockSpec((1, LANES), lambda i: (0, 0))],
        out_shape=[jax.ShapeDtypeStruct((n, LANES), F32),
                   jax.ShapeDtypeStruct((1, LANES), F32)],
        scratch_shapes=[pltpu.VMEM((1, LANES), F32)],
        compiler_params=_cparams(("arbitrary",)),
        name="moe_rank",
    )(oh, route)


def _row_copy(src_ref, src_row, dst_ref, dst_row, sem):
    return pltpu.make_async_copy(src_ref.at[pl.ds(src_row, 1), :], dst_ref.at[pl.ds(dst_row, 1), :], sem)


def _dispatch_kernel(pos1_ref, pos2_ref, x_ref, init_ref, xs_ref, sem):
    del init_ref
    base = pl.program_id(0) * MOE_TOK

    def issue(t, carry):
        _row_copy(x_ref, t, xs_ref, pos1_ref[base + t], sem).start()
        _row_copy(x_ref, t, xs_ref, pos2_ref[base + t], sem).start()
        return carry

    lax.fori_loop(0, MOE_TOK, issue, 0, unroll=8)
    for _ in range(2):
        pltpu.make_async_copy(x_ref, xs_ref.at[pl.ds(0, MOE_TOK), :], sem).wait()


def _dispatch(pos1, pos2, x1, n_slots):
    n = x1.shape[0]
    init = jnp.zeros((n_slots, D_MODEL), F32)
    return pl.pallas_call(
        _dispatch_kernel,
        grid_spec=pltpu.PrefetchScalarGridSpec(
            num_scalar_prefetch=2,
            grid=(n // MOE_TOK,),
            in_specs=[pl.BlockSpec((MOE_TOK, D_MODEL), lambda i, p1, p2: (i, 0)),
                      pl.BlockSpec(memory_space=pl.ANY)],
            out_specs=pl.BlockSpec(memory_space=pl.ANY),
            scratch_shapes=[pltpu.SemaphoreType.DMA(())]),
        out_shape=jax.ShapeDtypeStruct((n_slots, D_MODEL), F32),
        input_output_aliases={3: 0},
        compiler_params=_cparams(("arbitrary",)),
        name="moe_dispatch",
    )(pos1, pos2, x1, init)


def _ffn_kernel(te_ref, nu_ref, xs_ref, wg_ref, wu_ref, wd_ref, ys_ref):
    del te_ref
    i = pl.program_id(0)

    @pl.when(i < nu_ref[0])
    def _():
        xb = xs_ref[...].astype(BF16)
        gte = _dot(xb, wg_ref[...])
        hid = gte * _sigmoid(gte) * _dot(xb, wu_ref[...])
        ys_ref[...] = _dot(hid.astype(BF16), wd_ref[...])

    @pl.when(i >= nu_ref[0])
    def _():
        ys_ref[...] = jnp.zeros_like(ys_ref)


def _ffn(tile_expert, n_used, xs, w_gate, w_up, w_down):
    n_slots = xs.shape[0]
    wspec = lambda r, c: pl.BlockSpec((None, r, c), lambda i, te, nu: (te[i], 0, 0))
    return pl.pallas_call(
        _ffn_kernel,
        grid_spec=pltpu.PrefetchScalarGridSpec(
            num_scalar_prefetch=2,
            grid=(n_slots // FFN_TILE,),
            in_specs=[pl.BlockSpec((FFN_TILE, D_MODEL), lambda i, te, nu: (i, 0)),
                      wspec(D_MODEL, D_EXPERT), wspec(D_MODEL, D_EXPERT), wspec(D_EXPERT, D_MODEL)],
            out_specs=pl.BlockSpec((FFN_TILE, D_MODEL), lambda i, te, nu: (i, 0))),
        out_shape=jax.ShapeDtypeStruct((n_slots, D_MODEL), F32),
        compiler_params=_cparams(("arbitrary",)),
        name="moe_ffn",
    )(tile_expert, n_used, xs, w_gate, w_up, w_down)


def _combine_kernel(pos1_ref, pos2_ref, x_ref, route_ref, lg_ref, lb_ref, ys_ref, o_ref, buf_ref, sem):
    base = pl.program_id(0) * MOE_TOK

    def issue(t, carry):
        _row_copy(ys_ref, pos1_ref[base + t], buf_ref.at[0], t, sem).start()
        _row_copy(ys_ref, pos2_ref[base + t], buf_ref.at[1], t, sem).start()
        return carry

    lax.fori_loop(0, MOE_TOK, issue, 0, unroll=8)
    for s in range(2):
        pltpu.make_async_copy(ys_ref.at[pl.ds(0, MOE_TOK), :], buf_ref.at[s], sem).wait()
    route = route_ref[...]
    ffn = route[:, 2:3] * buf_ref[0] + route[:, 3:4] * buf_ref[1]
    o_ref[...] = _layer_norm(DN_ALPHA * x_ref[...] + ffn, lg_ref[...], lb_ref[...])


def _combine(pos1, pos2, x1, route, ln_g, ln_b, ys):
    n = x1.shape[0]
    row = lambda w: pl.BlockSpec((MOE_TOK, w), lambda i, p1, p2: (i, 0))
    const = lambda w: pl.BlockSpec((1, w), lambda i, p1, p2: (0, 0))
    return pl.pallas_call(
        _combine_kernel,
        grid_spec=pltpu.PrefetchScalarGridSpec(
            num_scalar_prefetch=2,
            grid=(n // MOE_TOK,),
            in_specs=[row(D_MODEL), row(LANES), const(D_MODEL), const(D_MODEL),
                      pl.BlockSpec(memory_space=pl.ANY)],
            out_specs=row(D_MODEL),
            scratch_shapes=[pltpu.VMEM((2, MOE_TOK, D_MODEL), F32),
                            pltpu.SemaphoreType.DMA(())]),
        out_shape=jax.ShapeDtypeStruct((n, D_MODEL), F32),
        compiler_params=_cparams(("arbitrary",)),
        name="moe_combine",
    )(pos1, pos2, x1, route, ln_g, ln_b, ys)


def _moe(x1, oh, route, w_gate, w_up, w_down, ln_g, ln_b):
    n = x1.shape[0]
    n_slots = TOP_K_INNER * n + N_EXPERTS * FFN_TILE
    rank, cnt = _rank(oh, route)
    cnt = cnt[0, :N_EXPERTS].astype(jnp.int32)
    padded = (cnt + FFN_TILE - 1) // FFN_TILE * FFN_TILE
    ends = jnp.cumsum(padded)
    offs = ends - padded
    experts = jnp.arange(N_EXPERTS, dtype=jnp.int32)
    i1 = route[:, 0].astype(jnp.int32)
    i2 = route[:, 1].astype(jnp.int32)
    off_of = lambda idx: jnp.sum(jnp.where(idx[:, None] == experts[None, :], offs[None, :], 0), axis=1)
    pos1 = off_of(i1) + rank[:, 0].astype(jnp.int32)
    pos2 = off_of(i2) + rank[:, 1].astype(jnp.int32)
    tile_start = jnp.arange(n_slots // FFN_TILE, dtype=jnp.int32) * FFN_TILE
    tile_expert = jnp.minimum(jnp.sum(tile_start[:, None] >= ends[None, :], axis=1), N_EXPERTS - 1)
    n_used = (ends[-1] // FFN_TILE).reshape(1)

    xs = _dispatch(pos1, pos2, x1, n_slots)
    ys = _ffn(tile_expert.astype(jnp.int32), n_used.astype(jnp.int32), xs, w_gate, w_up, w_down)
    return _combine(pos1, pos2, x1, route, ln_g, ln_b, ys)


def _rope_tables(positions):
    inv_freq = ROPE_THETA ** (-jnp.arange(0, ROT_DIM, 2, dtype=F32) / ROT_DIM)
    ang = positions.astype(F32).reshape(-1, 1) * inv_freq
    cos, sin = jnp.cos(ang), jnp.sin(ang)
    n = ang.shape[0]
    half = ROT_DIM // 2
    c64 = jnp.concatenate([cos, cos, jnp.ones((n, A_HEAD_DIM - ROT_DIM), F32)], axis=1)
    s64 = jnp.concatenate([-sin, sin, jnp.zeros((n, A_HEAD_DIM - ROT_DIM), F32)], axis=1)
    del half
    return jnp.tile(c64, (1, 2)), jnp.tile(s64, (1, 2))


def _rep_cols(w):
    r = w.shape[0]
    w4 = w.reshape(r, B_HEADS, 1, B_KEY_DIM)
    return jnp.broadcast_to(w4, (r, B_HEADS, N_SUB, B_KEY_DIM)).reshape(r, GLA_REP)


def _split_w_in(w):
    widths = (A_OUT, A_HEAD_DIM, A_HEAD_DIM, IDX_HEADS * IDX_DIM, IDX_DIM, IDX_HEADS,
              B_HEADS * B_KEY_DIM, B_HEADS * B_KEY_DIM, B_OUT, B_GATE_RANK, B_OUT,
              2 * C_OUT, D_MODEL, D_MODEL, D_MODEL)
    pts = np.cumsum((0,) + widths)
    return [w[:, int(pts[i]):int(pts[i + 1])] for i in range(len(widths))]


def kernel(x, positions, ln_in_g, ln_in_b, w_in, idx_k_g, gla_wa2, gla_ba, gla_norm_g, gm_ln_g, gm_ln_b, gm_ws, gm_bs, w_branch_a, w_branch_b, w_branch_c, w_out, ln1_g, ln1_b, w_rg, b_rg, w_re, b_re, w_gate, w_up, w_down, ln2_g, ln2_b):
    bsz, seq, d = x.shape
    n = bsz * seq
    cos_t, sin_t = _rope_tables(positions)
    h = _entry_ln(x.reshape(n, d), ln_in_g, ln_in_b)
    for l in range(DEPTH):
        (a_q, a_k, a_v, i_q, i_k, i_w, b_q, b_k, b_v, b_glr, b_r, c_uv,
         g_a, g_b, g_c) = _split_w_in(w_in[l])
        zpad = lambda c: jnp.zeros((d, c), F32)
        w_a = jnp.concatenate([a_q, a_k, a_v, i_q, i_k, i_w, zpad(LANES - IDX_DIM - IDX_HEADS)],
                              axis=1).astype(BF16)
        w_b = jnp.concatenate([_rep_cols(b_q * (B_KEY_DIM ** -0.5)), _rep_cols(b_k), b_v, b_r,
                               b_glr, zpad(LANES - B_GATE_RANK)], axis=1).astype(BF16)
        wa_rep = jnp.concatenate([_rep_cols(gla_wa2[l]),
                                  jnp.zeros((LANES - B_GATE_RANK, GLA_REP), F32)], axis=0).astype(BF16)
        ba_rep = _rep_cols(gla_ba[l].reshape(1, -1))
        w_g = jnp.concatenate([g_a, g_b, g_c], axis=1).astype(BF16)
        ikg = jnp.concatenate([idx_k_g[l], jnp.zeros((LANES - IDX_DIM,), F32)]).reshape(1, LANES)
        bs_t = jnp.concatenate([gm_bs[l].T, jnp.zeros((C_CHUNK, LANES - C_GROUPS), F32)], axis=1)
        w_r = jnp.concatenate([w_re[l], w_rg[l], zpad(LANES - N_GROUPS - N_EXPERTS)], axis=1)
        b_r_all = jnp.concatenate([b_re[l], b_rg[l],
                                   jnp.zeros((LANES - N_GROUPS - N_EXPERTS,), F32)]).reshape(1, LANES)

        q, k, v, iq, ki, wi = _proj_a(h, w_a, cos_t, sin_t, ikg)
        y_a = _dsa(q, iq, wi, k, v, ki, bsz, seq)
        y_b = _gla(h, w_b, wa_rep, ba_rep, gla_norm_g[l].reshape(1, -1), bsz, seq)
        y_c = _gmlp(h, c_uv.astype(BF16), gm_ln_g[l].reshape(1, -1), gm_ln_b[l].reshape(1, -1),
                    gm_ws[l], bs_t)
        x1, oh, route = _merge(h, y_a, y_b, y_c, w_g,
                               w_branch_a[l].astype(BF16), w_branch_b[l].astype(BF16),
                               w_branch_c[l].astype(BF16), w_out[l].astype(BF16),
                               ln1_g[l].reshape(1, -1), ln1_b[l].reshape(1, -1), w_r, b_r_all)
        h = _moe(x1, oh, route, w_gate[l].astype(BF16), w_up[l].astype(BF16),
                 w_down[l].astype(BF16), ln2_g[l].reshape(1, -1), ln2_b[l].reshape(1, -1))
    return h.reshape(bsz, seq, d)
```

```python
import functools

import numpy as np
import jax
import jax.numpy as jnp
from jax import lax
from jax.experimental import pallas as pl
from jax.experimental.pallas import tpu as pltpu

F32 = jnp.float32
BF16 = jnp.bfloat16

D_MODEL = 1024
DEPTH = 2
A_HEADS = 8
A_HEAD_DIM = 64
A_OUT = 512
IDX_HEADS = 4
IDX_DIM = 64
TOPK_MAX = 256
Q_BLOCK = 128
ROPE_THETA = 500000.0
ROT_DIM = 16
B_HEADS = 4
B_KEY_DIM = 64
B_VAL_DIM = 128
B_OUT = 512
B_GATE_RANK = 16
B_GATE_TAU = 16.0
B_CHUNK = 64
B_SUB = 16
N_SUB = B_CHUNK // B_SUB
C_GROUPS = 4
C_GROUP_DIM = 128
C_OUT = 512
C_CHUNK = 128
N_GROUPS = 4
EXPERTS_PER_GROUP = 8
N_EXPERTS = 32
D_EXPERT = 256
TOP_K_INNER = 2
DN_ALPHA = (2 * DEPTH) ** 0.25
LN_EPS = 1e-5
RMS_EPS = 1e-6

LANES = 128
NEG_BIG = -1e30
INT_MIN = -2 ** 31
VMEM_LIMIT = 56 * 1024 * 1024

HIGHEST = lax.Precision.HIGHEST


def _cparams(sem):
    return pltpu.CompilerParams(dimension_semantics=sem, vmem_limit_bytes=VMEM_LIMIT)


def _layer_norm(x, g, b):
    mu = jnp.mean(x, axis=-1, keepdims=True)
    xc = x - mu
    var = jnp.mean(xc * xc, axis=-1, keepdims=True)
    return xc * lax.rsqrt(var + LN_EPS) * g + b


def _dot(a, b):
    return jnp.dot(a, b, preferred_element_type=F32)


def _dot_nt(a, b):
    return lax.dot_general(a, b, (((1,), (1,)), ((), ())), preferred_element_type=F32)


def _ln_kernel(x_ref, g_ref, b_ref, o_ref):
    o_ref[...] = _layer_norm(x_ref[...], g_ref[...], b_ref[...])


def _entry_ln(x, g, b, tm=1024):
    n = x.shape[0]
    return pl.pallas_call(
        _ln_kernel,
        grid=(n // tm,),
        in_specs=[pl.BlockSpec((tm, D_MODEL), lambda i: (i, 0)),
                  pl.BlockSpec((1, D_MODEL), lambda i: (0, 0)),
                  pl.BlockSpec((1, D_MODEL), lambda i: (0, 0))],
        out_specs=pl.BlockSpec((tm, D_MODEL), lambda i: (i, 0)),
        out_shape=jax.ShapeDtypeStruct((n, D_MODEL), F32),
        compiler_params=_cparams(("parallel",)),
        name="entry_ln",
    )(x, g.reshape(1, -1), b.reshape(1, -1))


def _rope_slab(xs, c, s):
    lane = lax.broadcasted_iota(jnp.int32, xs.shape, 1) % A_HEAD_DIM
    fwd = pltpu.roll(xs, LANES - ROT_DIM // 2, axis=1)
    bwd = pltpu.roll(xs, ROT_DIM // 2, axis=1)
    partner = jnp.where(lane < ROT_DIM // 2, fwd, bwd)
    return xs * c + partner * s


def _proj_a_kernel(h_ref, w_ref, c_ref, s_ref, g_ref,
                   q_ref, k_ref, v_ref, iq_ref, ki_ref, wi_ref):
    p = _dot(h_ref[...].astype(BF16), w_ref[...])
    c = c_ref[...]
    s = s_ref[...]
    lane = lax.broadcasted_iota(jnp.int32, c.shape, 1)
    lo = lane < A_HEAD_DIM
    n_blk = p.shape[0] // Q_BLOCK

    def put_heads(dst_ref, slab, first_head):
        for half in range(2):
            hd = first_head + half
            cols = slab[:, half * A_HEAD_DIM:(half + 1) * A_HEAD_DIM].astype(BF16)
            for qb in range(n_blk):
                dst_ref[qb, hd * Q_BLOCK:(hd + 1) * Q_BLOCK, :] = cols[qb * Q_BLOCK:(qb + 1) * Q_BLOCK, :]

    for i in range(4):
        sl = slice(i * LANES, (i + 1) * LANES)
        put_heads(q_ref, _rope_slab(p[:, sl], c, s) * (A_HEAD_DIM ** -0.5), 2 * i)
    kv = p[:, 512:640]
    kv = _rope_slab(kv, jnp.where(lo, c, 1.0), jnp.where(lo, s, 0.0))
    k_ref[...] = kv[:, :A_HEAD_DIM].astype(BF16)
    v_ones = jnp.where(lo, pltpu.roll(kv, A_HEAD_DIM, axis=1), jnp.where(lane == A_HEAD_DIM, 1.0, 0.0))
    v_ref[...] = v_ones.astype(BF16)
    for i in range(2):
        sl = slice(640 + i * LANES, 640 + (i + 1) * LANES)
        put_heads(iq_ref, _rope_slab(p[:, sl], c, s), 2 * i)
    last = p[:, 896:1024]
    mu = jnp.sum(jnp.where(lo, last, 0.0), axis=-1, keepdims=True) * (1.0 / IDX_DIM)
    xc = jnp.where(lo, last - mu, 0.0)
    var = jnp.sum(xc * xc, axis=-1, keepdims=True) * (1.0 / IDX_DIM)
    kin = xc * lax.rsqrt(var + LN_EPS) * g_ref[...]
    kin = _rope_slab(kin, jnp.where(lo, c, 1.0), jnp.where(lo, s, 0.0))
    ki_ref[...] = kin[:, :IDX_DIM].astype(BF16)
    wi = pltpu.roll(last, LANES - IDX_DIM, axis=1) * (IDX_HEADS ** -0.5 * IDX_DIM ** -0.5)
    for qb in range(n_blk):
        wi_ref[qb] = wi[qb * Q_BLOCK:(qb + 1) * Q_BLOCK, :].T[0:8, :]


def _proj_a(h, w_a, cos_t, sin_t, ikg, tm=512):
    n = h.shape[0]
    row = lambda w: pl.BlockSpec((tm, w), lambda i: (i, 0))
    stacked = lambda heads: pl.BlockSpec((tm // Q_BLOCK, heads * Q_BLOCK, A_HEAD_DIM), lambda i: (i, 0, 0))
    return pl.pallas_call(
        _proj_a_kernel,
        grid=(n // tm,),
        in_specs=[row(D_MODEL),
                  pl.BlockSpec((D_MODEL, 1024), lambda i: (0, 0)),
                  row(LANES), row(LANES),
                  pl.BlockSpec((1, LANES), lambda i: (0, 0))],
        out_specs=[stacked(A_HEADS), row(A_HEAD_DIM),
                   row(LANES),
                   stacked(IDX_HEADS), row(IDX_DIM),
                   pl.BlockSpec((tm // Q_BLOCK, 8, Q_BLOCK), lambda i: (i, 0, 0))],
        out_shape=[jax.ShapeDtypeStruct((n // Q_BLOCK, A_HEADS * Q_BLOCK, A_HEAD_DIM), BF16),
                   jax.ShapeDtypeStruct((n, A_HEAD_DIM), BF16),
                   jax.ShapeDtypeStruct((n, LANES), BF16),
                   jax.ShapeDtypeStruct((n // Q_BLOCK, IDX_HEADS * Q_BLOCK, IDX_DIM), BF16),
                   jax.ShapeDtypeStruct((n, IDX_DIM), BF16),
                   jax.ShapeDtypeStruct((n // Q_BLOCK, 8, Q_BLOCK), F32)],
        compiler_params=_cparams(("parallel",)),
        name="proj_a",
    )(h, w_a, cos_t, sin_t, ikg)


KEY_CHUNK = 512


def _dsa_kernel(q_ref, iq_ref, wi_ref, k_ref, v_ref, ki_ref, o_ref,
                key_ref, bias_ref, *, seq):
    j = pl.program_id(1)
    n_sel = min(TOPK_MAX, seq // 4)
    n_kc = (j * Q_BLOCK + Q_BLOCK + KEY_CHUNK - 1) // KEY_CHUNK
    qpos = j * Q_BLOCK + lax.broadcasted_iota(jnp.int32, (1, Q_BLOCK), 1)
    kk = jnp.minimum(n_sel, qpos + 1).astype(F32)
    row_pos = lax.broadcasted_iota(jnp.int32, (KEY_CHUNK, 1), 0)

    def key_rows(c):
        return pl.ds(pl.multiple_of(c * KEY_CHUNK, KEY_CHUNK), KEY_CHUNK)

    iq = iq_ref[...]
    wi = wi_ref[...]

    def score_chunk(c, carry):
        d = jnp.maximum(_dot_nt(ki_ref[key_rows(c), :], iq), 0.0)
        score = wi[0:1, :] * d[:, 0:Q_BLOCK]
        for h in range(1, IDX_HEADS):
            score = score + wi[h:h + 1, :] * d[:, h * Q_BLOCK:(h + 1) * Q_BLOCK]
        bits = pltpu.bitcast(score + 0.0, jnp.int32)
        key = bits ^ ((bits >> 31) & 0x7FFFFFFF)
        key_ref[c] = jnp.where(c * KEY_CHUNK + row_pos <= qpos, key, INT_MIN)
        return carry

    lax.fori_loop(0, n_kc, score_chunk, 0)

    def count(pred):
        def body(c, acc):
            m = jnp.where(pred(key_ref[c]), 1.0, 0.0)
            return acc + jnp.sum(m.reshape(KEY_CHUNK // 64, 64, Q_BLOCK), axis=0)
        acc = lax.fori_loop(0, n_kc, body, jnp.zeros((64, Q_BLOCK), F32))
        return jnp.sum(acc, axis=0, keepdims=True)

    def search(i, t_u):
        cand_u = t_u | jnp.left_shift(jnp.int32(1), 31 - i)
        cand_s = cand_u ^ INT_MIN
        return jnp.where(count(lambda kc: kc >= cand_s) >= kk, cand_u, t_u)

    thr = lax.fori_loop(0, 32, search, jnp.zeros((1, Q_BLOCK), jnp.int32)) ^ INT_MIN
    c_gt = count(lambda kc: kc > thr)
    c_ge = count(lambda kc: kc >= thr)
    need = kk - c_gt
    has_ties = jnp.max(c_ge - kk) > 0.0

    @pl.when(jnp.logical_not(has_ties))
    def _():
        def fill(c, carry):
            bias_ref[c] = jnp.where(key_ref[c] >= thr, 0.0, NEG_BIG)
            return carry
        lax.fori_loop(0, n_kc, fill, 0)

    @pl.when(has_ties)
    def _():
        r_i = lax.broadcasted_iota(jnp.int32, (KEY_CHUNK, KEY_CHUNK), 0)
        c_i = lax.broadcasted_iota(jnp.int32, (KEY_CHUNK, KEY_CHUNK), 1)
        lower = jnp.where(c_i < r_i, 1.0, 0.0).astype(BF16)

        def fill(c, carry):
            kc = key_ref[c]
            eq = kc == thr
            eq_f = jnp.where(eq, 1.0, 0.0)
            pre = _dot(lower, eq_f.astype(BF16)) + carry
            sel = (kc > thr) | (eq & (pre < need))
            bias_ref[c] = jnp.where(sel, 0.0, NEG_BIG)
            return carry + jnp.sum(eq_f, axis=0, keepdims=True)

        lax.fori_loop(0, n_kc, fill, jnp.zeros((1, Q_BLOCK), F32))

    q = q_ref[...]
    cols = A_HEADS * Q_BLOCK

    def attn_chunk(c, carry):
        m, acc = carry
        rows = key_rows(c)
        lg = _dot_nt(k_ref[rows, :], q) + jnp.concatenate([bias_ref[c]] * A_HEADS, axis=1)
        m_new = jnp.maximum(m, jnp.max(lg, axis=0, keepdims=True))
        p = jnp.exp(lg - m_new).astype(BF16)
        pv = lax.dot_general(v_ref[rows, :], p, (((0,), (0,)), ((), ())),
                             preferred_element_type=F32)
        return m_new, jnp.exp(m - m_new) * acc + pv

    init = (jnp.full((1, cols), -3e38, F32), jnp.zeros((LANES, cols), F32))
    _, acc = lax.fori_loop(0, n_kc, attn_chunk, init)
    o = acc[:A_HEAD_DIM] / acc[A_HEAD_DIM:A_HEAD_DIM + 1]
    for h in range(A_HEADS):
        o_ref[:, h * A_HEAD_DIM:(h + 1) * A_HEAD_DIM] = (
            o[:, h * Q_BLOCK:(h + 1) * Q_BLOCK].T.astype(BF16))


def _dsa(q, iq, wi, k, v, ki, bsz, seq):
    nq = seq // Q_BLOCK
    n_kc = seq // KEY_CHUNK
    stacked = lambda heads: pl.BlockSpec((None, heads * Q_BLOCK, A_HEAD_DIM),
                                         lambda b, j: (b * nq + j, 0, 0))
    kspec = lambda w: pl.BlockSpec((seq, w), lambda b, j: (b, 0))
    return pl.pallas_call(
        functools.partial(_dsa_kernel, seq=seq),
        grid=(bsz, nq),
        in_specs=[stacked(A_HEADS), stacked(IDX_HEADS),
                  pl.BlockSpec((None, 8, Q_BLOCK), lambda b, j: (b * nq + j, 0, 0)),
                  kspec(A_HEAD_DIM),
                  kspec(LANES),
                  kspec(IDX_DIM)],
        out_specs=pl.BlockSpec((Q_BLOCK, A_OUT), lambda b, j: (b * nq + j, 0)),
        out_shape=jax.ShapeDtypeStruct((bsz * seq, A_OUT), BF16),
        scratch_shapes=[pltpu.VMEM((n_kc, KEY_CHUNK, Q_BLOCK), jnp.int32),
                        pltpu.VMEM((n_kc, KEY_CHUNK, Q_BLOCK), F32)],
        compiler_params=_cparams(("parallel", "parallel")),
        name="dsa",
    )(q, iq, wi, k, v, ki)


GLA_REP = B_HEADS * N_SUB * B_KEY_DIM
GLA_COLS = 2 * GLA_REP + 2 * B_OUT + LANES


def _gla_kernel(h_ref, w_ref, wa_ref, ba_ref, ng_ref, o_ref, p_ref, g_ref, acc_ref, st_ref, *, ts):
    @pl.when(pl.program_id(1) == 0)
    def _():
        st_ref[...] = jnp.zeros_like(st_ref)

    p_ref[...] = _dot(h_ref[...].astype(BF16), w_ref[...])
    x = _dot(p_ref[:, 2 * GLA_REP + 2 * B_OUT:].astype(BF16), wa_ref[...]) + ba_ref[...]
    g_ref[...] = (jnp.minimum(x, 0.0) - jnp.log1p(jnp.exp(-jnp.abs(x)))) * (1.0 / B_GATE_TAU)

    t_i = lax.broadcasted_iota(jnp.int32, (B_CHUNK, B_CHUNK), 0)
    s_i = lax.broadcasted_iota(jnp.int32, (B_CHUNK, B_CHUNK), 1)
    tri = t_i >= s_i
    tri_f = jnp.where(tri, 1.0, 0.0)
    lane_sub = (lax.broadcasted_iota(jnp.int32, (1, GLA_REP), 1) // B_KEY_DIM) % N_SUB
    row_sub = lax.broadcasted_iota(jnp.int32, (B_CHUNK, 1), 0) // B_SUB
    q_mask = lane_sub <= row_sub
    k_mask = lane_sub == row_sub

    def chunk(c, carry):
        rows = pl.ds(pl.multiple_of(c * B_CHUNK, B_CHUNK), B_CHUNK)
        b = jnp.dot(tri_f, g_ref[rows, :], precision=HIGHEST, preferred_element_type=F32)
        ref = jnp.zeros((1, GLA_REP), F32)
        for jj in range(1, N_SUB):
            ref = ref + jnp.where(lane_sub == jj, b[jj * B_SUB - 1:jj * B_SUB, :], 0.0)
        d = b - ref
        e_q = jnp.where(q_mask, jnp.exp(jnp.minimum(d, 0.0)), 0.0)
        e_k = jnp.where(k_mask, jnp.exp(jnp.where(k_mask, -d, 0.0)), 0.0)
        b_last = b[B_CHUNK - 1:B_CHUNK, :]
        q_c = (p_ref[rows, 0:GLA_REP] * e_q).astype(BF16)
        k_raw = p_ref[rows, GLA_REP:2 * GLA_REP]
        k_c = (k_raw * e_k).astype(BF16)
        k_st = (k_raw * jnp.exp(b_last - b)).astype(BF16)
        e_last = jnp.exp(b_last)
        for hh in range(B_HEADS):
            base = hh * N_SUB * B_KEY_DIM
            q_h = q_c[:, base:base + N_SUB * B_KEY_DIM]
            k_h = k_c[:, base:base + N_SUB * B_KEY_DIM]
            att = jnp.where(tri, _dot_nt(q_h, k_h), 0.0)
            v_h = p_ref[rows, 2 * GLA_REP + hh * B_VAL_DIM:2 * GLA_REP + (hh + 1) * B_VAL_DIM].astype(BF16)
            st = st_ref[hh]
            o = _dot(att.astype(BF16), v_h) + _dot_nt(q_h[:, :B_KEY_DIM], st.astype(BF16))
            acc_ref[rows, hh * B_VAL_DIM:(hh + 1) * B_VAL_DIM] = o
            upd = _dot(v_h.T, k_st[:, base:base + B_KEY_DIM])
            st_ref[hh] = st * e_last[:, base:base + B_KEY_DIM] + upd
        return carry

    lax.fori_loop(0, ts // B_CHUNK, chunk, 0)

    r = p_ref[:, 2 * GLA_REP + B_OUT:2 * GLA_REP + 2 * B_OUT]
    gate = r * (1.0 / (1.0 + jnp.exp(-r)))
    ng = ng_ref[...]
    for hh in range(B_HEADS):
        sl = slice(hh * B_VAL_DIM, (hh + 1) * B_VAL_DIM)
        o = acc_ref[:, sl]
        y = o * lax.rsqrt(jnp.mean(o * o, axis=-1, keepdims=True) + RMS_EPS) * ng
        o_ref[:, sl] = (gate[:, sl] * y).astype(BF16)


def _gla(h, w_b, wa_rep, ba_rep, norm_g, bsz, seq, ts=512):
    ns = seq // ts
    const = lambda shape: pl.BlockSpec(shape, lambda b, i: (0, 0))
    return pl.pallas_call(
        functools.partial(_gla_kernel, ts=ts),
        grid=(bsz, ns),
        in_specs=[pl.BlockSpec((ts, D_MODEL), lambda b, i: (b * ns + i, 0)),
                  const((D_MODEL, GLA_COLS)), const((LANES, GLA_REP)),
                  const((1, GLA_REP)), const((1, B_VAL_DIM))],
        out_specs=pl.BlockSpec((ts, B_OUT), lambda b, i: (b * ns + i, 0)),
        out_shape=jax.ShapeDtypeStruct((bsz * seq, B_OUT), BF16),
        scratch_shapes=[pltpu.VMEM((ts, GLA_COLS), F32),
                        pltpu.VMEM((ts, GLA_REP), F32),
                        pltpu.VMEM((ts, B_OUT), F32),
                        pltpu.VMEM((B_HEADS, B_VAL_DIM, B_KEY_DIM), F32)],
        compiler_params=_cparams(("parallel", "arbitrary")),
        name="gla",
    )(h, w_b, wa_rep, ba_rep, norm_g)


def _gmlp_kernel(h_ref, w_ref, lg_ref, lb_ref, ws_ref, bs_ref, o_ref, *, tm):
    z = _dot(h_ref[...].astype(BF16), w_ref[...])
    z = z * (0.5 * (1.0 + jnp.tanh(np.sqrt(2.0 / np.pi) * (z + 0.044715 * (z * z * z)))))
    u = z[:, :C_OUT]
    v = _layer_norm(z[:, C_OUT:], lg_ref[...], lb_ref[...]).astype(BF16)
    t_i = lax.broadcasted_iota(jnp.int32, (C_CHUNK, C_CHUNK), 0)
    s_i = lax.broadcasted_iota(jnp.int32, (C_CHUNK, C_CHUNK), 1)
    bs = bs_ref[...]
    for g in range(C_GROUPS):
        w = jnp.where(t_i >= s_i, ws_ref[g], 0.0).astype(BF16)
        sl = slice(g * C_GROUP_DIM, (g + 1) * C_GROUP_DIM)
        for c in range(tm // C_CHUNK):
            rows = slice(c * C_CHUNK, (c + 1) * C_CHUNK)
            mixed = _dot(w, v[rows, sl]) + bs[:, g:g + 1]
            o_ref[rows, sl] = (u[rows, sl] * mixed).astype(BF16)


def _gmlp(h, w_c, ln_g, ln_b, ws, bs_t, tm=512):
    n = h.shape[0]
    return pl.pallas_call(
        functools.partial(_gmlp_kernel, tm=tm),
        grid=(n // tm,),
        in_specs=[pl.BlockSpec((tm, D_MODEL), lambda i: (i, 0)),
                  pl.BlockSpec((D_MODEL, 2 * C_OUT), lambda i: (0, 0)),
                  pl.BlockSpec((1, C_OUT), lambda i: (0, 0)),
                  pl.BlockSpec((1, C_OUT), lambda i: (0, 0)),
                  pl.BlockSpec((C_GROUPS, C_CHUNK, C_CHUNK), lambda i: (0, 0, 0)),
                  pl.BlockSpec((C_CHUNK, LANES), lambda i: (0, 0))],
        out_specs=pl.BlockSpec((tm, C_OUT), lambda i: (i, 0)),
        out_shape=jax.ShapeDtypeStruct((n, C_OUT), BF16),
        compiler_params=_cparams(("parallel",)),
        name="gmlp",
    )(h, w_c, ln_g, ln_b, ws, bs_t)


def _sigmoid(x):
    return 1.0 / (1.0 + jnp.exp(-x))


def _merge_kernel(h_ref, ya_ref, yb_ref, yc_ref, wg_ref, wa_ref, wb_ref, wc_ref, wo_ref,
                  lg_ref, lb_ref, wr_ref, br_ref, x_ref, oh_ref, route_ref):
    h = h_ref[...]
    hb = h.astype(BF16)
    acc = None
    for i, (y_ref, wbr_ref) in enumerate(((ya_ref, wa_ref), (yb_ref, wb_ref), (yc_ref, wc_ref))):
        gate = _sigmoid(_dot(hb, wg_ref[:, i * D_MODEL:(i + 1) * D_MODEL]))
        term = gate * _dot(y_ref[...], wbr_ref[...])
        acc = term if acc is None else acc + term
    mix = _dot(acc.astype(BF16), wo_ref[...])
    x1 = _layer_norm(DN_ALPHA * h + mix, lg_ref[...], lb_ref[...])
    x_ref[...] = x1

    logit = jnp.dot(x1, wr_ref[...], precision=HIGHEST, preferred_element_type=F32) + br_ref[...]
    lane = lax.broadcasted_iota(jnp.int32, logit.shape, 1)
    big = jnp.int32(LANES)
    is_g = (lane >= N_EXPERTS) & (lane < N_EXPERTS + N_GROUPS)
    gl = jnp.where(is_g, logit, -jnp.inf)
    g_max = jnp.max(gl, axis=-1, keepdims=True)
    g_idx = jnp.min(jnp.where(is_g & (gl == g_max), lane, big), axis=-1, keepdims=True) - N_EXPERTS
    g_p = 1.0 / jnp.sum(jnp.where(is_g, jnp.exp(gl - g_max), 0.0), axis=-1, keepdims=True)
    lo = g_idx * EXPERTS_PER_GROUP
    in_g = (lane >= lo) & (lane < lo + EXPERTS_PER_GROUP)
    el = jnp.where(in_g, logit, -jnp.inf)
    e1 = jnp.max(el, axis=-1, keepdims=True)
    i1 = jnp.min(jnp.where(in_g & (el == e1), lane, big), axis=-1, keepdims=True)
    el2 = jnp.where(lane == i1, -jnp.inf, el)
    e2 = jnp.max(el2, axis=-1, keepdims=True)
    i2 = jnp.min(jnp.where(in_g & (lane != i1) & (el2 == e2), lane, big), axis=-1, keepdims=True)
    z2 = jnp.exp(e2 - e1)
    den = 1.0 + z2
    oh_ref[...] = jnp.where((lane == i1) | (lane == i2), 1.0, 0.0).astype(BF16)
    route_ref[...] = jnp.where(lane == 0, i1.astype(F32),
                               jnp.where(lane == 1, i2.astype(F32),
                                         jnp.where(lane == 2, (1.0 / den) * g_p,
                                                   jnp.where(lane == 3, (z2 / den) * g_p, 0.0))))


def _merge(h, ya, yb, yc, wg, wa, wb, wc, wo, ln_g, ln_b, wr, br, tm=1024):
    n = h.shape[0]
    row = lambda w: pl.BlockSpec((tm, w), lambda i: (i, 0))
    const = lambda r, c: pl.BlockSpec((r, c), lambda i: (0, 0), pipeline_mode=pl.Buffered(1))
    return pl.pallas_call(
        _merge_kernel,
        grid=(n // tm,),
        in_specs=[row(D_MODEL), row(A_OUT), row(B_OUT), row(C_OUT),
                  const(D_MODEL, 3 * D_MODEL), const(A_OUT, D_MODEL), const(B_OUT, D_MODEL),
                  const(C_OUT, D_MODEL), const(D_MODEL, D_MODEL),
                  const(1, D_MODEL), const(1, D_MODEL), const(D_MODEL, LANES), const(1, LANES)],
        out_specs=[row(D_MODEL), row(LANES), row(LANES)],
        out_shape=[jax.ShapeDtypeStruct((n, D_MODEL), F32),
                   jax.ShapeDtypeStruct((n, LANES), BF16),
                   jax.ShapeDtypeStruct((n, LANES), F32)],
        compiler_params=_cparams(("parallel",)),
        name="merge",
    )(h, ya, yb, yc, wg, wa, wb, wc, wo, ln_g, ln_b, wr, br)


FFN_TILE = 512
MOE_TOK = 256


def _rank_kernel(oh_ref, route_ref, r_ref, cnt_ref, carry_ref):
    @pl.when(pl.program_id(0) == 0)
    def _():
        carry_ref[...] = jnp.zeros_like(carry_ref)

    oh = oh_ref[...]
    tt = oh.shape[0]
    r_i = lax.broadcasted_iota(jnp.int32, (tt, tt), 0)
    c_i = lax.broadcasted_iota(jnp.int32, (tt, tt), 1)
    lower = jnp.where(c_i < r_i, 1.0, 0.0).astype(BF16)
    rank = _dot(lower, oh) + carry_ref[...]
    route = route_ref[...]
    lane = lax.broadcasted_iota(jnp.int32, rank.shape, 1)
    i1 = route[:, 0:1].astype(jnp.int32)
    i2 = route[:, 1:2].astype(jnp.int32)
    r1 = jnp.sum(jnp.where(lane == i1, rank, 0.0), axis=-1, keepdims=True)
    r2 = jnp.sum(jnp.where(lane == i2, rank, 0.0), axis=-1, keepdims=True)
    r_ref[...] = jnp.where(lane == 0, r1, jnp.where(lane == 1, r2, 0.0))
    carry_ref[...] += jnp.sum(oh.astype(F32), axis=0, keepdims=True)
    cnt_ref[...] = carry_ref[...]


def _rank(oh, route, tt=512):
    n = oh.shape[0]
    return pl.pallas_call(
        _rank_kernel,
        grid=(n // tt,),
        in_specs=[pl.BlockSpec((tt, LANES), lambda i: (i, 0)),
                  pl.BlockSpec((tt, LANES), lambda i: (i, 0))],
        out_specs=[pl.BlockSpec((tt, LANES), lambda i: (i, 0)),
                   pl.BlockSpec((1, LANES), lambda i: (0, 0))],
        out_shape=[jax.ShapeDtypeStruct((n, LANES), F32),
                   jax.ShapeDtypeStruct((1, LANES), F32)],
        scratch_shapes=[pltpu.VMEM((1, LANES), F32)],
        compiler_params=_cparams(("arbitrary",)),
        name="moe_rank",
    )(oh, route)


def _row_copy(src_ref, src_row, dst_ref, dst_row, sem):
    return pltpu.make_async_copy(src_ref.at[pl.ds(src_row, 1), :], dst_ref.at[pl.ds(dst_row, 1), :], sem)


def _dispatch_kernel(pos1_ref, pos2_ref, x_ref, init_ref, xs_ref, sem):
    del init_ref
    base = pl.program_id(0) * MOE_TOK

    def issue(t, carry):
        _row_copy(x_ref, t, xs_ref, pos1_ref[base + t], sem).start()
        _row_copy(x_ref, t, xs_ref, pos2_ref[base + t], sem).start()
        return carry

    lax.fori_loop(0, MOE_TOK, issue, 0, unroll=8)
    for _ in range(2):
        pltpu.make_async_copy(x_ref, xs_ref.at[pl.ds(0, MOE_TOK), :], sem).wait()


def _dispatch(pos1, pos2, x1, n_slots):
    n = x1.shape[0]
    init = jnp.zeros((n_slots, D_MODEL), F32)
    return pl.pallas_call(
        _dispatch_kernel,
        grid_spec=pltpu.PrefetchScalarGridSpec(
            num_scalar_prefetch=2,
            grid=(n // MOE_TOK,),
            in_specs=[pl.BlockSpec((MOE_TOK, D_MODEL), lambda i, p1, p2: (i, 0)),
                      pl.BlockSpec(memory_space=pl.ANY)],
            out_specs=pl.BlockSpec(memory_space=pl.ANY),
            scratch_shapes=[pltpu.SemaphoreType.DMA(())]),
        out_shape=jax.ShapeDtypeStruct((n_slots, D_MODEL), F32),
        input_output_aliases={3: 0},
        compiler_params=_cparams(("arbitrary",)),
        name="moe_dispatch",
    )(pos1, pos2, x1, init)


def _ffn_kernel(te_ref, nu_ref, xs_ref, wg_ref, wu_ref, wd_ref, ys_ref):
    del te_ref
    i = pl.program_id(0)

    @pl.when(i < nu_ref[0])
    def _():
        xb = xs_ref[...].astype(BF16)
        gte = _dot(xb, wg_ref[...])
        hid = gte * _sigmoid(gte) * _dot(xb, wu_ref[...])
        ys_ref[...] = _dot(hid.astype(BF16), wd_ref[...])

    @pl.when(i >= nu_ref[0])
    def _():
        ys_ref[...] = jnp.zeros_like(ys_ref)


def _ffn(tile_expert, n_used, xs, w_gate, w_up, w_down):
    n_slots = xs.shape[0]
    wspec = lambda r, c: pl.BlockSpec((None, r, c), lambda i, te, nu: (te[i], 0, 0))
    return pl.pallas_call(
        _ffn_kernel,
        grid_spec=pltpu.PrefetchScalarGridSpec(
            num_scalar_prefetch=2,
            grid=(n_slots // FFN_TILE,),
            in_specs=[pl.BlockSpec((FFN_TILE, D_MODEL), lambda i, te, nu: (i, 0)),
                      wspec(D_MODEL, D_EXPERT), wspec(D_MODEL, D_EXPERT), wspec(D_EXPERT, D_MODEL)],
            out_specs=pl.BlockSpec((FFN_TILE, D_MODEL), lambda i, te, nu: (i, 0))),
        out_shape=jax.ShapeDtypeStruct((n_slots, D_MODEL), F32),
        compiler_params=_cparams(("arbitrary",)),
        name="moe_ffn",
    )(tile_expert, n_used, xs, w_gate, w_up, w_down)


def _combine_kernel(pos1_ref, pos2_ref, x_ref, route_ref, lg_ref, lb_ref, ys_ref, o_ref, buf_ref, sem):
    base = pl.program_id(0) * MOE_TOK

    def issue(t, carry):
        _row_copy(ys_ref, pos1_ref[base + t], buf_ref.at[0], t, sem).start()
        _row_copy(ys_ref, pos2_ref[base + t], buf_ref.at[1], t, sem).start()
        return carry

    lax.fori_loop(0, MOE_TOK, issue, 0, unroll=8)
    for s in range(2):
        pltpu.make_async_copy(ys_ref.at[pl.ds(0, MOE_TOK), :], buf_ref.at[s], sem).wait()
    route = route_ref[...]
    ffn = route[:, 2:3] * buf_ref[0] + route[:, 3:4] * buf_ref[1]
    o_ref[...] = _layer_norm(DN_ALPHA * x_ref[...] + ffn, lg_ref[...], lb_ref[...])


def _combine(pos1, pos2, x1, route, ln_g, ln_b, ys):
    n = x1.shape[0]
    row = lambda w: pl.BlockSpec((MOE_TOK, w), lambda i, p1, p2: (i, 0))
    const = lambda w: pl.BlockSpec((1, w), lambda i, p1, p2: (0, 0))
    return pl.pallas_call(
        _combine_kernel,
        grid_spec=pltpu.PrefetchScalarGridSpec(
            num_scalar_prefetch=2,
            grid=(n // MOE_TOK,),
            in_specs=[row(D_MODEL), row(LANES), const(D_MODEL), const(D_MODEL),
                      pl.BlockSpec(memory_space=pl.ANY)],
            out_specs=row(D_MODEL),
            scratch_shapes=[pltpu.VMEM((2, MOE_TOK, D_MODEL), F32),
                            pltpu.SemaphoreType.DMA(())]),
        out_shape=jax.ShapeDtypeStruct((n, D_MODEL), F32),
        compiler_params=_cparams(("arbitrary",)),
        name="moe_combine",
    )(pos1, pos2, x1, route, ln_g, ln_b, ys)


def _moe(x1, oh, route, w_gate, w_up, w_down, ln_g, ln_b):
    n = x1.shape[0]
    n_slots = TOP_K_INNER * n + N_EXPERTS * FFN_TILE
    rank, cnt = _rank(oh, route)
    cnt = cnt[0, :N_EXPERTS].astype(jnp.int32)
    padded = (cnt + FFN_TILE - 1) // FFN_TILE * FFN_TILE
    ends = jnp.cumsum(padded)
    offs = ends - padded
    experts = jnp.arange(N_EXPERTS, dtype=jnp.int32)
    i1 = route[:, 0].astype(jnp.int32)
    i2 = route[:, 1].astype(jnp.int32)
    off_of = lambda idx: jnp.sum(jnp.where(idx[:, None] == experts[None, :], offs[None, :], 0), axis=1)
    pos1 = off_of(i1) + rank[:, 0].astype(jnp.int32)
    pos2 = off_of(i2) + rank[:, 1].astype(jnp.int32)
    tile_start = jnp.arange(n_slots // FFN_TILE, dtype=jnp.int32) * FFN_TILE
    tile_expert = jnp.minimum(jnp.sum(tile_start[:, None] >= ends[None, :], axis=1), N_EXPERTS - 1)
    n_used = (ends[-1] // FFN_TILE).reshape(1)

    xs = _dispatch(pos1, pos2, x1, n_slots)
    ys = _ffn(tile_expert.astype(jnp.int32), n_used.astype(jnp.int32), xs, w_gate, w_up, w_down)
    return _combine(pos1, pos2, x1, route, ln_g, ln_b, ys)


def _rope_tables(positions):
    inv_freq = ROPE_THETA ** (-jnp.arange(0, ROT_DIM, 2, dtype=F32) / ROT_DIM)
    ang = positions.astype(F32).reshape(-1, 1) * inv_freq
    cos, sin = jnp.cos(ang), jnp.sin(ang)
    n = ang.shape[0]
    half = ROT_DIM // 2
    c64 = jnp.concatenate([cos, cos, jnp.ones((n, A_HEAD_DIM - ROT_DIM), F32)], axis=1)
    s64 = jnp.concatenate([-sin, sin, jnp.zeros((n, A_HEAD_DIM - ROT_DIM), F32)], axis=1)
    del half
    return jnp.tile(c64, (1, 2)), jnp.tile(s64, (1, 2))


def _rep_cols(w):
    r = w.shape[0]
    w4 = w.reshape(r, B_HEADS, 1, B_KEY_DIM)
    return jnp.broadcast_to(w4, (r, B_HEADS, N_SUB, B_KEY_DIM)).reshape(r, GLA_REP)


def _split_w_in(w):
    widths = (A_OUT, A_HEAD_DIM, A_HEAD_DIM, IDX_HEADS * IDX_DIM, IDX_DIM, IDX_HEADS,
              B_HEADS * B_KEY_DIM, B_HEADS * B_KEY_DIM, B_OUT, B_GATE_RANK, B_OUT,
              2 * C_OUT, D_MODEL, D_MODEL, D_MODEL)
    pts = np.cumsum((0,) + widths)
    return [w[:, int(pts[i]):int(pts[i + 1])] for i in range(len(widths))]


def kernel(x, positions, ln_in_g, ln_in_b, w_in, idx_k_g, gla_wa2, gla_ba, gla_norm_g, gm_ln_g, gm_ln_b, gm_ws, gm_bs, w_branch_a, w_branch_b, w_branch_c, w_out, ln1_g, ln1_b, w_rg, b_rg, w_re, b_re, w_gate, w_up, w_down, ln2_g, ln2_b):
    bsz, seq, d = x.shape
    n = bsz * seq
    cos_t, sin_t = _rope_tables(positions)
    h = _entry_ln(x.reshape(n, d), ln_in_g, ln_in_b)
    for l in range(DEPTH):
        (a_q, a_k, a_v, i_q, i_k, i_w, b_q, b_k, b_v, b_glr, b_r, c_uv,
         g_a, g_b, g_c) = _split_w_in(w_in[l])
        zpad = lambda c: jnp.zeros((d, c), F32)
        w_a = jnp.concatenate([a_q, a_k, a_v, i_q, i_k, i_w, zpad(LANES - IDX_DIM - IDX_HEADS)],
                              axis=1).astype(BF16)
        w_b = jnp.concatenate([_rep_cols(b_q * (B_KEY_DIM ** -0.5)), _rep_cols(b_k), b_v, b_r,
                               b_glr, zpad(LANES - B_GATE_RANK)], axis=1).astype(BF16)
        wa_rep = jnp.concatenate([_rep_cols(gla_wa2[l]),
                                  jnp.zeros((LANES - B_GATE_RANK, GLA_REP), F32)], axis=0).astype(BF16)
        ba_rep = _rep_cols(gla_ba[l].reshape(1, -1))
        w_g = jnp.concatenate([g_a, g_b, g_c], axis=1).astype(BF16)
        ikg = jnp.concatenate([idx_k_g[l], jnp.zeros((LANES - IDX_DIM,), F32)]).reshape(1, LANES)
        bs_t = jnp.concatenate([gm_bs[l].T, jnp.zeros((C_CHUNK, LANES - C_GROUPS), F32)], axis=1)
        w_r = jnp.concatenate([w_re[l], w_rg[l], zpad(LANES - N_GROUPS - N_EXPERTS)], axis=1)
        b_r_all = jnp.concatenate([b_re[l], b_rg[l],
                                   jnp.zeros((LANES - N_GROUPS - N_EXPERTS,), F32)]).reshape(1, LANES)

        q, k, v, iq, ki, wi = _proj_a(h, w_a, cos_t, sin_t, ikg)
        y_a = _dsa(q, iq, wi, k, v, ki, bsz, seq)
        y_b = _gla(h, w_b, wa_rep, ba_rep, gla_norm_g[l].reshape(1, -1), bsz, seq)
        y_c = _gmlp(h, c_uv.astype(BF16), gm_ln_g[l].reshape(1, -1), gm_ln_b[l].reshape(1, -1),
                    gm_ws[l], bs_t)
        x1, oh, route = _merge(h, y_a, y_b, y_c, w_g,
                               w_branch_a[l].astype(BF16), w_branch_b[l].astype(BF16),
                               w_branch_c[l].astype(BF16), w_out[l].astype(BF16),
                               ln1_g[l].reshape(1, -1), ln1_b[l].reshape(1, -1), w_r, b_r_all)
        h = _moe(x1, oh, route, w_gate[l].astype(BF16), w_up[l].astype(BF16),
                 w_down[l].astype(BF16), ln2_g[l].reshape(1, -1), ln2_b[l].reshape(1, -1))
    return h.reshape(bsz, seq, d)
```

```python
import functools

import numpy as np
import jax
import jax.numpy as jnp
from jax import lax
from jax.experimental import pallas as pl
from jax.experimental.pallas import tpu as pltpu

F32 = jnp.float32
BF16 = jnp.bfloat16

D_MODEL = 1024
DEPTH = 2
A_HEADS = 8
A_HEAD_DIM = 64
A_OUT = 512
IDX_HEADS = 4
IDX_DIM = 64
TOPK_MAX = 256
Q_BLOCK = 128
ROPE_THETA = 500000.0
ROT_DIM = 16
B_HEADS = 4
B_KEY_DIM = 64
B_VAL_DIM = 128
B_OUT = 512
B_GATE_RANK = 16
B_GATE_TAU = 16.0
B_CHUNK = 64
B_SUB = 16
N_SUB = B_CHUNK // B_SUB
C_GROUPS = 4
C_GROUP_DIM = 128
C_OUT = 512
C_CHUNK = 128
N_GROUPS = 4
EXPERTS_PER_GROUP = 8
N_EXPERTS = 32
D_EXPERT = 256
TOP_K_INNER = 2
DN_ALPHA = (2 * DEPTH) ** 0.25
LN_EPS = 1e-5
RMS_EPS = 1e-6

LANES = 128
NEG_BIG = -1e30
INT_MIN = -2 ** 31
VMEM_LIMIT = 56 * 1024 * 1024

HIGHEST = lax.Precision.HIGHEST


def _cparams(sem):
    return pltpu.CompilerParams(dimension_semantics=sem, vmem_limit_bytes=VMEM_LIMIT)


def _layer_norm(x, g, b):
    mu = jnp.mean(x, axis=-1, keepdims=True)
    xc = x - mu
    var = jnp.mean(xc * xc, axis=-1, keepdims=True)
    return xc * lax.rsqrt(var + LN_EPS) * g + b


def _dot(a, b):
    return jnp.dot(a, b, preferred_element_type=F32)


def _dot_nt(a, b):
    return lax.dot_general(a, b, (((1,), (1,)), ((), ())), preferred_element_type=F32)


def _ln_kernel(x_ref, g_ref, b_ref, o_ref):
    o_ref[...] = _layer_norm(x_ref[...], g_ref[...], b_ref[...])


def _entry_ln(x, g, b, tm=1024):
    n = x.shape[0]
    return pl.pallas_call(
        _ln_kernel,
        grid=(n // tm,),
        in_specs=[pl.BlockSpec((tm, D_MODEL), lambda i: (i, 0)),
                  pl.BlockSpec((1, D_MODEL), lambda i: (0, 0)),
                  pl.BlockSpec((1, D_MODEL), lambda i: (0, 0))],
        out_specs=pl.BlockSpec((tm, D_MODEL), lambda i: (i, 0)),
        out_shape=jax.ShapeDtypeStruct((n, D_MODEL), F32),
        compiler_params=_cparams(("parallel",)),
        name="entry_ln",
    )(x, g.reshape(1, -1), b.reshape(1, -1))


def _rope_slab(xs, c, s):
    lane = lax.broadcasted_iota(jnp.int32, xs.shape, 1) % A_HEAD_DIM
    fwd = pltpu.roll(xs, LANES - ROT_DIM // 2, axis=1)
    bwd = pltpu.roll(xs, ROT_DIM // 2, axis=1)
    partner = jnp.where(lane < ROT_DIM // 2, fwd, bwd)
    return xs * c + partner * s


def _proj_a_kernel(h_ref, w_ref, c_ref, s_ref, g_ref,
                   q_ref, k_ref, v_ref, iq_ref, ki_ref, wi_ref):
    p = _dot(h_ref[...].astype(BF16), w_ref[...])
    c = c_ref[...]
    s = s_ref[...]
    lane = lax.broadcasted_iota(jnp.int32, c.shape, 1)
    lo = lane < A_HEAD_DIM
    n_blk = p.shape[0] // Q_BLOCK

    def put_heads(dst_ref, slab, first_head):
        for half in range(2):
            hd = first_head + half
            cols = slab[:, half * A_HEAD_DIM:(half + 1) * A_HEAD_DIM].astype(BF16)
            for qb in range(n_blk):
                dst_ref[qb, hd * Q_BLOCK:(hd + 1) * Q_BLOCK, :] = cols[qb * Q_BLOCK:(qb + 1) * Q_BLOCK, :]

    for i in range(4):
        sl = slice(i * LANES, (i + 1) * LANES)
        put_heads(q_ref, _rope_slab(p[:, sl], c, s) * (A_HEAD_DIM ** -0.5), 2 * i)
    kv = p[:, 512:640]
    kv = _rope_slab(kv, jnp.where(lo, c, 1.0), jnp.where(lo, s, 0.0))
    k_ref[...] = kv[:, :A_HEAD_DIM].astype(BF16)
    v_ones = jnp.where(lo, pltpu.roll(kv, A_HEAD_DIM, axis=1), jnp.where(lane == A_HEAD_DIM, 1.0, 0.0))
    v_ref[...] = v_ones.astype(BF16)
    for i in range(2):
        sl = slice(640 + i * LANES, 640 + (i + 1) * LANES)
        put_heads(iq_ref, _rope_slab(p[:, sl], c, s), 2 * i)
    last = p[:, 896:1024]
    mu = jnp.sum(jnp.where(lo, last, 0.0), axis=-1, keepdims=True) * (1.0 / IDX_DIM)
    xc = jnp.where(lo, last - mu, 0.0)
    var = jnp.sum(xc * xc, axis=-1, keepdims=True) * (1.0 / IDX_DIM)
    kin = xc * lax.rsqrt(var + LN_EPS) * g_ref[...]
    kin = _rope_slab(kin, jnp.where(lo, c, 1.0), jnp.where(lo, s, 0.0))
    ki_ref[...] = kin[:, :IDX_DIM].astype(BF16)
    wi = pltpu.roll(last, LANES - IDX_DIM, axis=1) * (IDX_HEADS ** -0.5 * IDX_DIM ** -0.5)
    for qb in range(n_blk):
        wi_ref[qb] = wi[qb * Q_BLOCK:(qb + 1) * Q_BLOCK, :].T[0:8, :]


def _proj_a(h, w_a, cos_t, sin_t, ikg, tm=512):
    n = h.shape[0]
    row = lambda w: pl.BlockSpec((tm, w), lambda i: (i, 0))
    stacked = lambda heads: pl.BlockSpec((tm // Q_BLOCK, heads * Q_BLOCK, A_HEAD_DIM), lambda i: (i, 0, 0))
    return pl.pallas_call(
        _proj_a_kernel,
        grid=(n // tm,),
        in_specs=[row(D_MODEL),
                  pl.BlockSpec((D_MODEL, 1024), lambda i: (0, 0)),
                  row(LANES), row(LANES),
                  pl.BlockSpec((1, LANES), lambda i: (0, 0))],
        out_specs=[stacked(A_HEADS), row(A_HEAD_DIM),
                   row(LANES),
                   stacked(IDX_HEADS), row(IDX_DIM),
                   pl.BlockSpec((tm // Q_BLOCK, 8, Q_BLOCK), lambda i: (i, 0, 0))],
        out_shape=[jax.ShapeDtypeStruct((n // Q_BLOCK, A_HEADS * Q_BLOCK, A_HEAD_DIM), BF16),
                   jax.ShapeDtypeStruct((n, A_HEAD_DIM), BF16),
                   jax.ShapeDtypeStruct((n, LANES), BF16),
                   jax.ShapeDtypeStruct((n // Q_BLOCK, IDX_HEADS * Q_BLOCK, IDX_DIM), BF16),
                   jax.ShapeDtypeStruct((n, IDX_DIM), BF16),
                   jax.ShapeDtypeStruct((n // Q_BLOCK, 8, Q_BLOCK), F32)],
        compiler_params=_cparams(("parallel",)),
        name="proj_a",
    )(h, w_a, cos_t, sin_t, ikg)


KEY_CHUNK = 512


def _dsa_kernel(q_ref, iq_ref, wi_ref, k_ref, v_ref, ki_ref, o_ref,
                key_ref, bias_ref, *, seq):
    j = pl.program_id(1)
    n_sel = min(TOPK_MAX, seq // 4)
    n_kc = (j * Q_BLOCK + Q_BLOCK + KEY_CHUNK - 1) // KEY_CHUNK
    qpos = j * Q_BLOCK + lax.broadcasted_iota(jnp.int32, (1, Q_BLOCK), 1)
    kk = jnp.minimum(n_sel, qpos + 1).astype(F32)
    row_pos = lax.broadcasted_iota(jnp.int32, (KEY_CHUNK, 1), 0)

    def key_rows(c):
        return pl.ds(pl.multiple_of(c * KEY_CHUNK, KEY_CHUNK), KEY_CHUNK)

    iq = iq_ref[...]
    wi = wi_ref[...]

    def score_chunk(c, carry):
        d = jnp.maximum(_dot_nt(ki_ref[key_rows(c), :], iq), 0.0)
        score = wi[0:1, :] * d[:, 0:Q_BLOCK]
        for h in range(1, IDX_HEADS):
            score = score + wi[h:h + 1, :] * d[:, h * Q_BLOCK:(h + 1) * Q_BLOCK]
        bits = pltpu.bitcast(score + 0.0, jnp.int32)
        key = bits ^ ((bits >> 31) & 0x7FFFFFFF)
        key_ref[c] = jnp.where(c * KEY_CHUNK + row_pos <= qpos, key, INT_MIN)
        return carry

    lax.fori_loop(0, n_kc, score_chunk, 0)

    def count(pred):
        def body(c, acc):
            m = jnp.where(pred(key_ref[c]), 1.0, 0.0)
            return acc + jnp.sum(m.reshape(KEY_CHUNK // 64, 64, Q_BLOCK), axis=0)
        acc = lax.fori_loop(0, n_kc, body, jnp.zeros((64, Q_BLOCK), F32))
        return jnp.sum(acc, axis=0, keepdims=True)

    def search(i, t_u):
        cand_u = t_u | jnp.left_shift(jnp.int32(1), 31 - i)
        cand_s = cand_u ^ INT_MIN
        return jnp.where(count(lambda kc: kc >= cand_s) >= kk, cand_u, t_u)

    thr = lax.fori_loop(0, 32, search, jnp.zeros((1, Q_BLOCK), jnp.int32)) ^ INT_MIN
    need = kk - count(lambda kc: kc > thr)
    r_i = lax.broadcasted_iota(jnp.int32, (KEY_CHUNK, KEY_CHUNK), 0)
    c_i = lax.broadcasted_iota(jnp.int32, (KEY_CHUNK, KEY_CHUNK), 1)
    lower = jnp.where(c_i < r_i, 1.0, 0.0).astype(BF16)

    def fill(c, carry):
        kc = key_ref[c]
        eq = kc == thr
        eq_f = jnp.where(eq, 1.0, 0.0)
        pre = _dot(lower, eq_f.astype(BF16)) + carry
        sel = (kc > thr) | (eq & (pre < need))
        bias_ref[c] = jnp.where(sel, 0.0, NEG_BIG)
        return carry + jnp.sum(eq_f, axis=0, keepdims=True)

    lax.fori_loop(0, n_kc, fill, jnp.zeros((1, Q_BLOCK), F32))

    q = q_ref[...]
    cols = A_HEADS * Q_BLOCK

    def attn_chunk(c, carry):
        m, acc = carry
        rows = key_rows(c)
        lg = _dot_nt(k_ref[rows, :], q) + jnp.concatenate([bias_ref[c]] * A_HEADS, axis=1)
        m_new = jnp.maximum(m, jnp.max(lg, axis=0, keepdims=True))
        p = jnp.exp(lg - m_new).astype(BF16)
        pv = lax.dot_general(v_ref[rows, :], p, (((0,), (0,)), ((), ())),
                             preferred_element_type=F32)
        return m_new, jnp.exp(m - m_new) * acc + pv

    init = (jnp.full((1, cols), -3e38, F32), jnp.zeros((LANES, cols), F32))
    _, acc = lax.fori_loop(0, n_kc, attn_chunk, init)
    o = acc[:A_HEAD_DIM] / acc[A_HEAD_DIM:A_HEAD_DIM + 1]
    for h in range(A_HEADS):
        o_ref[:, h * A_HEAD_DIM:(h + 1) * A_HEAD_DIM] = (
            o[:, h * Q_BLOCK:(h + 1) * Q_BLOCK].T.astype(BF16))


def _dsa(q, iq, wi, k, v, ki, bsz, seq):
    nq = seq // Q_BLOCK
    n_kc = seq // KEY_CHUNK
    stacked = lambda heads: pl.BlockSpec((None, heads * Q_BLOCK, A_HEAD_DIM),
                                         lambda b, j: (b * nq + j, 0, 0))
    kspec = lambda w: pl.BlockSpec((seq, w), lambda b, j: (b, 0))
    return pl.pallas_call(
        functools.partial(_dsa_kernel, seq=seq),
        grid=(bsz, nq),
        in_specs=[stacked(A_HEADS), stacked(IDX_HEADS),
                  pl.BlockSpec((None, 8, Q_BLOCK), lambda b, j: (b * nq + j, 0, 0)),
                  kspec(A_HEAD_DIM),
                  kspec(LANES),
                  kspec(IDX_DIM)],
        out_specs=pl.BlockSpec((Q_BLOCK, A_OUT), lambda b, j: (b * nq + j, 0)),
        out_shape=jax.ShapeDtypeStruct((bsz * seq, A_OUT), BF16),
        scratch_shapes=[pltpu.VMEM((n_kc, KEY_CHUNK, Q_BLOCK), jnp.int32),
                        pltpu.VMEM((n_kc, KEY_CHUNK, Q_BLOCK), F32)],
        compiler_params=_cparams(("parallel", "parallel")),
        name="dsa",
    )(q, iq, wi, k, v, ki)


GLA_QK = B_HEADS * B_KEY_DIM
GLA_REP = N_SUB * GLA_QK
GLA_COLS = 2 * GLA_QK + 2 * B_OUT + LANES
GLA_V0 = 2 * GLA_QK


def _gla_kernel(h_ref, w_ref, wa_ref, ba_ref, ng_ref, o_ref, p_ref, g_ref, acc_ref, st_ref, *, ts):
    @pl.when(pl.program_id(1) == 0)
    def _():
        st_ref[...] = jnp.zeros_like(st_ref)

    p_ref[...] = _dot(h_ref[...].astype(BF16), w_ref[...])
    x = _dot(p_ref[:, GLA_V0 + 2 * B_OUT:].astype(BF16), wa_ref[...]) + ba_ref[...]
    g_ref[...] = (jnp.minimum(x, 0.0) - jnp.log1p(jnp.exp(-jnp.abs(x)))) * (1.0 / B_GATE_TAU)

    t_i = lax.broadcasted_iota(jnp.int32, (B_CHUNK, B_CHUNK), 0)
    s_i = lax.broadcasted_iota(jnp.int32, (B_CHUNK, B_CHUNK), 1)
    tri = t_i >= s_i
    tri_f = jnp.where(tri, 1.0, 0.0)
    lane_sub = (lax.broadcasted_iota(jnp.int32, (1, GLA_REP), 1) // B_KEY_DIM) % N_SUB
    row_sub = lax.broadcasted_iota(jnp.int32, (B_CHUNK, 1), 0) // B_SUB
    q_mask = lane_sub <= row_sub
    k_mask = lane_sub == row_sub
    low_half = lax.broadcasted_iota(jnp.int32, (B_CHUNK, LANES), 1) < B_KEY_DIM

    def replicate(x):
        outs = []
        for pair in range(GLA_QK // LANES):
            v = x[:, pair * LANES:(pair + 1) * LANES]
            r = pltpu.roll(v, B_KEY_DIM, axis=1)
            outs += [jnp.where(low_half, v, r)] * (N_SUB // 2) + [jnp.where(low_half, r, v)] * (N_SUB // 2)
        return jnp.concatenate(outs, axis=1)

    def chunk(c, carry):
        rows = pl.ds(pl.multiple_of(c * B_CHUNK, B_CHUNK), B_CHUNK)
        b = replicate(jnp.dot(tri_f, g_ref[rows, :], precision=HIGHEST, preferred_element_type=F32))
        ref = jnp.zeros((1, GLA_REP), F32)
        for jj in range(1, N_SUB):
            ref = ref + jnp.where(lane_sub == jj, b[jj * B_SUB - 1:jj * B_SUB, :], 0.0)
        d = b - ref
        e_q = jnp.where(q_mask, jnp.exp(jnp.minimum(d, 0.0)), 0.0)
        e_k = jnp.where(k_mask, jnp.exp(jnp.where(k_mask, -d, 0.0)), 0.0)
        b_last = b[B_CHUNK - 1:B_CHUNK, :]
        q_c = (replicate(p_ref[rows, 0:GLA_QK]) * e_q).astype(BF16)
        k_raw = replicate(p_ref[rows, GLA_QK:2 * GLA_QK])
        k_c = (k_raw * e_k).astype(BF16)
        k_st = (k_raw * jnp.exp(b_last - b)).astype(BF16)
        e_last = jnp.exp(b_last)
        for hh in range(B_HEADS):
            base = hh * N_SUB * B_KEY_DIM
            q_h = q_c[:, base:base + N_SUB * B_KEY_DIM]
            k_h = k_c[:, base:base + N_SUB * B_KEY_DIM]
            att = jnp.where(tri, _dot_nt(q_h, k_h), 0.0)
            v_h = p_ref[rows, GLA_V0 + hh * B_VAL_DIM:GLA_V0 + (hh + 1) * B_VAL_DIM].astype(BF16)
            st = st_ref[hh]
            o = _dot(att.astype(BF16), v_h) + _dot_nt(q_h[:, :B_KEY_DIM], st.astype(BF16))
            acc_ref[rows, hh * B_VAL_DIM:(hh + 1) * B_VAL_DIM] = o
            upd = _dot(v_h.T, k_st[:, base:base + B_KEY_DIM])
            st_ref[hh] = st * e_last[:, base:base + B_KEY_DIM] + upd
        return carry

    lax.fori_loop(0, ts // B_CHUNK, chunk, 0)

    r = p_ref[:, GLA_V0 + B_OUT:GLA_V0 + 2 * B_OUT]
    gate = r * (1.0 / (1.0 + jnp.exp(-r)))
    ng = ng_ref[...]
    for hh in range(B_HEADS):
        sl = slice(hh * B_VAL_DIM, (hh + 1) * B_VAL_DIM)
        o = acc_ref[:, sl]
        y = o * lax.rsqrt(jnp.mean(o * o, axis=-1, keepdims=True) + RMS_EPS) * ng
        o_ref[:, sl] = (gate[:, sl] * y).astype(BF16)


def _gla(h, w_b, wa, ba, norm_g, bsz, seq, ts=512):
    ns = seq // ts
    const = lambda shape: pl.BlockSpec(shape, lambda b, i: (0, 0))
    return pl.pallas_call(
        functools.partial(_gla_kernel, ts=ts),
        grid=(bsz, ns),
        in_specs=[pl.BlockSpec((ts, D_MODEL), lambda b, i: (b * ns + i, 0)),
                  const((D_MODEL, GLA_COLS)), const((LANES, GLA_QK)),
                  const((1, GLA_QK)), const((1, B_VAL_DIM))],
        out_specs=pl.BlockSpec((ts, B_OUT), lambda b, i: (b * ns + i, 0)),
        out_shape=jax.ShapeDtypeStruct((bsz * seq, B_OUT), BF16),
        scratch_shapes=[pltpu.VMEM((ts, GLA_COLS), F32),
                        pltpu.VMEM((ts, GLA_QK), F32),
                        pltpu.VMEM((ts, B_OUT), F32),
                        pltpu.VMEM((B_HEADS, B_VAL_DIM, B_KEY_DIM), F32)],
        compiler_params=_cparams(("parallel", "arbitrary")),
        name="gla",
    )(h, w_b, wa, ba, norm_g)


def _gmlp_kernel(h_ref, w_ref, lg_ref, lb_ref, ws_ref, bs_ref, o_ref, *, tm):
    z = _dot(h_ref[...].astype(BF16), w_ref[...])
    z = z * (0.5 * (1.0 + jnp.tanh(np.sqrt(2.0 / np.pi) * (z + 0.044715 * (z * z * z)))))
    u = z[:, :C_OUT]
    v = _layer_norm(z[:, C_OUT:], lg_ref[...], lb_ref[...]).astype(BF16)
    t_i = lax.broadcasted_iota(jnp.int32, (C_CHUNK, C_CHUNK), 0)
    s_i = lax.broadcasted_iota(jnp.int32, (C_CHUNK, C_CHUNK), 1)
    bs = bs_ref[...]
    for g in range(C_GROUPS):
        w = jnp.where(t_i >= s_i, ws_ref[g], 0.0).astype(BF16)
        sl = slice(g * C_GROUP_DIM, (g + 1) * C_GROUP_DIM)
        for c in range(tm // C_CHUNK):
            rows = slice(c * C_CHUNK, (c + 1) * C_CHUNK)
            mixed = _dot(w, v[rows, sl]) + bs[:, g:g + 1]
            o_ref[rows, sl] = (u[rows, sl] * mixed).astype(BF16)


def _gmlp(h, w_c, ln_g, ln_b, ws, bs_t, tm=512):
    n = h.shape[0]
    return pl.pallas_call(
        functools.partial(_gmlp_kernel, tm=tm),
        grid=(n // tm,),
        in_specs=[pl.BlockSpec((tm, D_MODEL), lambda i: (i, 0)),
                  pl.BlockSpec((D_MODEL, 2 * C_OUT), lambda i: (0, 0)),
                  pl.BlockSpec((1, C_OUT), lambda i: (0, 0)),
                  pl.BlockSpec((1, C_OUT), lambda i: (0, 0)),
                  pl.BlockSpec((C_GROUPS, C_CHUNK, C_CHUNK), lambda i: (0, 0, 0)),
                  pl.BlockSpec((C_CHUNK, LANES), lambda i: (0, 0))],
        out_specs=pl.BlockSpec((tm, C_OUT), lambda i: (i, 0)),
        out_shape=jax.ShapeDtypeStruct((n, C_OUT), BF16),
        compiler_params=_cparams(("parallel",)),
        name="gmlp",
    )(h, w_c, ln_g, ln_b, ws, bs_t)


def _sigmoid(x):
    return 1.0 / (1.0 + jnp.exp(-x))


def _merge_kernel(h_ref, ya_ref, yb_ref, yc_ref, wg_ref, wa_ref, wb_ref, wc_ref, wo_ref,
                  lg_ref, lb_ref, wr_ref, wrl_ref, br_ref, x_ref, oh_ref, route_ref):
    h = h_ref[...]
    hb = h.astype(BF16)
    acc = None
    for i, (y_ref, wbr_ref) in enumerate(((ya_ref, wa_ref), (yb_ref, wb_ref), (yc_ref, wc_ref))):
        gate = _sigmoid(_dot(hb, wg_ref[:, i * D_MODEL:(i + 1) * D_MODEL]))
        term = gate * _dot(y_ref[...], wbr_ref[...])
        acc = term if acc is None else acc + term
    mix = _dot(acc.astype(BF16), wo_ref[...])
    x1 = _layer_norm(DN_ALPHA * h + mix, lg_ref[...], lb_ref[...])
    x_ref[...] = x1

    x_hi = x1.astype(BF16)
    x_lo = (x1 - x_hi.astype(F32)).astype(BF16)
    logit = (_dot(x_hi, wr_ref[...]) + _dot(x_hi, wrl_ref[...]) + _dot(x_lo, wr_ref[...])) + br_ref[...]
    lane = lax.broadcasted_iota(jnp.int32, logit.shape, 1)
    big = jnp.int32(LANES)
    is_g = (lane >= N_EXPERTS) & (lane < N_EXPERTS + N_GROUPS)
    gl = jnp.where(is_g, logit, -jnp.inf)
    g_max = jnp.max(gl, axis=-1, keepdims=True)
    g_idx = jnp.min(jnp.where(is_g & (gl == g_max), lane, big), axis=-1, keepdims=True) - N_EXPERTS
    g_p = 1.0 / jnp.sum(jnp.where(is_g, jnp.exp(gl - g_max), 0.0), axis=-1, keepdims=True)
    lo = g_idx * EXPERTS_PER_GROUP
    in_g = (lane >= lo) & (lane < lo + EXPERTS_PER_GROUP)
    el = jnp.where(in_g, logit, -jnp.inf)
    e1 = jnp.max(el, axis=-1, keepdims=True)
    i1 = jnp.min(jnp.where(in_g & (el == e1), lane, big), axis=-1, keepdims=True)
    el2 = jnp.where(lane == i1, -jnp.inf, el)
    e2 = jnp.max(el2, axis=-1, keepdims=True)
    i2 = jnp.min(jnp.where(in_g & (lane != i1) & (el2 == e2), lane, big), axis=-1, keepdims=True)
    z2 = jnp.exp(e2 - e1)
    den = 1.0 + z2
    oh_ref[...] = jnp.where((lane == i1) | (lane == i2), 1.0, 0.0).astype(BF16)
    route_ref[...] = jnp.where(lane == 0, i1.astype(F32),
                               jnp.where(lane == 1, i2.astype(F32),
                                         jnp.where(lane == 2, (1.0 / den) * g_p,
                                                   jnp.where(lane == 3, (z2 / den) * g_p, 0.0))))


def _merge(h, ya, yb, yc, wg, wa, wb, wc, wo, ln_g, ln_b, wr, wr_lo, br, tm=1024):
    n = h.shape[0]
    row = lambda w: pl.BlockSpec((tm, w), lambda i: (i, 0))
    const = lambda r, c: pl.BlockSpec((r, c), lambda i: (0, 0), pipeline_mode=pl.Buffered(1))
    return pl.pallas_call(
        _merge_kernel,
        grid=(n // tm,),
        in_specs=[row(D_MODEL), row(A_OUT), row(B_OUT), row(C_OUT),
                  const(D_MODEL, 3 * D_MODEL), const(A_OUT, D_MODEL), const(B_OUT, D_MODEL),
                  const(C_OUT, D_MODEL), const(D_MODEL, D_MODEL),
                  const(1, D_MODEL), const(1, D_MODEL), const(D_MODEL, LANES), const(D_MODEL, LANES),
                  const(1, LANES)],
        out_specs=[row(D_MODEL), row(LANES), row(LANES)],
        out_shape=[jax.ShapeDtypeStruct((n, D_MODEL), F32),
                   jax.ShapeDtypeStruct((n, LANES), BF16),
                   jax.ShapeDtypeStruct((n, LANES), F32)],
        compiler_params=_cparams(("parallel",)),
        name="merge",
    )(h, ya, yb, yc, wg, wa, wb, wc, wo, ln_g, ln_b, wr, wr_lo, br)


FFN_TILE = 512
MOE_TOK = 256


def _rank_kernel(oh_ref, route_ref, r_ref, cnt_ref, carry_ref):
    @pl.when(pl.program_id(0) == 0)
    def _():
        carry_ref[...] = jnp.zeros_like(carry_ref)

    oh = oh_ref[...]
    tt = oh.shape[0]
    r_i = lax.broadcasted_iota(jnp.int32, (tt, tt), 0)
    c_i = lax.broadcasted_iota(jnp.int32, (tt, tt), 1)
    lower = jnp.where(c_i < r_i, 1.0, 0.0).astype(BF16)
    rank = _dot(lower, oh) + carry_ref[...]
    route = route_ref[...]
    lane = lax.broadcasted_iota(jnp.int32, rank.shape, 1)
    i1 = route[:, 0:1].astype(jnp.int32)
    i2 = route[:, 1:2].astype(jnp.int32)
    r1 = jnp.sum(jnp.where(lane == i1, rank, 0.0), axis=-1, keepdims=True)
    r2 = jnp.sum(jnp.where(lane == i2, rank, 0.0), axis=-1, keepdims=True)
    r_ref[...] = jnp.where(lane == 0, r1, jnp.where(lane == 1, r2, 0.0))
    carry_ref[...] += jnp.sum(oh.astype(F32), axis=0, keepdims=True)
    cnt_ref[...] = carry_ref[...]


def _rank(oh, route, tt=512):
    n = oh.shape[0]
    return pl.pallas_call(
        _rank_kernel,
        grid=(n // tt,),
        in_specs=[pl.BlockSpec((tt, LANES), lambda i: (i, 0)),
                  pl.BlockSpec((tt, LANES), lambda i: (i, 0))],
        out_specs=[pl.BlockSpec((tt, LANES), lambda i: (i, 0)),
                   pl.BlockSpec((1, LANES), lambda i: (0, 0))],
        out_shape=[jax.ShapeDtypeStruct((n, LANES), F32),
                   jax.ShapeDtypeStruct((1, LANES), F32)],
        scratch_shapes=[pltpu.VMEM((1, LANES), F32)],
        compiler_params=_cparams(("arbitrary",)),
        name="moe_rank",
    )(oh, route)


def _row_copy(src_ref, src_row, dst_ref, dst_row, sem):
    return pltpu.make_async_copy(src_ref.at[pl.ds(src_row, 1), :], dst_ref.at[pl.ds(dst_row, 1), :], sem)


def _dispatch_kernel(pos1_ref, pos2_ref, x_ref, init_ref, xs_ref, sem):
    del init_ref
    base = pl.program_id(0) * MOE_TOK

    def issue(t, carry):
        _row_copy(x_ref, t, xs_ref, pos1_ref[base + t], sem).start()
        _row_copy(x_ref, t, xs_ref, pos2_ref[base + t], sem).start()
        return carry

    lax.fori_loop(0, MOE_TOK, issue, 0, unroll=8)
    for _ in range(2):
        pltpu.make_async_copy(x_ref, xs_ref.at[pl.ds(0, MOE_TOK), :], sem).wait()


def _dispatch(pos1, pos2, x1, n_slots):
    n = x1.shape[0]
    init = jnp.zeros((n_slots, D_MODEL), F32)
    return pl.pallas_call(
        _dispatch_kernel,
        grid_spec=pltpu.PrefetchScalarGridSpec(
            num_scalar_prefetch=2,
            grid=(n // MOE_TOK,),
            in_specs=[pl.BlockSpec((MOE_TOK, D_MODEL), lambda i, p1, p2: (i, 0)),
                      pl.BlockSpec(memory_space=pl.ANY)],
            out_specs=pl.BlockSpec(memory_space=pl.ANY),
            scratch_shapes=[pltpu.SemaphoreType.DMA(())]),
        out_shape=jax.ShapeDtypeStruct((n_slots, D_MODEL), F32),
        input_output_aliases={3: 0},
        compiler_params=_cparams(("arbitrary",)),
        name="moe_dispatch",
    )(pos1, pos2, x1, init)


def _ffn_kernel(te_ref, nu_ref, xs_ref, wg_ref, wu_ref, wd_ref, ys_ref):
    del te_ref
    i = pl.program_id(0)

    @pl.when(i < nu_ref[0])
    def _():
        xb = xs_ref[...].astype(BF16)
        gte = _dot(xb, wg_ref[...])
        hid = gte * _sigmoid(gte) * _dot(xb, wu_ref[...])
        ys_ref[...] = _dot(hid.astype(BF16), wd_ref[...])

    @pl.when(i >= nu_ref[0])
    def _():
        ys_ref[...] = jnp.zeros_like(ys_ref)


def _ffn(tile_expert, n_used, xs, w_gate, w_up, w_down):
    n_slots = xs.shape[0]
    wspec = lambda r, c: pl.BlockSpec((None, r, c), lambda i, te, nu: (te[i], 0, 0))
    return pl.pallas_call(
        _ffn_kernel,
        grid_spec=pltpu.PrefetchScalarGridSpec(
            num_scalar_prefetch=2,
            grid=(n_slots // FFN_TILE,),
            in_specs=[pl.BlockSpec((FFN_TILE, D_MODEL), lambda i, te, nu: (i, 0)),
                      wspec(D_MODEL, D_EXPERT), wspec(D_MODEL, D_EXPERT), wspec(D_EXPERT, D_MODEL)],
            out_specs=pl.BlockSpec((FFN_TILE, D_MODEL), lambda i, te, nu: (i, 0))),
        out_shape=jax.ShapeDtypeStruct((n_slots, D_MODEL), F32),
        compiler_params=_cparams(("arbitrary",)),
        name="moe_ffn",
    )(tile_expert, n_used, xs, w_gate, w_up, w_down)


def _combine_kernel(pos1_ref, pos2_ref, x_ref, route_ref, lg_ref, lb_ref, ys_ref, o_ref, buf_ref, sem):
    base = pl.program_id(0) * MOE_TOK

    def issue(t, carry):
        _row_copy(ys_ref, pos1_ref[base + t], buf_ref.at[0], t, sem).start()
        _row_copy(ys_ref, pos2_ref[base + t], buf_ref.at[1], t, sem).start()
        return carry

    lax.fori_loop(0, MOE_TOK, issue, 0, unroll=8)
    for s in range(2):
        pltpu.make_async_copy(ys_ref.at[pl.ds(0, MOE_TOK), :], buf_ref.at[s], sem).wait()
    route = route_ref[...]
    ffn = route[:, 2:3] * buf_ref[0] + route[:, 3:4] * buf_ref[1]
    o_ref[...] = _layer_norm(DN_ALPHA * x_ref[...] + ffn, lg_ref[...], lb_ref[...])


def _combine(pos1, pos2, x1, route, ln_g, ln_b, ys):
    n = x1.shape[0]
    row = lambda w: pl.BlockSpec((MOE_TOK, w), lambda i, p1, p2: (i, 0))
    const = lambda w: pl.BlockSpec((1, w), lambda i, p1, p2: (0, 0))
    return pl.pallas_call(
        _combine_kernel,
        grid_spec=pltpu.PrefetchScalarGridSpec(
            num_scalar_prefetch=2,
            grid=(n // MOE_TOK,),
            in_specs=[row(D_MODEL), row(LANES), const(D_MODEL), const(D_MODEL),
                      pl.BlockSpec(memory_space=pl.ANY)],
            out_specs=row(D_MODEL),
            scratch_shapes=[pltpu.VMEM((2, MOE_TOK, D_MODEL), F32),
                            pltpu.SemaphoreType.DMA(())]),
        out_shape=jax.ShapeDtypeStruct((n, D_MODEL), F32),
        compiler_params=_cparams(("arbitrary",)),
        name="moe_combine",
    )(pos1, pos2, x1, route, ln_g, ln_b, ys)


def _moe(x1, oh, route, w_gate, w_up, w_down, ln_g, ln_b):
    n = x1.shape[0]
    n_slots = TOP_K_INNER * n + N_EXPERTS * FFN_TILE
    rank, cnt = _rank(oh, route)
    cnt = cnt[0, :N_EXPERTS].astype(jnp.int32)
    padded = (cnt + FFN_TILE - 1) // FFN_TILE * FFN_TILE
    ends = jnp.cumsum(padded)
    offs = ends - padded
    experts = jnp.arange(N_EXPERTS, dtype=jnp.int32)
    i1 = route[:, 0].astype(jnp.int32)
    i2 = route[:, 1].astype(jnp.int32)
    off_of = lambda idx: jnp.sum(jnp.where(idx[:, None] == experts[None, :], offs[None, :], 0), axis=1)
    pos1 = off_of(i1) + rank[:, 0].astype(jnp.int32)
    pos2 = off_of(i2) + rank[:, 1].astype(jnp.int32)
    tile_start = jnp.arange(n_slots // FFN_TILE, dtype=jnp.int32) * FFN_TILE
    tile_expert = jnp.minimum(jnp.sum(tile_start[:, None] >= ends[None, :], axis=1), N_EXPERTS - 1)
    n_used = (ends[-1] // FFN_TILE).reshape(1)

    xs = _dispatch(pos1, pos2, x1, n_slots)
    ys = _ffn(tile_expert.astype(jnp.int32), n_used.astype(jnp.int32), xs, w_gate, w_up, w_down)
    return _combine(pos1, pos2, x1, route, ln_g, ln_b, ys)


def _rope_tables(positions):
    inv_freq = ROPE_THETA ** (-jnp.arange(0, ROT_DIM, 2, dtype=F32) / ROT_DIM)
    ang = positions.astype(F32).reshape(-1, 1) * inv_freq
    cos, sin = jnp.cos(ang), jnp.sin(ang)
    n = ang.shape[0]
    half = ROT_DIM // 2
    c64 = jnp.concatenate([cos, cos, jnp.ones((n, A_HEAD_DIM - ROT_DIM), F32)], axis=1)
    s64 = jnp.concatenate([-sin, sin, jnp.zeros((n, A_HEAD_DIM - ROT_DIM), F32)], axis=1)
    del half
    return jnp.tile(c64, (1, 2)), jnp.tile(s64, (1, 2))


def _split_w_in(w):
    widths = (A_OUT, A_HEAD_DIM, A_HEAD_DIM, IDX_HEADS * IDX_DIM, IDX_DIM, IDX_HEADS,
              B_HEADS * B_KEY_DIM, B_HEADS * B_KEY_DIM, B_OUT, B_GATE_RANK, B_OUT,
              2 * C_OUT, D_MODEL, D_MODEL, D_MODEL)
    pts = np.cumsum((0,) + widths)
    return [w[:, int(pts[i]):int(pts[i + 1])] for i in range(len(widths))]


def kernel(x, positions, ln_in_g, ln_in_b, w_in, idx_k_g, gla_wa2, gla_ba, gla_norm_g, gm_ln_g, gm_ln_b, gm_ws, gm_bs, w_branch_a, w_branch_b, w_branch_c, w_out, ln1_g, ln1_b, w_rg, b_rg, w_re, b_re, w_gate, w_up, w_down, ln2_g, ln2_b):
    bsz, seq, d = x.shape
    n = bsz * seq
    cos_t, sin_t = _rope_tables(positions)
    h = _entry_ln(x.reshape(n, d), ln_in_g, ln_in_b)
    for l in range(DEPTH):
        (a_q, a_k, a_v, i_q, i_k, i_w, b_q, b_k, b_v, b_glr, b_r, c_uv,
         g_a, g_b, g_c) = _split_w_in(w_in[l])
        zpad = lambda c: jnp.zeros((d, c), F32)
        w_a = jnp.concatenate([a_q, a_k, a_v, i_q, i_k, i_w, zpad(LANES - IDX_DIM - IDX_HEADS)],
                              axis=1).astype(BF16)
        w_b = jnp.concatenate([b_q * (B_KEY_DIM ** -0.5), b_k, b_v, b_r,
                               b_glr, zpad(LANES - B_GATE_RANK)], axis=1).astype(BF16)
        wa_pad = jnp.concatenate([gla_wa2[l],
                                  jnp.zeros((LANES - B_GATE_RANK, GLA_QK), F32)], axis=0).astype(BF16)
        w_g = jnp.concatenate([g_a, g_b, g_c], axis=1).astype(BF16)
        ikg = jnp.concatenate([idx_k_g[l], jnp.zeros((LANES - IDX_DIM,), F32)]).reshape(1, LANES)
        bs_t = jnp.concatenate([gm_bs[l].T, jnp.zeros((C_CHUNK, LANES - C_GROUPS), F32)], axis=1)
        w_r = jnp.concatenate([w_re[l], w_rg[l], zpad(LANES - N_GROUPS - N_EXPERTS)], axis=1)
        w_r_hi = w_r.astype(BF16)
        w_r_lo = (w_r - w_r_hi.astype(F32)).astype(BF16)
        b_r_all = jnp.concatenate([b_re[l], b_rg[l],
                                   jnp.zeros((LANES - N_GROUPS - N_EXPERTS,), F32)]).reshape(1, LANES)

        q, k, v, iq, ki, wi = _proj_a(h, w_a, cos_t, sin_t, ikg)
        y_a = _dsa(q, iq, wi, k, v, ki, bsz, seq)
        y_b = _gla(h, w_b, wa_pad, gla_ba[l].reshape(1, -1), gla_norm_g[l].reshape(1, -1), bsz, seq)
        y_c = _gmlp(h, c_uv.astype(BF16), gm_ln_g[l].reshape(1, -1), gm_ln_b[l].reshape(1, -1),
                    gm_ws[l], bs_t)
        x1, oh, route = _merge(h, y_a, y_b, y_c, w_g,
                               w_branch_a[l].astype(BF16), w_branch_b[l].astype(BF16),
                               w_branch_c[l].astype(BF16), w_out[l].astype(BF16),
                               ln1_g[l].reshape(1, -1), ln1_b[l].reshape(1, -1), w_r_hi, w_r_lo, b_r_all)
        h = _moe(x1, oh, route, w_gate[l].astype(BF16), w_up[l].astype(BF16),
                 w_down[l].astype(BF16), ln2_g[l].reshape(1, -1), ln2_b[l].reshape(1, -1))
    return h.reshape(bsz, seq, d)
```

```python
import functools

import numpy as np
import jax
import jax.numpy as jnp
from jax import lax
from jax.experimental import pallas as pl
from jax.experimental.pallas import tpu as pltpu

F32 = jnp.float32
BF16 = jnp.bfloat16

D_MODEL = 1024
DEPTH = 2
A_HEADS = 8
A_HEAD_DIM = 64
A_OUT = 512
IDX_HEADS = 4
IDX_DIM = 64
TOPK_MAX = 256
Q_BLOCK = 128
ROPE_THETA = 500000.0
ROT_DIM = 16
B_HEADS = 4
B_KEY_DIM = 64
B_VAL_DIM = 128
B_OUT = 512
B_GATE_RANK = 16
B_GATE_TAU = 16.0
B_CHUNK = 64
B_SUB = 16
N_SUB = B_CHUNK // B_SUB
C_GROUPS = 4
C_GROUP_DIM = 128
C_OUT = 512
C_CHUNK = 128
N_GROUPS = 4
EXPERTS_PER_GROUP = 8
N_EXPERTS = 32
D_EXPERT = 256
TOP_K_INNER = 2
DN_ALPHA = (2 * DEPTH) ** 0.25
LN_EPS = 1e-5
RMS_EPS = 1e-6

LANES = 128
NEG_BIG = -1e30
INT_MIN = -2 ** 31
VMEM_LIMIT = 56 * 1024 * 1024


def _cparams(sem):
    return pltpu.CompilerParams(dimension_semantics=sem, vmem_limit_bytes=VMEM_LIMIT)


def _layer_norm(x, g, b):
    mu = jnp.mean(x, axis=-1, keepdims=True)
    xc = x - mu
    var = jnp.mean(xc * xc, axis=-1, keepdims=True)
    return xc * lax.rsqrt(var + LN_EPS) * g + b


def _dot(a, b):
    return jnp.dot(a, b, preferred_element_type=F32)


def _dot_nt(a, b):
    return lax.dot_general(a, b, (((1,), (1,)), ((), ())), preferred_element_type=F32)


def _ln_kernel(x_ref, g_ref, b_ref, o_ref):
    o_ref[...] = _layer_norm(x_ref[...], g_ref[...], b_ref[...])


def _entry_ln(x, g, b, tm=1024):
    n = x.shape[0]
    return pl.pallas_call(
        _ln_kernel,
        grid=(n // tm,),
        in_specs=[pl.BlockSpec((tm, D_MODEL), lambda i: (i, 0)),
                  pl.BlockSpec((1, D_MODEL), lambda i: (0, 0)),
                  pl.BlockSpec((1, D_MODEL), lambda i: (0, 0))],
        out_specs=pl.BlockSpec((tm, D_MODEL), lambda i: (i, 0)),
        out_shape=jax.ShapeDtypeStruct((n, D_MODEL), F32),
        compiler_params=_cparams(("parallel",)),
        name="entry_ln",
    )(x, g.reshape(1, -1), b.reshape(1, -1))


def _rope_slab(xs, c, s):
    lane = lax.broadcasted_iota(jnp.int32, xs.shape, 1) % A_HEAD_DIM
    fwd = pltpu.roll(xs, LANES - ROT_DIM // 2, axis=1)
    bwd = pltpu.roll(xs, ROT_DIM // 2, axis=1)
    partner = jnp.where(lane < ROT_DIM // 2, fwd, bwd)
    return xs * c + partner * s


def _proj_a_kernel(h_ref, w_ref, c_ref, s_ref, g_ref,
                   q_ref, k_ref, v_ref, iq_ref, ki_ref, wi_ref):
    p = _dot(h_ref[...].astype(BF16), w_ref[...])
    c = c_ref[...]
    s = s_ref[...]
    lane = lax.broadcasted_iota(jnp.int32, c.shape, 1)
    lo = lane < A_HEAD_DIM
    n_blk = p.shape[0] // Q_BLOCK

    def put_heads(dst_ref, slab, first_head):
        for half in range(2):
            hd = first_head + half
            cols = slab[:, half * A_HEAD_DIM:(half + 1) * A_HEAD_DIM].astype(BF16)
            for qb in range(n_blk):
                dst_ref[qb, hd * Q_BLOCK:(hd + 1) * Q_BLOCK, :] = cols[qb * Q_BLOCK:(qb + 1) * Q_BLOCK, :]

    for i in range(4):
        sl = slice(i * LANES, (i + 1) * LANES)
        put_heads(q_ref, _rope_slab(p[:, sl], c, s) * (A_HEAD_DIM ** -0.5), 2 * i)
    kv = p[:, 512:640]
    kv = _rope_slab(kv, jnp.where(lo, c, 1.0), jnp.where(lo, s, 0.0))
    k_ref[...] = kv[:, :A_HEAD_DIM].astype(BF16)
    v_ones = jnp.where(lo, pltpu.roll(kv, A_HEAD_DIM, axis=1), jnp.where(lane == A_HEAD_DIM, 1.0, 0.0))
    v_ref[...] = v_ones.astype(BF16)
    for i in range(2):
        sl = slice(640 + i * LANES, 640 + (i + 1) * LANES)
        put_heads(iq_ref, _rope_slab(p[:, sl], c, s), 2 * i)
    last = p[:, 896:1024]
    mu = jnp.sum(jnp.where(lo, last, 0.0), axis=-1, keepdims=True) * (1.0 / IDX_DIM)
    xc = jnp.where(lo, last - mu, 0.0)
    var = jnp.sum(xc * xc, axis=-1, keepdims=True) * (1.0 / IDX_DIM)
    kin = xc * lax.rsqrt(var + LN_EPS) * g_ref[...]
    kin = _rope_slab(kin, jnp.where(lo, c, 1.0), jnp.where(lo, s, 0.0))
    ki_ref[...] = kin[:, :IDX_DIM].astype(BF16)
    wi = pltpu.roll(last, LANES - IDX_DIM, axis=1) * (IDX_HEADS ** -0.5 * IDX_DIM ** -0.5)
    for qb in range(n_blk):
        wi_ref[qb] = wi[qb * Q_BLOCK:(qb + 1) * Q_BLOCK, :].T[0:8, :]


def _proj_a(h, w_a, cos_t, sin_t, ikg, tm=512):
    n = h.shape[0]
    row = lambda w: pl.BlockSpec((tm, w), lambda i: (i, 0))
    stacked = lambda heads: pl.BlockSpec((tm // Q_BLOCK, heads * Q_BLOCK, A_HEAD_DIM), lambda i: (i, 0, 0))
    return pl.pallas_call(
        _proj_a_kernel,
        grid=(n // tm,),
        in_specs=[row(D_MODEL),
                  pl.BlockSpec((D_MODEL, 1024), lambda i: (0, 0)),
                  row(LANES), row(LANES),
                  pl.BlockSpec((1, LANES), lambda i: (0, 0))],
        out_specs=[stacked(A_HEADS), row(A_HEAD_DIM),
                   row(LANES),
                   stacked(IDX_HEADS), row(IDX_DIM),
                   pl.BlockSpec((tm // Q_BLOCK, 8, Q_BLOCK), lambda i: (i, 0, 0))],
        out_shape=[jax.ShapeDtypeStruct((n // Q_BLOCK, A_HEADS * Q_BLOCK, A_HEAD_DIM), BF16),
                   jax.ShapeDtypeStruct((n, A_HEAD_DIM), BF16),
                   jax.ShapeDtypeStruct((n, LANES), BF16),
                   jax.ShapeDtypeStruct((n // Q_BLOCK, IDX_HEADS * Q_BLOCK, IDX_DIM), BF16),
                   jax.ShapeDtypeStruct((n, IDX_DIM), BF16),
                   jax.ShapeDtypeStruct((n // Q_BLOCK, 8, Q_BLOCK), F32)],
        compiler_params=_cparams(("parallel",)),
        name="proj_a",
    )(h, w_a, cos_t, sin_t, ikg)


KEY_CHUNK = 512


def _dsa_kernel(q_ref, iq_ref, wi_ref, k_ref, v_ref, ki_ref, o_ref,
                key_ref, bias_ref, *, seq):
    j = pl.program_id(1)
    n_sel = min(TOPK_MAX, seq // 4)
    n_kc = (j * Q_BLOCK + Q_BLOCK + KEY_CHUNK - 1) // KEY_CHUNK
    qpos = j * Q_BLOCK + lax.broadcasted_iota(jnp.int32, (1, Q_BLOCK), 1)
    kk = jnp.minimum(n_sel, qpos + 1).astype(F32)
    row_pos = lax.broadcasted_iota(jnp.int32, (KEY_CHUNK, 1), 0)

    def key_rows(c):
        return pl.ds(pl.multiple_of(c * KEY_CHUNK, KEY_CHUNK), KEY_CHUNK)

    iq = iq_ref[...]
    wi = wi_ref[...]

    def score_chunk(c, carry):
        d = jnp.maximum(_dot_nt(ki_ref[key_rows(c), :], iq), 0.0)
        score = wi[0:1, :] * d[:, 0:Q_BLOCK]
        for h in range(1, IDX_HEADS):
            score = score + wi[h:h + 1, :] * d[:, h * Q_BLOCK:(h + 1) * Q_BLOCK]
        bits = pltpu.bitcast(score + 0.0, jnp.int32)
        key = bits ^ ((bits >> 31) & 0x7FFFFFFF)
        key_ref[c] = jnp.where(c * KEY_CHUNK + row_pos <= qpos, key, INT_MIN)
        return carry

    lax.fori_loop(0, n_kc, score_chunk, 0)

    def count(pred):
        def body(c, acc):
            m = jnp.where(pred(key_ref[c]), 1.0, 0.0)
            return acc + jnp.sum(m.reshape(KEY_CHUNK // 64, 64, Q_BLOCK), axis=0)
        acc = lax.fori_loop(0, n_kc, body, jnp.zeros((64, Q_BLOCK), F32))
        return jnp.sum(acc, axis=0, keepdims=True)

    def search(i, t_u):
        cand_u = t_u | jnp.left_shift(jnp.int32(1), 31 - i)
        cand_s = cand_u ^ INT_MIN
        return jnp.where(count(lambda kc: kc >= cand_s) >= kk, cand_u, t_u)

    thr = lax.fori_loop(0, 32, search, jnp.zeros((1, Q_BLOCK), jnp.int32)) ^ INT_MIN
    need = kk - count(lambda kc: kc > thr)
    r_i = lax.broadcasted_iota(jnp.int32, (KEY_CHUNK, KEY_CHUNK), 0)
    c_i = lax.broadcasted_iota(jnp.int32, (KEY_CHUNK, KEY_CHUNK), 1)
    lower = jnp.where(c_i < r_i, 1.0, 0.0).astype(BF16)

    def fill(c, carry):
        kc = key_ref[c]
        eq = kc == thr
        eq_f = jnp.where(eq, 1.0, 0.0)
        pre = _dot(lower, eq_f.astype(BF16)) + carry
        sel = (kc > thr) | (eq & (pre < need))
        bias_ref[c] = jnp.where(sel, 0.0, NEG_BIG)
        return carry + jnp.sum(eq_f, axis=0, keepdims=True)

    lax.fori_loop(0, n_kc, fill, jnp.zeros((1, Q_BLOCK), F32))

    q = q_ref[...]
    cols = A_HEADS * Q_BLOCK

    def attn_chunk(c, carry):
        m, acc = carry
        rows = key_rows(c)
        lg = _dot_nt(k_ref[rows, :], q) + jnp.concatenate([bias_ref[c]] * A_HEADS, axis=1)
        m_new = jnp.maximum(m, jnp.max(lg, axis=0, keepdims=True))
        p = jnp.exp(lg - m_new).astype(BF16)
        pv = lax.dot_general(v_ref[rows, :], p, (((0,), (0,)), ((), ())),
                             preferred_element_type=F32)
        return m_new, jnp.exp(m - m_new) * acc + pv

    init = (jnp.full((1, cols), -3e38, F32), jnp.zeros((LANES, cols), F32))
    _, acc = lax.fori_loop(0, n_kc, attn_chunk, init)
    o = acc[:A_HEAD_DIM] / acc[A_HEAD_DIM:A_HEAD_DIM + 1]
    for h in range(A_HEADS):
        o_ref[:, h * A_HEAD_DIM:(h + 1) * A_HEAD_DIM] = (
            o[:, h * Q_BLOCK:(h + 1) * Q_BLOCK].T.astype(BF16))


def _dsa(q, iq, wi, k, v, ki, bsz, seq):
    nq = seq // Q_BLOCK
    n_kc = seq // KEY_CHUNK
    stacked = lambda heads: pl.BlockSpec((None, heads * Q_BLOCK, A_HEAD_DIM),
                                         lambda b, j: (b * nq + j, 0, 0))
    kspec = lambda w: pl.BlockSpec((seq, w), lambda b, j: (b, 0))
    return pl.pallas_call(
        functools.partial(_dsa_kernel, seq=seq),
        grid=(bsz, nq),
        in_specs=[stacked(A_HEADS), stacked(IDX_HEADS),
                  pl.BlockSpec((None, 8, Q_BLOCK), lambda b, j: (b * nq + j, 0, 0)),
                  kspec(A_HEAD_DIM),
                  kspec(LANES),
                  kspec(IDX_DIM)],
        out_specs=pl.BlockSpec((Q_BLOCK, A_OUT), lambda b, j: (b * nq + j, 0)),
        out_shape=jax.ShapeDtypeStruct((bsz * seq, A_OUT), BF16),
        scratch_shapes=[pltpu.VMEM((n_kc, KEY_CHUNK, Q_BLOCK), jnp.int32),
                        pltpu.VMEM((n_kc, KEY_CHUNK, Q_BLOCK), F32)],
        compiler_params=_cparams(("parallel", "parallel")),
        name="dsa",
    )(q, iq, wi, k, v, ki)


GLA_QK = B_HEADS * B_KEY_DIM
GLA_REP = N_SUB * GLA_QK
GLA_COLS = 2 * GLA_QK + 2 * B_OUT + LANES
GLA_V0 = 2 * GLA_QK


def _gla_kernel(h_ref, w_ref, wa_ref, ba_ref, ng_ref, o_ref, p_ref, g_ref, acc_ref, st_ref, *, ts):
    @pl.when(pl.program_id(1) == 0)
    def _():
        st_ref[...] = jnp.zeros_like(st_ref)

    p_ref[...] = _dot(h_ref[...].astype(BF16), w_ref[...])
    x = _dot(p_ref[:, GLA_V0 + 2 * B_OUT:].astype(BF16), wa_ref[...]) + ba_ref[...]
    g_ref[...] = (jnp.minimum(x, 0.0) - jnp.log1p(jnp.exp(-jnp.abs(x)))) * (1.0 / B_GATE_TAU)

    t_i = lax.broadcasted_iota(jnp.int32, (B_CHUNK, B_CHUNK), 0)
    s_i = lax.broadcasted_iota(jnp.int32, (B_CHUNK, B_CHUNK), 1)
    tri = t_i >= s_i
    tri_b = jnp.where(tri, 1.0, 0.0).astype(BF16)
    lane_sub = (lax.broadcasted_iota(jnp.int32, (1, GLA_REP), 1) // B_KEY_DIM) % N_SUB
    row_sub = lax.broadcasted_iota(jnp.int32, (B_CHUNK, 1), 0) // B_SUB
    q_mask = lane_sub <= row_sub
    k_mask = lane_sub == row_sub
    low_half = lax.broadcasted_iota(jnp.int32, (B_CHUNK, LANES), 1) < B_KEY_DIM

    def replicate(x):
        outs = []
        for pair in range(GLA_QK // LANES):
            v = x[:, pair * LANES:(pair + 1) * LANES]
            r = pltpu.roll(v, B_KEY_DIM, axis=1)
            outs += [jnp.where(low_half, v, r)] * (N_SUB // 2) + [jnp.where(low_half, r, v)] * (N_SUB // 2)
        return jnp.concatenate(outs, axis=1)

    def chunk(c, carry):
        rows = pl.ds(pl.multiple_of(c * B_CHUNK, B_CHUNK), B_CHUNK)
        g = g_ref[rows, :]
        g_hi = g.astype(BF16)
        g_mid = (g - g_hi.astype(F32)).astype(BF16)
        g_lo = (g - g_hi.astype(F32) - g_mid.astype(F32)).astype(BF16)
        b = replicate(_dot(tri_b, g_hi) + _dot(tri_b, g_mid) + _dot(tri_b, g_lo))
        ref = jnp.zeros((1, GLA_REP), F32)
        for jj in range(1, N_SUB):
            ref = ref + jnp.where(lane_sub == jj, b[jj * B_SUB - 1:jj * B_SUB, :], 0.0)
        d = b - ref
        e_q = jnp.where(q_mask, jnp.exp(jnp.minimum(d, 0.0)), 0.0)
        e_k = jnp.where(k_mask, jnp.exp(jnp.where(k_mask, -d, 0.0)), 0.0)
        b_last = b[B_CHUNK - 1:B_CHUNK, :]
        q_c = (replicate(p_ref[rows, 0:GLA_QK]) * e_q).astype(BF16)
        k_raw = replicate(p_ref[rows, GLA_QK:2 * GLA_QK])
        k_c = (k_raw * e_k).astype(BF16)
        k_st = (k_raw * jnp.exp(b_last - b)).astype(BF16)
        e_last = jnp.exp(b_last)
        for hh in range(B_HEADS):
            base = hh * N_SUB * B_KEY_DIM
            q_h = q_c[:, base:base + N_SUB * B_KEY_DIM]
            k_h = k_c[:, base:base + N_SUB * B_KEY_DIM]
            att = jnp.where(tri, _dot_nt(q_h, k_h), 0.0)
            v_h = p_ref[rows, GLA_V0 + hh * B_VAL_DIM:GLA_V0 + (hh + 1) * B_VAL_DIM].astype(BF16)
            st = st_ref[hh]
            o = _dot(att.astype(BF16), v_h) + _dot_nt(q_h[:, :B_KEY_DIM], st.astype(BF16))
            acc_ref[rows, hh * B_VAL_DIM:(hh + 1) * B_VAL_DIM] = o
            upd = _dot(v_h.T, k_st[:, base:base + B_KEY_DIM])
            st_ref[hh] = st * e_last[:, base:base + B_KEY_DIM] + upd
        return carry

    lax.fori_loop(0, ts // B_CHUNK, chunk, 0, unroll=2)

    r = p_ref[:, GLA_V0 + B_OUT:GLA_V0 + 2 * B_OUT]
    gate = r * (1.0 / (1.0 + jnp.exp(-r)))
    ng = ng_ref[...]
    for hh in range(B_HEADS):
        sl = slice(hh * B_VAL_DIM, (hh + 1) * B_VAL_DIM)
        o = acc_ref[:, sl]
        y = o * lax.rsqrt(jnp.mean(o * o, axis=-1, keepdims=True) + RMS_EPS) * ng
        o_ref[:, sl] = (gate[:, sl] * y).astype(BF16)


def _gla(h, w_b, wa, ba, norm_g, bsz, seq, ts=512):
    ns = seq // ts
    const = lambda shape: pl.BlockSpec(shape, lambda b, i: (0, 0))
    return pl.pallas_call(
        functools.partial(_gla_kernel, ts=ts),
        grid=(bsz, ns),
        in_specs=[pl.BlockSpec((ts, D_MODEL), lambda b, i: (b * ns + i, 0)),
                  const((D_MODEL, GLA_COLS)), const((LANES, GLA_QK)),
                  const((1, GLA_QK)), const((1, B_VAL_DIM))],
        out_specs=pl.BlockSpec((ts, B_OUT), lambda b, i: (b * ns + i, 0)),
        out_shape=jax.ShapeDtypeStruct((bsz * seq, B_OUT), BF16),
        scratch_shapes=[pltpu.VMEM((ts, GLA_COLS), F32),
                        pltpu.VMEM((ts, GLA_QK), F32),
                        pltpu.VMEM((ts, B_OUT), F32),
                        pltpu.VMEM((B_HEADS, B_VAL_DIM, B_KEY_DIM), F32)],
        compiler_params=_cparams(("parallel", "arbitrary")),
        name="gla",
    )(h, w_b, wa, ba, norm_g)


def _gmlp_kernel(h_ref, w_ref, lg_ref, lb_ref, ws_ref, bs_ref, o_ref, *, tm):
    z = _dot(h_ref[...].astype(BF16), w_ref[...])
    z = z * (0.5 * (1.0 + jnp.tanh(np.sqrt(2.0 / np.pi) * (z + 0.044715 * (z * z * z)))))
    u = z[:, :C_OUT]
    v = _layer_norm(z[:, C_OUT:], lg_ref[...], lb_ref[...]).astype(BF16)
    t_i = lax.broadcasted_iota(jnp.int32, (C_CHUNK, C_CHUNK), 0)
    s_i = lax.broadcasted_iota(jnp.int32, (C_CHUNK, C_CHUNK), 1)
    bs = bs_ref[...]
    for g in range(C_GROUPS):
        w = jnp.where(t_i >= s_i, ws_ref[g], 0.0).astype(BF16)
        sl = slice(g * C_GROUP_DIM, (g + 1) * C_GROUP_DIM)
        for c in range(tm // C_CHUNK):
            rows = slice(c * C_CHUNK, (c + 1) * C_CHUNK)
            mixed = _dot(w, v[rows, sl]) + bs[:, g:g + 1]
            o_ref[rows, sl] = (u[rows, sl] * mixed).astype(BF16)


def _gmlp(h, w_c, ln_g, ln_b, ws, bs_t, tm=512):
    n = h.shape[0]
    return pl.pallas_call(
        functools.partial(_gmlp_kernel, tm=tm),
        grid=(n // tm,),
        in_specs=[pl.BlockSpec((tm, D_MODEL), lambda i: (i, 0)),
                  pl.BlockSpec((D_MODEL, 2 * C_OUT), lambda i: (0, 0)),
                  pl.BlockSpec((1, C_OUT), lambda i: (0, 0)),
                  pl.BlockSpec((1, C_OUT), lambda i: (0, 0)),
                  pl.BlockSpec((C_GROUPS, C_CHUNK, C_CHUNK), lambda i: (0, 0, 0)),
                  pl.BlockSpec((C_CHUNK, LANES), lambda i: (0, 0))],
        out_specs=pl.BlockSpec((tm, C_OUT), lambda i: (i, 0)),
        out_shape=jax.ShapeDtypeStruct((n, C_OUT), BF16),
        compiler_params=_cparams(("parallel",)),
        name="gmlp",
    )(h, w_c, ln_g, ln_b, ws, bs_t)


def _sigmoid(x):
    return 1.0 / (1.0 + jnp.exp(-x))


def _merge_kernel(h_ref, ya_ref, yb_ref, yc_ref, wg_ref, wa_ref, wb_ref, wc_ref, wo_ref,
                  lg_ref, lb_ref, wr_ref, wrl_ref, br_ref, x_ref, oh_ref, route_ref):
    h = h_ref[...]
    hb = h.astype(BF16)
    acc = None
    for i, (y_ref, wbr_ref) in enumerate(((ya_ref, wa_ref), (yb_ref, wb_ref), (yc_ref, wc_ref))):
        gate = _sigmoid(_dot(hb, wg_ref[:, i * D_MODEL:(i + 1) * D_MODEL]))
        term = gate * _dot(y_ref[...], wbr_ref[...])
        acc = term if acc is None else acc + term
    mix = _dot(acc.astype(BF16), wo_ref[...])
    x1 = _layer_norm(DN_ALPHA * h + mix, lg_ref[...], lb_ref[...])
    x_ref[...] = x1

    x_hi = x1.astype(BF16)
    x_lo = (x1 - x_hi.astype(F32)).astype(BF16)
    logit = (_dot(x_hi, wr_ref[...]) + _dot(x_hi, wrl_ref[...]) + _dot(x_lo, wr_ref[...])) + br_ref[...]
    lane = lax.broadcasted_iota(jnp.int32, logit.shape, 1)
    big = jnp.int32(LANES)
    is_g = (lane >= N_EXPERTS) & (lane < N_EXPERTS + N_GROUPS)
    gl = jnp.where(is_g, logit, -jnp.inf)
    g_max = jnp.max(gl, axis=-1, keepdims=True)
    g_idx = jnp.min(jnp.where(is_g & (gl == g_max), lane, big), axis=-1, keepdims=True) - N_EXPERTS
    g_p = 1.0 / jnp.sum(jnp.where(is_g, jnp.exp(gl - g_max), 0.0), axis=-1, keepdims=True)
    lo = g_idx * EXPERTS_PER_GROUP
    in_g = (lane >= lo) & (lane < lo + EXPERTS_PER_GROUP)
    el = jnp.where(in_g, logit, -jnp.inf)
    e1 = jnp.max(el, axis=-1, keepdims=True)
    i1 = jnp.min(jnp.where(in_g & (el == e1), lane, big), axis=-1, keepdims=True)
    el2 = jnp.where(lane == i1, -jnp.inf, el)
    e2 = jnp.max(el2, axis=-1, keepdims=True)
    i2 = jnp.min(jnp.where(in_g & (lane != i1) & (el2 == e2), lane, big), axis=-1, keepdims=True)
    z2 = jnp.exp(e2 - e1)
    den = 1.0 + z2
    oh_ref[...] = jnp.where((lane == i1) | (lane == i2), 1.0, 0.0).astype(BF16)
    route_ref[...] = jnp.where(lane == 0, i1.astype(F32),
                               jnp.where(lane == 1, i2.astype(F32),
                                         jnp.where(lane == 2, (1.0 / den) * g_p,
                                                   jnp.where(lane == 3, (z2 / den) * g_p, 0.0))))


def _merge(h, ya, yb, yc, wg, wa, wb, wc, wo, ln_g, ln_b, wr, wr_lo, br, tm=1024):
    n = h.shape[0]
    row = lambda w: pl.BlockSpec((tm, w), lambda i: (i, 0))
    const = lambda r, c: pl.BlockSpec((r, c), lambda i: (0, 0), pipeline_mode=pl.Buffered(1))
    return pl.pallas_call(
        _merge_kernel,
        grid=(n // tm,),
        in_specs=[row(D_MODEL), row(A_OUT), row(B_OUT), row(C_OUT),
                  const(D_MODEL, 3 * D_MODEL), const(A_OUT, D_MODEL), const(B_OUT, D_MODEL),
                  const(C_OUT, D_MODEL), const(D_MODEL, D_MODEL),
                  const(1, D_MODEL), const(1, D_MODEL), const(D_MODEL, LANES), const(D_MODEL, LANES),
                  const(1, LANES)],
        out_specs=[row(D_MODEL), row(LANES), row(LANES)],
        out_shape=[jax.ShapeDtypeStruct((n, D_MODEL), F32),
                   jax.ShapeDtypeStruct((n, LANES), BF16),
                   jax.ShapeDtypeStruct((n, LANES), F32)],
        compiler_params=_cparams(("parallel",)),
        name="merge",
    )(h, ya, yb, yc, wg, wa, wb, wc, wo, ln_g, ln_b, wr, wr_lo, br)


FFN_TILE = 512
MOE_TOK = 512


def _rank_kernel(oh_ref, route_ref, r_ref, cnt_ref, carry_ref):
    @pl.when(pl.program_id(0) == 0)
    def _():
        carry_ref[...] = jnp.zeros_like(carry_ref)

    oh = oh_ref[...]
    tt = oh.shape[0]
    r_i = lax.broadcasted_iota(jnp.int32, (tt, tt), 0)
    c_i = lax.broadcasted_iota(jnp.int32, (tt, tt), 1)
    lower = jnp.where(c_i < r_i, 1.0, 0.0).astype(BF16)
    rank = _dot(lower, oh) + carry_ref[...]
    route = route_ref[...]
    lane = lax.broadcasted_iota(jnp.int32, rank.shape, 1)
    i1 = route[:, 0:1].astype(jnp.int32)
    i2 = route[:, 1:2].astype(jnp.int32)
    r1 = jnp.sum(jnp.where(lane == i1, rank, 0.0), axis=-1, keepdims=True)
    r2 = jnp.sum(jnp.where(lane == i2, rank, 0.0), axis=-1, keepdims=True)
    r_ref[...] = jnp.where(lane == 0, r1, jnp.where(lane == 1, r2, 0.0))
    carry_ref[...] += jnp.sum(oh.astype(F32), axis=0, keepdims=True)
    cnt_ref[...] = carry_ref[...]


def _rank(oh, route, tt=512):
    n = oh.shape[0]
    return pl.pallas_call(
        _rank_kernel,
        grid=(n // tt,),
        in_specs=[pl.BlockSpec((tt, LANES), lambda i: (i, 0)),
                  pl.BlockSpec((tt, LANES), lambda i: (i, 0))],
        out_specs=[pl.BlockSpec((tt, LANES), lambda i: (i, 0)),
                   pl.BlockSpec((1, LANES), lambda i: (0, 0))],
        out_shape=[jax.ShapeDtypeStruct((n, LANES), F32),
                   jax.ShapeDtypeStruct((1, LANES), F32)],
        scratch_shapes=[pltpu.VMEM((1, LANES), F32)],
        compiler_params=_cparams(("arbitrary",)),
        name="moe_rank",
    )(oh, route)


def _row_copy(src_ref, src_row, dst_ref, dst_row, sem):
    return pltpu.make_async_copy(src_ref.at[pl.ds(src_row, 1), :], dst_ref.at[pl.ds(dst_row, 1), :], sem)


def _dispatch_kernel(pos1_ref, pos2_ref, x_ref, init_ref, xs_ref, sem):
    del init_ref
    base = pl.program_id(0) * MOE_TOK

    def issue(t, carry):
        _row_copy(x_ref, t, xs_ref, pos1_ref[base + t], sem).start()
        _row_copy(x_ref, t, xs_ref, pos2_ref[base + t], sem).start()
        return carry

    lax.fori_loop(0, MOE_TOK, issue, 0, unroll=8)
    for _ in range(2):
        pltpu.make_async_copy(x_ref, xs_ref.at[pl.ds(0, MOE_TOK), :], sem).wait()


def _dispatch(pos1, pos2, x1, n_slots):
    n = x1.shape[0]
    init = jnp.zeros((n_slots, D_MODEL), F32)
    return pl.pallas_call(
        _dispatch_kernel,
        grid_spec=pltpu.PrefetchScalarGridSpec(
            num_scalar_prefetch=2,
            grid=(n // MOE_TOK,),
            in_specs=[pl.BlockSpec((MOE_TOK, D_MODEL), lambda i, p1, p2: (i, 0)),
                      pl.BlockSpec(memory_space=pl.ANY)],
            out_specs=pl.BlockSpec(memory_space=pl.ANY),
            scratch_shapes=[pltpu.SemaphoreType.DMA(())]),
        out_shape=jax.ShapeDtypeStruct((n_slots, D_MODEL), F32),
        input_output_aliases={3: 0},
        compiler_params=_cparams(("arbitrary",)),
        name="moe_dispatch",
    )(pos1, pos2, x1, init)


def _ffn_kernel(te_ref, nu_ref, xs_ref, wg_ref, wu_ref, wd_ref, ys_ref):
    del te_ref
    i = pl.program_id(0)

    @pl.when(i < nu_ref[0])
    def _():
        xb = xs_ref[...].astype(BF16)
        gte = _dot(xb, wg_ref[...])
        hid = gte * _sigmoid(gte) * _dot(xb, wu_ref[...])
        ys_ref[...] = _dot(hid.astype(BF16), wd_ref[...])

    @pl.when(i >= nu_ref[0])
    def _():
        ys_ref[...] = jnp.zeros_like(ys_ref)


def _ffn(tile_expert, n_used, xs, w_gate, w_up, w_down):
    n_slots = xs.shape[0]
    wspec = lambda r, c: pl.BlockSpec((None, r, c), lambda i, te, nu: (te[i], 0, 0))
    return pl.pallas_call(
        _ffn_kernel,
        grid_spec=pltpu.PrefetchScalarGridSpec(
            num_scalar_prefetch=2,
            grid=(n_slots // FFN_TILE,),
            in_specs=[pl.BlockSpec((FFN_TILE, D_MODEL), lambda i, te, nu: (i, 0)),
                      wspec(D_MODEL, D_EXPERT), wspec(D_MODEL, D_EXPERT), wspec(D_EXPERT, D_MODEL)],
            out_specs=pl.BlockSpec((FFN_TILE, D_MODEL), lambda i, te, nu: (i, 0))),
        out_shape=jax.ShapeDtypeStruct((n_slots, D_MODEL), F32),
        compiler_params=_cparams(("arbitrary",)),
        name="moe_ffn",
    )(tile_expert, n_used, xs, w_gate, w_up, w_down)


def _combine_kernel(pos1_ref, pos2_ref, x_ref, route_ref, lg_ref, lb_ref, ys_ref, o_ref, buf_ref, sem):
    i = pl.program_id(0)

    def gather(step, slot):
        base = step * MOE_TOK

        def issue(t, carry):
            _row_copy(ys_ref, pos1_ref[base + t], buf_ref.at[slot, 0], t, sem.at[slot]).start()
            _row_copy(ys_ref, pos2_ref[base + t], buf_ref.at[slot, 1], t, sem.at[slot]).start()
            return carry

        lax.fori_loop(0, MOE_TOK, issue, 0, unroll=8)

    @pl.when(i == 0)
    def _():
        gather(0, 0)

    @pl.when(i + 1 < pl.num_programs(0))
    def _():
        gather(i + 1, (i + 1) % 2)

    slot = i % 2
    for s in range(2):
        pltpu.make_async_copy(ys_ref.at[pl.ds(0, MOE_TOK), :], buf_ref.at[slot, s], sem.at[slot]).wait()
    route = route_ref[...]
    ffn = route[:, 2:3] * buf_ref[slot, 0] + route[:, 3:4] * buf_ref[slot, 1]
    o_ref[...] = _layer_norm(DN_ALPHA * x_ref[...] + ffn, lg_ref[...], lb_ref[...])


def _combine(pos1, pos2, x1, route, ln_g, ln_b, ys):
    n = x1.shape[0]
    row = lambda w: pl.BlockSpec((MOE_TOK, w), lambda i, p1, p2: (i, 0))
    const = lambda w: pl.BlockSpec((1, w), lambda i, p1, p2: (0, 0))
    return pl.pallas_call(
        _combine_kernel,
        grid_spec=pltpu.PrefetchScalarGridSpec(
            num_scalar_prefetch=2,
            grid=(n // MOE_TOK,),
            in_specs=[row(D_MODEL), row(LANES), const(D_MODEL), const(D_MODEL),
                      pl.BlockSpec(memory_space=pl.ANY)],
            out_specs=row(D_MODEL),
            scratch_shapes=[pltpu.VMEM((2, 2, MOE_TOK, D_MODEL), F32),
                            pltpu.SemaphoreType.DMA((2,))]),
        out_shape=jax.ShapeDtypeStruct((n, D_MODEL), F32),
        compiler_params=_cparams(("arbitrary",)),
        name="moe_combine",
    )(pos1, pos2, x1, route, ln_g, ln_b, ys)


def _moe(x1, oh, route, w_gate, w_up, w_down, ln_g, ln_b):
    n = x1.shape[0]
    n_slots = TOP_K_INNER * n + N_EXPERTS * FFN_TILE
    rank, cnt = _rank(oh, route)
    cnt = cnt[0, :N_EXPERTS].astype(jnp.int32)
    padded = (cnt + FFN_TILE - 1) // FFN_TILE * FFN_TILE
    ends = jnp.cumsum(padded)
    offs = ends - padded
    experts = jnp.arange(N_EXPERTS, dtype=jnp.int32)
    i1 = route[:, 0].astype(jnp.int32)
    i2 = route[:, 1].astype(jnp.int32)
    off_of = lambda idx: jnp.sum(jnp.where(idx[:, None] == experts[None, :], offs[None, :], 0), axis=1)
    pos1 = off_of(i1) + rank[:, 0].astype(jnp.int32)
    pos2 = off_of(i2) + rank[:, 1].astype(jnp.int32)
    tile_start = jnp.arange(n_slots // FFN_TILE, dtype=jnp.int32) * FFN_TILE
    tile_expert = jnp.minimum(jnp.sum(tile_start[:, None] >= ends[None, :], axis=1), N_EXPERTS - 1)
    n_used = (ends[-1] // FFN_TILE).reshape(1)

    xs = _dispatch(pos1, pos2, x1, n_slots)
    ys = _ffn(tile_expert.astype(jnp.int32), n_used.astype(jnp.int32), xs, w_gate, w_up, w_down)
    return _combine(pos1, pos2, x1, route, ln_g, ln_b, ys)


def _rope_tables(positions):
    inv_freq = ROPE_THETA ** (-jnp.arange(0, ROT_DIM, 2, dtype=F32) / ROT_DIM)
    ang = positions.astype(F32).reshape(-1, 1) * inv_freq
    cos, sin = jnp.cos(ang), jnp.sin(ang)
    n = ang.shape[0]
    half = ROT_DIM // 2
    c64 = jnp.concatenate([cos, cos, jnp.ones((n, A_HEAD_DIM - ROT_DIM), F32)], axis=1)
    s64 = jnp.concatenate([-sin, sin, jnp.zeros((n, A_HEAD_DIM - ROT_DIM), F32)], axis=1)
    del half
    return jnp.tile(c64, (1, 2)), jnp.tile(s64, (1, 2))


def _split_w_in(w):
    widths = (A_OUT, A_HEAD_DIM, A_HEAD_DIM, IDX_HEADS * IDX_DIM, IDX_DIM, IDX_HEADS,
              B_HEADS * B_KEY_DIM, B_HEADS * B_KEY_DIM, B_OUT, B_GATE_RANK, B_OUT,
              2 * C_OUT, D_MODEL, D_MODEL, D_MODEL)
    pts = np.cumsum((0,) + widths)
    return [w[:, int(pts[i]):int(pts[i + 1])] for i in range(len(widths))]


def kernel(x, positions, ln_in_g, ln_in_b, w_in, idx_k_g, gla_wa2, gla_ba, gla_norm_g, gm_ln_g, gm_ln_b, gm_ws, gm_bs, w_branch_a, w_branch_b, w_branch_c, w_out, ln1_g, ln1_b, w_rg, b_rg, w_re, b_re, w_gate, w_up, w_down, ln2_g, ln2_b):
    bsz, seq, d = x.shape
    n = bsz * seq
    cos_t, sin_t = _rope_tables(positions)
    h = _entry_ln(x.reshape(n, d), ln_in_g, ln_in_b)
    for l in range(DEPTH):
        (a_q, a_k, a_v, i_q, i_k, i_w, b_q, b_k, b_v, b_glr, b_r, c_uv,
         g_a, g_b, g_c) = _split_w_in(w_in[l])
        zpad = lambda c: jnp.zeros((d, c), F32)
        w_a = jnp.concatenate([a_q, a_k, a_v, i_q, i_k, i_w, zpad(LANES - IDX_DIM - IDX_HEADS)],
                              axis=1).astype(BF16)
        w_b = jnp.concatenate([b_q * (B_KEY_DIM ** -0.5), b_k, b_v, b_r,
                               b_glr, zpad(LANES - B_GATE_RANK)], axis=1).astype(BF16)
        wa_pad = jnp.concatenate([gla_wa2[l],
                                  jnp.zeros((LANES - B_GATE_RANK, GLA_QK), F32)], axis=0).astype(BF16)
        w_g = jnp.concatenate([g_a, g_b, g_c], axis=1).astype(BF16)
        ikg = jnp.concatenate([idx_k_g[l], jnp.zeros((LANES - IDX_DIM,), F32)]).reshape(1, LANES)
        bs_t = jnp.concatenate([gm_bs[l].T, jnp.zeros((C_CHUNK, LANES - C_GROUPS), F32)], axis=1)
        w_r = jnp.concatenate([w_re[l], w_rg[l], zpad(LANES - N_GROUPS - N_EXPERTS)], axis=1)
        w_r_hi = w_r.astype(BF16)
        w_r_lo = (w_r - w_r_hi.astype(F32)).astype(BF16)
        b_r_all = jnp.concatenate([b_re[l], b_rg[l],
                                   jnp.zeros((LANES - N_GROUPS - N_EXPERTS,), F32)]).reshape(1, LANES)

        q, k, v, iq, ki, wi = _proj_a(h, w_a, cos_t, sin_t, ikg)
        y_a = _dsa(q, iq, wi, k, v, ki, bsz, seq)
        y_b = _gla(h, w_b, wa_pad, gla_ba[l].reshape(1, -1), gla_norm_g[l].reshape(1, -1), bsz, seq)
        y_c = _gmlp(h, c_uv.astype(BF16), gm_ln_g[l].reshape(1, -1), gm_ln_b[l].reshape(1, -1),
                    gm_ws[l], bs_t)
        x1, oh, route = _merge(h, y_a, y_b, y_c, w_g,
                               w_branch_a[l].astype(BF16), w_branch_b[l].astype(BF16),
                               w_branch_c[l].astype(BF16), w_out[l].astype(BF16),
                               ln1_g[l].reshape(1, -1), ln1_b[l].reshape(1, -1), w_r_hi, w_r_lo, b_r_all)
        h = _moe(x1, oh, route, w_gate[l].astype(BF16), w_up[l].astype(BF16),
                 w_down[l].astype(BF16), ln2_g[l].reshape(1, -1), ln2_b[l].reshape(1, -1))
    return h.reshape(bsz, seq, d)
```

```python
import functools

import numpy as np
import jax
import jax.numpy as jnp
from jax import lax
from jax.experimental import pallas as pl
from jax.experimental.pallas import tpu as pltpu

F32 = jnp.float32
BF16 = jnp.bfloat16

D_MODEL = 1024
DEPTH = 2
A_HEADS = 8
A_HEAD_DIM = 64
A_OUT = 512
IDX_HEADS = 4
IDX_DIM = 64
TOPK_MAX = 256
Q_BLOCK = 128
ROPE_THETA = 500000.0
ROT_DIM = 16
B_HEADS = 4
B_KEY_DIM = 64
B_VAL_DIM = 128
B_OUT = 512
B_GATE_RANK = 16
B_GATE_TAU = 16.0
B_CHUNK = 64
B_SUB = 16
N_SUB = B_CHUNK // B_SUB
C_GROUPS = 4
C_GROUP_DIM = 128
C_OUT = 512
C_CHUNK = 128
N_GROUPS = 4
EXPERTS_PER_GROUP = 8
N_EXPERTS = 32
D_EXPERT = 256
TOP_K_INNER = 2
DN_ALPHA = (2 * DEPTH) ** 0.25
LN_EPS = 1e-5
RMS_EPS = 1e-6

LANES = 128
NEG_BIG = -1e30
LOG2_E = 1.4426950408889634
INT_MIN = -2 ** 31
VMEM_LIMIT = 56 * 1024 * 1024


def _cparams(sem):
    return pltpu.CompilerParams(dimension_semantics=sem, vmem_limit_bytes=VMEM_LIMIT)


def _layer_norm(x, g, b):
    mu = jnp.mean(x, axis=-1, keepdims=True)
    xc = x - mu
    var = jnp.mean(xc * xc, axis=-1, keepdims=True)
    return xc * lax.rsqrt(var + LN_EPS) * g + b


def _dot(a, b):
    return jnp.dot(a, b, preferred_element_type=F32)


def _dot_nt(a, b):
    return lax.dot_general(a, b, (((1,), (1,)), ((), ())), preferred_element_type=F32)


def _ln_kernel(x_ref, g_ref, b_ref, o_ref):
    o_ref[...] = _layer_norm(x_ref[...], g_ref[...], b_ref[...])


def _entry_ln(x, g, b, tm=1024):
    n = x.shape[0]
    return pl.pallas_call(
        _ln_kernel,
        grid=(n // tm,),
        in_specs=[pl.BlockSpec((tm, D_MODEL), lambda i: (i, 0)),
                  pl.BlockSpec((1, D_MODEL), lambda i: (0, 0)),
                  pl.BlockSpec((1, D_MODEL), lambda i: (0, 0))],
        out_specs=pl.BlockSpec((tm, D_MODEL), lambda i: (i, 0)),
        out_shape=jax.ShapeDtypeStruct((n, D_MODEL), F32),
        compiler_params=_cparams(("parallel",)),
        name="entry_ln",
    )(x, g.reshape(1, -1), b.reshape(1, -1))


def _rope_slab(xs, c, s):
    lane = lax.broadcasted_iota(jnp.int32, xs.shape, 1) % A_HEAD_DIM
    fwd = pltpu.roll(xs, LANES - ROT_DIM // 2, axis=1)
    bwd = pltpu.roll(xs, ROT_DIM // 2, axis=1)
    partner = jnp.where(lane < ROT_DIM // 2, fwd, bwd)
    return xs * c + partner * s


def _proj_a_kernel(h_ref, w_ref, c_ref, s_ref, g_ref,
                   q_ref, k_ref, v_ref, iq_ref, ki_ref, wi_ref):
    p = _dot(h_ref[...].astype(BF16), w_ref[...])
    c = c_ref[...]
    s = s_ref[...]
    lane = lax.broadcasted_iota(jnp.int32, c.shape, 1)
    lo = lane < A_HEAD_DIM
    n_blk = p.shape[0] // Q_BLOCK

    def put_heads(dst_ref, slab, first_head):
        for half in range(2):
            hd = first_head + half
            cols = slab[:, half * A_HEAD_DIM:(half + 1) * A_HEAD_DIM].astype(BF16)
            for qb in range(n_blk):
                dst_ref[qb, hd * Q_BLOCK:(hd + 1) * Q_BLOCK, :] = cols[qb * Q_BLOCK:(qb + 1) * Q_BLOCK, :]

    for i in range(4):
        sl = slice(i * LANES, (i + 1) * LANES)
        put_heads(q_ref, _rope_slab(p[:, sl], c, s) * (A_HEAD_DIM ** -0.5 * LOG2_E), 2 * i)
    kv = p[:, 512:640]
    kv = _rope_slab(kv, jnp.where(lo, c, 1.0), jnp.where(lo, s, 0.0))
    k_ref[...] = kv[:, :A_HEAD_DIM].astype(BF16)
    v_ones = jnp.where(lo, pltpu.roll(kv, A_HEAD_DIM, axis=1), jnp.where(lane == A_HEAD_DIM, 1.0, 0.0))
    v_ref[...] = v_ones.astype(BF16)
    for i in range(2):
        sl = slice(640 + i * LANES, 640 + (i + 1) * LANES)
        put_heads(iq_ref, _rope_slab(p[:, sl], c, s), 2 * i)
    last = p[:, 896:1024]
    mu = jnp.sum(jnp.where(lo, last, 0.0), axis=-1, keepdims=True) * (1.0 / IDX_DIM)
    xc = jnp.where(lo, last - mu, 0.0)
    var = jnp.sum(xc * xc, axis=-1, keepdims=True) * (1.0 / IDX_DIM)
    kin = xc * lax.rsqrt(var + LN_EPS) * g_ref[...]
    kin = _rope_slab(kin, jnp.where(lo, c, 1.0), jnp.where(lo, s, 0.0))
    ki_ref[...] = kin[:, :IDX_DIM].astype(BF16)
    wi = pltpu.roll(last, LANES - IDX_DIM, axis=1) * (IDX_HEADS ** -0.5 * IDX_DIM ** -0.5)
    for qb in range(n_blk):
        wi_ref[qb] = wi[qb * Q_BLOCK:(qb + 1) * Q_BLOCK, :].T[0:8, :]


def _proj_a(h, w_a, cos_t, sin_t, ikg, tm=512):
    n = h.shape[0]
    row = lambda w: pl.BlockSpec((tm, w), lambda i: (i, 0))
    stacked = lambda heads: pl.BlockSpec((tm // Q_BLOCK, heads * Q_BLOCK, A_HEAD_DIM), lambda i: (i, 0, 0))
    return pl.pallas_call(
        _proj_a_kernel,
        grid=(n // tm,),
        in_specs=[row(D_MODEL),
                  pl.BlockSpec((D_MODEL, 1024), lambda i: (0, 0)),
                  row(LANES), row(LANES),
                  pl.BlockSpec((1, LANES), lambda i: (0, 0))],
        out_specs=[stacked(A_HEADS), row(A_HEAD_DIM),
                   row(LANES),
                   stacked(IDX_HEADS), row(IDX_DIM),
                   pl.BlockSpec((tm // Q_BLOCK, 8, Q_BLOCK), lambda i: (i, 0, 0))],
        out_shape=[jax.ShapeDtypeStruct((n // Q_BLOCK, A_HEADS * Q_BLOCK, A_HEAD_DIM), BF16),
                   jax.ShapeDtypeStruct((n, A_HEAD_DIM), BF16),
                   jax.ShapeDtypeStruct((n, LANES), BF16),
                   jax.ShapeDtypeStruct((n // Q_BLOCK, IDX_HEADS * Q_BLOCK, IDX_DIM), BF16),
                   jax.ShapeDtypeStruct((n, IDX_DIM), BF16),
                   jax.ShapeDtypeStruct((n // Q_BLOCK, 8, Q_BLOCK), F32)],
        compiler_params=_cparams(("parallel",)),
        name="proj_a",
    )(h, w_a, cos_t, sin_t, ikg)


KEY_CHUNK = 512


def _dsa_kernel(q_ref, iq_ref, wi_ref, k_ref, v_ref, ki_ref, o_ref,
                key_ref, bias_ref, *, seq):
    j = pl.program_id(1)
    n_sel = min(TOPK_MAX, seq // 4)
    n_kc = (j * Q_BLOCK + Q_BLOCK + KEY_CHUNK - 1) // KEY_CHUNK
    qpos = j * Q_BLOCK + lax.broadcasted_iota(jnp.int32, (1, Q_BLOCK), 1)
    kk = jnp.minimum(n_sel, qpos + 1).astype(F32)
    row_pos = lax.broadcasted_iota(jnp.int32, (KEY_CHUNK, 1), 0)

    def key_rows(c):
        return pl.ds(pl.multiple_of(c * KEY_CHUNK, KEY_CHUNK), KEY_CHUNK)

    iq = iq_ref[...]
    wi = wi_ref[...]

    def score_chunk(c, carry):
        d = jnp.maximum(_dot_nt(ki_ref[key_rows(c), :], iq), 0.0)
        score = wi[0:1, :] * d[:, 0:Q_BLOCK]
        for h in range(1, IDX_HEADS):
            score = score + wi[h:h + 1, :] * d[:, h * Q_BLOCK:(h + 1) * Q_BLOCK]
        bits = pltpu.bitcast(score + 0.0, jnp.int32)
        key = bits ^ ((bits >> 31) & 0x7FFFFFFF)
        key_ref[c] = jnp.where(c * KEY_CHUNK + row_pos <= qpos, key, INT_MIN)
        return carry

    lax.fori_loop(0, n_kc, score_chunk, 0)

    def count(pred):
        def body(c, acc):
            m = jnp.where(pred(key_ref[c]), 1.0, 0.0)
            return acc + jnp.sum(m.reshape(KEY_CHUNK // 64, 64, Q_BLOCK), axis=0)
        acc = lax.fori_loop(0, n_kc, body, jnp.zeros((64, Q_BLOCK), F32))
        return jnp.sum(acc, axis=0, keepdims=True)

    def search(i, t_u):
        cand_u = t_u | jnp.left_shift(jnp.int32(1), 31 - i)
        cand_s = cand_u ^ INT_MIN
        return jnp.where(count(lambda kc: kc >= cand_s) >= kk, cand_u, t_u)

    thr = lax.fori_loop(0, 32, search, jnp.zeros((1, Q_BLOCK), jnp.int32)) ^ INT_MIN
    need = kk - count(lambda kc: kc > thr)
    r_i = lax.broadcasted_iota(jnp.int32, (LANES, LANES), 0)
    c_i = lax.broadcasted_iota(jnp.int32, (LANES, LANES), 1)
    lower = jnp.where(c_i < r_i, 1.0, 0.0).astype(BF16)

    def fill(c, carry):
        for t in range(KEY_CHUNK // LANES):
            rows = slice(t * LANES, (t + 1) * LANES)
            kt = key_ref[c, rows, :]
            eq = kt == thr
            eq_f = jnp.where(eq, 1.0, 0.0)
            pre = _dot(lower, eq_f.astype(BF16)) + carry
            sel = (kt > thr) | (eq & (pre < need))
            bias_ref[c, rows, :] = jnp.where(sel, 0.0, NEG_BIG)
            carry = carry + jnp.sum(eq_f, axis=0, keepdims=True)
        return carry

    lax.fori_loop(0, n_kc, fill, jnp.zeros((1, Q_BLOCK), F32))

    q = q_ref[...]
    cols = A_HEADS * Q_BLOCK

    def attn_chunk(c, carry):
        m, acc = carry
        rows = key_rows(c)
        lg = _dot_nt(k_ref[rows, :], q) + jnp.concatenate([bias_ref[c]] * A_HEADS, axis=1)
        m_new = jnp.maximum(m, jnp.max(lg, axis=0, keepdims=True))
        p = jnp.exp2(lg - m_new).astype(BF16)
        pv = lax.dot_general(v_ref[rows, :], p, (((0,), (0,)), ((), ())),
                             preferred_element_type=F32)
        return m_new, jnp.exp2(m - m_new) * acc + pv

    init = (jnp.full((1, cols), -3e38, F32), jnp.zeros((LANES, cols), F32))
    _, acc = lax.fori_loop(0, n_kc, attn_chunk, init)
    o = acc[:A_HEAD_DIM] / acc[A_HEAD_DIM:A_HEAD_DIM + 1]
    for h in range(A_HEADS):
        o_ref[:, h * A_HEAD_DIM:(h + 1) * A_HEAD_DIM] = (
            o[:, h * Q_BLOCK:(h + 1) * Q_BLOCK].T.astype(BF16))


def _dsa(q, iq, wi, k, v, ki, bsz, seq):
    nq = seq // Q_BLOCK
    n_kc = seq // KEY_CHUNK
    stacked = lambda heads: pl.BlockSpec((None, heads * Q_BLOCK, A_HEAD_DIM),
                                         lambda b, j: (b * nq + j, 0, 0))
    kspec = lambda w: pl.BlockSpec((seq, w), lambda b, j: (b, 0))
    return pl.pallas_call(
        functools.partial(_dsa_kernel, seq=seq),
        grid=(bsz, nq),
        in_specs=[stacked(A_HEADS), stacked(IDX_HEADS),
                  pl.BlockSpec((None, 8, Q_BLOCK), lambda b, j: (b * nq + j, 0, 0)),
                  kspec(A_HEAD_DIM),
                  kspec(LANES),
                  kspec(IDX_DIM)],
        out_specs=pl.BlockSpec((Q_BLOCK, A_OUT), lambda b, j: (b * nq + j, 0)),
        out_shape=jax.ShapeDtypeStruct((bsz * seq, A_OUT), BF16),
        scratch_shapes=[pltpu.VMEM((n_kc, KEY_CHUNK, Q_BLOCK), jnp.int32),
                        pltpu.VMEM((n_kc, KEY_CHUNK, Q_BLOCK), F32)],
        compiler_params=_cparams(("parallel", "parallel")),
        name="dsa",
    )(q, iq, wi, k, v, ki)


GLA_QK = B_HEADS * B_KEY_DIM
GLA_REP = N_SUB * GLA_QK
GLA_COLS = 2 * GLA_QK + 2 * B_OUT + LANES
GLA_V0 = 2 * GLA_QK


def _gla_kernel(h_ref, w_ref, wa_ref, ba_ref, ng_ref, o_ref, p_ref, g_ref, acc_ref, st_ref, *, ts):
    @pl.when(pl.program_id(1) == 0)
    def _():
        st_ref[...] = jnp.zeros_like(st_ref)

    p_ref[...] = _dot(h_ref[...].astype(BF16), w_ref[...])
    x = _dot(p_ref[:, GLA_V0 + 2 * B_OUT:].astype(BF16), wa_ref[...]) + ba_ref[...]
    g_ref[...] = (jnp.minimum(x, 0.0) - jnp.log1p(jnp.exp(-jnp.abs(x)))) * (1.0 / B_GATE_TAU)

    t_i = lax.broadcasted_iota(jnp.int32, (B_CHUNK, B_CHUNK), 0)
    s_i = lax.broadcasted_iota(jnp.int32, (B_CHUNK, B_CHUNK), 1)
    tri = t_i >= s_i
    tri_b = jnp.where(tri, 1.0, 0.0).astype(BF16)
    lane_sub = (lax.broadcasted_iota(jnp.int32, (1, GLA_REP), 1) // B_KEY_DIM) % N_SUB
    row_sub = lax.broadcasted_iota(jnp.int32, (B_CHUNK, 1), 0) // B_SUB
    q_mask = lane_sub <= row_sub
    k_mask = lane_sub == row_sub
    low_half = lax.broadcasted_iota(jnp.int32, (B_CHUNK, LANES), 1) < B_KEY_DIM

    def replicate(x):
        outs = []
        for pair in range(GLA_QK // LANES):
            v = x[:, pair * LANES:(pair + 1) * LANES]
            r = pltpu.roll(v, B_KEY_DIM, axis=1)
            outs += [jnp.where(low_half, v, r)] * (N_SUB // 2) + [jnp.where(low_half, r, v)] * (N_SUB // 2)
        return jnp.concatenate(outs, axis=1)

    def chunk(c, carry):
        rows = pl.ds(pl.multiple_of(c * B_CHUNK, B_CHUNK), B_CHUNK)
        g = g_ref[rows, :]
        g_hi = g.astype(BF16)
        g_mid = (g - g_hi.astype(F32)).astype(BF16)
        g_lo = (g - g_hi.astype(F32) - g_mid.astype(F32)).astype(BF16)
        b = replicate(_dot(tri_b, g_hi) + _dot(tri_b, g_mid) + _dot(tri_b, g_lo))
        ref = jnp.zeros((1, GLA_REP), F32)
        for jj in range(1, N_SUB):
            ref = ref + jnp.where(lane_sub == jj, b[jj * B_SUB - 1:jj * B_SUB, :], 0.0)
        d = b - ref
        e_q = jnp.where(q_mask, jnp.exp(jnp.minimum(d, 0.0)), 0.0)
        e_k = jnp.where(k_mask, jnp.exp(jnp.where(k_mask, -d, 0.0)), 0.0)
        b_last = b[B_CHUNK - 1:B_CHUNK, :]
        q_c = (replicate(p_ref[rows, 0:GLA_QK]) * e_q).astype(BF16)
        k_raw = replicate(p_ref[rows, GLA_QK:2 * GLA_QK])
        k_c = (k_raw * e_k).astype(BF16)
        k_st = (k_raw * jnp.exp(b_last - b)).astype(BF16)
        e_last = jnp.exp(b_last)
        for hh in range(B_HEADS):
            base = hh * N_SUB * B_KEY_DIM
            q_h = q_c[:, base:base + N_SUB * B_KEY_DIM]
            k_h = k_c[:, base:base + N_SUB * B_KEY_DIM]
            att = jnp.where(tri, _dot_nt(q_h, k_h), 0.0)
            v_h = p_ref[rows, GLA_V0 + hh * B_VAL_DIM:GLA_V0 + (hh + 1) * B_VAL_DIM].astype(BF16)
            st = st_ref[hh]
            o = _dot(att.astype(BF16), v_h) + _dot_nt(q_h[:, :B_KEY_DIM], st.astype(BF16))
            acc_ref[rows, hh * B_VAL_DIM:(hh + 1) * B_VAL_DIM] = o
            upd = _dot(v_h.T, k_st[:, base:base + B_KEY_DIM])
            st_ref[hh] = st * e_last[:, base:base + B_KEY_DIM] + upd
        return carry

    lax.fori_loop(0, ts // B_CHUNK, chunk, 0, unroll=4)

    r = p_ref[:, GLA_V0 + B_OUT:GLA_V0 + 2 * B_OUT]
    gate = r * (1.0 / (1.0 + jnp.exp(-r)))
    ng = ng_ref[...]
    for hh in range(B_HEADS):
        sl = slice(hh * B_VAL_DIM, (hh + 1) * B_VAL_DIM)
        o = acc_ref[:, sl]
        y = o * lax.rsqrt(jnp.mean(o * o, axis=-1, keepdims=True) + RMS_EPS) * ng
        o_ref[:, sl] = (gate[:, sl] * y).astype(BF16)


def _gla(h, w_b, wa, ba, norm_g, bsz, seq, ts=512):
    ns = seq // ts
    const = lambda shape: pl.BlockSpec(shape, lambda b, i: (0, 0))
    return pl.pallas_call(
        functools.partial(_gla_kernel, ts=ts),
        grid=(bsz, ns),
        in_specs=[pl.BlockSpec((ts, D_MODEL), lambda b, i: (b * ns + i, 0)),
                  const((D_MODEL, GLA_COLS)), const((LANES, GLA_QK)),
                  const((1, GLA_QK)), const((1, B_VAL_DIM))],
        out_specs=pl.BlockSpec((ts, B_OUT), lambda b, i: (b * ns + i, 0)),
        out_shape=jax.ShapeDtypeStruct((bsz * seq, B_OUT), BF16),
        scratch_shapes=[pltpu.VMEM((ts, GLA_COLS), F32),
                        pltpu.VMEM((ts, GLA_QK), F32),
                        pltpu.VMEM((ts, B_OUT), F32),
                        pltpu.VMEM((B_HEADS, B_VAL_DIM, B_KEY_DIM), F32)],
        compiler_params=_cparams(("parallel", "arbitrary")),
        name="gla",
    )(h, w_b, wa, ba, norm_g)


def _gmlp_kernel(h_ref, w_ref, lg_ref, lb_ref, ws_ref, bs_ref, o_ref, *, tm):
    z = _dot(h_ref[...].astype(BF16), w_ref[...])
    z = z * (0.5 * (1.0 + jnp.tanh(np.sqrt(2.0 / np.pi) * (z + 0.044715 * (z * z * z)))))
    u = z[:, :C_OUT]
    v = _layer_norm(z[:, C_OUT:], lg_ref[...], lb_ref[...]).astype(BF16)
    t_i = lax.broadcasted_iota(jnp.int32, (C_CHUNK, C_CHUNK), 0)
    s_i = lax.broadcasted_iota(jnp.int32, (C_CHUNK, C_CHUNK), 1)
    bs = bs_ref[...]
    for g in range(C_GROUPS):
        w = jnp.where(t_i >= s_i, ws_ref[g], 0.0).astype(BF16)
        sl = slice(g * C_GROUP_DIM, (g + 1) * C_GROUP_DIM)
        for c in range(tm // C_CHUNK):
            rows = slice(c * C_CHUNK, (c + 1) * C_CHUNK)
            mixed = _dot(w, v[rows, sl]) + bs[:, g:g + 1]
            o_ref[rows, sl] = (u[rows, sl] * mixed).astype(BF16)


def _gmlp(h, w_c, ln_g, ln_b, ws, bs_t, tm=512):
    n = h.shape[0]
    return pl.pallas_call(
        functools.partial(_gmlp_kernel, tm=tm),
        grid=(n // tm,),
        in_specs=[pl.BlockSpec((tm, D_MODEL), lambda i: (i, 0)),
                  pl.BlockSpec((D_MODEL, 2 * C_OUT), lambda i: (0, 0)),
                  pl.BlockSpec((1, C_OUT), lambda i: (0, 0)),
                  pl.BlockSpec((1, C_OUT), lambda i: (0, 0)),
                  pl.BlockSpec((C_GROUPS, C_CHUNK, C_CHUNK), lambda i: (0, 0, 0)),
                  pl.BlockSpec((C_CHUNK, LANES), lambda i: (0, 0))],
        out_specs=pl.BlockSpec((tm, C_OUT), lambda i: (i, 0)),
        out_shape=jax.ShapeDtypeStruct((n, C_OUT), BF16),
        compiler_params=_cparams(("parallel",)),
        name="gmlp",
    )(h, w_c, ln_g, ln_b, ws, bs_t)


def _sigmoid(x):
    return 1.0 / (1.0 + jnp.exp(-x))


def _merge_kernel(h_ref, ya_ref, yb_ref, yc_ref, wg_ref, wa_ref, wb_ref, wc_ref, wo_ref,
                  lg_ref, lb_ref, wr_ref, wrl_ref, br_ref, x_ref, oh_ref, route_ref):
    h = h_ref[...]
    hb = h.astype(BF16)
    acc = None
    for i, (y_ref, wbr_ref) in enumerate(((ya_ref, wa_ref), (yb_ref, wb_ref), (yc_ref, wc_ref))):
        gate = _sigmoid(_dot(hb, wg_ref[:, i * D_MODEL:(i + 1) * D_MODEL]))
        term = gate * _dot(y_ref[...], wbr_ref[...])
        acc = term if acc is None else acc + term
    mix = _dot(acc.astype(BF16), wo_ref[...])
    x1 = _layer_norm(DN_ALPHA * h + mix, lg_ref[...], lb_ref[...])
    x_ref[...] = x1

    x_hi = x1.astype(BF16)
    x_lo = (x1 - x_hi.astype(F32)).astype(BF16)
    logit = (_dot(x_hi, wr_ref[...]) + _dot(x_hi, wrl_ref[...]) + _dot(x_lo, wr_ref[...])) + br_ref[...]
    lane = lax.broadcasted_iota(jnp.int32, logit.shape, 1)
    big = jnp.int32(LANES)
    is_g = (lane >= N_EXPERTS) & (lane < N_EXPERTS + N_GROUPS)
    gl = jnp.where(is_g, logit, -jnp.inf)
    g_max = jnp.max(gl, axis=-1, keepdims=True)
    g_idx = jnp.min(jnp.where(is_g & (gl == g_max), lane, big), axis=-1, keepdims=True) - N_EXPERTS
    g_p = 1.0 / jnp.sum(jnp.where(is_g, jnp.exp(gl - g_max), 0.0), axis=-1, keepdims=True)
    lo = g_idx * EXPERTS_PER_GROUP
    in_g = (lane >= lo) & (lane < lo + EXPERTS_PER_GROUP)
    el = jnp.where(in_g, logit, -jnp.inf)
    e1 = jnp.max(el, axis=-1, keepdims=True)
    i1 = jnp.min(jnp.where(in_g & (el == e1), lane, big), axis=-1, keepdims=True)
    el2 = jnp.where(lane == i1, -jnp.inf, el)
    e2 = jnp.max(el2, axis=-1, keepdims=True)
    i2 = jnp.min(jnp.where(in_g & (lane != i1) & (el2 == e2), lane, big), axis=-1, keepdims=True)
    z2 = jnp.exp(e2 - e1)
    den = 1.0 + z2
    oh_ref[...] = jnp.where((lane == i1) | (lane == i2), 1.0, 0.0).astype(BF16)
    route_ref[...] = jnp.where(lane == 0, i1.astype(F32),
                               jnp.where(lane == 1, i2.astype(F32),
                                         jnp.where(lane == 2, (1.0 / den) * g_p,
                                                   jnp.where(lane == 3, (z2 / den) * g_p, 0.0))))


def _merge(h, ya, yb, yc, wg, wa, wb, wc, wo, ln_g, ln_b, wr, wr_lo, br, tm=1024):
    n = h.shape[0]
    row = lambda w: pl.BlockSpec((tm, w), lambda i: (i, 0))
    const = lambda r, c: pl.BlockSpec((r, c), lambda i: (0, 0), pipeline_mode=pl.Buffered(1))
    return pl.pallas_call(
        _merge_kernel,
        grid=(n // tm,),
        in_specs=[row(D_MODEL), row(A_OUT), row(B_OUT), row(C_OUT),
                  const(D_MODEL, 3 * D_MODEL), const(A_OUT, D_MODEL), const(B_OUT, D_MODEL),
                  const(C_OUT, D_MODEL), const(D_MODEL, D_MODEL),
                  const(1, D_MODEL), const(1, D_MODEL), const(D_MODEL, LANES), const(D_MODEL, LANES),
                  const(1, LANES)],
        out_specs=[row(D_MODEL), row(LANES), row(LANES)],
        out_shape=[jax.ShapeDtypeStruct((n, D_MODEL), F32),
                   jax.ShapeDtypeStruct((n, LANES), BF16),
                   jax.ShapeDtypeStruct((n, LANES), F32)],
        compiler_params=_cparams(("parallel",)),
        name="merge",
    )(h, ya, yb, yc, wg, wa, wb, wc, wo, ln_g, ln_b, wr, wr_lo, br)


FFN_TILE = 512
MOE_TOK = 512


def _rank_kernel(oh_ref, route_ref, r_ref, cnt_ref, carry_ref):
    @pl.when(pl.program_id(0) == 0)
    def _():
        carry_ref[...] = jnp.zeros_like(carry_ref)

    oh = oh_ref[...]
    tt = oh.shape[0]
    r_i = lax.broadcasted_iota(jnp.int32, (tt, tt), 0)
    c_i = lax.broadcasted_iota(jnp.int32, (tt, tt), 1)
    lower = jnp.where(c_i < r_i, 1.0, 0.0).astype(BF16)
    rank = _dot(lower, oh) + carry_ref[...]
    route = route_ref[...]
    lane = lax.broadcasted_iota(jnp.int32, rank.shape, 1)
    i1 = route[:, 0:1].astype(jnp.int32)
    i2 = route[:, 1:2].astype(jnp.int32)
    r1 = jnp.sum(jnp.where(lane == i1, rank, 0.0), axis=-1, keepdims=True)
    r2 = jnp.sum(jnp.where(lane == i2, rank, 0.0), axis=-1, keepdims=True)
    r_ref[...] = jnp.where(lane == 0, r1, jnp.where(lane == 1, r2, 0.0))
    carry_ref[...] += jnp.sum(oh.astype(F32), axis=0, keepdims=True)
    cnt_ref[...] = carry_ref[...]


def _rank(oh, route, tt=512):
    n = oh.shape[0]
    return pl.pallas_call(
        _rank_kernel,
        grid=(n // tt,),
        in_specs=[pl.BlockSpec((tt, LANES), lambda i: (i, 0)),
                  pl.BlockSpec((tt, LANES), lambda i: (i, 0))],
        out_specs=[pl.BlockSpec((tt, LANES), lambda i: (i, 0)),
                   pl.BlockSpec((1, LANES), lambda i: (0, 0))],
        out_shape=[jax.ShapeDtypeStruct((n, LANES), F32),
                   jax.ShapeDtypeStruct((1, LANES), F32)],
        scratch_shapes=[pltpu.VMEM((1, LANES), F32)],
        compiler_params=_cparams(("arbitrary",)),
        name="moe_rank",
    )(oh, route)


def _row_copy(src_ref, src_row, dst_ref, dst_row, sem):
    return pltpu.make_async_copy(src_ref.at[pl.ds(src_row, 1), :], dst_ref.at[pl.ds(dst_row, 1), :], sem)


def _dispatch_kernel(pos1_ref, pos2_ref, x_ref, init_ref, xs_ref, sem):
    del init_ref
    base = pl.program_id(0) * MOE_TOK

    def issue(t, carry):
        _row_copy(x_ref, t, xs_ref, pos1_ref[base + t], sem).start()
        _row_copy(x_ref, t, xs_ref, pos2_ref[base + t], sem).start()
        return carry

    lax.fori_loop(0, MOE_TOK, issue, 0, unroll=8)
    for _ in range(2):
        pltpu.make_async_copy(x_ref, xs_ref.at[pl.ds(0, MOE_TOK), :], sem).wait()


def _dispatch(pos1, pos2, x1, n_slots):
    n = x1.shape[0]
    init = jnp.zeros((n_slots, D_MODEL), F32)
    return pl.pallas_call(
        _dispatch_kernel,
        grid_spec=pltpu.PrefetchScalarGridSpec(
            num_scalar_prefetch=2,
            grid=(n // MOE_TOK,),
            in_specs=[pl.BlockSpec((MOE_TOK, D_MODEL), lambda i, p1, p2: (i, 0)),
                      pl.BlockSpec(memory_space=pl.ANY)],
            out_specs=pl.BlockSpec(memory_space=pl.ANY),
            scratch_shapes=[pltpu.SemaphoreType.DMA(())]),
        out_shape=jax.ShapeDtypeStruct((n_slots, D_MODEL), F32),
        input_output_aliases={3: 0},
        compiler_params=_cparams(("arbitrary",)),
        name="moe_dispatch",
    )(pos1, pos2, x1, init)


def _ffn_kernel(te_ref, nu_ref, xs_ref, wg_ref, wu_ref, wd_ref, ys_ref):
    del te_ref
    i = pl.program_id(0)

    @pl.when(i < nu_ref[0])
    def _():
        xb = xs_ref[...].astype(BF16)
        gte = _dot(xb, wg_ref[...])
        hid = gte * _sigmoid(gte) * _dot(xb, wu_ref[...])
        ys_ref[...] = _dot(hid.astype(BF16), wd_ref[...])

    @pl.when(i >= nu_ref[0])
    def _():
        ys_ref[...] = jnp.zeros_like(ys_ref)


def _ffn(tile_expert, n_used, xs, w_gate, w_up, w_down):
    n_slots = xs.shape[0]
    wspec = lambda r, c: pl.BlockSpec((None, r, c), lambda i, te, nu: (te[i], 0, 0))
    return pl.pallas_call(
        _ffn_kernel,
        grid_spec=pltpu.PrefetchScalarGridSpec(
            num_scalar_prefetch=2,
            grid=(n_slots // FFN_TILE,),
            in_specs=[pl.BlockSpec((FFN_TILE, D_MODEL), lambda i, te, nu: (i, 0)),
                      wspec(D_MODEL, D_EXPERT), wspec(D_MODEL, D_EXPERT), wspec(D_EXPERT, D_MODEL)],
            out_specs=pl.BlockSpec((FFN_TILE, D_MODEL), lambda i, te, nu: (i, 0))),
        out_shape=jax.ShapeDtypeStruct((n_slots, D_MODEL), F32),
        compiler_params=_cparams(("arbitrary",)),
        name="moe_ffn",
    )(tile_expert, n_used, xs, w_gate, w_up, w_down)


def _combine_kernel(pos1_ref, pos2_ref, x_ref, route_ref, lg_ref, lb_ref, ys_ref, o_ref, buf_ref, sem):
    i = pl.program_id(0)

    def gather(step, slot):
        base = step * MOE_TOK

        def issue(t, carry):
            _row_copy(ys_ref, pos1_ref[base + t], buf_ref.at[slot, 0], t, sem.at[slot]).start()
            _row_copy(ys_ref, pos2_ref[base + t], buf_ref.at[slot, 1], t, sem.at[slot]).start()
            return carry

        lax.fori_loop(0, MOE_TOK, issue, 0, unroll=8)

    @pl.when(i == 0)
    def _():
        gather(0, 0)

    @pl.when(i + 1 < pl.num_programs(0))
    def _():
        gather(i + 1, (i + 1) % 2)

    slot = i % 2
    for s in range(2):
        pltpu.make_async_copy(ys_ref.at[pl.ds(0, MOE_TOK), :], buf_ref.at[slot, s], sem.at[slot]).wait()
    route = route_ref[...]
    ffn = route[:, 2:3] * buf_ref[slot, 0] + route[:, 3:4] * buf_ref[slot, 1]
    o_ref[...] = _layer_norm(DN_ALPHA * x_ref[...] + ffn, lg_ref[...], lb_ref[...])


def _combine(pos1, pos2, x1, route, ln_g, ln_b, ys):
    n = x1.shape[0]
    row = lambda w: pl.BlockSpec((MOE_TOK, w), lambda i, p1, p2: (i, 0))
    const = lambda w: pl.BlockSpec((1, w), lambda i, p1, p2: (0, 0))
    return pl.pallas_call(
        _combine_kernel,
        grid_spec=pltpu.PrefetchScalarGridSpec(
            num_scalar_prefetch=2,
            grid=(n // MOE_TOK,),
            in_specs=[row(D_MODEL), row(LANES), const(D_MODEL), const(D_MODEL),
                      pl.BlockSpec(memory_space=pl.ANY)],
            out_specs=row(D_MODEL),
            scratch_shapes=[pltpu.VMEM((2, 2, MOE_TOK, D_MODEL), F32),
                            pltpu.SemaphoreType.DMA((2,))]),
        out_shape=jax.ShapeDtypeStruct((n, D_MODEL), F32),
        compiler_params=_cparams(("arbitrary",)),
        name="moe_combine",
    )(pos1, pos2, x1, route, ln_g, ln_b, ys)


def _moe(x1, oh, route, w_gate, w_up, w_down, ln_g, ln_b):
    n = x1.shape[0]
    n_slots = TOP_K_INNER * n + N_EXPERTS * FFN_TILE
    rank, cnt = _rank(oh, route)
    cnt = cnt[0, :N_EXPERTS].astype(jnp.int32)
    padded = (cnt + FFN_TILE - 1) // FFN_TILE * FFN_TILE
    ends = jnp.cumsum(padded)
    offs = ends - padded
    experts = jnp.arange(N_EXPERTS, dtype=jnp.int32)
    i1 = route[:, 0].astype(jnp.int32)
    i2 = route[:, 1].astype(jnp.int32)
    off_of = lambda idx: jnp.sum(jnp.where(idx[:, None] == experts[None, :], offs[None, :], 0), axis=1)
    pos1 = off_of(i1) + rank[:, 0].astype(jnp.int32)
    pos2 = off_of(i2) + rank[:, 1].astype(jnp.int32)
    tile_start = jnp.arange(n_slots // FFN_TILE, dtype=jnp.int32) * FFN_TILE
    tile_expert = jnp.minimum(jnp.sum(tile_start[:, None] >= ends[None, :], axis=1), N_EXPERTS - 1)
    n_used = (ends[-1] // FFN_TILE).reshape(1)

    xs = _dispatch(pos1, pos2, x1, n_slots)
    ys = _ffn(tile_expert.astype(jnp.int32), n_used.astype(jnp.int32), xs, w_gate, w_up, w_down)
    return _combine(pos1, pos2, x1, route, ln_g, ln_b, ys)


def _rope_tables(positions):
    inv_freq = ROPE_THETA ** (-jnp.arange(0, ROT_DIM, 2, dtype=F32) / ROT_DIM)
    ang = positions.astype(F32).reshape(-1, 1) * inv_freq
    cos, sin = jnp.cos(ang), jnp.sin(ang)
    n = ang.shape[0]
    half = ROT_DIM // 2
    c64 = jnp.concatenate([cos, cos, jnp.ones((n, A_HEAD_DIM - ROT_DIM), F32)], axis=1)
    s64 = jnp.concatenate([-sin, sin, jnp.zeros((n, A_HEAD_DIM - ROT_DIM), F32)], axis=1)
    del half
    return jnp.tile(c64, (1, 2)), jnp.tile(s64, (1, 2))


def _split_w_in(w):
    widths = (A_OUT, A_HEAD_DIM, A_HEAD_DIM, IDX_HEADS * IDX_DIM, IDX_DIM, IDX_HEADS,
              B_HEADS * B_KEY_DIM, B_HEADS * B_KEY_DIM, B_OUT, B_GATE_RANK, B_OUT,
              2 * C_OUT, D_MODEL, D_MODEL, D_MODEL)
    pts = np.cumsum((0,) + widths)
    return [w[:, int(pts[i]):int(pts[i + 1])] for i in range(len(widths))]


def kernel(x, positions, ln_in_g, ln_in_b, w_in, idx_k_g, gla_wa2, gla_ba, gla_norm_g, gm_ln_g, gm_ln_b, gm_ws, gm_bs, w_branch_a, w_branch_b, w_branch_c, w_out, ln1_g, ln1_b, w_rg, b_rg, w_re, b_re, w_gate, w_up, w_down, ln2_g, ln2_b):
    bsz, seq, d = x.shape
    n = bsz * seq
    cos_t, sin_t = _rope_tables(positions)
    h = _entry_ln(x.reshape(n, d), ln_in_g, ln_in_b)
    for l in range(DEPTH):
        (a_q, a_k, a_v, i_q, i_k, i_w, b_q, b_k, b_v, b_glr, b_r, c_uv,
         g_a, g_b, g_c) = _split_w_in(w_in[l])
        zpad = lambda c: jnp.zeros((d, c), F32)
        w_a = jnp.concatenate([a_q, a_k, a_v, i_q, i_k, i_w, zpad(LANES - IDX_DIM - IDX_HEADS)],
                              axis=1).astype(BF16)
        w_b = jnp.concatenate([b_q * (B_KEY_DIM ** -0.5), b_k, b_v, b_r,
                               b_glr, zpad(LANES - B_GATE_RANK)], axis=1).astype(BF16)
        wa_pad = jnp.concatenate([gla_wa2[l],
                                  jnp.zeros((LANES - B_GATE_RANK, GLA_QK), F32)], axis=0).astype(BF16)
        w_g = jnp.concatenate([g_a, g_b, g_c], axis=1).astype(BF16)
        ikg = jnp.concatenate([idx_k_g[l], jnp.zeros((LANES - IDX_DIM,), F32)]).reshape(1, LANES)
        bs_t = jnp.concatenate([gm_bs[l].T, jnp.zeros((C_CHUNK, LANES - C_GROUPS), F32)], axis=1)
        w_r = jnp.concatenate([w_re[l], w_rg[l], zpad(LANES - N_GROUPS - N_EXPERTS)], axis=1)
        w_r_hi = w_r.astype(BF16)
        w_r_lo = (w_r - w_r_hi.astype(F32)).astype(BF16)
        b_r_all = jnp.concatenate([b_re[l], b_rg[l],
                                   jnp.zeros((LANES - N_GROUPS - N_EXPERTS,), F32)]).reshape(1, LANES)

        q, k, v, iq, ki, wi = _proj_a(h, w_a, cos_t, sin_t, ikg)
        y_a = _dsa(q, iq, wi, k, v, ki, bsz, seq)
        y_b = _gla(h, w_b, wa_pad, gla_ba[l].reshape(1, -1), gla_norm_g[l].reshape(1, -1), bsz, seq)
        y_c = _gmlp(h, c_uv.astype(BF16), gm_ln_g[l].reshape(1, -1), gm_ln_b[l].reshape(1, -1),
                    gm_ws[l], bs_t)
        x1, oh, route = _merge(h, y_a, y_b, y_c, w_g,
                               w_branch_a[l].astype(BF16), w_branch_b[l].astype(BF16),
                               w_branch_c[l].astype(BF16), w_out[l].astype(BF16),
                               ln1_g[l].reshape(1, -1), ln1_b[l].reshape(1, -1), w_r_hi, w_r_lo, b_r_all)
        h = _moe(x1, oh, route, w_gate[l].astype(BF16), w_up[l].astype(BF16),
                 w_down[l].astype(BF16), ln2_g[l].reshape(1, -1), ln2_b[l].reshape(1, -1))
    return h.reshape(bsz, seq, d)
```

```python
import functools

import numpy as np
import jax
import jax.numpy as jnp
from jax import lax
from jax.experimental import pallas as pl
from jax.experimental.pallas import tpu as pltpu

F32 = jnp.float32
BF16 = jnp.bfloat16

D_MODEL = 1024
DEPTH = 2
A_HEADS = 8
A_HEAD_DIM = 64
A_OUT = 512
IDX_HEADS = 4
IDX_DIM = 64
TOPK_MAX = 256
Q_BLOCK = 128
ROPE_THETA = 500000.0
ROT_DIM = 16
B_HEADS = 4
B_KEY_DIM = 64
B_VAL_DIM = 128
B_OUT = 512
B_GATE_RANK = 16
B_GATE_TAU = 16.0
B_CHUNK = 64
B_SUB = 16
N_SUB = B_CHUNK // B_SUB
C_GROUPS = 4
C_GROUP_DIM = 128
C_OUT = 512
C_CHUNK = 128
N_GROUPS = 4
EXPERTS_PER_GROUP = 8
N_EXPERTS = 32
D_EXPERT = 256
TOP_K_INNER = 2
DN_ALPHA = (2 * DEPTH) ** 0.25
LN_EPS = 1e-5
RMS_EPS = 1e-6

LANES = 128
NEG_BIG = -1e30
LOG2_E = 1.4426950408889634
INT_MIN = -2 ** 31
VMEM_LIMIT = 56 * 1024 * 1024


def _cparams(sem):
    return pltpu.CompilerParams(dimension_semantics=sem, vmem_limit_bytes=VMEM_LIMIT)


def _layer_norm(x, g, b):
    mu = jnp.mean(x, axis=-1, keepdims=True)
    xc = x - mu
    var = jnp.mean(xc * xc, axis=-1, keepdims=True)
    return xc * lax.rsqrt(var + LN_EPS) * g + b


def _dot(a, b):
    return jnp.dot(a, b, preferred_element_type=F32)


def _dot_nt(a, b):
    return lax.dot_general(a, b, (((1,), (1,)), ((), ())), preferred_element_type=F32)


def _ln_kernel(x_ref, g_ref, b_ref, o_ref):
    o_ref[...] = _layer_norm(x_ref[...], g_ref[...], b_ref[...])


def _entry_ln(x, g, b, tm=1024):
    n = x.shape[0]
    return pl.pallas_call(
        _ln_kernel,
        grid=(n // tm,),
        in_specs=[pl.BlockSpec((tm, D_MODEL), lambda i: (i, 0)),
                  pl.BlockSpec((1, D_MODEL), lambda i: (0, 0)),
                  pl.BlockSpec((1, D_MODEL), lambda i: (0, 0))],
        out_specs=pl.BlockSpec((tm, D_MODEL), lambda i: (i, 0)),
        out_shape=jax.ShapeDtypeStruct((n, D_MODEL), F32),
        compiler_params=_cparams(("parallel",)),
        name="entry_ln",
    )(x, g.reshape(1, -1), b.reshape(1, -1))


def _rope_slab(xs, c, s):
    lane = lax.broadcasted_iota(jnp.int32, xs.shape, 1) % A_HEAD_DIM
    fwd = pltpu.roll(xs, LANES - ROT_DIM // 2, axis=1)
    bwd = pltpu.roll(xs, ROT_DIM // 2, axis=1)
    partner = jnp.where(lane < ROT_DIM // 2, fwd, bwd)
    return xs * c + partner * s


def _proj_a_kernel(h_ref, w_ref, c_ref, s_ref, g_ref,
                   q_ref, k_ref, v_ref, iq_ref, ki_ref, wi_ref):
    p = _dot(h_ref[...].astype(BF16), w_ref[...])
    c = c_ref[...]
    s = s_ref[...]
    lane = lax.broadcasted_iota(jnp.int32, c.shape, 1)
    lo = lane < A_HEAD_DIM
    n_blk = p.shape[0] // Q_BLOCK

    def put_heads(dst_ref, slab, first_head):
        for half in range(2):
            hd = first_head + half
            cols = slab[:, half * A_HEAD_DIM:(half + 1) * A_HEAD_DIM].astype(BF16)
            for qb in range(n_blk):
                dst_ref[qb, hd * Q_BLOCK:(hd + 1) * Q_BLOCK, :] = cols[qb * Q_BLOCK:(qb + 1) * Q_BLOCK, :]

    for i in range(4):
        sl = slice(i * LANES, (i + 1) * LANES)
        put_heads(q_ref, _rope_slab(p[:, sl], c, s) * (A_HEAD_DIM ** -0.5 * LOG2_E), 2 * i)
    kv = p[:, 512:640]
    kv = _rope_slab(kv, jnp.where(lo, c, 1.0), jnp.where(lo, s, 0.0))
    k_ref[...] = kv[:, :A_HEAD_DIM].astype(BF16)
    v_ones = jnp.where(lo, pltpu.roll(kv, A_HEAD_DIM, axis=1), jnp.where(lane == A_HEAD_DIM, 1.0, 0.0))
    v_ref[...] = v_ones.astype(BF16)
    for i in range(2):
        sl = slice(640 + i * LANES, 640 + (i + 1) * LANES)
        put_heads(iq_ref, _rope_slab(p[:, sl], c, s), 2 * i)
    last = p[:, 896:1024]
    mu = jnp.sum(jnp.where(lo, last, 0.0), axis=-1, keepdims=True) * (1.0 / IDX_DIM)
    xc = jnp.where(lo, last - mu, 0.0)
    var = jnp.sum(xc * xc, axis=-1, keepdims=True) * (1.0 / IDX_DIM)
    kin = xc * lax.rsqrt(var + LN_EPS) * g_ref[...]
    kin = _rope_slab(kin, jnp.where(lo, c, 1.0), jnp.where(lo, s, 0.0))
    ki_ref[...] = kin[:, :IDX_DIM].astype(BF16)
    wi = pltpu.roll(last, LANES - IDX_DIM, axis=1) * (IDX_HEADS ** -0.5 * IDX_DIM ** -0.5)
    for qb in range(n_blk):
        wi_ref[qb] = wi[qb * Q_BLOCK:(qb + 1) * Q_BLOCK, :].T[0:8, :]


def _proj_a(h, w_a, cos_t, sin_t, ikg, tm=512):
    n = h.shape[0]
    row = lambda w: pl.BlockSpec((tm, w), lambda i: (i, 0))
    stacked = lambda heads: pl.BlockSpec((tm // Q_BLOCK, heads * Q_BLOCK, A_HEAD_DIM), lambda i: (i, 0, 0))
    return pl.pallas_call(
        _proj_a_kernel,
        grid=(n // tm,),
        in_specs=[row(D_MODEL),
                  pl.BlockSpec((D_MODEL, 1024), lambda i: (0, 0)),
                  row(LANES), row(LANES),
                  pl.BlockSpec((1, LANES), lambda i: (0, 0))],
        out_specs=[stacked(A_HEADS), row(A_HEAD_DIM),
                   row(LANES),
                   stacked(IDX_HEADS), row(IDX_DIM),
                   pl.BlockSpec((tm // Q_BLOCK, 8, Q_BLOCK), lambda i: (i, 0, 0))],
        out_shape=[jax.ShapeDtypeStruct((n // Q_BLOCK, A_HEADS * Q_BLOCK, A_HEAD_DIM), BF16),
                   jax.ShapeDtypeStruct((n, A_HEAD_DIM), BF16),
                   jax.ShapeDtypeStruct((n, LANES), BF16),
                   jax.ShapeDtypeStruct((n // Q_BLOCK, IDX_HEADS * Q_BLOCK, IDX_DIM), BF16),
                   jax.ShapeDtypeStruct((n, IDX_DIM), BF16),
                   jax.ShapeDtypeStruct((n // Q_BLOCK, 8, Q_BLOCK), F32)],
        compiler_params=_cparams(("parallel",)),
        name="proj_a",
    )(h, w_a, cos_t, sin_t, ikg)


KEY_CHUNK = 512


def _dsa_kernel(q_ref, iq_ref, wi_ref, k_ref, v_ref, ki_ref, o_ref,
                key_ref, bias_ref, *, seq):
    j = pl.program_id(1)
    n_sel = min(TOPK_MAX, seq // 4)
    n_kc = (j * Q_BLOCK + Q_BLOCK + KEY_CHUNK - 1) // KEY_CHUNK
    qpos = j * Q_BLOCK + lax.broadcasted_iota(jnp.int32, (1, Q_BLOCK), 1)
    kk = jnp.minimum(n_sel, qpos + 1).astype(F32)
    row_pos = lax.broadcasted_iota(jnp.int32, (KEY_CHUNK, 1), 0)

    def key_rows(c):
        return pl.ds(pl.multiple_of(c * KEY_CHUNK, KEY_CHUNK), KEY_CHUNK)

    iq = iq_ref[...]
    wi = wi_ref[...]

    def score_chunk(c, carry):
        d = jnp.maximum(_dot_nt(ki_ref[key_rows(c), :], iq), 0.0)
        score = wi[0:1, :] * d[:, 0:Q_BLOCK]
        for h in range(1, IDX_HEADS):
            score = score + wi[h:h + 1, :] * d[:, h * Q_BLOCK:(h + 1) * Q_BLOCK]
        bits = pltpu.bitcast(score + 0.0, jnp.int32)
        key = bits ^ ((bits >> 31) & 0x7FFFFFFF)
        key_ref[c] = jnp.where(c * KEY_CHUNK + row_pos <= qpos, key, INT_MIN)
        return carry

    lax.fori_loop(0, n_kc, score_chunk, 0)

    def count(pred):
        def body(c, acc):
            m = jnp.where(pred(key_ref[c]), 1.0, 0.0)
            return acc + jnp.sum(m.reshape(KEY_CHUNK // 64, 64, Q_BLOCK), axis=0)
        acc = lax.fori_loop(0, n_kc, body, jnp.zeros((64, Q_BLOCK), F32))
        return jnp.sum(acc, axis=0, keepdims=True)

    def search(i, t_u):
        cand_u = t_u | jnp.left_shift(jnp.int32(1), 31 - i)
        cand_s = cand_u ^ INT_MIN
        return jnp.where(count(lambda kc: kc >= cand_s) >= kk, cand_u, t_u)

    thr = lax.fori_loop(0, 32, search, jnp.zeros((1, Q_BLOCK), jnp.int32)) ^ INT_MIN
    need = kk - count(lambda kc: kc > thr)
    r_i = lax.broadcasted_iota(jnp.int32, (LANES, LANES), 0)
    c_i = lax.broadcasted_iota(jnp.int32, (LANES, LANES), 1)
    lower = jnp.where(c_i < r_i, 1.0, 0.0).astype(BF16)

    def fill(c, carry):
        for t in range(KEY_CHUNK // LANES):
            rows = slice(t * LANES, (t + 1) * LANES)
            kt = key_ref[c, rows, :]
            eq = kt == thr
            eq_f = jnp.where(eq, 1.0, 0.0)
            pre = _dot(lower, eq_f.astype(BF16)) + carry
            sel = (kt > thr) | (eq & (pre < need))
            bias_ref[c, rows, :] = jnp.where(sel, 0.0, NEG_BIG)
            carry = carry + jnp.sum(eq_f, axis=0, keepdims=True)
        return carry

    lax.fori_loop(0, n_kc, fill, jnp.zeros((1, Q_BLOCK), F32))

    q = q_ref[...]
    cols = A_HEADS * Q_BLOCK

    def attn_chunk(c, carry):
        m, acc = carry
        rows = key_rows(c)
        lg = _dot_nt(k_ref[rows, :], q) + jnp.concatenate([bias_ref[c]] * A_HEADS, axis=1)
        m_new = jnp.maximum(m, jnp.max(lg, axis=0, keepdims=True))
        p = jnp.exp2(lg - m_new).astype(BF16)
        pv = lax.dot_general(v_ref[rows, :], p, (((0,), (0,)), ((), ())),
                             preferred_element_type=F32)
        return m_new, jnp.exp2(m - m_new) * acc + pv

    init = (jnp.full((1, cols), -3e38, F32), jnp.zeros((LANES, cols), F32))
    _, acc = lax.fori_loop(0, n_kc, attn_chunk, init)
    o = acc[:A_HEAD_DIM] / acc[A_HEAD_DIM:A_HEAD_DIM + 1]
    for h in range(A_HEADS):
        o_ref[:, h * A_HEAD_DIM:(h + 1) * A_HEAD_DIM] = (
            o[:, h * Q_BLOCK:(h + 1) * Q_BLOCK].T.astype(BF16))


def _dsa(q, iq, wi, k, v, ki, bsz, seq):
    nq = seq // Q_BLOCK
    n_kc = seq // KEY_CHUNK
    stacked = lambda heads: pl.BlockSpec((None, heads * Q_BLOCK, A_HEAD_DIM),
                                         lambda b, j: (b * nq + j, 0, 0))
    kspec = lambda w: pl.BlockSpec((seq, w), lambda b, j: (b, 0))
    return pl.pallas_call(
        functools.partial(_dsa_kernel, seq=seq),
        grid=(bsz, nq),
        in_specs=[stacked(A_HEADS), stacked(IDX_HEADS),
                  pl.BlockSpec((None, 8, Q_BLOCK), lambda b, j: (b * nq + j, 0, 0)),
                  kspec(A_HEAD_DIM),
                  kspec(LANES),
                  kspec(IDX_DIM)],
        out_specs=pl.BlockSpec((Q_BLOCK, A_OUT), lambda b, j: (b * nq + j, 0)),
        out_shape=jax.ShapeDtypeStruct((bsz * seq, A_OUT), BF16),
        scratch_shapes=[pltpu.VMEM((n_kc, KEY_CHUNK, Q_BLOCK), jnp.int32),
                        pltpu.VMEM((n_kc, KEY_CHUNK, Q_BLOCK), F32)],
        compiler_params=_cparams(("parallel", "parallel")),
        name="dsa",
    )(q, iq, wi, k, v, ki)


GLA_QK = B_HEADS * B_KEY_DIM
GLA_REP = N_SUB * GLA_QK
GLA_COLS = 2 * GLA_QK + 2 * B_OUT + LANES
GLA_V0 = 2 * GLA_QK


def _gla_kernel(h_ref, w_ref, wa_ref, ba_ref, ng_ref, o_ref, p_ref, g_ref, acc_ref, st_ref, *, ts):
    @pl.when(pl.program_id(1) == 0)
    def _():
        st_ref[...] = jnp.zeros_like(st_ref)

    p_ref[...] = _dot(h_ref[...].astype(BF16), w_ref[...])
    x = _dot(p_ref[:, GLA_V0 + 2 * B_OUT:].astype(BF16), wa_ref[...]) + ba_ref[...]
    g_ref[...] = (jnp.minimum(x, 0.0) - jnp.log1p(jnp.exp(-jnp.abs(x)))) * (1.0 / B_GATE_TAU)

    t_i = lax.broadcasted_iota(jnp.int32, (B_CHUNK, B_CHUNK), 0)
    s_i = lax.broadcasted_iota(jnp.int32, (B_CHUNK, B_CHUNK), 1)
    tri = t_i >= s_i
    tri_b = jnp.where(tri, 1.0, 0.0).astype(BF16)
    lane_sub = (lax.broadcasted_iota(jnp.int32, (1, GLA_REP), 1) // B_KEY_DIM) % N_SUB
    row_sub = lax.broadcasted_iota(jnp.int32, (B_CHUNK, 1), 0) // B_SUB
    q_mask = lane_sub <= row_sub
    k_mask = lane_sub == row_sub
    low_half = lax.broadcasted_iota(jnp.int32, (B_CHUNK, LANES), 1) < B_KEY_DIM

    def replicate(x):
        outs = []
        for pair in range(GLA_QK // LANES):
            v = x[:, pair * LANES:(pair + 1) * LANES]
            r = pltpu.roll(v, B_KEY_DIM, axis=1)
            outs += [jnp.where(low_half, v, r)] * (N_SUB // 2) + [jnp.where(low_half, r, v)] * (N_SUB // 2)
        return jnp.concatenate(outs, axis=1)

    def chunk(c, carry):
        rows = pl.ds(pl.multiple_of(c * B_CHUNK, B_CHUNK), B_CHUNK)
        g = g_ref[rows, :]
        g_hi = g.astype(BF16)
        g_mid = (g - g_hi.astype(F32)).astype(BF16)
        g_lo = (g - g_hi.astype(F32) - g_mid.astype(F32)).astype(BF16)
        b = replicate(_dot(tri_b, g_hi) + _dot(tri_b, g_mid) + _dot(tri_b, g_lo))
        ref = jnp.zeros((1, GLA_REP), F32)
        for jj in range(1, N_SUB):
            ref = ref + jnp.where(lane_sub == jj, b[jj * B_SUB - 1:jj * B_SUB, :], 0.0)
        d = b - ref
        e_q = jnp.where(q_mask, jnp.exp(jnp.minimum(d, 0.0)), 0.0)
        e_k = jnp.where(k_mask, jnp.exp(jnp.where(k_mask, -d, 0.0)), 0.0)
        b_last = b[B_CHUNK - 1:B_CHUNK, :]
        q_c = (replicate(p_ref[rows, 0:GLA_QK]) * e_q).astype(BF16)
        k_raw = replicate(p_ref[rows, GLA_QK:2 * GLA_QK])
        k_c = (k_raw * e_k).astype(BF16)
        k_st = (k_raw * jnp.exp(b_last - b)).astype(BF16)
        e_last = jnp.exp(b_last)
        for hh in range(B_HEADS):
            base = hh * N_SUB * B_KEY_DIM
            q_h = q_c[:, base:base + N_SUB * B_KEY_DIM]
            k_h = k_c[:, base:base + N_SUB * B_KEY_DIM]
            att = jnp.where(tri, _dot_nt(q_h, k_h), 0.0)
            v_h = p_ref[rows, GLA_V0 + hh * B_VAL_DIM:GLA_V0 + (hh + 1) * B_VAL_DIM].astype(BF16)
            st = st_ref[hh]
            o = _dot(att.astype(BF16), v_h) + _dot_nt(q_h[:, :B_KEY_DIM], st.astype(BF16))
            acc_ref[rows, hh * B_VAL_DIM:(hh + 1) * B_VAL_DIM] = o
            upd = _dot(v_h.T, k_st[:, base:base + B_KEY_DIM])
            st_ref[hh] = st * e_last[:, base:base + B_KEY_DIM] + upd
        return carry

    lax.fori_loop(0, ts // B_CHUNK, chunk, 0, unroll=4)

    r = p_ref[:, GLA_V0 + B_OUT:GLA_V0 + 2 * B_OUT]
    gate = r * (1.0 / (1.0 + jnp.exp(-r)))
    ng = ng_ref[...]
    for hh in range(B_HEADS):
        sl = slice(hh * B_VAL_DIM, (hh + 1) * B_VAL_DIM)
        o = acc_ref[:, sl]
        y = o * lax.rsqrt(jnp.mean(o * o, axis=-1, keepdims=True) + RMS_EPS) * ng
        o_ref[:, sl] = (gate[:, sl] * y).astype(BF16)


def _gla(h, w_b, wa, ba, norm_g, bsz, seq, ts=512):
    ns = seq // ts
    const = lambda shape: pl.BlockSpec(shape, lambda b, i: (0, 0))
    return pl.pallas_call(
        functools.partial(_gla_kernel, ts=ts),
        grid=(bsz, ns),
        in_specs=[pl.BlockSpec((ts, D_MODEL), lambda b, i: (b * ns + i, 0)),
                  const((D_MODEL, GLA_COLS)), const((LANES, GLA_QK)),
                  const((1, GLA_QK)), const((1, B_VAL_DIM))],
        out_specs=pl.BlockSpec((ts, B_OUT), lambda b, i: (b * ns + i, 0)),
        out_shape=jax.ShapeDtypeStruct((bsz * seq, B_OUT), BF16),
        scratch_shapes=[pltpu.VMEM((ts, GLA_COLS), F32),
                        pltpu.VMEM((ts, GLA_QK), F32),
                        pltpu.VMEM((ts, B_OUT), F32),
                        pltpu.VMEM((B_HEADS, B_VAL_DIM, B_KEY_DIM), F32)],
        compiler_params=_cparams(("parallel", "arbitrary")),
        name="gla",
    )(h, w_b, wa, ba, norm_g)


def _gmlp_kernel(h_ref, w_ref, lg_ref, lb_ref, ws_ref, bs_ref, o_ref, *, tm):
    z = _dot(h_ref[...].astype(BF16), w_ref[...])
    z = z * (0.5 * (1.0 + jnp.tanh(np.sqrt(2.0 / np.pi) * (z + 0.044715 * (z * z * z)))))
    u = z[:, :C_OUT]
    v = _layer_norm(z[:, C_OUT:], lg_ref[...], lb_ref[...]).astype(BF16)
    t_i = lax.broadcasted_iota(jnp.int32, (C_CHUNK, C_CHUNK), 0)
    s_i = lax.broadcasted_iota(jnp.int32, (C_CHUNK, C_CHUNK), 1)
    bs = bs_ref[...]
    for g in range(C_GROUPS):
        w = jnp.where(t_i >= s_i, ws_ref[g], 0.0).astype(BF16)
        sl = slice(g * C_GROUP_DIM, (g + 1) * C_GROUP_DIM)
        for c in range(tm // C_CHUNK):
            rows = slice(c * C_CHUNK, (c + 1) * C_CHUNK)
            mixed = _dot(w, v[rows, sl]) + bs[:, g:g + 1]
            o_ref[rows, sl] = (u[rows, sl] * mixed).astype(BF16)


def _gmlp(h, w_c, ln_g, ln_b, ws, bs_t, tm=512):
    n = h.shape[0]
    return pl.pallas_call(
        functools.partial(_gmlp_kernel, tm=tm),
        grid=(n // tm,),
        in_specs=[pl.BlockSpec((tm, D_MODEL), lambda i: (i, 0)),
                  pl.BlockSpec((D_MODEL, 2 * C_OUT), lambda i: (0, 0)),
                  pl.BlockSpec((1, C_OUT), lambda i: (0, 0)),
                  pl.BlockSpec((1, C_OUT), lambda i: (0, 0)),
                  pl.BlockSpec((C_GROUPS, C_CHUNK, C_CHUNK), lambda i: (0, 0, 0)),
                  pl.BlockSpec((C_CHUNK, LANES), lambda i: (0, 0))],
        out_specs=pl.BlockSpec((tm, C_OUT), lambda i: (i, 0)),
        out_shape=jax.ShapeDtypeStruct((n, C_OUT), BF16),
        compiler_params=_cparams(("parallel",)),
        name="gmlp",
    )(h, w_c, ln_g, ln_b, ws, bs_t)


def _sigmoid(x):
    return 1.0 / (1.0 + jnp.exp(-x))


def _merge_kernel(h_ref, ya_ref, yb_ref, yc_ref, wg_ref, wa_ref, wb_ref, wc_ref, wo_ref,
                  lg_ref, lb_ref, wr_ref, wrl_ref, br_ref, x_ref, oh_ref, route_ref):
    h = h_ref[...]
    hb = h.astype(BF16)
    acc = None
    for i, (y_ref, wbr_ref) in enumerate(((ya_ref, wa_ref), (yb_ref, wb_ref), (yc_ref, wc_ref))):
        gate = _sigmoid(_dot(hb, wg_ref[:, i * D_MODEL:(i + 1) * D_MODEL]))
        term = gate * _dot(y_ref[...], wbr_ref[...])
        acc = term if acc is None else acc + term
    mix = _dot(acc.astype(BF16), wo_ref[...])
    x1 = _layer_norm(DN_ALPHA * h + mix, lg_ref[...], lb_ref[...])
    x_ref[...] = x1

    x_hi = x1.astype(BF16)
    x_lo = (x1 - x_hi.astype(F32)).astype(BF16)
    logit = (_dot(x_hi, wr_ref[...]) + _dot(x_hi, wrl_ref[...]) + _dot(x_lo, wr_ref[...])) + br_ref[...]
    lane = lax.broadcasted_iota(jnp.int32, logit.shape, 1)
    big = jnp.int32(LANES)
    is_g = (lane >= N_EXPERTS) & (lane < N_EXPERTS + N_GROUPS)
    gl = jnp.where(is_g, logit, -jnp.inf)
    g_max = jnp.max(gl, axis=-1, keepdims=True)
    g_idx = jnp.min(jnp.where(is_g & (gl == g_max), lane, big), axis=-1, keepdims=True) - N_EXPERTS
    g_p = 1.0 / jnp.sum(jnp.where(is_g, jnp.exp(gl - g_max), 0.0), axis=-1, keepdims=True)
    lo = g_idx * EXPERTS_PER_GROUP
    in_g = (lane >= lo) & (lane < lo + EXPERTS_PER_GROUP)
    el = jnp.where(in_g, logit, -jnp.inf)
    e1 = jnp.max(el, axis=-1, keepdims=True)
    i1 = jnp.min(jnp.where(in_g & (el == e1), lane, big), axis=-1, keepdims=True)
    el2 = jnp.where(lane == i1, -jnp.inf, el)
    e2 = jnp.max(el2, axis=-1, keepdims=True)
    i2 = jnp.min(jnp.where(in_g & (lane != i1) & (el2 == e2), lane, big), axis=-1, keepdims=True)
    z2 = jnp.exp(e2 - e1)
    den = 1.0 + z2
    oh_ref[...] = jnp.where((lane == i1) | (lane == i2), 1.0, 0.0).astype(BF16)
    route_ref[...] = jnp.where(lane == 0, i1.astype(F32),
                               jnp.where(lane == 1, i2.astype(F32),
                                         jnp.where(lane == 2, (1.0 / den) * g_p,
                                                   jnp.where(lane == 3, (z2 / den) * g_p, 0.0))))


def _merge(h, ya, yb, yc, wg, wa, wb, wc, wo, ln_g, ln_b, wr, wr_lo, br, tm=1024):
    n = h.shape[0]
    row = lambda w: pl.BlockSpec((tm, w), lambda i: (i, 0))
    const = lambda r, c: pl.BlockSpec((r, c), lambda i: (0, 0), pipeline_mode=pl.Buffered(1))
    return pl.pallas_call(
        _merge_kernel,
        grid=(n // tm,),
        in_specs=[row(D_MODEL), row(A_OUT), row(B_OUT), row(C_OUT),
                  const(D_MODEL, 3 * D_MODEL), const(A_OUT, D_MODEL), const(B_OUT, D_MODEL),
                  const(C_OUT, D_MODEL), const(D_MODEL, D_MODEL),
                  const(1, D_MODEL), const(1, D_MODEL), const(D_MODEL, LANES), const(D_MODEL, LANES),
                  const(1, LANES)],
        out_specs=[row(D_MODEL), row(LANES), row(LANES)],
        out_shape=[jax.ShapeDtypeStruct((n, D_MODEL), F32),
                   jax.ShapeDtypeStruct((n, LANES), BF16),
                   jax.ShapeDtypeStruct((n, LANES), F32)],
        compiler_params=_cparams(("parallel",)),
        name="merge",
    )(h, ya, yb, yc, wg, wa, wb, wc, wo, ln_g, ln_b, wr, wr_lo, br)


FFN_TILE = 512
MOE_TOK = 512


def _rank_kernel(oh_ref, route_ref, r_ref, cnt_ref, carry_ref):
    @pl.when(pl.program_id(0) == 0)
    def _():
        carry_ref[...] = jnp.zeros_like(carry_ref)

    oh = oh_ref[...]
    tt = oh.shape[0]
    r_i = lax.broadcasted_iota(jnp.int32, (tt, tt), 0)
    c_i = lax.broadcasted_iota(jnp.int32, (tt, tt), 1)
    lower = jnp.where(c_i < r_i, 1.0, 0.0).astype(BF16)
    rank = _dot(lower, oh) + carry_ref[...]
    route = route_ref[...]
    lane = lax.broadcasted_iota(jnp.int32, rank.shape, 1)
    i1 = route[:, 0:1].astype(jnp.int32)
    i2 = route[:, 1:2].astype(jnp.int32)
    r1 = jnp.sum(jnp.where(lane == i1, rank, 0.0), axis=-1, keepdims=True)
    r2 = jnp.sum(jnp.where(lane == i2, rank, 0.0), axis=-1, keepdims=True)
    r_ref[...] = jnp.where(lane == 0, r1, jnp.where(lane == 1, r2, 0.0))
    carry_ref[...] += jnp.sum(oh.astype(F32), axis=0, keepdims=True)
    cnt_ref[...] = carry_ref[...]


def _rank(oh, route, tt=512):
    n = oh.shape[0]
    return pl.pallas_call(
        _rank_kernel,
        grid=(n // tt,),
        in_specs=[pl.BlockSpec((tt, LANES), lambda i: (i, 0)),
                  pl.BlockSpec((tt, LANES), lambda i: (i, 0))],
        out_specs=[pl.BlockSpec((tt, LANES), lambda i: (i, 0)),
                   pl.BlockSpec((1, LANES), lambda i: (0, 0))],
        out_shape=[jax.ShapeDtypeStruct((n, LANES), F32),
                   jax.ShapeDtypeStruct((1, LANES), F32)],
        scratch_shapes=[pltpu.VMEM((1, LANES), F32)],
        compiler_params=_cparams(("arbitrary",)),
        name="moe_rank",
    )(oh, route)


def _row_copy(src_ref, src_row, dst_ref, dst_row, sem):
    return pltpu.make_async_copy(src_ref.at[pl.ds(src_row, 1), :], dst_ref.at[pl.ds(dst_row, 1), :], sem)


def _dispatch_kernel(pos1_ref, pos2_ref, x_ref, init_ref, xs_ref, sem):
    del init_ref
    base = pl.program_id(0) * MOE_TOK

    def issue(t, carry):
        _row_copy(x_ref, t, xs_ref, pos1_ref[base + t], sem).start()
        _row_copy(x_ref, t, xs_ref, pos2_ref[base + t], sem).start()
        return carry

    lax.fori_loop(0, MOE_TOK, issue, 0, unroll=32)
    for _ in range(2):
        pltpu.make_async_copy(x_ref, xs_ref.at[pl.ds(0, MOE_TOK), :], sem).wait()


def _dispatch(pos1, pos2, x1, n_slots):
    n = x1.shape[0]
    init = jnp.zeros((n_slots, D_MODEL), F32)
    return pl.pallas_call(
        _dispatch_kernel,
        grid_spec=pltpu.PrefetchScalarGridSpec(
            num_scalar_prefetch=2,
            grid=(n // MOE_TOK,),
            in_specs=[pl.BlockSpec((MOE_TOK, D_MODEL), lambda i, p1, p2: (i, 0)),
                      pl.BlockSpec(memory_space=pl.ANY)],
            out_specs=pl.BlockSpec(memory_space=pl.ANY),
            scratch_shapes=[pltpu.SemaphoreType.DMA(())]),
        out_shape=jax.ShapeDtypeStruct((n_slots, D_MODEL), F32),
        input_output_aliases={3: 0},
        compiler_params=_cparams(("arbitrary",)),
        name="moe_dispatch",
    )(pos1, pos2, x1, init)


def _ffn_kernel(te_ref, nu_ref, xs_ref, wg_ref, wu_ref, wd_ref, ys_ref):
    del te_ref
    i = pl.program_id(0)

    @pl.when(i < nu_ref[0])
    def _():
        xb = xs_ref[...].astype(BF16)
        gte = _dot(xb, wg_ref[...].astype(BF16))
        hid = gte * _sigmoid(gte) * _dot(xb, wu_ref[...].astype(BF16))
        ys_ref[...] = _dot(hid.astype(BF16), wd_ref[...].astype(BF16))

    @pl.when(i >= nu_ref[0])
    def _():
        ys_ref[...] = jnp.zeros_like(ys_ref)


def _ffn(tile_expert, n_used, xs, w_gate, w_up, w_down, layer):
    n_slots = xs.shape[0]
    wspec = lambda r, c: pl.BlockSpec((None, None, r, c), lambda i, te, nu: (layer, te[i], 0, 0))
    return pl.pallas_call(
        _ffn_kernel,
        grid_spec=pltpu.PrefetchScalarGridSpec(
            num_scalar_prefetch=2,
            grid=(n_slots // FFN_TILE,),
            in_specs=[pl.BlockSpec((FFN_TILE, D_MODEL), lambda i, te, nu: (i, 0)),
                      wspec(D_MODEL, D_EXPERT), wspec(D_MODEL, D_EXPERT), wspec(D_EXPERT, D_MODEL)],
            out_specs=pl.BlockSpec((FFN_TILE, D_MODEL), lambda i, te, nu: (i, 0))),
        out_shape=jax.ShapeDtypeStruct((n_slots, D_MODEL), F32),
        compiler_params=_cparams(("arbitrary",)),
        name="moe_ffn",
    )(tile_expert, n_used, xs, w_gate, w_up, w_down)


def _combine_kernel(pos1_ref, pos2_ref, x_ref, route_ref, lg_ref, lb_ref, ys_ref, o_ref, buf_ref, sem):
    i = pl.program_id(0)

    def gather(step, slot):
        base = step * MOE_TOK

        def issue(t, carry):
            _row_copy(ys_ref, pos1_ref[base + t], buf_ref.at[slot, 0], t, sem.at[slot]).start()
            _row_copy(ys_ref, pos2_ref[base + t], buf_ref.at[slot, 1], t, sem.at[slot]).start()
            return carry

        lax.fori_loop(0, MOE_TOK, issue, 0, unroll=32)

    @pl.when(i == 0)
    def _():
        gather(0, 0)

    @pl.when(i + 1 < pl.num_programs(0))
    def _():
        gather(i + 1, (i + 1) % 2)

    slot = i % 2
    for s in range(2):
        pltpu.make_async_copy(ys_ref.at[pl.ds(0, MOE_TOK), :], buf_ref.at[slot, s], sem.at[slot]).wait()
    route = route_ref[...]
    ffn = route[:, 2:3] * buf_ref[slot, 0] + route[:, 3:4] * buf_ref[slot, 1]
    o_ref[...] = _layer_norm(DN_ALPHA * x_ref[...] + ffn, lg_ref[...], lb_ref[...])


def _combine(pos1, pos2, x1, route, ln_g, ln_b, ys):
    n = x1.shape[0]
    row = lambda w: pl.BlockSpec((MOE_TOK, w), lambda i, p1, p2: (i, 0))
    const = lambda w: pl.BlockSpec((1, w), lambda i, p1, p2: (0, 0))
    return pl.pallas_call(
        _combine_kernel,
        grid_spec=pltpu.PrefetchScalarGridSpec(
            num_scalar_prefetch=2,
            grid=(n // MOE_TOK,),
            in_specs=[row(D_MODEL), row(LANES), const(D_MODEL), const(D_MODEL),
                      pl.BlockSpec(memory_space=pl.ANY)],
            out_specs=row(D_MODEL),
            scratch_shapes=[pltpu.VMEM((2, 2, MOE_TOK, D_MODEL), F32),
                            pltpu.SemaphoreType.DMA((2,))]),
        out_shape=jax.ShapeDtypeStruct((n, D_MODEL), F32),
        compiler_params=_cparams(("arbitrary",)),
        name="moe_combine",
    )(pos1, pos2, x1, route, ln_g, ln_b, ys)


def _moe(x1, oh, route, w_gate, w_up, w_down, layer, ln_g, ln_b):
    n = x1.shape[0]
    n_slots = TOP_K_INNER * n + N_EXPERTS * FFN_TILE
    rank, cnt = _rank(oh, route)
    cnt = cnt[0, :N_EXPERTS].astype(jnp.int32)
    padded = (cnt + FFN_TILE - 1) // FFN_TILE * FFN_TILE
    ends = jnp.cumsum(padded)
    offs = ends - padded
    experts = jnp.arange(N_EXPERTS, dtype=jnp.int32)
    i1 = route[:, 0].astype(jnp.int32)
    i2 = route[:, 1].astype(jnp.int32)
    off_of = lambda idx: jnp.sum(jnp.where(idx[:, None] == experts[None, :], offs[None, :], 0), axis=1)
    pos1 = off_of(i1) + rank[:, 0].astype(jnp.int32)
    pos2 = off_of(i2) + rank[:, 1].astype(jnp.int32)
    tile_start = jnp.arange(n_slots // FFN_TILE, dtype=jnp.int32) * FFN_TILE
    tile_expert = jnp.minimum(jnp.sum(tile_start[:, None] >= ends[None, :], axis=1), N_EXPERTS - 1)
    n_used = (ends[-1] // FFN_TILE).reshape(1)

    xs = _dispatch(pos1, pos2, x1, n_slots)
    ys = _ffn(tile_expert.astype(jnp.int32), n_used.astype(jnp.int32), xs, w_gate, w_up, w_down, layer)
    return _combine(pos1, pos2, x1, route, ln_g, ln_b, ys)


def _rope_tables(positions):
    inv_freq = ROPE_THETA ** (-jnp.arange(0, ROT_DIM, 2, dtype=F32) / ROT_DIM)
    ang = positions.astype(F32).reshape(-1, 1) * inv_freq
    cos, sin = jnp.cos(ang), jnp.sin(ang)
    n = ang.shape[0]
    half = ROT_DIM // 2
    c64 = jnp.concatenate([cos, cos, jnp.ones((n, A_HEAD_DIM - ROT_DIM), F32)], axis=1)
    s64 = jnp.concatenate([-sin, sin, jnp.zeros((n, A_HEAD_DIM - ROT_DIM), F32)], axis=1)
    del half
    return jnp.tile(c64, (1, 2)), jnp.tile(s64, (1, 2))


def _split_w_in(w):
    widths = (A_OUT, A_HEAD_DIM, A_HEAD_DIM, IDX_HEADS * IDX_DIM, IDX_DIM, IDX_HEADS,
              B_HEADS * B_KEY_DIM, B_HEADS * B_KEY_DIM, B_OUT, B_GATE_RANK, B_OUT,
              2 * C_OUT, D_MODEL, D_MODEL, D_MODEL)
    pts = np.cumsum((0,) + widths)
    return [w[:, int(pts[i]):int(pts[i + 1])] for i in range(len(widths))]


def kernel(x, positions, ln_in_g, ln_in_b, w_in, idx_k_g, gla_wa2, gla_ba, gla_norm_g, gm_ln_g, gm_ln_b, gm_ws, gm_bs, w_branch_a, w_branch_b, w_branch_c, w_out, ln1_g, ln1_b, w_rg, b_rg, w_re, b_re, w_gate, w_up, w_down, ln2_g, ln2_b):
    bsz, seq, d = x.shape
    n = bsz * seq
    cos_t, sin_t = _rope_tables(positions)
    h = _entry_ln(x.reshape(n, d), ln_in_g, ln_in_b)
    for l in range(DEPTH):
        (a_q, a_k, a_v, i_q, i_k, i_w, b_q, b_k, b_v, b_glr, b_r, c_uv,
         g_a, g_b, g_c) = _split_w_in(w_in[l])
        zpad = lambda c: jnp.zeros((d, c), F32)
        w_a = jnp.concatenate([a_q, a_k, a_v, i_q, i_k, i_w, zpad(LANES - IDX_DIM - IDX_HEADS)],
                              axis=1).astype(BF16)
        w_b = jnp.concatenate([b_q * (B_KEY_DIM ** -0.5), b_k, b_v, b_r,
                               b_glr, zpad(LANES - B_GATE_RANK)], axis=1).astype(BF16)
        wa_pad = jnp.concatenate([gla_wa2[l],
                                  jnp.zeros((LANES - B_GATE_RANK, GLA_QK), F32)], axis=0).astype(BF16)
        w_g = jnp.concatenate([g_a, g_b, g_c], axis=1).astype(BF16)
        ikg = jnp.concatenate([idx_k_g[l], jnp.zeros((LANES - IDX_DIM,), F32)]).reshape(1, LANES)
        bs_t = jnp.concatenate([gm_bs[l].T, jnp.zeros((C_CHUNK, LANES - C_GROUPS), F32)], axis=1)
        w_r = jnp.concatenate([w_re[l], w_rg[l], zpad(LANES - N_GROUPS - N_EXPERTS)], axis=1)
        w_r_hi = w_r.astype(BF16)
        w_r_lo = (w_r - w_r_hi.astype(F32)).astype(BF16)
        b_r_all = jnp.concatenate([b_re[l], b_rg[l],
                                   jnp.zeros((LANES - N_GROUPS - N_EXPERTS,), F32)]).reshape(1, LANES)

        q, k, v, iq, ki, wi = _proj_a(h, w_a, cos_t, sin_t, ikg)
        y_a = _dsa(q, iq, wi, k, v, ki, bsz, seq)
        y_b = _gla(h, w_b, wa_pad, gla_ba[l].reshape(1, -1), gla_norm_g[l].reshape(1, -1), bsz, seq)
        y_c = _gmlp(h, c_uv.astype(BF16), gm_ln_g[l].reshape(1, -1), gm_ln_b[l].reshape(1, -1),
                    gm_ws[l], bs_t)
        x1, oh, route = _merge(h, y_a, y_b, y_c, w_g,
                               w_branch_a[l].astype(BF16), w_branch_b[l].astype(BF16),
                               w_branch_c[l].astype(BF16), w_out[l].astype(BF16),
                               ln1_g[l].reshape(1, -1), ln1_b[l].reshape(1, -1), w_r_hi, w_r_lo, b_r_all)
        h = _moe(x1, oh, route, w_gate, w_up, w_down, l,
                 ln2_g[l].reshape(1, -1), ln2_b[l].reshape(1, -1))
    return h.reshape(bsz, seq, d)
```

```python
import functools

import numpy as np
import jax
import jax.numpy as jnp
from jax import lax
from jax.experimental import pallas as pl
from jax.experimental.pallas import tpu as pltpu

F32 = jnp.float32
BF16 = jnp.bfloat16

D_MODEL = 1024
DEPTH = 2
A_HEADS = 8
A_HEAD_DIM = 64
A_OUT = 512
IDX_HEADS = 4
IDX_DIM = 64
TOPK_MAX = 256
Q_BLOCK = 128
ROPE_THETA = 500000.0
ROT_DIM = 16
B_HEADS = 4
B_KEY_DIM = 64
B_VAL_DIM = 128
B_OUT = 512
B_GATE_RANK = 16
B_GATE_TAU = 16.0
B_CHUNK = 64
B_SUB = 16
N_SUB = B_CHUNK // B_SUB
C_GROUPS = 4
C_GROUP_DIM = 128
C_OUT = 512
C_CHUNK = 128
N_GROUPS = 4
EXPERTS_PER_GROUP = 8
N_EXPERTS = 32
D_EXPERT = 256
TOP_K_INNER = 2
DN_ALPHA = (2 * DEPTH) ** 0.25
LN_EPS = 1e-5
RMS_EPS = 1e-6

LANES = 128
NEG_BIG = -1e30
LOG2_E = 1.4426950408889634
INT_MIN = -2 ** 31
VMEM_LIMIT = 56 * 1024 * 1024


def _cparams(sem):
    return pltpu.CompilerParams(dimension_semantics=sem, vmem_limit_bytes=VMEM_LIMIT)


def _layer_norm(x, g, b):
    mu = jnp.mean(x, axis=-1, keepdims=True)
    xc = x - mu
    var = jnp.mean(xc * xc, axis=-1, keepdims=True)
    return xc * lax.rsqrt(var + LN_EPS) * g + b


def _dot(a, b):
    return jnp.dot(a, b, preferred_element_type=F32)


def _dot_nt(a, b):
    return lax.dot_general(a, b, (((1,), (1,)), ((), ())), preferred_element_type=F32)


def _ln_kernel(x_ref, g_ref, b_ref, o_ref):
    o_ref[...] = _layer_norm(x_ref[...], g_ref[...], b_ref[...])


def _entry_ln(x, g, b, tm=1024):
    n = x.shape[0]
    return pl.pallas_call(
        _ln_kernel,
        grid=(n // tm,),
        in_specs=[pl.BlockSpec((tm, D_MODEL), lambda i: (i, 0)),
                  pl.BlockSpec((1, D_MODEL), lambda i: (0, 0)),
                  pl.BlockSpec((1, D_MODEL), lambda i: (0, 0))],
        out_specs=pl.BlockSpec((tm, D_MODEL), lambda i: (i, 0)),
        out_shape=jax.ShapeDtypeStruct((n, D_MODEL), F32),
        compiler_params=_cparams(("parallel",)),
        name="entry_ln",
    )(x, g.reshape(1, -1), b.reshape(1, -1))


def _rope_slab(xs, c, s):
    lane = lax.broadcasted_iota(jnp.int32, xs.shape, 1) % A_HEAD_DIM
    fwd = pltpu.roll(xs, LANES - ROT_DIM // 2, axis=1)
    bwd = pltpu.roll(xs, ROT_DIM // 2, axis=1)
    partner = jnp.where(lane < ROT_DIM // 2, fwd, bwd)
    return xs * c + partner * s


def _proj_a_kernel(h_ref, w_ref, c_ref, s_ref, g_ref,
                   q_ref, k_ref, v_ref, iq_ref, ki_ref, wi_ref):
    p = _dot(h_ref[...].astype(BF16), w_ref[...])
    c = c_ref[...]
    s = s_ref[...]
    lane = lax.broadcasted_iota(jnp.int32, c.shape, 1)
    lo = lane < A_HEAD_DIM
    n_blk = p.shape[0] // Q_BLOCK

    def put_heads(dst_ref, slab, first_head):
        for half in range(2):
            hd = first_head + half
            cols = slab[:, half * A_HEAD_DIM:(half + 1) * A_HEAD_DIM].astype(BF16)
            for qb in range(n_blk):
                dst_ref[qb, hd * Q_BLOCK:(hd + 1) * Q_BLOCK, :] = cols[qb * Q_BLOCK:(qb + 1) * Q_BLOCK, :]

    for i in range(4):
        sl = slice(i * LANES, (i + 1) * LANES)
        put_heads(q_ref, _rope_slab(p[:, sl], c, s) * (A_HEAD_DIM ** -0.5 * LOG2_E), 2 * i)
    kv = p[:, 512:640]
    kv = _rope_slab(kv, jnp.where(lo, c, 1.0), jnp.where(lo, s, 0.0))
    k_ref[...] = kv[:, :A_HEAD_DIM].astype(BF16)
    v_ones = jnp.where(lo, pltpu.roll(kv, A_HEAD_DIM, axis=1), jnp.where(lane == A_HEAD_DIM, 1.0, 0.0))
    v_ref[...] = v_ones.astype(BF16)
    for i in range(2):
        sl = slice(640 + i * LANES, 640 + (i + 1) * LANES)
        put_heads(iq_ref, _rope_slab(p[:, sl], c, s), 2 * i)
    last = p[:, 896:1024]
    mu = jnp.sum(jnp.where(lo, last, 0.0), axis=-1, keepdims=True) * (1.0 / IDX_DIM)
    xc = jnp.where(lo, last - mu, 0.0)
    var = jnp.sum(xc * xc, axis=-1, keepdims=True) * (1.0 / IDX_DIM)
    kin = xc * lax.rsqrt(var + LN_EPS) * g_ref[...]
    kin = _rope_slab(kin, jnp.where(lo, c, 1.0), jnp.where(lo, s, 0.0))
    ki_ref[...] = kin[:, :IDX_DIM].astype(BF16)
    wi = pltpu.roll(last, LANES - IDX_DIM, axis=1) * (IDX_HEADS ** -0.5 * IDX_DIM ** -0.5)
    for qb in range(n_blk):
        wi_ref[qb] = wi[qb * Q_BLOCK:(qb + 1) * Q_BLOCK, :].T[0:8, :]


def _proj_a(h, w_a, cos_t, sin_t, ikg, tm=512):
    n = h.shape[0]
    row = lambda w: pl.BlockSpec((tm, w), lambda i: (i, 0))
    stacked = lambda heads: pl.BlockSpec((tm // Q_BLOCK, heads * Q_BLOCK, A_HEAD_DIM), lambda i: (i, 0, 0))
    return pl.pallas_call(
        _proj_a_kernel,
        grid=(n // tm,),
        in_specs=[row(D_MODEL),
                  pl.BlockSpec((D_MODEL, 1024), lambda i: (0, 0)),
                  row(LANES), row(LANES),
                  pl.BlockSpec((1, LANES), lambda i: (0, 0))],
        out_specs=[stacked(A_HEADS), row(A_HEAD_DIM),
                   row(LANES),
                   stacked(IDX_HEADS), row(IDX_DIM),
                   pl.BlockSpec((tm // Q_BLOCK, 8, Q_BLOCK), lambda i: (i, 0, 0))],
        out_shape=[jax.ShapeDtypeStruct((n // Q_BLOCK, A_HEADS * Q_BLOCK, A_HEAD_DIM), BF16),
                   jax.ShapeDtypeStruct((n, A_HEAD_DIM), BF16),
                   jax.ShapeDtypeStruct((n, LANES), BF16),
                   jax.ShapeDtypeStruct((n // Q_BLOCK, IDX_HEADS * Q_BLOCK, IDX_DIM), BF16),
                   jax.ShapeDtypeStruct((n, IDX_DIM), BF16),
                   jax.ShapeDtypeStruct((n // Q_BLOCK, 8, Q_BLOCK), F32)],
        compiler_params=_cparams(("parallel",)),
        name="proj_a",
    )(h, w_a, cos_t, sin_t, ikg)


KEY_CHUNK = 512
FIELD_BITS = 15
LOW_BITS = 32 - 2 * FIELD_BITS
FIELD_MASK = (1 << FIELD_BITS) - 1
GUARD_BITS = -2147450880
PAIR_ONE = 0x00010001


def _dsa_kernel(q_ref, iq_ref, wi_ref, k_ref, v_ref, ki_ref, o_ref,
                key_ref, pk_ref, bias_ref, *, seq):
    j = pl.program_id(1)
    n_sel = min(TOPK_MAX, seq // 4)
    n_kc = (j * Q_BLOCK + Q_BLOCK + KEY_CHUNK - 1) // KEY_CHUNK
    qpos = j * Q_BLOCK + lax.broadcasted_iota(jnp.int32, (1, Q_BLOCK), 1)
    kk = jnp.minimum(n_sel, qpos + 1).astype(F32)
    row_pos = lax.broadcasted_iota(jnp.int32, (KEY_CHUNK, 1), 0)

    def key_rows(c):
        return pl.ds(pl.multiple_of(c * KEY_CHUNK, KEY_CHUNK), KEY_CHUNK)

    iq = iq_ref[...]
    wi = wi_ref[...]

    def score_chunk(c, carry):
        d = jnp.maximum(_dot_nt(ki_ref[key_rows(c), :], iq), 0.0)
        score = wi[0:1, :] * d[:, 0:Q_BLOCK]
        for h in range(1, IDX_HEADS):
            score = score + wi[h:h + 1, :] * d[:, h * Q_BLOCK:(h + 1) * Q_BLOCK]
        bits = pltpu.bitcast(score + 0.0, jnp.int32)
        key = bits ^ ((bits >> 31) & 0x7FFFFFFF)
        key = jnp.where(c * KEY_CHUNK + row_pos <= qpos, key, INT_MIN)
        key_ref[c] = key
        pk_ref[c] = pack(lax.shift_right_logical(key ^ INT_MIN, FIELD_BITS + LOW_BITS))
        return carry

    def pack(f):
        hf = KEY_CHUNK // 2
        return (f[:hf] << 16) | f[hf:] | GUARD_BITS

    lax.fori_loop(0, n_kc, score_chunk, 0)

    def count(pred):
        def body(c, acc):
            m = jnp.where(pred(key_ref[c]), 1.0, 0.0)
            return acc + jnp.sum(m.reshape(KEY_CHUNK // 64, 64, Q_BLOCK), axis=0)
        acc = lax.fori_loop(0, n_kc, body, jnp.zeros((64, Q_BLOCK), F32))
        return jnp.sum(acc, axis=0, keepdims=True)

    def count_packed(cand):
        both = (cand << 16) | cand

        def body(c, acc):
            flags = ((pk_ref[c] - both) >> 15) & PAIR_ONE
            parts = [flags[t * 64:(t + 1) * 64] for t in range(KEY_CHUNK // 128)]
            while len(parts) > 1:
                parts = [a + b for a, b in zip(parts[::2], parts[1::2])]
            return acc + parts[0]

        acc = lax.fori_loop(0, n_kc, body, jnp.zeros((64, Q_BLOCK), jnp.int32))
        return jnp.sum(((acc & 0xFFFF) + (acc >> 16)).astype(F32), axis=0, keepdims=True)

    def search_field(target):
        def step(i, t):
            cand = t | jnp.left_shift(jnp.int32(1), FIELD_BITS - 1 - i)
            return jnp.where(count_packed(cand) >= target, cand, t)
        return lax.fori_loop(0, FIELD_BITS, step, jnp.zeros((1, Q_BLOCK), jnp.int32))

    t_a = search_field(kk)
    top_of_a = ((t_a << (FIELD_BITS + LOW_BITS)) | ((1 << (FIELD_BITS + LOW_BITS)) - 1)) ^ INT_MIN
    need_b = kk - count(lambda kc: kc > top_of_a)

    def repack(c, carry):
        u = key_ref[c] ^ INT_MIN
        member = lax.shift_right_logical(u, FIELD_BITS + LOW_BITS) == t_a
        pk_ref[c] = pack(jnp.where(member, lax.shift_right_logical(u, LOW_BITS) & FIELD_MASK, 0))
        return carry

    lax.fori_loop(0, n_kc, repack, 0)
    base = (t_a << (FIELD_BITS + LOW_BITS)) | (search_field(need_b) << LOW_BITS)

    def low_step(i, t_c):
        bit = jnp.left_shift(jnp.int32(1), LOW_BITS - 1 - i)
        cand_s = (base | t_c | bit) ^ INT_MIN
        return jnp.where(count(lambda kc: kc >= cand_s) >= kk, t_c | bit, t_c)

    thr = (base | lax.fori_loop(0, LOW_BITS, low_step, jnp.zeros((1, Q_BLOCK), jnp.int32))) ^ INT_MIN
    need = kk - count(lambda kc: kc > thr)
    r_i = lax.broadcasted_iota(jnp.int32, (LANES, LANES), 0)
    c_i = lax.broadcasted_iota(jnp.int32, (LANES, LANES), 1)
    lower = jnp.where(c_i < r_i, 1.0, 0.0).astype(BF16)

    def fill(c, carry):
        for t in range(KEY_CHUNK // LANES):
            rows = slice(t * LANES, (t + 1) * LANES)
            kt = key_ref[c, rows, :]
            eq = kt == thr
            eq_f = jnp.where(eq, 1.0, 0.0)
            pre = _dot(lower, eq_f.astype(BF16)) + carry
            sel = (kt > thr) | (eq & (pre < need))
            bias_ref[c, rows, :] = jnp.where(sel, 0.0, NEG_BIG)
            carry = carry + jnp.sum(eq_f, axis=0, keepdims=True)
        return carry

    lax.fori_loop(0, n_kc, fill, jnp.zeros((1, Q_BLOCK), F32))

    q = q_ref[...]
    cols = A_HEADS * Q_BLOCK

    def attn_chunk(c, carry):
        m, acc = carry
        rows = key_rows(c)
        lg = _dot_nt(k_ref[rows, :], q) + jnp.concatenate([bias_ref[c]] * A_HEADS, axis=1)
        m_new = jnp.maximum(m, jnp.max(lg, axis=0, keepdims=True))
        p = jnp.exp2(lg - m_new).astype(BF16)
        pv = lax.dot_general(v_ref[rows, :], p, (((0,), (0,)), ((), ())),
                             preferred_element_type=F32)
        return m_new, jnp.exp2(m - m_new) * acc + pv

    init = (jnp.full((1, cols), -3e38, F32), jnp.zeros((LANES, cols), F32))
    _, acc = lax.fori_loop(0, n_kc, attn_chunk, init)
    o = acc[:A_HEAD_DIM] / acc[A_HEAD_DIM:A_HEAD_DIM + 1]
    for h in range(A_HEADS):
        o_ref[:, h * A_HEAD_DIM:(h + 1) * A_HEAD_DIM] = (
            o[:, h * Q_BLOCK:(h + 1) * Q_BLOCK].T.astype(BF16))


def _dsa(q, iq, wi, k, v, ki, bsz, seq):
    nq = seq // Q_BLOCK
    n_kc = seq // KEY_CHUNK
    stacked = lambda heads: pl.BlockSpec((None, heads * Q_BLOCK, A_HEAD_DIM),
                                         lambda b, j: (b * nq + j, 0, 0))
    kspec = lambda w: pl.BlockSpec((seq, w), lambda b, j: (b, 0))
    return pl.pallas_call(
        functools.partial(_dsa_kernel, seq=seq),
        grid=(bsz, nq),
        in_specs=[stacked(A_HEADS), stacked(IDX_HEADS),
                  pl.BlockSpec((None, 8, Q_BLOCK), lambda b, j: (b * nq + j, 0, 0)),
                  kspec(A_HEAD_DIM),
                  kspec(LANES),
                  kspec(IDX_DIM)],
        out_specs=pl.BlockSpec((Q_BLOCK, A_OUT), lambda b, j: (b * nq + j, 0)),
        out_shape=jax.ShapeDtypeStruct((bsz * seq, A_OUT), BF16),
        scratch_shapes=[pltpu.VMEM((n_kc, KEY_CHUNK, Q_BLOCK), jnp.int32),
                        pltpu.VMEM((n_kc, KEY_CHUNK // 2, Q_BLOCK), jnp.int32),
                        pltpu.VMEM((n_kc, KEY_CHUNK, Q_BLOCK), F32)],
        compiler_params=_cparams(("parallel", "parallel")),
        name="dsa",
    )(q, iq, wi, k, v, ki)


GLA_QK = B_HEADS * B_KEY_DIM
GLA_REP = N_SUB * GLA_QK
GLA_COLS = 2 * GLA_QK + 2 * B_OUT + LANES
GLA_V0 = 2 * GLA_QK


def _gla_kernel(h_ref, w_ref, wa_ref, ba_ref, ng_ref, o_ref, p_ref, g_ref, acc_ref, st_ref, *, ts):
    @pl.when(pl.program_id(1) == 0)
    def _():
        st_ref[...] = jnp.zeros_like(st_ref)

    p_ref[...] = _dot(h_ref[...].astype(BF16), w_ref[...])
    x = _dot(p_ref[:, GLA_V0 + 2 * B_OUT:].astype(BF16), wa_ref[...]) + ba_ref[...]
    g_ref[...] = (jnp.minimum(x, 0.0) - jnp.log1p(jnp.exp(-jnp.abs(x)))) * (1.0 / B_GATE_TAU)

    t_i = lax.broadcasted_iota(jnp.int32, (B_CHUNK, B_CHUNK), 0)
    s_i = lax.broadcasted_iota(jnp.int32, (B_CHUNK, B_CHUNK), 1)
    tri = t_i >= s_i
    tri_b = jnp.where(tri, 1.0, 0.0).astype(BF16)
    lane_sub = (lax.broadcasted_iota(jnp.int32, (1, GLA_REP), 1) // B_KEY_DIM) % N_SUB
    row_sub = lax.broadcasted_iota(jnp.int32, (B_CHUNK, 1), 0) // B_SUB
    q_mask = lane_sub <= row_sub
    k_mask = lane_sub == row_sub
    low_half = lax.broadcasted_iota(jnp.int32, (B_CHUNK, LANES), 1) < B_KEY_DIM

    def replicate(x):
        outs = []
        for pair in range(GLA_QK // LANES):
            v = x[:, pair * LANES:(pair + 1) * LANES]
            r = pltpu.roll(v, B_KEY_DIM, axis=1)
            outs += [jnp.where(low_half, v, r)] * (N_SUB // 2) + [jnp.where(low_half, r, v)] * (N_SUB // 2)
        return jnp.concatenate(outs, axis=1)

    def chunk(c, carry):
        rows = pl.ds(pl.multiple_of(c * B_CHUNK, B_CHUNK), B_CHUNK)
        g = g_ref[rows, :]
        g_hi = g.astype(BF16)
        g_mid = (g - g_hi.astype(F32)).astype(BF16)
        g_lo = (g - g_hi.astype(F32) - g_mid.astype(F32)).astype(BF16)
        b = replicate(_dot(tri_b, g_hi) + _dot(tri_b, g_mid) + _dot(tri_b, g_lo))
        ref = jnp.zeros((1, GLA_REP), F32)
        for jj in range(1, N_SUB):
            ref = ref + jnp.where(lane_sub == jj, b[jj * B_SUB - 1:jj * B_SUB, :], 0.0)
        d = b - ref
        e_q = jnp.where(q_mask, jnp.exp(jnp.minimum(d, 0.0)), 0.0)
        e_k = jnp.where(k_mask, jnp.exp(jnp.where(k_mask, -d, 0.0)), 0.0)
        b_last = b[B_CHUNK - 1:B_CHUNK, :]
        q_c = (replicate(p_ref[rows, 0:GLA_QK]) * e_q).astype(BF16)
        k_raw = replicate(p_ref[rows, GLA_QK:2 * GLA_QK])
        k_c = (k_raw * e_k).astype(BF16)
        k_st = (k_raw * jnp.exp(b_last - b)).astype(BF16)
        e_last = jnp.exp(b_last)
        for hh in range(B_HEADS):
            base = hh * N_SUB * B_KEY_DIM
            q_h = q_c[:, base:base + N_SUB * B_KEY_DIM]
            k_h = k_c[:, base:base + N_SUB * B_KEY_DIM]
            att = jnp.where(tri, _dot_nt(q_h, k_h), 0.0)
            v_h = p_ref[rows, GLA_V0 + hh * B_VAL_DIM:GLA_V0 + (hh + 1) * B_VAL_DIM].astype(BF16)
            st = st_ref[hh]
            o = _dot(att.astype(BF16), v_h) + _dot_nt(q_h[:, :B_KEY_DIM], st.astype(BF16))
            acc_ref[rows, hh * B_VAL_DIM:(hh + 1) * B_VAL_DIM] = o
            upd = _dot(v_h.T, k_st[:, base:base + B_KEY_DIM])
            st_ref[hh] = st * e_last[:, base:base + B_KEY_DIM] + upd
        return carry

    lax.fori_loop(0, ts // B_CHUNK, chunk, 0, unroll=4)

    r = p_ref[:, GLA_V0 + B_OUT:GLA_V0 + 2 * B_OUT]
    gate = r * (1.0 / (1.0 + jnp.exp(-r)))
    ng = ng_ref[...]
    for hh in range(B_HEADS):
        sl = slice(hh * B_VAL_DIM, (hh + 1) * B_VAL_DIM)
        o = acc_ref[:, sl]
        y = o * lax.rsqrt(jnp.mean(o * o, axis=-1, keepdims=True) + RMS_EPS) * ng
        o_ref[:, sl] = (gate[:, sl] * y).astype(BF16)


def _gla(h, w_b, wa, ba, norm_g, bsz, seq, ts=512):
    ns = seq // ts
    const = lambda shape: pl.BlockSpec(shape, lambda b, i: (0, 0))
    return pl.pallas_call(
        functools.partial(_gla_kernel, ts=ts),
        grid=(bsz, ns),
        in_specs=[pl.BlockSpec((ts, D_MODEL), lambda b, i: (b * ns + i, 0)),
                  const((D_MODEL, GLA_COLS)), const((LANES, GLA_QK)),
                  const((1, GLA_QK)), const((1, B_VAL_DIM))],
        out_specs=pl.BlockSpec((ts, B_OUT), lambda b, i: (b * ns + i, 0)),
        out_shape=jax.ShapeDtypeStruct((bsz * seq, B_OUT), BF16),
        scratch_shapes=[pltpu.VMEM((ts, GLA_COLS), F32),
                        pltpu.VMEM((ts, GLA_QK), F32),
                        pltpu.VMEM((ts, B_OUT), F32),
                        pltpu.VMEM((B_HEADS, B_VAL_DIM, B_KEY_DIM), F32)],
        compiler_params=_cparams(("parallel", "arbitrary")),
        name="gla",
    )(h, w_b, wa, ba, norm_g)


def _gmlp_kernel(h_ref, w_ref, lg_ref, lb_ref, ws_ref, bs_ref, o_ref, *, tm):
    z = _dot(h_ref[...].astype(BF16), w_ref[...])
    z = z * (0.5 * (1.0 + jnp.tanh(np.sqrt(2.0 / np.pi) * (z + 0.044715 * (z * z * z)))))
    u = z[:, :C_OUT]
    v = _layer_norm(z[:, C_OUT:], lg_ref[...], lb_ref[...]).astype(BF16)
    t_i = lax.broadcasted_iota(jnp.int32, (C_CHUNK, C_CHUNK), 0)
    s_i = lax.broadcasted_iota(jnp.int32, (C_CHUNK, C_CHUNK), 1)
    bs = bs_ref[...]
    for g in range(C_GROUPS):
        w = jnp.where(t_i >= s_i, ws_ref[g], 0.0).astype(BF16)
        sl = slice(g * C_GROUP_DIM, (g + 1) * C_GROUP_DIM)
        for c in range(tm // C_CHUNK):
            rows = slice(c * C_CHUNK, (c + 1) * C_CHUNK)
            mixed = _dot(w, v[rows, sl]) + bs[:, g:g + 1]
            o_ref[rows, sl] = (u[rows, sl] * mixed).astype(BF16)


def _gmlp(h, w_c, ln_g, ln_b, ws, bs_t, tm=512):
    n = h.shape[0]
    return pl.pallas_call(
        functools.partial(_gmlp_kernel, tm=tm),
        grid=(n // tm,),
        in_specs=[pl.BlockSpec((tm, D_MODEL), lambda i: (i, 0)),
                  pl.BlockSpec((D_MODEL, 2 * C_OUT), lambda i: (0, 0)),
                  pl.BlockSpec((1, C_OUT), lambda i: (0, 0)),
                  pl.BlockSpec((1, C_OUT), lambda i: (0, 0)),
                  pl.BlockSpec((C_GROUPS, C_CHUNK, C_CHUNK), lambda i: (0, 0, 0)),
                  pl.BlockSpec((C_CHUNK, LANES), lambda i: (0, 0))],
        out_specs=pl.BlockSpec((tm, C_OUT), lambda i: (i, 0)),
        out_shape=jax.ShapeDtypeStruct((n, C_OUT), BF16),
        compiler_params=_cparams(("parallel",)),
        name="gmlp",
    )(h, w_c, ln_g, ln_b, ws, bs_t)


def _sigmoid(x):
    return 1.0 / (1.0 + jnp.exp(-x))


def _merge_kernel(h_ref, ya_ref, yb_ref, yc_ref, wg_ref, wa_ref, wb_ref, wc_ref, wo_ref,
                  lg_ref, lb_ref, wr_ref, wrl_ref, br_ref, x_ref, oh_ref, route_ref):
    h = h_ref[...]
    hb = h.astype(BF16)
    acc = None
    for i, (y_ref, wbr_ref) in enumerate(((ya_ref, wa_ref), (yb_ref, wb_ref), (yc_ref, wc_ref))):
        gate = _sigmoid(_dot(hb, wg_ref[:, i * D_MODEL:(i + 1) * D_MODEL]))
        term = gate * _dot(y_ref[...], wbr_ref[...])
        acc = term if acc is None else acc + term
    mix = _dot(acc.astype(BF16), wo_ref[...])
    x1 = _layer_norm(DN_ALPHA * h + mix, lg_ref[...], lb_ref[...])
    x_ref[...] = x1

    x_hi = x1.astype(BF16)
    x_lo = (x1 - x_hi.astype(F32)).astype(BF16)
    logit = (_dot(x_hi, wr_ref[...]) + _dot(x_hi, wrl_ref[...]) + _dot(x_lo, wr_ref[...])) + br_ref[...]
    lane = lax.broadcasted_iota(jnp.int32, logit.shape, 1)
    big = jnp.int32(LANES)
    is_g = (lane >= N_EXPERTS) & (lane < N_EXPERTS + N_GROUPS)
    gl = jnp.where(is_g, logit, -jnp.inf)
    g_max = jnp.max(gl, axis=-1, keepdims=True)
    g_idx = jnp.min(jnp.where(is_g & (gl == g_max), lane, big), axis=-1, keepdims=True) - N_EXPERTS
    g_p = 1.0 / jnp.sum(jnp.where(is_g, jnp.exp(gl - g_max), 0.0), axis=-1, keepdims=True)
    lo = g_idx * EXPERTS_PER_GROUP
    in_g = (lane >= lo) & (lane < lo + EXPERTS_PER_GROUP)
    el = jnp.where(in_g, logit, -jnp.inf)
    e1 = jnp.max(el, axis=-1, keepdims=True)
    i1 = jnp.min(jnp.where(in_g & (el == e1), lane, big), axis=-1, keepdims=True)
    el2 = jnp.where(lane == i1, -jnp.inf, el)
    e2 = jnp.max(el2, axis=-1, keepdims=True)
    i2 = jnp.min(jnp.where(in_g & (lane != i1) & (el2 == e2), lane, big), axis=-1, keepdims=True)
    z2 = jnp.exp(e2 - e1)
    den = 1.0 + z2
    oh_ref[...] = jnp.where((lane == i1) | (lane == i2), 1.0, 0.0).astype(BF16)
    route_ref[...] = jnp.where(lane == 0, i1.astype(F32),
                               jnp.where(lane == 1, i2.astype(F32),
                                         jnp.where(lane == 2, (1.0 / den) * g_p,
                                                   jnp.where(lane == 3, (z2 / den) * g_p, 0.0))))


def _merge(h, ya, yb, yc, wg, wa, wb, wc, wo, ln_g, ln_b, wr, wr_lo, br, tm=1024):
    n = h.shape[0]
    row = lambda w: pl.BlockSpec((tm, w), lambda i: (i, 0))
    const = lambda r, c: pl.BlockSpec((r, c), lambda i: (0, 0), pipeline_mode=pl.Buffered(1))
    return pl.pallas_call(
        _merge_kernel,
        grid=(n // tm,),
        in_specs=[row(D_MODEL), row(A_OUT), row(B_OUT), row(C_OUT),
                  const(D_MODEL, 3 * D_MODEL), const(A_OUT, D_MODEL), const(B_OUT, D_MODEL),
                  const(C_OUT, D_MODEL), const(D_MODEL, D_MODEL),
                  const(1, D_MODEL), const(1, D_MODEL), const(D_MODEL, LANES), const(D_MODEL, LANES),
                  const(1, LANES)],
        out_specs=[row(D_MODEL), row(LANES), row(LANES)],
        out_shape=[jax.ShapeDtypeStruct((n, D_MODEL), F32),
                   jax.ShapeDtypeStruct((n, LANES), BF16),
                   jax.ShapeDtypeStruct((n, LANES), F32)],
        compiler_params=_cparams(("parallel",)),
        name="merge",
    )(h, ya, yb, yc, wg, wa, wb, wc, wo, ln_g, ln_b, wr, wr_lo, br)


FFN_TILE = 512
MOE_TOK = 512


def _rank_kernel(oh_ref, route_ref, r_ref, cnt_ref, carry_ref):
    @pl.when(pl.program_id(0) == 0)
    def _():
        carry_ref[...] = jnp.zeros_like(carry_ref)

    oh = oh_ref[...]
    tt = oh.shape[0]
    r_i = lax.broadcasted_iota(jnp.int32, (tt, tt), 0)
    c_i = lax.broadcasted_iota(jnp.int32, (tt, tt), 1)
    lower = jnp.where(c_i < r_i, 1.0, 0.0).astype(BF16)
    rank = _dot(lower, oh) + carry_ref[...]
    route = route_ref[...]
    lane = lax.broadcasted_iota(jnp.int32, rank.shape, 1)
    i1 = route[:, 0:1].astype(jnp.int32)
    i2 = route[:, 1:2].astype(jnp.int32)
    r1 = jnp.sum(jnp.where(lane == i1, rank, 0.0), axis=-1, keepdims=True)
    r2 = jnp.sum(jnp.where(lane == i2, rank, 0.0), axis=-1, keepdims=True)
    r_ref[...] = jnp.where(lane == 0, r1, jnp.where(lane == 1, r2, 0.0))
    carry_ref[...] += jnp.sum(oh.astype(F32), axis=0, keepdims=True)
    cnt_ref[...] = carry_ref[...]


def _rank(oh, route, tt=512):
    n = oh.shape[0]
    return pl.pallas_call(
        _rank_kernel,
        grid=(n // tt,),
        in_specs=[pl.BlockSpec((tt, LANES), lambda i: (i, 0)),
                  pl.BlockSpec((tt, LANES), lambda i: (i, 0))],
        out_specs=[pl.BlockSpec((tt, LANES), lambda i: (i, 0)),
                   pl.BlockSpec((1, LANES), lambda i: (0, 0))],
        out_shape=[jax.ShapeDtypeStruct((n, LANES), F32),
                   jax.ShapeDtypeStruct((1, LANES), F32)],
        scratch_shapes=[pltpu.VMEM((1, LANES), F32)],
        compiler_params=_cparams(("arbitrary",)),
        name="moe_rank",
    )(oh, route)


def _row_copy(src_ref, src_row, dst_ref, dst_row, sem):
    return pltpu.make_async_copy(src_ref.at[pl.ds(src_row, 1), :], dst_ref.at[pl.ds(dst_row, 1), :], sem)


def _dispatch_kernel(pos1_ref, pos2_ref, x_ref, init_ref, xs_ref, sem):
    del init_ref
    base = pl.program_id(0) * MOE_TOK

    def issue(t, carry):
        _row_copy(x_ref, t, xs_ref, pos1_ref[base + t], sem).start()
        _row_copy(x_ref, t, xs_ref, pos2_ref[base + t], sem).start()
        return carry

    lax.fori_loop(0, MOE_TOK, issue, 0, unroll=32)
    for _ in range(2):
        pltpu.make_async_copy(x_ref, xs_ref.at[pl.ds(0, MOE_TOK), :], sem).wait()


def _dispatch(pos1, pos2, x1, n_slots):
    n = x1.shape[0]
    init = jnp.zeros((n_slots, D_MODEL), F32)
    return pl.pallas_call(
        _dispatch_kernel,
        grid_spec=pltpu.PrefetchScalarGridSpec(
            num_scalar_prefetch=2,
            grid=(n // MOE_TOK,),
            in_specs=[pl.BlockSpec((MOE_TOK, D_MODEL), lambda i, p1, p2: (i, 0)),
                      pl.BlockSpec(memory_space=pl.ANY)],
            out_specs=pl.BlockSpec(memory_space=pl.ANY),
            scratch_shapes=[pltpu.SemaphoreType.DMA(())]),
        out_shape=jax.ShapeDtypeStruct((n_slots, D_MODEL), F32),
        input_output_aliases={3: 0},
        compiler_params=_cparams(("arbitrary",)),
        name="moe_dispatch",
    )(pos1, pos2, x1, init)


def _ffn_kernel(te_ref, nu_ref, xs_ref, wg_ref, wu_ref, wd_ref, ys_ref):
    del te_ref
    i = pl.program_id(0)

    @pl.when(i < nu_ref[0])
    def _():
        xb = xs_ref[...].astype(BF16)
        gte = _dot(xb, wg_ref[...].astype(BF16))
        hid = gte * _sigmoid(gte) * _dot(xb, wu_ref[...].astype(BF16))
        ys_ref[...] = _dot(hid.astype(BF16), wd_ref[...].astype(BF16))

    @pl.when(i >= nu_ref[0])
    def _():
        ys_ref[...] = jnp.zeros_like(ys_ref)


def _ffn(tile_expert, n_used, xs, w_gate, w_up, w_down, layer):
    n_slots = xs.shape[0]
    wspec = lambda r, c: pl.BlockSpec((None, None, r, c), lambda i, te, nu: (layer, te[i], 0, 0))
    return pl.pallas_call(
        _ffn_kernel,
        grid_spec=pltpu.PrefetchScalarGridSpec(
            num_scalar_prefetch=2,
            grid=(n_slots // FFN_TILE,),
            in_specs=[pl.BlockSpec((FFN_TILE, D_MODEL), lambda i, te, nu: (i, 0)),
                      wspec(D_MODEL, D_EXPERT), wspec(D_MODEL, D_EXPERT), wspec(D_EXPERT, D_MODEL)],
            out_specs=pl.BlockSpec((FFN_TILE, D_MODEL), lambda i, te, nu: (i, 0))),
        out_shape=jax.ShapeDtypeStruct((n_slots, D_MODEL), F32),
        compiler_params=_cparams(("arbitrary",)),
        name="moe_ffn",
    )(tile_expert, n_used, xs, w_gate, w_up, w_down)


def _combine_kernel(pos1_ref, pos2_ref, x_ref, route_ref, lg_ref, lb_ref, ys_ref, o_ref, buf_ref, sem):
    i = pl.program_id(0)

    def gather(step, slot):
        base = step * MOE_TOK

        def issue(t, carry):
            _row_copy(ys_ref, pos1_ref[base + t], buf_ref.at[slot, 0], t, sem.at[slot]).start()
            _row_copy(ys_ref, pos2_ref[base + t], buf_ref.at[slot, 1], t, sem.at[slot]).start()
            return carry

        lax.fori_loop(0, MOE_TOK, issue, 0, unroll=32)

    @pl.when(i == 0)
    def _():
        gather(0, 0)

    @pl.when(i + 1 < pl.num_programs(0))
    def _():
        gather(i + 1, (i + 1) % 2)

    slot = i % 2
    for s in range(2):
        pltpu.make_async_copy(ys_ref.at[pl.ds(0, MOE_TOK), :], buf_ref.at[slot, s], sem.at[slot]).wait()
    route = route_ref[...]
    ffn = route[:, 2:3] * buf_ref[slot, 0] + route[:, 3:4] * buf_ref[slot, 1]
    o_ref[...] = _layer_norm(DN_ALPHA * x_ref[...] + ffn, lg_ref[...], lb_ref[...])


def _combine(pos1, pos2, x1, route, ln_g, ln_b, ys):
    n = x1.shape[0]
    row = lambda w: pl.BlockSpec((MOE_TOK, w), lambda i, p1, p2: (i, 0))
    const = lambda w: pl.BlockSpec((1, w), lambda i, p1, p2: (0, 0))
    return pl.pallas_call(
        _combine_kernel,
        grid_spec=pltpu.PrefetchScalarGridSpec(
            num_scalar_prefetch=2,
            grid=(n // MOE_TOK,),
            in_specs=[row(D_MODEL), row(LANES), const(D_MODEL), const(D_MODEL),
                      pl.BlockSpec(memory_space=pl.ANY)],
            out_specs=row(D_MODEL),
            scratch_shapes=[pltpu.VMEM((2, 2, MOE_TOK, D_MODEL), F32),
                            pltpu.SemaphoreType.DMA((2,))]),
        out_shape=jax.ShapeDtypeStruct((n, D_MODEL), F32),
        compiler_params=_cparams(("arbitrary",)),
        name="moe_combine",
    )(pos1, pos2, x1, route, ln_g, ln_b, ys)


def _moe(x1, oh, route, w_gate, w_up, w_down, layer, ln_g, ln_b):
    n = x1.shape[0]
    n_slots = TOP_K_INNER * n + N_EXPERTS * FFN_TILE
    rank, cnt = _rank(oh, route)
    cnt = cnt[0, :N_EXPERTS].astype(jnp.int32)
    padded = (cnt + FFN_TILE - 1) // FFN_TILE * FFN_TILE
    ends = jnp.cumsum(padded)
    offs = ends - padded
    experts = jnp.arange(N_EXPERTS, dtype=jnp.int32)
    i1 = route[:, 0].astype(jnp.int32)
    i2 = route[:, 1].astype(jnp.int32)
    off_of = lambda idx: jnp.sum(jnp.where(idx[:, None] == experts[None, :], offs[None, :], 0), axis=1)
    pos1 = off_of(i1) + rank[:, 0].astype(jnp.int32)
    pos2 = off_of(i2) + rank[:, 1].astype(jnp.int32)
    tile_start = jnp.arange(n_slots // FFN_TILE, dtype=jnp.int32) * FFN_TILE
    tile_expert = jnp.minimum(jnp.sum(tile_start[:, None] >= ends[None, :], axis=1), N_EXPERTS - 1)
    n_used = (ends[-1] // FFN_TILE).reshape(1)

    xs = _dispatch(pos1, pos2, x1, n_slots)
    ys = _ffn(tile_expert.astype(jnp.int32), n_used.astype(jnp.int32), xs, w_gate, w_up, w_down, layer)
    return _combine(pos1, pos2, x1, route, ln_g, ln_b, ys)


def _rope_tables(positions):
    inv_freq = ROPE_THETA ** (-jnp.arange(0, ROT_DIM, 2, dtype=F32) / ROT_DIM)
    ang = positions.astype(F32).reshape(-1, 1) * inv_freq
    cos, sin = jnp.cos(ang), jnp.sin(ang)
    n = ang.shape[0]
    half = ROT_DIM // 2
    c64 = jnp.concatenate([cos, cos, jnp.ones((n, A_HEAD_DIM - ROT_DIM), F32)], axis=1)
    s64 = jnp.concatenate([-sin, sin, jnp.zeros((n, A_HEAD_DIM - ROT_DIM), F32)], axis=1)
    del half
    return jnp.tile(c64, (1, 2)), jnp.tile(s64, (1, 2))


def _split_w_in(w):
    widths = (A_OUT, A_HEAD_DIM, A_HEAD_DIM, IDX_HEADS * IDX_DIM, IDX_DIM, IDX_HEADS,
              B_HEADS * B_KEY_DIM, B_HEADS * B_KEY_DIM, B_OUT, B_GATE_RANK, B_OUT,
              2 * C_OUT, D_MODEL, D_MODEL, D_MODEL)
    pts = np.cumsum((0,) + widths)
    return [w[:, int(pts[i]):int(pts[i + 1])] for i in range(len(widths))]


def kernel(x, positions, ln_in_g, ln_in_b, w_in, idx_k_g, gla_wa2, gla_ba, gla_norm_g, gm_ln_g, gm_ln_b, gm_ws, gm_bs, w_branch_a, w_branch_b, w_branch_c, w_out, ln1_g, ln1_b, w_rg, b_rg, w_re, b_re, w_gate, w_up, w_down, ln2_g, ln2_b):
    bsz, seq, d = x.shape
    n = bsz * seq
    cos_t, sin_t = _rope_tables(positions)
    h = _entry_ln(x.reshape(n, d), ln_in_g, ln_in_b)
    for l in range(DEPTH):
        (a_q, a_k, a_v, i_q, i_k, i_w, b_q, b_k, b_v, b_glr, b_r, c_uv,
         g_a, g_b, g_c) = _split_w_in(w_in[l])
        zpad = lambda c: jnp.zeros((d, c), F32)
        w_a = jnp.concatenate([a_q, a_k, a_v, i_q, i_k, i_w, zpad(LANES - IDX_DIM - IDX_HEADS)],
                              axis=1).astype(BF16)
        w_b = jnp.concatenate([b_q * (B_KEY_DIM ** -0.5), b_k, b_v, b_r,
                               b_glr, zpad(LANES - B_GATE_RANK)], axis=1).astype(BF16)
        wa_pad = jnp.concatenate([gla_wa2[l],
                                  jnp.zeros((LANES - B_GATE_RANK, GLA_QK), F32)], axis=0).astype(BF16)
        w_g = jnp.concatenate([g_a, g_b, g_c], axis=1).astype(BF16)
        ikg = jnp.concatenate([idx_k_g[l], jnp.zeros((LANES - IDX_DIM,), F32)]).reshape(1, LANES)
        bs_t = jnp.concatenate([gm_bs[l].T, jnp.zeros((C_CHUNK, LANES - C_GROUPS), F32)], axis=1)
        w_r = jnp.concatenate([w_re[l], w_rg[l], zpad(LANES - N_GROUPS - N_EXPERTS)], axis=1)
        w_r_hi = w_r.astype(BF16)
        w_r_lo = (w_r - w_r_hi.astype(F32)).astype(BF16)
        b_r_all = jnp.concatenate([b_re[l], b_rg[l],
                                   jnp.zeros((LANES - N_GROUPS - N_EXPERTS,), F32)]).reshape(1, LANES)

        q, k, v, iq, ki, wi = _proj_a(h, w_a, cos_t, sin_t, ikg)
        y_a = _dsa(q, iq, wi, k, v, ki, bsz, seq)
        y_b = _gla(h, w_b, wa_pad, gla_ba[l].reshape(1, -1), gla_norm_g[l].reshape(1, -1), bsz, seq)
        y_c = _gmlp(h, c_uv.astype(BF16), gm_ln_g[l].reshape(1, -1), gm_ln_b[l].reshape(1, -1),
                    gm_ws[l], bs_t)
        x1, oh, route = _merge(h, y_a, y_b, y_c, w_g,
                               w_branch_a[l].astype(BF16), w_branch_b[l].astype(BF16),
                               w_branch_c[l].astype(BF16), w_out[l].astype(BF16),
                               ln1_g[l].reshape(1, -1), ln1_b[l].reshape(1, -1), w_r_hi, w_r_lo, b_r_all)
        h = _moe(x1, oh, route, w_gate, w_up, w_down, l,
                 ln2_g[l].reshape(1, -1), ln2_b[l].reshape(1, -1))
    return h.reshape(bsz, seq, d)
```

```python
import functools

import numpy as np
import jax
import jax.numpy as jnp
from jax import lax
from jax.experimental import pallas as pl
from jax.experimental.pallas import tpu as pltpu

F32 = jnp.float32
BF16 = jnp.bfloat16

D_MODEL = 1024
DEPTH = 2
A_HEADS = 8
A_HEAD_DIM = 64
A_OUT = 512
IDX_HEADS = 4
IDX_DIM = 64
TOPK_MAX = 256
Q_BLOCK = 128
ROPE_THETA = 500000.0
ROT_DIM = 16
B_HEADS = 4
B_KEY_DIM = 64
B_VAL_DIM = 128
B_OUT = 512
B_GATE_RANK = 16
B_GATE_TAU = 16.0
B_CHUNK = 64
B_SUB = 16
N_SUB = B_CHUNK // B_SUB
C_GROUPS = 4
C_GROUP_DIM = 128
C_OUT = 512
C_CHUNK = 128
N_GROUPS = 4
EXPERTS_PER_GROUP = 8
N_EXPERTS = 32
D_EXPERT = 256
TOP_K_INNER = 2
DN_ALPHA = (2 * DEPTH) ** 0.25
LN_EPS = 1e-5
RMS_EPS = 1e-6

LANES = 128
NEG_BIG = -1e30
LOG2_E = 1.4426950408889634
INT_MIN = -2 ** 31
VMEM_LIMIT = 56 * 1024 * 1024


def _cparams(sem):
    return pltpu.CompilerParams(dimension_semantics=sem, vmem_limit_bytes=VMEM_LIMIT)


def _layer_norm(x, g, b):
    mu = jnp.mean(x, axis=-1, keepdims=True)
    xc = x - mu
    var = jnp.mean(xc * xc, axis=-1, keepdims=True)
    return xc * lax.rsqrt(var + LN_EPS) * g + b


def _dot(a, b):
    return jnp.dot(a, b, preferred_element_type=F32)


def _dot_nt(a, b):
    return lax.dot_general(a, b, (((1,), (1,)), ((), ())), preferred_element_type=F32)


def _ln_kernel(x_ref, g_ref, b_ref, o_ref):
    o_ref[...] = _layer_norm(x_ref[...], g_ref[...], b_ref[...])


def _entry_ln(x, g, b, tm=1024):
    n = x.shape[0]
    return pl.pallas_call(
        _ln_kernel,
        grid=(n // tm,),
        in_specs=[pl.BlockSpec((tm, D_MODEL), lambda i: (i, 0)),
                  pl.BlockSpec((1, D_MODEL), lambda i: (0, 0)),
                  pl.BlockSpec((1, D_MODEL), lambda i: (0, 0))],
        out_specs=pl.BlockSpec((tm, D_MODEL), lambda i: (i, 0)),
        out_shape=jax.ShapeDtypeStruct((n, D_MODEL), F32),
        compiler_params=_cparams(("parallel",)),
        name="entry_ln",
    )(x, g.reshape(1, -1), b.reshape(1, -1))


def _rope_slab(xs, c, s):
    lane = lax.broadcasted_iota(jnp.int32, xs.shape, 1) % A_HEAD_DIM
    fwd = pltpu.roll(xs, LANES - ROT_DIM // 2, axis=1)
    bwd = pltpu.roll(xs, ROT_DIM // 2, axis=1)
    partner = jnp.where(lane < ROT_DIM // 2, fwd, bwd)
    return xs * c + partner * s


def _proj_a_kernel(h_ref, w_ref, c_ref, s_ref, g_ref,
                   q_ref, k_ref, v_ref, iq_ref, ki_ref, wi_ref):
    p = _dot(h_ref[...].astype(BF16), w_ref[...])
    c = c_ref[...]
    s = s_ref[...]
    lane = lax.broadcasted_iota(jnp.int32, c.shape, 1)
    lo = lane < A_HEAD_DIM
    n_blk = p.shape[0] // Q_BLOCK

    def put_heads(dst_ref, slab, first_head):
        for half in range(2):
            hd = first_head + half
            cols = slab[:, half * A_HEAD_DIM:(half + 1) * A_HEAD_DIM].astype(BF16)
            for qb in range(n_blk):
                dst_ref[qb, hd * Q_BLOCK:(hd + 1) * Q_BLOCK, :] = cols[qb * Q_BLOCK:(qb + 1) * Q_BLOCK, :]

    for i in range(4):
        sl = slice(i * LANES, (i + 1) * LANES)
        put_heads(q_ref, _rope_slab(p[:, sl], c, s) * (A_HEAD_DIM ** -0.5 * LOG2_E), 2 * i)
    kv = p[:, 512:640]
    kv = _rope_slab(kv, jnp.where(lo, c, 1.0), jnp.where(lo, s, 0.0))
    k_ref[...] = kv[:, :A_HEAD_DIM].astype(BF16)
    v_ones = jnp.where(lo, pltpu.roll(kv, A_HEAD_DIM, axis=1), jnp.where(lane == A_HEAD_DIM, 1.0, 0.0))
    v_ref[...] = v_ones.astype(BF16)
    for i in range(2):
        sl = slice(640 + i * LANES, 640 + (i + 1) * LANES)
        put_heads(iq_ref, _rope_slab(p[:, sl], c, s), 2 * i)
    last = p[:, 896:1024]
    mu = jnp.sum(jnp.where(lo, last, 0.0), axis=-1, keepdims=True) * (1.0 / IDX_DIM)
    xc = jnp.where(lo, last - mu, 0.0)
    var = jnp.sum(xc * xc, axis=-1, keepdims=True) * (1.0 / IDX_DIM)
    kin = xc * lax.rsqrt(var + LN_EPS) * g_ref[...]
    kin = _rope_slab(kin, jnp.where(lo, c, 1.0), jnp.where(lo, s, 0.0))
    ki_ref[...] = kin[:, :IDX_DIM].astype(BF16)
    wi = pltpu.roll(last, LANES - IDX_DIM, axis=1) * (IDX_HEADS ** -0.5 * IDX_DIM ** -0.5)
    for qb in range(n_blk):
        wi_ref[qb] = wi[qb * Q_BLOCK:(qb + 1) * Q_BLOCK, :].T[0:8, :]


def _proj_a(h, w_a, cos_t, sin_t, ikg, tm=512):
    n = h.shape[0]
    row = lambda w: pl.BlockSpec((tm, w), lambda i: (i, 0))
    stacked = lambda heads: pl.BlockSpec((tm // Q_BLOCK, heads * Q_BLOCK, A_HEAD_DIM), lambda i: (i, 0, 0))
    return pl.pallas_call(
        _proj_a_kernel,
        grid=(n // tm,),
        in_specs=[row(D_MODEL),
                  pl.BlockSpec((D_MODEL, 1024), lambda i: (0, 0)),
                  row(LANES), row(LANES),
                  pl.BlockSpec((1, LANES), lambda i: (0, 0))],
        out_specs=[stacked(A_HEADS), row(A_HEAD_DIM),
                   row(LANES),
                   stacked(IDX_HEADS), row(IDX_DIM),
                   pl.BlockSpec((tm // Q_BLOCK, 8, Q_BLOCK), lambda i: (i, 0, 0))],
        out_shape=[jax.ShapeDtypeStruct((n // Q_BLOCK, A_HEADS * Q_BLOCK, A_HEAD_DIM), BF16),
                   jax.ShapeDtypeStruct((n, A_HEAD_DIM), BF16),
                   jax.ShapeDtypeStruct((n, LANES), BF16),
                   jax.ShapeDtypeStruct((n // Q_BLOCK, IDX_HEADS * Q_BLOCK, IDX_DIM), BF16),
                   jax.ShapeDtypeStruct((n, IDX_DIM), BF16),
                   jax.ShapeDtypeStruct((n // Q_BLOCK, 8, Q_BLOCK), F32)],
        compiler_params=_cparams(("parallel",)),
        name="proj_a",
    )(h, w_a, cos_t, sin_t, ikg)


KEY_CHUNK = 512
FIELD_BITS = 15
LOW_BITS = 32 - 2 * FIELD_BITS
FIELD_MASK = (1 << FIELD_BITS) - 1
GUARD_BITS = -2147450880
PAIR_ONE = 0x00010001


def _dsa_kernel(q_ref, iq_ref, wi_ref, k_ref, v_ref, ki_ref, o_ref,
                key_ref, pk_ref, bias_ref, *, seq):
    j = pl.program_id(1)
    n_sel = min(TOPK_MAX, seq // 4)
    n_kc = (j * Q_BLOCK + Q_BLOCK + KEY_CHUNK - 1) // KEY_CHUNK
    qpos = j * Q_BLOCK + lax.broadcasted_iota(jnp.int32, (1, Q_BLOCK), 1)
    kk = jnp.minimum(n_sel, qpos + 1).astype(F32)
    row_pos = lax.broadcasted_iota(jnp.int32, (KEY_CHUNK, 1), 0)

    def key_rows(c):
        return pl.ds(pl.multiple_of(c * KEY_CHUNK, KEY_CHUNK), KEY_CHUNK)

    iq = iq_ref[...]
    wi = wi_ref[...]

    def score_chunk(c, carry):
        d = jnp.maximum(_dot_nt(ki_ref[key_rows(c), :], iq), 0.0)
        score = wi[0:1, :] * d[:, 0:Q_BLOCK]
        for h in range(1, IDX_HEADS):
            score = score + wi[h:h + 1, :] * d[:, h * Q_BLOCK:(h + 1) * Q_BLOCK]
        bits = pltpu.bitcast(score + 0.0, jnp.int32)
        key = bits ^ ((bits >> 31) & 0x7FFFFFFF)
        key = jnp.where(c * KEY_CHUNK + row_pos <= qpos, key, INT_MIN)
        key_ref[c] = key
        pk_ref[c] = pack(lax.shift_right_logical(key ^ INT_MIN, FIELD_BITS + LOW_BITS))
        return carry

    def pack(f):
        hf = KEY_CHUNK // 2
        return (f[:hf] << 16) | f[hf:] | GUARD_BITS

    lax.fori_loop(0, n_kc, score_chunk, 0)

    def count(pred):
        def body(c, acc):
            m = jnp.where(pred(key_ref[c]), 1.0, 0.0)
            return acc + jnp.sum(m.reshape(KEY_CHUNK // 64, 64, Q_BLOCK), axis=0)
        acc = lax.fori_loop(0, n_kc, body, jnp.zeros((64, Q_BLOCK), F32))
        return jnp.sum(acc, axis=0, keepdims=True)

    def count_packed(cand):
        both = (cand << 16) | cand

        def body(c, acc):
            flags = ((pk_ref[c] - both) >> 15) & PAIR_ONE
            parts = [flags[t * 64:(t + 1) * 64] for t in range(KEY_CHUNK // 128)]
            while len(parts) > 1:
                parts = [a + b for a, b in zip(parts[::2], parts[1::2])]
            return acc + parts[0]

        acc = lax.fori_loop(0, n_kc, body, jnp.zeros((64, Q_BLOCK), jnp.int32))
        return jnp.sum(((acc & 0xFFFF) + (acc >> 16)).astype(F32), axis=0, keepdims=True)

    def search_field(target):
        def step(i, t):
            cand = t | jnp.left_shift(jnp.int32(1), FIELD_BITS - 1 - i)
            return jnp.where(count_packed(cand) >= target, cand, t)
        return lax.fori_loop(0, FIELD_BITS, step, jnp.zeros((1, Q_BLOCK), jnp.int32))

    t_a = search_field(kk)
    top_of_a = ((t_a << (FIELD_BITS + LOW_BITS)) | ((1 << (FIELD_BITS + LOW_BITS)) - 1)) ^ INT_MIN
    need_b = kk - count(lambda kc: kc > top_of_a)

    def repack(c, carry):
        u = key_ref[c] ^ INT_MIN
        member = lax.shift_right_logical(u, FIELD_BITS + LOW_BITS) == t_a
        pk_ref[c] = pack(jnp.where(member, lax.shift_right_logical(u, LOW_BITS) & FIELD_MASK, 0))
        return carry

    lax.fori_loop(0, n_kc, repack, 0)
    base = (t_a << (FIELD_BITS + LOW_BITS)) | (search_field(need_b) << LOW_BITS)

    def low_step(i, t_c):
        bit = jnp.left_shift(jnp.int32(1), LOW_BITS - 1 - i)
        cand_s = (base | t_c | bit) ^ INT_MIN
        return jnp.where(count(lambda kc: kc >= cand_s) >= kk, t_c | bit, t_c)

    thr = (base | lax.fori_loop(0, LOW_BITS, low_step, jnp.zeros((1, Q_BLOCK), jnp.int32))) ^ INT_MIN
    need = kk - count(lambda kc: kc > thr)
    r_i = lax.broadcasted_iota(jnp.int32, (LANES, LANES), 0)
    c_i = lax.broadcasted_iota(jnp.int32, (LANES, LANES), 1)
    lower = jnp.where(c_i < r_i, 1.0, 0.0).astype(BF16)

    def fill(c, carry):
        for t in range(KEY_CHUNK // LANES):
            rows = slice(t * LANES, (t + 1) * LANES)
            kt = key_ref[c, rows, :]
            eq = kt == thr
            eq_f = jnp.where(eq, 1.0, 0.0)
            pre = _dot(lower, eq_f.astype(BF16)) + carry
            sel = (kt > thr) | (eq & (pre < need))
            bias_ref[c, rows, :] = jnp.where(sel, 0.0, NEG_BIG)
            carry = carry + jnp.sum(eq_f, axis=0, keepdims=True)
        return carry

    lax.fori_loop(0, n_kc, fill, jnp.zeros((1, Q_BLOCK), F32))

    q = q_ref[...]
    cols = A_HEADS * Q_BLOCK

    def attn_chunk(c, carry):
        m, acc = carry
        rows = key_rows(c)
        lg = _dot_nt(k_ref[rows, :], q) + jnp.concatenate([bias_ref[c]] * A_HEADS, axis=1)
        m_new = jnp.maximum(m, jnp.max(lg, axis=0, keepdims=True))
        p = jnp.exp2(lg - m_new).astype(BF16)
        pv = lax.dot_general(v_ref[rows, :], p, (((0,), (0,)), ((), ())),
                             preferred_element_type=F32)
        return m_new, jnp.exp2(m - m_new) * acc + pv

    init = (jnp.full((1, cols), -3e38, F32), jnp.zeros((LANES, cols), F32))
    _, acc = lax.fori_loop(0, n_kc, attn_chunk, init)
    o = acc[:A_HEAD_DIM] / acc[A_HEAD_DIM:A_HEAD_DIM + 1]
    for h in range(A_HEADS):
        o_ref[:, h * A_HEAD_DIM:(h + 1) * A_HEAD_DIM] = (
            o[:, h * Q_BLOCK:(h + 1) * Q_BLOCK].T.astype(BF16))


def _dsa(q, iq, wi, k, v, ki, bsz, seq):
    nq = seq // Q_BLOCK
    n_kc = seq // KEY_CHUNK
    stacked = lambda heads: pl.BlockSpec((None, heads * Q_BLOCK, A_HEAD_DIM),
                                         lambda b, j: (b * nq + j, 0, 0))
    kspec = lambda w: pl.BlockSpec((seq, w), lambda b, j: (b, 0))
    return pl.pallas_call(
        functools.partial(_dsa_kernel, seq=seq),
        grid=(bsz, nq),
        in_specs=[stacked(A_HEADS), stacked(IDX_HEADS),
                  pl.BlockSpec((None, 8, Q_BLOCK), lambda b, j: (b * nq + j, 0, 0)),
                  kspec(A_HEAD_DIM),
                  kspec(LANES),
                  kspec(IDX_DIM)],
        out_specs=pl.BlockSpec((Q_BLOCK, A_OUT), lambda b, j: (b * nq + j, 0)),
        out_shape=jax.ShapeDtypeStruct((bsz * seq, A_OUT), BF16),
        scratch_shapes=[pltpu.VMEM((n_kc, KEY_CHUNK, Q_BLOCK), jnp.int32),
                        pltpu.VMEM((n_kc, KEY_CHUNK // 2, Q_BLOCK), jnp.int32),
                        pltpu.VMEM((n_kc, KEY_CHUNK, Q_BLOCK), F32)],
        compiler_params=_cparams(("parallel", "parallel")),
        name="dsa",
    )(q, iq, wi, k, v, ki)


GLA_QK = B_HEADS * B_KEY_DIM
GLA_REP = N_SUB * GLA_QK
GLA_COLS = 2 * GLA_QK + 2 * B_OUT + LANES
GLA_V0 = 2 * GLA_QK


def _gla_kernel(h_ref, w_ref, wa_ref, ba_ref, ng_ref, o_ref, p_ref, g_ref, acc_ref, st_ref, *, ts):
    @pl.when(pl.program_id(1) == 0)
    def _():
        st_ref[...] = jnp.zeros_like(st_ref)

    p_ref[...] = _dot(h_ref[...].astype(BF16), w_ref[...])
    x = _dot(p_ref[:, GLA_V0 + 2 * B_OUT:].astype(BF16), wa_ref[...]) + ba_ref[...]
    g_ref[...] = (jnp.minimum(x, 0.0) - jnp.log1p(jnp.exp(-jnp.abs(x)))) * (1.0 / B_GATE_TAU)

    t_i = lax.broadcasted_iota(jnp.int32, (B_CHUNK, B_CHUNK), 0)
    s_i = lax.broadcasted_iota(jnp.int32, (B_CHUNK, B_CHUNK), 1)
    tri = t_i >= s_i
    tri_b = jnp.where(tri, 1.0, 0.0).astype(BF16)
    lane_sub = (lax.broadcasted_iota(jnp.int32, (1, GLA_REP), 1) // B_KEY_DIM) % N_SUB
    row_sub = lax.broadcasted_iota(jnp.int32, (B_CHUNK, 1), 0) // B_SUB
    q_mask = lane_sub <= row_sub
    k_mask = lane_sub == row_sub
    low_half = lax.broadcasted_iota(jnp.int32, (B_CHUNK, LANES), 1) < B_KEY_DIM

    def replicate(x):
        outs = []
        for pair in range(GLA_QK // LANES):
            v = x[:, pair * LANES:(pair + 1) * LANES]
            r = pltpu.roll(v, B_KEY_DIM, axis=1)
            outs += [jnp.where(low_half, v, r)] * (N_SUB // 2) + [jnp.where(low_half, r, v)] * (N_SUB // 2)
        return jnp.concatenate(outs, axis=1)

    def chunk(c, carry):
        rows = pl.ds(pl.multiple_of(c * B_CHUNK, B_CHUNK), B_CHUNK)
        g = g_ref[rows, :]
        g_hi = g.astype(BF16)
        g_mid = (g - g_hi.astype(F32)).astype(BF16)
        g_lo = (g - g_hi.astype(F32) - g_mid.astype(F32)).astype(BF16)
        b = replicate(_dot(tri_b, g_hi) + _dot(tri_b, g_mid) + _dot(tri_b, g_lo))
        ref = jnp.zeros((1, GLA_REP), F32)
        for jj in range(1, N_SUB):
            ref = ref + jnp.where(lane_sub == jj, b[jj * B_SUB - 1:jj * B_SUB, :], 0.0)
        d = b - ref
        e_q = jnp.where(q_mask, jnp.exp(jnp.minimum(d, 0.0)), 0.0)
        e_k = jnp.where(k_mask, jnp.exp(jnp.where(k_mask, -d, 0.0)), 0.0)
        b_last = b[B_CHUNK - 1:B_CHUNK, :]
        q_c = (replicate(p_ref[rows, 0:GLA_QK]) * e_q).astype(BF16)
        k_raw = replicate(p_ref[rows, GLA_QK:2 * GLA_QK])
        k_c = (k_raw * e_k).astype(BF16)
        k_st = (k_raw * jnp.exp(b_last - b)).astype(BF16)
        e_last = jnp.exp(b_last)
        for hh in range(B_HEADS):
            base = hh * N_SUB * B_KEY_DIM
            q_h = q_c[:, base:base + N_SUB * B_KEY_DIM]
            k_h = k_c[:, base:base + N_SUB * B_KEY_DIM]
            att = jnp.where(tri, _dot_nt(q_h, k_h), 0.0)
            v_h = p_ref[rows, GLA_V0 + hh * B_VAL_DIM:GLA_V0 + (hh + 1) * B_VAL_DIM].astype(BF16)
            st = st_ref[hh]
            o = _dot(att.astype(BF16), v_h) + _dot_nt(q_h[:, :B_KEY_DIM], st.astype(BF16))
            acc_ref[rows, hh * B_VAL_DIM:(hh + 1) * B_VAL_DIM] = o
            upd = _dot(v_h.T, k_st[:, base:base + B_KEY_DIM])
            st_ref[hh] = st * e_last[:, base:base + B_KEY_DIM] + upd
        return carry

    lax.fori_loop(0, ts // B_CHUNK, chunk, 0, unroll=4)

    r = p_ref[:, GLA_V0 + B_OUT:GLA_V0 + 2 * B_OUT]
    gate = r * (1.0 / (1.0 + jnp.exp(-r)))
    ng = ng_ref[...]
    for hh in range(B_HEADS):
        sl = slice(hh * B_VAL_DIM, (hh + 1) * B_VAL_DIM)
        o = acc_ref[:, sl]
        y = o * lax.rsqrt(jnp.mean(o * o, axis=-1, keepdims=True) + RMS_EPS) * ng
        o_ref[:, sl] = (gate[:, sl] * y).astype(BF16)


def _gla(h, w_b, wa, ba, norm_g, bsz, seq, ts=512):
    ns = seq // ts
    const = lambda shape: pl.BlockSpec(shape, lambda b, i: (0, 0))
    return pl.pallas_call(
        functools.partial(_gla_kernel, ts=ts),
        grid=(bsz, ns),
        in_specs=[pl.BlockSpec((ts, D_MODEL), lambda b, i: (b * ns + i, 0)),
                  const((D_MODEL, GLA_COLS)), const((LANES, GLA_QK)),
                  const((1, GLA_QK)), const((1, B_VAL_DIM))],
        out_specs=pl.BlockSpec((ts, B_OUT), lambda b, i: (b * ns + i, 0)),
        out_shape=jax.ShapeDtypeStruct((bsz * seq, B_OUT), BF16),
        scratch_shapes=[pltpu.VMEM((ts, GLA_COLS), F32),
                        pltpu.VMEM((ts, GLA_QK), F32),
                        pltpu.VMEM((ts, B_OUT), F32),
                        pltpu.VMEM((B_HEADS, B_VAL_DIM, B_KEY_DIM), F32)],
        compiler_params=_cparams(("parallel", "arbitrary")),
        name="gla",
    )(h, w_b, wa, ba, norm_g)


def _gmlp_kernel(h_ref, w_ref, lg_ref, lb_ref, ws_ref, bs_ref, o_ref, *, tm):
    z = _dot(h_ref[...].astype(BF16), w_ref[...])
    z = z * (0.5 * (1.0 + jnp.tanh(np.sqrt(2.0 / np.pi) * (z + 0.044715 * (z * z * z)))))
    u = z[:, :C_OUT]
    v = _layer_norm(z[:, C_OUT:], lg_ref[...], lb_ref[...]).astype(BF16)
    t_i = lax.broadcasted_iota(jnp.int32, (C_CHUNK, C_CHUNK), 0)
    s_i = lax.broadcasted_iota(jnp.int32, (C_CHUNK, C_CHUNK), 1)
    bs = bs_ref[...]
    for g in range(C_GROUPS):
        w = jnp.where(t_i >= s_i, ws_ref[g], 0.0).astype(BF16)
        sl = slice(g * C_GROUP_DIM, (g + 1) * C_GROUP_DIM)
        for c in range(tm // C_CHUNK):
            rows = slice(c * C_CHUNK, (c + 1) * C_CHUNK)
            mixed = _dot(w, v[rows, sl]) + bs[:, g:g + 1]
            o_ref[rows, sl] = (u[rows, sl] * mixed).astype(BF16)


def _gmlp(h, w_c, ln_g, ln_b, ws, bs_t, tm=512):
    n = h.shape[0]
    return pl.pallas_call(
        functools.partial(_gmlp_kernel, tm=tm),
        grid=(n // tm,),
        in_specs=[pl.BlockSpec((tm, D_MODEL), lambda i: (i, 0)),
                  pl.BlockSpec((D_MODEL, 2 * C_OUT), lambda i: (0, 0)),
                  pl.BlockSpec((1, C_OUT), lambda i: (0, 0)),
                  pl.BlockSpec((1, C_OUT), lambda i: (0, 0)),
                  pl.BlockSpec((C_GROUPS, C_CHUNK, C_CHUNK), lambda i: (0, 0, 0)),
                  pl.BlockSpec((C_CHUNK, LANES), lambda i: (0, 0))],
        out_specs=pl.BlockSpec((tm, C_OUT), lambda i: (i, 0)),
        out_shape=jax.ShapeDtypeStruct((n, C_OUT), BF16),
        compiler_params=_cparams(("parallel",)),
        name="gmlp",
    )(h, w_c, ln_g, ln_b, ws, bs_t)


def _sigmoid(x):
    return 1.0 / (1.0 + jnp.exp(-x))


def _merge_kernel(h_ref, ya_ref, yb_ref, yc_ref, wg_ref, wa_ref, wb_ref, wc_ref, wo_ref,
                  lg_ref, lb_ref, wr_ref, wrl_ref, br_ref, x_ref, oh_ref, route_ref):
    h = h_ref[...]
    hb = h.astype(BF16)
    acc = None
    for i, (y_ref, wbr_ref) in enumerate(((ya_ref, wa_ref), (yb_ref, wb_ref), (yc_ref, wc_ref))):
        gate = _sigmoid(_dot(hb, wg_ref[:, i * D_MODEL:(i + 1) * D_MODEL]))
        term = gate * _dot(y_ref[...], wbr_ref[...])
        acc = term if acc is None else acc + term
    mix = _dot(acc.astype(BF16), wo_ref[...])
    x1 = _layer_norm(DN_ALPHA * h + mix, lg_ref[...], lb_ref[...])
    x_ref[...] = x1

    x_hi = x1.astype(BF16)
    x_lo = (x1 - x_hi.astype(F32)).astype(BF16)
    logit = (_dot(x_hi, wr_ref[...]) + _dot(x_hi, wrl_ref[...]) + _dot(x_lo, wr_ref[...])) + br_ref[...]
    lane = lax.broadcasted_iota(jnp.int32, logit.shape, 1)
    big = jnp.int32(LANES)
    is_g = (lane >= N_EXPERTS) & (lane < N_EXPERTS + N_GROUPS)
    gl = jnp.where(is_g, logit, -jnp.inf)
    g_max = jnp.max(gl, axis=-1, keepdims=True)
    g_idx = jnp.min(jnp.where(is_g & (gl == g_max), lane, big), axis=-1, keepdims=True) - N_EXPERTS
    g_p = 1.0 / jnp.sum(jnp.where(is_g, jnp.exp(gl - g_max), 0.0), axis=-1, keepdims=True)
    lo = g_idx * EXPERTS_PER_GROUP
    in_g = (lane >= lo) & (lane < lo + EXPERTS_PER_GROUP)
    el = jnp.where(in_g, logit, -jnp.inf)
    e1 = jnp.max(el, axis=-1, keepdims=True)
    i1 = jnp.min(jnp.where(in_g & (el == e1), lane, big), axis=-1, keepdims=True)
    el2 = jnp.where(lane == i1, -jnp.inf, el)
    e2 = jnp.max(el2, axis=-1, keepdims=True)
    i2 = jnp.min(jnp.where(in_g & (lane != i1) & (el2 == e2), lane, big), axis=-1, keepdims=True)
    z2 = jnp.exp(e2 - e1)
    den = 1.0 + z2
    oh_ref[...] = jnp.where((lane == i1) | (lane == i2), 1.0, 0.0).astype(BF16)
    route_ref[...] = jnp.where(lane == 0, i1.astype(F32),
                               jnp.where(lane == 1, i2.astype(F32),
                                         jnp.where(lane == 2, (1.0 / den) * g_p,
                                                   jnp.where(lane == 3, (z2 / den) * g_p, 0.0))))


def _merge(h, ya, yb, yc, wg, wa, wb, wc, wo, ln_g, ln_b, wr, wr_lo, br, tm=1024):
    n = h.shape[0]
    row = lambda w: pl.BlockSpec((tm, w), lambda i: (i, 0))
    const = lambda r, c: pl.BlockSpec((r, c), lambda i: (0, 0), pipeline_mode=pl.Buffered(1))
    return pl.pallas_call(
        _merge_kernel,
        grid=(n // tm,),
        in_specs=[row(D_MODEL), row(A_OUT), row(B_OUT), row(C_OUT),
                  const(D_MODEL, 3 * D_MODEL), const(A_OUT, D_MODEL), const(B_OUT, D_MODEL),
                  const(C_OUT, D_MODEL), const(D_MODEL, D_MODEL),
                  const(1, D_MODEL), const(1, D_MODEL), const(D_MODEL, LANES), const(D_MODEL, LANES),
                  const(1, LANES)],
        out_specs=[row(D_MODEL), row(LANES), row(LANES)],
        out_shape=[jax.ShapeDtypeStruct((n, D_MODEL), F32),
                   jax.ShapeDtypeStruct((n, LANES), BF16),
                   jax.ShapeDtypeStruct((n, LANES), F32)],
        compiler_params=_cparams(("parallel",)),
        name="merge",
    )(h, ya, yb, yc, wg, wa, wb, wc, wo, ln_g, ln_b, wr, wr_lo, br)


FFN_TILE = 512
MOE_TOK = 512


def _rank_kernel(oh_ref, route_ref, r_ref, cnt_ref, carry_ref):
    @pl.when(pl.program_id(0) == 0)
    def _():
        carry_ref[...] = jnp.zeros_like(carry_ref)

    oh = oh_ref[...]
    tt = oh.shape[0]
    r_i = lax.broadcasted_iota(jnp.int32, (tt, tt), 0)
    c_i = lax.broadcasted_iota(jnp.int32, (tt, tt), 1)
    lower = jnp.where(c_i < r_i, 1.0, 0.0).astype(BF16)
    rank = _dot(lower, oh) + carry_ref[...]
    route = route_ref[...]
    lane = lax.broadcasted_iota(jnp.int32, rank.shape, 1)
    i1 = route[:, 0:1].astype(jnp.int32)
    i2 = route[:, 1:2].astype(jnp.int32)
    r1 = jnp.sum(jnp.where(lane == i1, rank, 0.0), axis=-1, keepdims=True)
    r2 = jnp.sum(jnp.where(lane == i2, rank, 0.0), axis=-1, keepdims=True)
    r_ref[...] = jnp.where(lane == 0, r1, jnp.where(lane == 1, r2, 0.0))
    carry_ref[...] += jnp.sum(oh.astype(F32), axis=0, keepdims=True)
    cnt_ref[...] = carry_ref[...]


def _rank(oh, route, tt=512):
    n = oh.shape[0]
    return pl.pallas_call(
        _rank_kernel,
        grid=(n // tt,),
        in_specs=[pl.BlockSpec((tt, LANES), lambda i: (i, 0)),
                  pl.BlockSpec((tt, LANES), lambda i: (i, 0))],
        out_specs=[pl.BlockSpec((tt, LANES), lambda i: (i, 0)),
                   pl.BlockSpec((1, LANES), lambda i: (0, 0))],
        out_shape=[jax.ShapeDtypeStruct((n, LANES), F32),
                   jax.ShapeDtypeStruct((1, LANES), F32)],
        scratch_shapes=[pltpu.VMEM((1, LANES), F32)],
        compiler_params=_cparams(("arbitrary",)),
        name="moe_rank",
    )(oh, route)


def _row_copy(src_ref, src_row, dst_ref, dst_row, sem):
    return pltpu.make_async_copy(src_ref.at[pl.ds(src_row, 1), :], dst_ref.at[pl.ds(dst_row, 1), :], sem)


def _dispatch_kernel(pos1_ref, pos2_ref, x_ref, init_ref, xs_ref, sem):
    del init_ref
    base = pl.program_id(0) * MOE_TOK

    def issue(t, carry):
        _row_copy(x_ref, t, xs_ref, pos1_ref[base + t], sem).start()
        _row_copy(x_ref, t, xs_ref, pos2_ref[base + t], sem).start()
        return carry

    lax.fori_loop(0, MOE_TOK, issue, 0, unroll=32)
    for _ in range(2):
        pltpu.make_async_copy(x_ref, xs_ref.at[pl.ds(0, MOE_TOK), :], sem).wait()


def _dispatch(pos1, pos2, x1, n_slots, spare):
    n = x1.shape[0]
    init = jnp.zeros((n_slots, D_MODEL), F32) if spare is None else spare
    return pl.pallas_call(
        _dispatch_kernel,
        grid_spec=pltpu.PrefetchScalarGridSpec(
            num_scalar_prefetch=2,
            grid=(n // MOE_TOK,),
            in_specs=[pl.BlockSpec((MOE_TOK, D_MODEL), lambda i, p1, p2: (i, 0)),
                      pl.BlockSpec(memory_space=pl.ANY)],
            out_specs=pl.BlockSpec(memory_space=pl.ANY),
            scratch_shapes=[pltpu.SemaphoreType.DMA(())]),
        out_shape=jax.ShapeDtypeStruct((n_slots, D_MODEL), F32),
        input_output_aliases={3: 0},
        compiler_params=_cparams(("arbitrary",)),
        name="moe_dispatch",
    )(pos1, pos2, x1, init)


def _ffn_kernel(te_ref, nu_ref, xs_ref, wg_ref, wu_ref, wd_ref, ys_ref):
    del te_ref
    i = pl.program_id(0)

    @pl.when(i < nu_ref[0])
    def _():
        xb = xs_ref[...].astype(BF16)
        gte = _dot(xb, wg_ref[...].astype(BF16))
        hid = gte * _sigmoid(gte) * _dot(xb, wu_ref[...].astype(BF16))
        ys_ref[...] = _dot(hid.astype(BF16), wd_ref[...].astype(BF16))

    @pl.when(i >= nu_ref[0])
    def _():
        ys_ref[...] = jnp.zeros_like(ys_ref)


def _ffn(tile_expert, n_used, xs, w_gate, w_up, w_down, layer):
    n_slots = xs.shape[0]
    wspec = lambda r, c: pl.BlockSpec((None, None, r, c), lambda i, te, nu: (layer, te[i], 0, 0))
    return pl.pallas_call(
        _ffn_kernel,
        grid_spec=pltpu.PrefetchScalarGridSpec(
            num_scalar_prefetch=2,
            grid=(n_slots // FFN_TILE,),
            in_specs=[pl.BlockSpec((FFN_TILE, D_MODEL), lambda i, te, nu: (i, 0)),
                      wspec(D_MODEL, D_EXPERT), wspec(D_MODEL, D_EXPERT), wspec(D_EXPERT, D_MODEL)],
            out_specs=pl.BlockSpec((FFN_TILE, D_MODEL), lambda i, te, nu: (i, 0))),
        out_shape=jax.ShapeDtypeStruct((n_slots, D_MODEL), F32),
        compiler_params=_cparams(("arbitrary",)),
        name="moe_ffn",
    )(tile_expert, n_used, xs, w_gate, w_up, w_down)


def _combine_kernel(pos1_ref, pos2_ref, x_ref, route_ref, lg_ref, lb_ref, ys_ref, o_ref, buf_ref, sem):
    i = pl.program_id(0)

    def gather(step, slot):
        base = step * MOE_TOK

        def issue(t, carry):
            _row_copy(ys_ref, pos1_ref[base + t], buf_ref.at[slot, 0], t, sem.at[slot]).start()
            _row_copy(ys_ref, pos2_ref[base + t], buf_ref.at[slot, 1], t, sem.at[slot]).start()
            return carry

        lax.fori_loop(0, MOE_TOK, issue, 0, unroll=32)

    @pl.when(i == 0)
    def _():
        gather(0, 0)

    @pl.when(i + 1 < pl.num_programs(0))
    def _():
        gather(i + 1, (i + 1) % 2)

    slot = i % 2
    for s in range(2):
        pltpu.make_async_copy(ys_ref.at[pl.ds(0, MOE_TOK), :], buf_ref.at[slot, s], sem.at[slot]).wait()
    route = route_ref[...]
    ffn = route[:, 2:3] * buf_ref[slot, 0] + route[:, 3:4] * buf_ref[slot, 1]
    o_ref[...] = _layer_norm(DN_ALPHA * x_ref[...] + ffn, lg_ref[...], lb_ref[...])


def _combine(pos1, pos2, x1, route, ln_g, ln_b, ys):
    n = x1.shape[0]
    row = lambda w: pl.BlockSpec((MOE_TOK, w), lambda i, p1, p2: (i, 0))
    const = lambda w: pl.BlockSpec((1, w), lambda i, p1, p2: (0, 0))
    return pl.pallas_call(
        _combine_kernel,
        grid_spec=pltpu.PrefetchScalarGridSpec(
            num_scalar_prefetch=2,
            grid=(n // MOE_TOK,),
            in_specs=[row(D_MODEL), row(LANES), const(D_MODEL), const(D_MODEL),
                      pl.BlockSpec(memory_space=pl.ANY)],
            out_specs=row(D_MODEL),
            scratch_shapes=[pltpu.VMEM((2, 2, MOE_TOK, D_MODEL), F32),
                            pltpu.SemaphoreType.DMA((2,))]),
        out_shape=jax.ShapeDtypeStruct((n, D_MODEL), F32),
        compiler_params=_cparams(("arbitrary",)),
        name="moe_combine",
    )(pos1, pos2, x1, route, ln_g, ln_b, ys)


def _moe(x1, oh, route, w_gate, w_up, w_down, layer, ln_g, ln_b, spare):
    n = x1.shape[0]
    n_slots = TOP_K_INNER * n + N_EXPERTS * FFN_TILE
    rank, cnt = _rank(oh, route)
    cnt = cnt[0, :N_EXPERTS].astype(jnp.int32)
    padded = (cnt + FFN_TILE - 1) // FFN_TILE * FFN_TILE
    ends = jnp.cumsum(padded)
    offs = ends - padded
    experts = jnp.arange(N_EXPERTS, dtype=jnp.int32)
    i1 = route[:, 0].astype(jnp.int32)
    i2 = route[:, 1].astype(jnp.int32)
    off_of = lambda idx: jnp.sum(jnp.where(idx[:, None] == experts[None, :], offs[None, :], 0), axis=1)
    pos1 = off_of(i1) + rank[:, 0].astype(jnp.int32)
    pos2 = off_of(i2) + rank[:, 1].astype(jnp.int32)
    tile_start = jnp.arange(n_slots // FFN_TILE, dtype=jnp.int32) * FFN_TILE
    tile_expert = jnp.minimum(jnp.sum(tile_start[:, None] >= ends[None, :], axis=1), N_EXPERTS - 1)
    n_used = (ends[-1] // FFN_TILE).reshape(1)

    xs = _dispatch(pos1, pos2, x1, n_slots, spare)
    ys = _ffn(tile_expert.astype(jnp.int32), n_used.astype(jnp.int32), xs, w_gate, w_up, w_down, layer)
    return _combine(pos1, pos2, x1, route, ln_g, ln_b, ys), ys


def _rope_tables(positions):
    inv_freq = ROPE_THETA ** (-jnp.arange(0, ROT_DIM, 2, dtype=F32) / ROT_DIM)
    ang = positions.astype(F32).reshape(-1, 1) * inv_freq
    cos, sin = jnp.cos(ang), jnp.sin(ang)
    n = ang.shape[0]
    half = ROT_DIM // 2
    c64 = jnp.concatenate([cos, cos, jnp.ones((n, A_HEAD_DIM - ROT_DIM), F32)], axis=1)
    s64 = jnp.concatenate([-sin, sin, jnp.zeros((n, A_HEAD_DIM - ROT_DIM), F32)], axis=1)
    del half
    return jnp.tile(c64, (1, 2)), jnp.tile(s64, (1, 2))


def _split_w_in(w):
    widths = (A_OUT, A_HEAD_DIM, A_HEAD_DIM, IDX_HEADS * IDX_DIM, IDX_DIM, IDX_HEADS,
              B_HEADS * B_KEY_DIM, B_HEADS * B_KEY_DIM, B_OUT, B_GATE_RANK, B_OUT,
              2 * C_OUT, D_MODEL, D_MODEL, D_MODEL)
    pts = np.cumsum((0,) + widths)
    return [w[:, int(pts[i]):int(pts[i + 1])] for i in range(len(widths))]


def kernel(x, positions, ln_in_g, ln_in_b, w_in, idx_k_g, gla_wa2, gla_ba, gla_norm_g, gm_ln_g, gm_ln_b, gm_ws, gm_bs, w_branch_a, w_branch_b, w_branch_c, w_out, ln1_g, ln1_b, w_rg, b_rg, w_re, b_re, w_gate, w_up, w_down, ln2_g, ln2_b):
    bsz, seq, d = x.shape
    n = bsz * seq
    cos_t, sin_t = _rope_tables(positions)
    h = _entry_ln(x.reshape(n, d), ln_in_g, ln_in_b)
    spare = None
    for l in range(DEPTH):
        (a_q, a_k, a_v, i_q, i_k, i_w, b_q, b_k, b_v, b_glr, b_r, c_uv,
         g_a, g_b, g_c) = _split_w_in(w_in[l])
        zpad = lambda c: jnp.zeros((d, c), F32)
        w_a = jnp.concatenate([a_q, a_k, a_v, i_q, i_k, i_w, zpad(LANES - IDX_DIM - IDX_HEADS)],
                              axis=1).astype(BF16)
        w_b = jnp.concatenate([b_q * (B_KEY_DIM ** -0.5), b_k, b_v, b_r,
                               b_glr, zpad(LANES - B_GATE_RANK)], axis=1).astype(BF16)
        wa_pad = jnp.concatenate([gla_wa2[l],
                                  jnp.zeros((LANES - B_GATE_RANK, GLA_QK), F32)], axis=0).astype(BF16)
        w_g = jnp.concatenate([g_a, g_b, g_c], axis=1).astype(BF16)
        ikg = jnp.concatenate([idx_k_g[l], jnp.zeros((LANES - IDX_DIM,), F32)]).reshape(1, LANES)
        bs_t = jnp.concatenate([gm_bs[l].T, jnp.zeros((C_CHUNK, LANES - C_GROUPS), F32)], axis=1)
        w_r = jnp.concatenate([w_re[l], w_rg[l], zpad(LANES - N_GROUPS - N_EXPERTS)], axis=1)
        w_r_hi = w_r.astype(BF16)
        w_r_lo = (w_r - w_r_hi.astype(F32)).astype(BF16)
        b_r_all = jnp.concatenate([b_re[l], b_rg[l],
                                   jnp.zeros((LANES - N_GROUPS - N_EXPERTS,), F32)]).reshape(1, LANES)

        q, k, v, iq, ki, wi = _proj_a(h, w_a, cos_t, sin_t, ikg)
        y_a = _dsa(q, iq, wi, k, v, ki, bsz, seq)
        y_b = _gla(h, w_b, wa_pad, gla_ba[l].reshape(1, -1), gla_norm_g[l].reshape(1, -1), bsz, seq)
        y_c = _gmlp(h, c_uv.astype(BF16), gm_ln_g[l].reshape(1, -1), gm_ln_b[l].reshape(1, -1),
                    gm_ws[l], bs_t)
        x1, oh, route = _merge(h, y_a, y_b, y_c, w_g,
                               w_branch_a[l].astype(BF16), w_branch_b[l].astype(BF16),
                               w_branch_c[l].astype(BF16), w_out[l].astype(BF16),
                               ln1_g[l].reshape(1, -1), ln1_b[l].reshape(1, -1), w_r_hi, w_r_lo, b_r_all)
        h, spare = _moe(x1, oh, route, w_gate, w_up, w_down, l,
                        ln2_g[l].reshape(1, -1), ln2_b[l].reshape(1, -1), spare)
    return h.reshape(bsz, seq, d)
```

```python
import functools

import numpy as np
import jax
import jax.numpy as jnp
from jax import lax
from jax.experimental import pallas as pl
from jax.experimental.pallas import tpu as pltpu

F32 = jnp.float32
BF16 = jnp.bfloat16

D_MODEL = 1024
DEPTH = 2
A_HEADS = 8
A_HEAD_DIM = 64
A_OUT = 512
IDX_HEADS = 4
IDX_DIM = 64
TOPK_MAX = 256
Q_BLOCK = 256
ROPE_THETA = 500000.0
ROT_DIM = 16
B_HEADS = 4
B_KEY_DIM = 64
B_VAL_DIM = 128
B_OUT = 512
B_GATE_RANK = 16
B_GATE_TAU = 16.0
B_CHUNK = 64
B_SUB = 16
N_SUB = B_CHUNK // B_SUB
C_GROUPS = 4
C_GROUP_DIM = 128
C_OUT = 512
C_CHUNK = 128
N_GROUPS = 4
EXPERTS_PER_GROUP = 8
N_EXPERTS = 32
D_EXPERT = 256
TOP_K_INNER = 2
DN_ALPHA = (2 * DEPTH) ** 0.25
LN_EPS = 1e-5
RMS_EPS = 1e-6

LANES = 128
NEG_BIG = -1e30
LOG2_E = 1.4426950408889634
INT_MIN = -2 ** 31
VMEM_LIMIT = 56 * 1024 * 1024


def _cparams(sem):
    return pltpu.CompilerParams(dimension_semantics=sem, vmem_limit_bytes=VMEM_LIMIT)


def _layer_norm(x, g, b):
    mu = jnp.mean(x, axis=-1, keepdims=True)
    xc = x - mu
    var = jnp.mean(xc * xc, axis=-1, keepdims=True)
    return xc * lax.rsqrt(var + LN_EPS) * g + b


def _dot(a, b):
    return jnp.dot(a, b, preferred_element_type=F32)


def _dot_nt(a, b):
    return lax.dot_general(a, b, (((1,), (1,)), ((), ())), preferred_element_type=F32)


def _ln_kernel(x_ref, g_ref, b_ref, o_ref):
    o_ref[...] = _layer_norm(x_ref[...], g_ref[...], b_ref[...])


def _entry_ln(x, g, b, tm=1024):
    n = x.shape[0]
    return pl.pallas_call(
        _ln_kernel,
        grid=(n // tm,),
        in_specs=[pl.BlockSpec((tm, D_MODEL), lambda i: (i, 0)),
                  pl.BlockSpec((1, D_MODEL), lambda i: (0, 0)),
                  pl.BlockSpec((1, D_MODEL), lambda i: (0, 0))],
        out_specs=pl.BlockSpec((tm, D_MODEL), lambda i: (i, 0)),
        out_shape=jax.ShapeDtypeStruct((n, D_MODEL), F32),
        compiler_params=_cparams(("parallel",)),
        name="entry_ln",
    )(x, g.reshape(1, -1), b.reshape(1, -1))


def _rope_slab(xs, c, s):
    lane = lax.broadcasted_iota(jnp.int32, xs.shape, 1) % A_HEAD_DIM
    fwd = pltpu.roll(xs, LANES - ROT_DIM // 2, axis=1)
    bwd = pltpu.roll(xs, ROT_DIM // 2, axis=1)
    partner = jnp.where(lane < ROT_DIM // 2, fwd, bwd)
    return xs * c + partner * s


def _proj_a_kernel(h_ref, w_ref, c_ref, s_ref, g_ref,
                   q_ref, k_ref, v_ref, iq_ref, ki_ref, wi_ref):
    p = _dot(h_ref[...].astype(BF16), w_ref[...])
    c = c_ref[...]
    s = s_ref[...]
    lane = lax.broadcasted_iota(jnp.int32, c.shape, 1)
    lo = lane < A_HEAD_DIM
    n_blk = p.shape[0] // Q_BLOCK

    def put_heads(dst_ref, slab, first_head):
        for half in range(2):
            hd = first_head + half
            cols = slab[:, half * A_HEAD_DIM:(half + 1) * A_HEAD_DIM].astype(BF16)
            for qb in range(n_blk):
                dst_ref[qb, hd * Q_BLOCK:(hd + 1) * Q_BLOCK, :] = cols[qb * Q_BLOCK:(qb + 1) * Q_BLOCK, :]

    for i in range(4):
        sl = slice(i * LANES, (i + 1) * LANES)
        put_heads(q_ref, _rope_slab(p[:, sl], c, s) * (A_HEAD_DIM ** -0.5 * LOG2_E), 2 * i)
    kv = p[:, 512:640]
    kv = _rope_slab(kv, jnp.where(lo, c, 1.0), jnp.where(lo, s, 0.0))
    k_ref[...] = kv[:, :A_HEAD_DIM].astype(BF16)
    v_ones = jnp.where(lo, pltpu.roll(kv, A_HEAD_DIM, axis=1), jnp.where(lane == A_HEAD_DIM, 1.0, 0.0))
    v_ref[...] = v_ones.astype(BF16)
    for i in range(2):
        sl = slice(640 + i * LANES, 640 + (i + 1) * LANES)
        put_heads(iq_ref, _rope_slab(p[:, sl], c, s), 2 * i)
    last = p[:, 896:1024]
    mu = jnp.sum(jnp.where(lo, last, 0.0), axis=-1, keepdims=True) * (1.0 / IDX_DIM)
    xc = jnp.where(lo, last - mu, 0.0)
    var = jnp.sum(xc * xc, axis=-1, keepdims=True) * (1.0 / IDX_DIM)
    kin = xc * lax.rsqrt(var + LN_EPS) * g_ref[...]
    kin = _rope_slab(kin, jnp.where(lo, c, 1.0), jnp.where(lo, s, 0.0))
    ki_ref[...] = kin[:, :IDX_DIM].astype(BF16)
    wi = pltpu.roll(last, LANES - IDX_DIM, axis=1) * (IDX_HEADS ** -0.5 * IDX_DIM ** -0.5)
    for qb in range(n_blk):
        wi_ref[qb] = wi[qb * Q_BLOCK:(qb + 1) * Q_BLOCK, :].T[0:8, :]


def _proj_a(h, w_a, cos_t, sin_t, ikg, tm=512):
    n = h.shape[0]
    row = lambda w: pl.BlockSpec((tm, w), lambda i: (i, 0))
    stacked = lambda heads: pl.BlockSpec((tm // Q_BLOCK, heads * Q_BLOCK, A_HEAD_DIM), lambda i: (i, 0, 0))
    return pl.pallas_call(
        _proj_a_kernel,
        grid=(n // tm,),
        in_specs=[row(D_MODEL),
                  pl.BlockSpec((D_MODEL, 1024), lambda i: (0, 0)),
                  row(LANES), row(LANES),
                  pl.BlockSpec((1, LANES), lambda i: (0, 0))],
        out_specs=[stacked(A_HEADS), row(A_HEAD_DIM),
                   row(LANES),
                   stacked(IDX_HEADS), row(IDX_DIM),
                   pl.BlockSpec((tm // Q_BLOCK, 8, Q_BLOCK), lambda i: (i, 0, 0))],
        out_shape=[jax.ShapeDtypeStruct((n // Q_BLOCK, A_HEADS * Q_BLOCK, A_HEAD_DIM), BF16),
                   jax.ShapeDtypeStruct((n, A_HEAD_DIM), BF16),
                   jax.ShapeDtypeStruct((n, LANES), BF16),
                   jax.ShapeDtypeStruct((n // Q_BLOCK, IDX_HEADS * Q_BLOCK, IDX_DIM), BF16),
                   jax.ShapeDtypeStruct((n, IDX_DIM), BF16),
                   jax.ShapeDtypeStruct((n // Q_BLOCK, 8, Q_BLOCK), F32)],
        compiler_params=_cparams(("parallel",)),
        name="proj_a",
    )(h, w_a, cos_t, sin_t, ikg)


KEY_CHUNK = 512
FIELD_BITS = 15
LOW_BITS = 32 - 2 * FIELD_BITS
FIELD_MASK = (1 << FIELD_BITS) - 1
GUARD_BITS = -2147450880
PAIR_ONE = 0x00010001


def _dsa_kernel(q_ref, iq_ref, wi_ref, k_ref, v_ref, ki_ref, o_ref,
                key_ref, pk_ref, bias_ref, *, seq):
    j = pl.program_id(1)
    n_sel = min(TOPK_MAX, seq // 4)
    n_kc = (j * Q_BLOCK + Q_BLOCK + KEY_CHUNK - 1) // KEY_CHUNK
    qpos = j * Q_BLOCK + lax.broadcasted_iota(jnp.int32, (1, Q_BLOCK), 1)
    kk = jnp.minimum(n_sel, qpos + 1).astype(F32)
    row_pos = lax.broadcasted_iota(jnp.int32, (KEY_CHUNK, 1), 0)

    def key_rows(c):
        return pl.ds(pl.multiple_of(c * KEY_CHUNK, KEY_CHUNK), KEY_CHUNK)

    iq = iq_ref[...]
    wi = wi_ref[...]

    def score_chunk(c, carry):
        d = jnp.maximum(_dot_nt(ki_ref[key_rows(c), :], iq), 0.0)
        score = wi[0:1, :] * d[:, 0:Q_BLOCK]
        for h in range(1, IDX_HEADS):
            score = score + wi[h:h + 1, :] * d[:, h * Q_BLOCK:(h + 1) * Q_BLOCK]
        bits = pltpu.bitcast(score + 0.0, jnp.int32)
        key = bits ^ ((bits >> 31) & 0x7FFFFFFF)
        key = jnp.where(c * KEY_CHUNK + row_pos <= qpos, key, INT_MIN)
        key_ref[c] = key
        pk_ref[c] = pack(lax.shift_right_logical(key ^ INT_MIN, FIELD_BITS + LOW_BITS))
        return carry

    def pack(f):
        hf = KEY_CHUNK // 2
        return (f[:hf] << 16) | f[hf:] | GUARD_BITS

    lax.fori_loop(0, n_kc, score_chunk, 0)

    def count(pred):
        def body(c, acc):
            m = jnp.where(pred(key_ref[c]), 1.0, 0.0)
            return acc + jnp.sum(m.reshape(KEY_CHUNK // 64, 64, Q_BLOCK), axis=0)
        acc = lax.fori_loop(0, n_kc, body, jnp.zeros((64, Q_BLOCK), F32))
        return jnp.sum(acc, axis=0, keepdims=True)

    def count_packed(cand):
        both = (cand << 16) | cand

        def body(c, acc):
            flags = ((pk_ref[c] - both) >> 15) & PAIR_ONE
            parts = [flags[t * 64:(t + 1) * 64] for t in range(KEY_CHUNK // 128)]
            while len(parts) > 1:
                parts = [a + b for a, b in zip(parts[::2], parts[1::2])]
            return acc + parts[0]

        acc = lax.fori_loop(0, n_kc, body, jnp.zeros((64, Q_BLOCK), jnp.int32))
        return jnp.sum(((acc & 0xFFFF) + (acc >> 16)).astype(F32), axis=0, keepdims=True)

    def search_field(target):
        def step(i, t):
            cand = t | jnp.left_shift(jnp.int32(1), FIELD_BITS - 1 - i)
            return jnp.where(count_packed(cand) >= target, cand, t)
        return lax.fori_loop(0, FIELD_BITS, step, jnp.zeros((1, Q_BLOCK), jnp.int32))

    t_a = search_field(kk)
    top_of_a = ((t_a << (FIELD_BITS + LOW_BITS)) | ((1 << (FIELD_BITS + LOW_BITS)) - 1)) ^ INT_MIN
    need_b = kk - count(lambda kc: kc > top_of_a)

    def repack(c, carry):
        u = key_ref[c] ^ INT_MIN
        member = lax.shift_right_logical(u, FIELD_BITS + LOW_BITS) == t_a
        pk_ref[c] = pack(jnp.where(member, lax.shift_right_logical(u, LOW_BITS) & FIELD_MASK, 0))
        return carry

    lax.fori_loop(0, n_kc, repack, 0)
    base = (t_a << (FIELD_BITS + LOW_BITS)) | (search_field(need_b) << LOW_BITS)

    def low_step(i, t_c):
        bit = jnp.left_shift(jnp.int32(1), LOW_BITS - 1 - i)
        cand_s = (base | t_c | bit) ^ INT_MIN
        return jnp.where(count(lambda kc: kc >= cand_s) >= kk, t_c | bit, t_c)

    thr = (base | lax.fori_loop(0, LOW_BITS, low_step, jnp.zeros((1, Q_BLOCK), jnp.int32))) ^ INT_MIN
    need = kk - count(lambda kc: kc > thr)
    r_i = lax.broadcasted_iota(jnp.int32, (LANES, LANES), 0)
    c_i = lax.broadcasted_iota(jnp.int32, (LANES, LANES), 1)
    lower = jnp.where(c_i < r_i, 1.0, 0.0).astype(BF16)

    def fill(c, carry):
        for t in range(KEY_CHUNK // LANES):
            rows = slice(t * LANES, (t + 1) * LANES)
            kt = key_ref[c, rows, :]
            eq = kt == thr
            eq_f = jnp.where(eq, 1.0, 0.0)
            pre = _dot(lower, eq_f.astype(BF16)) + carry
            sel = (kt > thr) | (eq & (pre < need))
            bias_ref[c, rows, :] = jnp.where(sel, 0.0, NEG_BIG)
            carry = carry + jnp.sum(eq_f, axis=0, keepdims=True)
        return carry

    lax.fori_loop(0, n_kc, fill, jnp.zeros((1, Q_BLOCK), F32))

    q = q_ref[...]
    cols = A_HEADS * Q_BLOCK

    def attn_chunk(c, carry):
        m, acc = carry
        rows = key_rows(c)
        lg = _dot_nt(k_ref[rows, :], q) + jnp.concatenate([bias_ref[c]] * A_HEADS, axis=1)
        m_new = jnp.maximum(m, jnp.max(lg, axis=0, keepdims=True))
        p = jnp.exp2(lg - m_new).astype(BF16)
        pv = lax.dot_general(v_ref[rows, :], p, (((0,), (0,)), ((), ())),
                             preferred_element_type=F32)
        return m_new, jnp.exp2(m - m_new) * acc + pv

    init = (jnp.full((1, cols), -3e38, F32), jnp.zeros((LANES, cols), F32))
    _, acc = lax.fori_loop(0, n_kc, attn_chunk, init)
    o = acc[:A_HEAD_DIM] / acc[A_HEAD_DIM:A_HEAD_DIM + 1]
    for h in range(A_HEADS):
        o_ref[:, h * A_HEAD_DIM:(h + 1) * A_HEAD_DIM] = (
            o[:, h * Q_BLOCK:(h + 1) * Q_BLOCK].T.astype(BF16))


def _dsa(q, iq, wi, k, v, ki, bsz, seq):
    nq = seq // Q_BLOCK
    n_kc = seq // KEY_CHUNK
    stacked = lambda heads: pl.BlockSpec((None, heads * Q_BLOCK, A_HEAD_DIM),
                                         lambda b, j: (b * nq + j, 0, 0))
    kspec = lambda w: pl.BlockSpec((seq, w), lambda b, j: (b, 0))
    return pl.pallas_call(
        functools.partial(_dsa_kernel, seq=seq),
        grid=(bsz, nq),
        in_specs=[stacked(A_HEADS), stacked(IDX_HEADS),
                  pl.BlockSpec((None, 8, Q_BLOCK), lambda b, j: (b * nq + j, 0, 0)),
                  kspec(A_HEAD_DIM),
                  kspec(LANES),
                  kspec(IDX_DIM)],
        out_specs=pl.BlockSpec((Q_BLOCK, A_OUT), lambda b, j: (b * nq + j, 0)),
        out_shape=jax.ShapeDtypeStruct((bsz * seq, A_OUT), BF16),
        scratch_shapes=[pltpu.VMEM((n_kc, KEY_CHUNK, Q_BLOCK), jnp.int32),
                        pltpu.VMEM((n_kc, KEY_CHUNK // 2, Q_BLOCK), jnp.int32),
                        pltpu.VMEM((n_kc, KEY_CHUNK, Q_BLOCK), F32)],
        compiler_params=_cparams(("parallel", "parallel")),
        name="dsa",
    )(q, iq, wi, k, v, ki)


GLA_QK = B_HEADS * B_KEY_DIM
GLA_REP = N_SUB * GLA_QK
GLA_COLS = 2 * GLA_QK + 2 * B_OUT + LANES
GLA_V0 = 2 * GLA_QK


def _gla_kernel(h_ref, w_ref, wa_ref, ba_ref, ng_ref, o_ref, p_ref, g_ref, acc_ref, st_ref, *, ts):
    @pl.when(pl.program_id(1) == 0)
    def _():
        st_ref[...] = jnp.zeros_like(st_ref)

    p_ref[...] = _dot(h_ref[...].astype(BF16), w_ref[...])
    x = _dot(p_ref[:, GLA_V0 + 2 * B_OUT:].astype(BF16), wa_ref[...]) + ba_ref[...]
    g_ref[...] = (jnp.minimum(x, 0.0) - jnp.log1p(jnp.exp(-jnp.abs(x)))) * (1.0 / B_GATE_TAU)

    t_i = lax.broadcasted_iota(jnp.int32, (B_CHUNK, B_CHUNK), 0)
    s_i = lax.broadcasted_iota(jnp.int32, (B_CHUNK, B_CHUNK), 1)
    tri = t_i >= s_i
    tri_b = jnp.where(tri, 1.0, 0.0).astype(BF16)
    lane_sub = (lax.broadcasted_iota(jnp.int32, (1, GLA_REP), 1) // B_KEY_DIM) % N_SUB
    row_sub = lax.broadcasted_iota(jnp.int32, (B_CHUNK, 1), 0) // B_SUB
    q_mask = lane_sub <= row_sub
    k_mask = lane_sub == row_sub
    low_half = lax.broadcasted_iota(jnp.int32, (B_CHUNK, LANES), 1) < B_KEY_DIM

    def replicate(x):
        outs = []
        for pair in range(GLA_QK // LANES):
            v = x[:, pair * LANES:(pair + 1) * LANES]
            r = pltpu.roll(v, B_KEY_DIM, axis=1)
            outs += [jnp.where(low_half, v, r)] * (N_SUB // 2) + [jnp.where(low_half, r, v)] * (N_SUB // 2)
        return jnp.concatenate(outs, axis=1)

    def chunk(c, carry):
        rows = pl.ds(pl.multiple_of(c * B_CHUNK, B_CHUNK), B_CHUNK)
        g = g_ref[rows, :]
        g_hi = g.astype(BF16)
        g_mid = (g - g_hi.astype(F32)).astype(BF16)
        g_lo = (g - g_hi.astype(F32) - g_mid.astype(F32)).astype(BF16)
        b = replicate(_dot(tri_b, g_hi) + _dot(tri_b, g_mid) + _dot(tri_b, g_lo))
        ref = jnp.zeros((1, GLA_REP), F32)
        for jj in range(1, N_SUB):
            ref = ref + jnp.where(lane_sub == jj, b[jj * B_SUB - 1:jj * B_SUB, :], 0.0)
        d = b - ref
        e_q = jnp.where(q_mask, jnp.exp(jnp.minimum(d, 0.0)), 0.0)
        e_k = jnp.where(k_mask, jnp.exp(jnp.where(k_mask, -d, 0.0)), 0.0)
        b_last = b[B_CHUNK - 1:B_CHUNK, :]
        q_c = (replicate(p_ref[rows, 0:GLA_QK]) * e_q).astype(BF16)
        k_raw = replicate(p_ref[rows, GLA_QK:2 * GLA_QK])
        k_c = (k_raw * e_k).astype(BF16)
        k_st = (k_raw * jnp.exp(b_last - b)).astype(BF16)
        e_last = jnp.exp(b_last)
        for hh in range(B_HEADS):
            base = hh * N_SUB * B_KEY_DIM
            q_h = q_c[:, base:base + N_SUB * B_KEY_DIM]
            k_h = k_c[:, base:base + N_SUB * B_KEY_DIM]
            att = jnp.where(tri, _dot_nt(q_h, k_h), 0.0)
            v_h = p_ref[rows, GLA_V0 + hh * B_VAL_DIM:GLA_V0 + (hh + 1) * B_VAL_DIM].astype(BF16)
            st = st_ref[hh]
            o = _dot(att.astype(BF16), v_h) + _dot_nt(q_h[:, :B_KEY_DIM], st.astype(BF16))
            acc_ref[rows, hh * B_VAL_DIM:(hh + 1) * B_VAL_DIM] = o
            upd = _dot(v_h.T, k_st[:, base:base + B_KEY_DIM])
            st_ref[hh] = st * e_last[:, base:base + B_KEY_DIM] + upd
        return carry

    lax.fori_loop(0, ts // B_CHUNK, chunk, 0, unroll=4)

    r = p_ref[:, GLA_V0 + B_OUT:GLA_V0 + 2 * B_OUT]
    gate = r * (1.0 / (1.0 + jnp.exp(-r)))
    ng = ng_ref[...]
    for hh in range(B_HEADS):
        sl = slice(hh * B_VAL_DIM, (hh + 1) * B_VAL_DIM)
        o = acc_ref[:, sl]
        y = o * lax.rsqrt(jnp.mean(o * o, axis=-1, keepdims=True) + RMS_EPS) * ng
        o_ref[:, sl] = (gate[:, sl] * y).astype(BF16)


def _gla(h, w_b, wa, ba, norm_g, bsz, seq, ts=512):
    ns = seq // ts
    const = lambda shape: pl.BlockSpec(shape, lambda b, i: (0, 0))
    return pl.pallas_call(
        functools.partial(_gla_kernel, ts=ts),
        grid=(bsz, ns),
        in_specs=[pl.BlockSpec((ts, D_MODEL), lambda b, i: (b * ns + i, 0)),
                  const((D_MODEL, GLA_COLS)), const((LANES, GLA_QK)),
                  const((1, GLA_QK)), const((1, B_VAL_DIM))],
        out_specs=pl.BlockSpec((ts, B_OUT), lambda b, i: (b * ns + i, 0)),
        out_shape=jax.ShapeDtypeStruct((bsz * seq, B_OUT), BF16),
        scratch_shapes=[pltpu.VMEM((ts, GLA_COLS), F32),
                        pltpu.VMEM((ts, GLA_QK), F32),
                        pltpu.VMEM((ts, B_OUT), F32),
                        pltpu.VMEM((B_HEADS, B_VAL_DIM, B_KEY_DIM), F32)],
        compiler_params=_cparams(("parallel", "arbitrary")),
        name="gla",
    )(h, w_b, wa, ba, norm_g)


def _gmlp_kernel(h_ref, w_ref, lg_ref, lb_ref, ws_ref, bs_ref, o_ref, *, tm):
    z = _dot(h_ref[...].astype(BF16), w_ref[...])
    z = z * (0.5 * (1.0 + jnp.tanh(np.sqrt(2.0 / np.pi) * (z + 0.044715 * (z * z * z)))))
    u = z[:, :C_OUT]
    v = _layer_norm(z[:, C_OUT:], lg_ref[...], lb_ref[...]).astype(BF16)
    t_i = lax.broadcasted_iota(jnp.int32, (C_CHUNK, C_CHUNK), 0)
    s_i = lax.broadcasted_iota(jnp.int32, (C_CHUNK, C_CHUNK), 1)
    bs = bs_ref[...]
    for g in range(C_GROUPS):
        w = jnp.where(t_i >= s_i, ws_ref[g], 0.0).astype(BF16)
        sl = slice(g * C_GROUP_DIM, (g + 1) * C_GROUP_DIM)
        for c in range(tm // C_CHUNK):
            rows = slice(c * C_CHUNK, (c + 1) * C_CHUNK)
            mixed = _dot(w, v[rows, sl]) + bs[:, g:g + 1]
            o_ref[rows, sl] = (u[rows, sl] * mixed).astype(BF16)


def _gmlp(h, w_c, ln_g, ln_b, ws, bs_t, tm=512):
    n = h.shape[0]
    return pl.pallas_call(
        functools.partial(_gmlp_kernel, tm=tm),
        grid=(n // tm,),
        in_specs=[pl.BlockSpec((tm, D_MODEL), lambda i: (i, 0)),
                  pl.BlockSpec((D_MODEL, 2 * C_OUT), lambda i: (0, 0)),
                  pl.BlockSpec((1, C_OUT), lambda i: (0, 0)),
                  pl.BlockSpec((1, C_OUT), lambda i: (0, 0)),
                  pl.BlockSpec((C_GROUPS, C_CHUNK, C_CHUNK), lambda i: (0, 0, 0)),
                  pl.BlockSpec((C_CHUNK, LANES), lambda i: (0, 0))],
        out_specs=pl.BlockSpec((tm, C_OUT), lambda i: (i, 0)),
        out_shape=jax.ShapeDtypeStruct((n, C_OUT), BF16),
        compiler_params=_cparams(("parallel",)),
        name="gmlp",
    )(h, w_c, ln_g, ln_b, ws, bs_t)


def _sigmoid(x):
    return 1.0 / (1.0 + jnp.exp(-x))


def _merge_kernel(h_ref, ya_ref, yb_ref, yc_ref, wg_ref, wa_ref, wb_ref, wc_ref, wo_ref,
                  lg_ref, lb_ref, wr_ref, wrl_ref, br_ref, x_ref, oh_ref, route_ref):
    h = h_ref[...]
    hb = h.astype(BF16)
    acc = None
    for i, (y_ref, wbr_ref) in enumerate(((ya_ref, wa_ref), (yb_ref, wb_ref), (yc_ref, wc_ref))):
        gate = _sigmoid(_dot(hb, wg_ref[:, i * D_MODEL:(i + 1) * D_MODEL]))
        term = gate * _dot(y_ref[...], wbr_ref[...])
        acc = term if acc is None else acc + term
    mix = _dot(acc.astype(BF16), wo_ref[...])
    x1 = _layer_norm(DN_ALPHA * h + mix, lg_ref[...], lb_ref[...])
    x_ref[...] = x1

    x_hi = x1.astype(BF16)
    x_lo = (x1 - x_hi.astype(F32)).astype(BF16)
    logit = (_dot(x_hi, wr_ref[...]) + _dot(x_hi, wrl_ref[...]) + _dot(x_lo, wr_ref[...])) + br_ref[...]
    lane = lax.broadcasted_iota(jnp.int32, logit.shape, 1)
    big = jnp.int32(LANES)
    is_g = (lane >= N_EXPERTS) & (lane < N_EXPERTS + N_GROUPS)
    gl = jnp.where(is_g, logit, -jnp.inf)
    g_max = jnp.max(gl, axis=-1, keepdims=True)
    g_idx = jnp.min(jnp.where(is_g & (gl == g_max), lane, big), axis=-1, keepdims=True) - N_EXPERTS
    g_p = 1.0 / jnp.sum(jnp.where(is_g, jnp.exp(gl - g_max), 0.0), axis=-1, keepdims=True)
    lo = g_idx * EXPERTS_PER_GROUP
    in_g = (lane >= lo) & (lane < lo + EXPERTS_PER_GROUP)
    el = jnp.where(in_g, logit, -jnp.inf)
    e1 = jnp.max(el, axis=-1, keepdims=True)
    i1 = jnp.min(jnp.where(in_g & (el == e1), lane, big), axis=-1, keepdims=True)
    el2 = jnp.where(lane == i1, -jnp.inf, el)
    e2 = jnp.max(el2, axis=-1, keepdims=True)
    i2 = jnp.min(jnp.where(in_g & (lane != i1) & (el2 == e2), lane, big), axis=-1, keepdims=True)
    z2 = jnp.exp(e2 - e1)
    den = 1.0 + z2
    oh_ref[...] = jnp.where((lane == i1) | (lane == i2), 1.0, 0.0).astype(BF16)
    route_ref[...] = jnp.where(lane == 0, i1.astype(F32),
                               jnp.where(lane == 1, i2.astype(F32),
                                         jnp.where(lane == 2, (1.0 / den) * g_p,
                                                   jnp.where(lane == 3, (z2 / den) * g_p, 0.0))))


def _merge(h, ya, yb, yc, wg, wa, wb, wc, wo, ln_g, ln_b, wr, wr_lo, br, tm=1024):
    n = h.shape[0]
    row = lambda w: pl.BlockSpec((tm, w), lambda i: (i, 0))
    const = lambda r, c: pl.BlockSpec((r, c), lambda i: (0, 0), pipeline_mode=pl.Buffered(1))
    return pl.pallas_call(
        _merge_kernel,
        grid=(n // tm,),
        in_specs=[row(D_MODEL), row(A_OUT), row(B_OUT), row(C_OUT),
                  const(D_MODEL, 3 * D_MODEL), const(A_OUT, D_MODEL), const(B_OUT, D_MODEL),
                  const(C_OUT, D_MODEL), const(D_MODEL, D_MODEL),
                  const(1, D_MODEL), const(1, D_MODEL), const(D_MODEL, LANES), const(D_MODEL, LANES),
                  const(1, LANES)],
        out_specs=[row(D_MODEL), row(LANES), row(LANES)],
        out_shape=[jax.ShapeDtypeStruct((n, D_MODEL), F32),
                   jax.ShapeDtypeStruct((n, LANES), BF16),
                   jax.ShapeDtypeStruct((n, LANES), F32)],
        compiler_params=_cparams(("parallel",)),
        name="merge",
    )(h, ya, yb, yc, wg, wa, wb, wc, wo, ln_g, ln_b, wr, wr_lo, br)


FFN_TILE = 512
MOE_TOK = 512


def _rank_kernel(oh_ref, route_ref, r_ref, cnt_ref, carry_ref):
    @pl.when(pl.program_id(0) == 0)
    def _():
        carry_ref[...] = jnp.zeros_like(carry_ref)

    oh = oh_ref[...]
    tt = oh.shape[0]
    r_i = lax.broadcasted_iota(jnp.int32, (tt, tt), 0)
    c_i = lax.broadcasted_iota(jnp.int32, (tt, tt), 1)
    lower = jnp.where(c_i < r_i, 1.0, 0.0).astype(BF16)
    rank = _dot(lower, oh) + carry_ref[...]
    route = route_ref[...]
    lane = lax.broadcasted_iota(jnp.int32, rank.shape, 1)
    i1 = route[:, 0:1].astype(jnp.int32)
    i2 = route[:, 1:2].astype(jnp.int32)
    r1 = jnp.sum(jnp.where(lane == i1, rank, 0.0), axis=-1, keepdims=True)
    r2 = jnp.sum(jnp.where(lane == i2, rank, 0.0), axis=-1, keepdims=True)
    r_ref[...] = jnp.where(lane == 0, r1, jnp.where(lane == 1, r2, 0.0))
    carry_ref[...] += jnp.sum(oh.astype(F32), axis=0, keepdims=True)
    cnt_ref[...] = carry_ref[...]


def _rank(oh, route, tt=512):
    n = oh.shape[0]
    return pl.pallas_call(
        _rank_kernel,
        grid=(n // tt,),
        in_specs=[pl.BlockSpec((tt, LANES), lambda i: (i, 0)),
                  pl.BlockSpec((tt, LANES), lambda i: (i, 0))],
        out_specs=[pl.BlockSpec((tt, LANES), lambda i: (i, 0)),
                   pl.BlockSpec((1, LANES), lambda i: (0, 0))],
        out_shape=[jax.ShapeDtypeStruct((n, LANES), F32),
                   jax.ShapeDtypeStruct((1, LANES), F32)],
        scratch_shapes=[pltpu.VMEM((1, LANES), F32)],
        compiler_params=_cparams(("arbitrary",)),
        name="moe_rank",
    )(oh, route)


def _row_copy(src_ref, src_row, dst_ref, dst_row, sem):
    return pltpu.make_async_copy(src_ref.at[pl.ds(src_row, 1), :], dst_ref.at[pl.ds(dst_row, 1), :], sem)


def _dispatch_kernel(pos1_ref, pos2_ref, x_ref, init_ref, xs_ref, sem):
    del init_ref
    base = pl.program_id(0) * MOE_TOK

    def issue(t, carry):
        _row_copy(x_ref, t, xs_ref, pos1_ref[base + t], sem).start()
        _row_copy(x_ref, t, xs_ref, pos2_ref[base + t], sem).start()
        return carry

    lax.fori_loop(0, MOE_TOK, issue, 0, unroll=32)
    for _ in range(2):
        pltpu.make_async_copy(x_ref, xs_ref.at[pl.ds(0, MOE_TOK), :], sem).wait()


def _dispatch(pos1, pos2, x1, n_slots, spare):
    n = x1.shape[0]
    init = jnp.zeros((n_slots, D_MODEL), F32) if spare is None else spare
    return pl.pallas_call(
        _dispatch_kernel,
        grid_spec=pltpu.PrefetchScalarGridSpec(
            num_scalar_prefetch=2,
            grid=(n // MOE_TOK,),
            in_specs=[pl.BlockSpec((MOE_TOK, D_MODEL), lambda i, p1, p2: (i, 0)),
                      pl.BlockSpec(memory_space=pl.ANY)],
            out_specs=pl.BlockSpec(memory_space=pl.ANY),
            scratch_shapes=[pltpu.SemaphoreType.DMA(())]),
        out_shape=jax.ShapeDtypeStruct((n_slots, D_MODEL), F32),
        input_output_aliases={3: 0},
        compiler_params=_cparams(("arbitrary",)),
        name="moe_dispatch",
    )(pos1, pos2, x1, init)


def _ffn_kernel(te_ref, nu_ref, xs_ref, wg_ref, wu_ref, wd_ref, ys_ref):
    del te_ref
    i = pl.program_id(0)

    @pl.when(i < nu_ref[0])
    def _():
        xb = xs_ref[...].astype(BF16)
        gte = _dot(xb, wg_ref[...].astype(BF16))
        hid = gte * _sigmoid(gte) * _dot(xb, wu_ref[...].astype(BF16))
        ys_ref[...] = _dot(hid.astype(BF16), wd_ref[...].astype(BF16))

    @pl.when(i >= nu_ref[0])
    def _():
        ys_ref[...] = jnp.zeros_like(ys_ref)


def _ffn(tile_expert, n_used, xs, w_gate, w_up, w_down, layer):
    n_slots = xs.shape[0]
    wspec = lambda r, c: pl.BlockSpec((None, None, r, c), lambda i, te, nu: (layer, te[i], 0, 0))
    return pl.pallas_call(
        _ffn_kernel,
        grid_spec=pltpu.PrefetchScalarGridSpec(
            num_scalar_prefetch=2,
            grid=(n_slots // FFN_TILE,),
            in_specs=[pl.BlockSpec((FFN_TILE, D_MODEL), lambda i, te, nu: (i, 0)),
                      wspec(D_MODEL, D_EXPERT), wspec(D_MODEL, D_EXPERT), wspec(D_EXPERT, D_MODEL)],
            out_specs=pl.BlockSpec((FFN_TILE, D_MODEL), lambda i, te, nu: (i, 0))),
        out_shape=jax.ShapeDtypeStruct((n_slots, D_MODEL), F32),
        compiler_params=_cparams(("arbitrary",)),
        name="moe_ffn",
    )(tile_expert, n_used, xs, w_gate, w_up, w_down)


def _combine_kernel(pos1_ref, pos2_ref, x_ref, route_ref, lg_ref, lb_ref, ys_ref, o_ref, buf_ref, sem):
    i = pl.program_id(0)

    def gather(step, slot):
        base = step * MOE_TOK

        def issue(t, carry):
            _row_copy(ys_ref, pos1_ref[base + t], buf_ref.at[slot, 0], t, sem.at[slot]).start()
            _row_copy(ys_ref, pos2_ref[base + t], buf_ref.at[slot, 1], t, sem.at[slot]).start()
            return carry

        lax.fori_loop(0, MOE_TOK, issue, 0, unroll=32)

    @pl.when(i == 0)
    def _():
        gather(0, 0)

    @pl.when(i + 1 < pl.num_programs(0))
    def _():
        gather(i + 1, (i + 1) % 2)

    slot = i % 2
    for s in range(2):
        pltpu.make_async_copy(ys_ref.at[pl.ds(0, MOE_TOK), :], buf_ref.at[slot, s], sem.at[slot]).wait()
    route = route_ref[...]
    ffn = route[:, 2:3] * buf_ref[slot, 0] + route[:, 3:4] * buf_ref[slot, 1]
    o_ref[...] = _layer_norm(DN_ALPHA * x_ref[...] + ffn, lg_ref[...], lb_ref[...])


def _combine(pos1, pos2, x1, route, ln_g, ln_b, ys):
    n = x1.shape[0]
    row = lambda w: pl.BlockSpec((MOE_TOK, w), lambda i, p1, p2: (i, 0))
    const = lambda w: pl.BlockSpec((1, w), lambda i, p1, p2: (0, 0))
    return pl.pallas_call(
        _combine_kernel,
        grid_spec=pltpu.PrefetchScalarGridSpec(
            num_scalar_prefetch=2,
            grid=(n // MOE_TOK,),
            in_specs=[row(D_MODEL), row(LANES), const(D_MODEL), const(D_MODEL),
                      pl.BlockSpec(memory_space=pl.ANY)],
            out_specs=row(D_MODEL),
            scratch_shapes=[pltpu.VMEM((2, 2, MOE_TOK, D_MODEL), F32),
                            pltpu.SemaphoreType.DMA((2,))]),
        out_shape=jax.ShapeDtypeStruct((n, D_MODEL), F32),
        compiler_params=_cparams(("arbitrary",)),
        name="moe_combine",
    )(pos1, pos2, x1, route, ln_g, ln_b, ys)


def _moe(x1, oh, route, w_gate, w_up, w_down, layer, ln_g, ln_b, spare):
    n = x1.shape[0]
    n_slots = TOP_K_INNER * n + N_EXPERTS * FFN_TILE
    rank, cnt = _rank(oh, route)
    cnt = cnt[0, :N_EXPERTS].astype(jnp.int32)
    padded = (cnt + FFN_TILE - 1) // FFN_TILE * FFN_TILE
    ends = jnp.cumsum(padded)
    offs = ends - padded
    experts = jnp.arange(N_EXPERTS, dtype=jnp.int32)
    i1 = route[:, 0].astype(jnp.int32)
    i2 = route[:, 1].astype(jnp.int32)
    off_of = lambda idx: jnp.sum(jnp.where(idx[:, None] == experts[None, :], offs[None, :], 0), axis=1)
    pos1 = off_of(i1) + rank[:, 0].astype(jnp.int32)
    pos2 = off_of(i2) + rank[:, 1].astype(jnp.int32)
    tile_start = jnp.arange(n_slots // FFN_TILE, dtype=jnp.int32) * FFN_TILE
    tile_expert = jnp.minimum(jnp.sum(tile_start[:, None] >= ends[None, :], axis=1), N_EXPERTS - 1)
    n_used = (ends[-1] // FFN_TILE).reshape(1)

    xs = _dispatch(pos1, pos2, x1, n_slots, spare)
    ys = _ffn(tile_expert.astype(jnp.int32), n_used.astype(jnp.int32), xs, w_gate, w_up, w_down, layer)
    return _combine(pos1, pos2, x1, route, ln_g, ln_b, ys), ys


def _rope_tables(positions):
    inv_freq = ROPE_THETA ** (-jnp.arange(0, ROT_DIM, 2, dtype=F32) / ROT_DIM)
    ang = positions.astype(F32).reshape(-1, 1) * inv_freq
    cos, sin = jnp.cos(ang), jnp.sin(ang)
    n = ang.shape[0]
    half = ROT_DIM // 2
    c64 = jnp.concatenate([cos, cos, jnp.ones((n, A_HEAD_DIM - ROT_DIM), F32)], axis=1)
    s64 = jnp.concatenate([-sin, sin, jnp.zeros((n, A_HEAD_DIM - ROT_DIM), F32)], axis=1)
    del half
    return jnp.tile(c64, (1, 2)), jnp.tile(s64, (1, 2))


def _split_w_in(w):
    widths = (A_OUT, A_HEAD_DIM, A_HEAD_DIM, IDX_HEADS * IDX_DIM, IDX_DIM, IDX_HEADS,
              B_HEADS * B_KEY_DIM, B_HEADS * B_KEY_DIM, B_OUT, B_GATE_RANK, B_OUT,
              2 * C_OUT, D_MODEL, D_MODEL, D_MODEL)
    pts = np.cumsum((0,) + widths)
    return [w[:, int(pts[i]):int(pts[i + 1])] for i in range(len(widths))]


def kernel(x, positions, ln_in_g, ln_in_b, w_in, idx_k_g, gla_wa2, gla_ba, gla_norm_g, gm_ln_g, gm_ln_b, gm_ws, gm_bs, w_branch_a, w_branch_b, w_branch_c, w_out, ln1_g, ln1_b, w_rg, b_rg, w_re, b_re, w_gate, w_up, w_down, ln2_g, ln2_b):
    bsz, seq, d = x.shape
    n = bsz * seq
    cos_t, sin_t = _rope_tables(positions)
    h = _entry_ln(x.reshape(n, d), ln_in_g, ln_in_b)
    spare = None
    for l in range(DEPTH):
        (a_q, a_k, a_v, i_q, i_k, i_w, b_q, b_k, b_v, b_glr, b_r, c_uv,
         g_a, g_b, g_c) = _split_w_in(w_in[l])
        zpad = lambda c: jnp.zeros((d, c), F32)
        w_a = jnp.concatenate([a_q, a_k, a_v, i_q, i_k, i_w, zpad(LANES - IDX_DIM - IDX_HEADS)],
                              axis=1).astype(BF16)
        w_b = jnp.concatenate([b_q * (B_KEY_DIM ** -0.5), b_k, b_v, b_r,
                               b_glr, zpad(LANES - B_GATE_RANK)], axis=1).astype(BF16)
        wa_pad = jnp.concatenate([gla_wa2[l],
                                  jnp.zeros((LANES - B_GATE_RANK, GLA_QK), F32)], axis=0).astype(BF16)
        w_g = jnp.concatenate([g_a, g_b, g_c], axis=1).astype(BF16)
        ikg = jnp.concatenate([idx_k_g[l], jnp.zeros((LANES - IDX_DIM,), F32)]).reshape(1, LANES)
        bs_t = jnp.concatenate([gm_bs[l].T, jnp.zeros((C_CHUNK, LANES - C_GROUPS), F32)], axis=1)
        w_r = jnp.concatenate([w_re[l], w_rg[l], zpad(LANES - N_GROUPS - N_EXPERTS)], axis=1)
        w_r_hi = w_r.astype(BF16)
        w_r_lo = (w_r - w_r_hi.astype(F32)).astype(BF16)
        b_r_all = jnp.concatenate([b_re[l], b_rg[l],
                                   jnp.zeros((LANES - N_GROUPS - N_EXPERTS,), F32)]).reshape(1, LANES)

        q, k, v, iq, ki, wi = _proj_a(h, w_a, cos_t, sin_t, ikg)
        y_a = _dsa(q, iq, wi, k, v, ki, bsz, seq)
        y_b = _gla(h, w_b, wa_pad, gla_ba[l].reshape(1, -1), gla_norm_g[l].reshape(1, -1), bsz, seq)
        y_c = _gmlp(h, c_uv.astype(BF16), gm_ln_g[l].reshape(1, -1), gm_ln_b[l].reshape(1, -1),
                    gm_ws[l], bs_t)
        x1, oh, route = _merge(h, y_a, y_b, y_c, w_g,
                               w_branch_a[l].astype(BF16), w_branch_b[l].astype(BF16),
                               w_branch_c[l].astype(BF16), w_out[l].astype(BF16),
                               ln1_g[l].reshape(1, -1), ln1_b[l].reshape(1, -1), w_r_hi, w_r_lo, b_r_all)
        h, spare = _moe(x1, oh, route, w_gate, w_up, w_down, l,
                        ln2_g[l].reshape(1, -1), ln2_b[l].reshape(1, -1), spare)
    return h.reshape(bsz, seq, d)
```

```python
import functools

import numpy as np
import jax
import jax.numpy as jnp
from jax import lax
from jax.experimental import pallas as pl
from jax.experimental.pallas import tpu as pltpu

F32 = jnp.float32
BF16 = jnp.bfloat16

D_MODEL = 1024
DEPTH = 2
A_HEADS = 8
A_HEAD_DIM = 64
A_OUT = 512
IDX_HEADS = 4
IDX_DIM = 64
TOPK_MAX = 256
Q_BLOCK = 256
ROPE_THETA = 500000.0
ROT_DIM = 16
B_HEADS = 4
B_KEY_DIM = 64
B_VAL_DIM = 128
B_OUT = 512
B_GATE_RANK = 16
B_GATE_TAU = 16.0
B_CHUNK = 64
B_SUB = 16
N_SUB = B_CHUNK // B_SUB
C_GROUPS = 4
C_GROUP_DIM = 128
C_OUT = 512
C_CHUNK = 128
N_GROUPS = 4
EXPERTS_PER_GROUP = 8
N_EXPERTS = 32
D_EXPERT = 256
TOP_K_INNER = 2
DN_ALPHA = (2 * DEPTH) ** 0.25
LN_EPS = 1e-5
RMS_EPS = 1e-6

LANES = 128
NEG_BIG = -1e30
LOG2_E = 1.4426950408889634
INT_MIN = -2 ** 31
VMEM_LIMIT = 56 * 1024 * 1024


def _cparams(sem):
    return pltpu.CompilerParams(dimension_semantics=sem, vmem_limit_bytes=VMEM_LIMIT)


def _layer_norm(x, g, b):
    mu = jnp.mean(x, axis=-1, keepdims=True)
    xc = x - mu
    var = jnp.mean(xc * xc, axis=-1, keepdims=True)
    return xc * lax.rsqrt(var + LN_EPS) * g + b


def _dot(a, b):
    return jnp.dot(a, b, preferred_element_type=F32)


def _dot_nt(a, b):
    return lax.dot_general(a, b, (((1,), (1,)), ((), ())), preferred_element_type=F32)


def _ln_kernel(x_ref, g_ref, b_ref, o_ref):
    o_ref[...] = _layer_norm(x_ref[...], g_ref[...], b_ref[...])


def _entry_ln(x, g, b, tm=1024):
    n = x.shape[0]
    return pl.pallas_call(
        _ln_kernel,
        grid=(n // tm,),
        in_specs=[pl.BlockSpec((tm, D_MODEL), lambda i: (i, 0)),
                  pl.BlockSpec((1, D_MODEL), lambda i: (0, 0)),
                  pl.BlockSpec((1, D_MODEL), lambda i: (0, 0))],
        out_specs=pl.BlockSpec((tm, D_MODEL), lambda i: (i, 0)),
        out_shape=jax.ShapeDtypeStruct((n, D_MODEL), F32),
        compiler_params=_cparams(("parallel",)),
        name="entry_ln",
    )(x, g.reshape(1, -1), b.reshape(1, -1))


def _rope_slab(xs, c, s):
    lane = lax.broadcasted_iota(jnp.int32, xs.shape, 1) % A_HEAD_DIM
    fwd = pltpu.roll(xs, LANES - ROT_DIM // 2, axis=1)
    bwd = pltpu.roll(xs, ROT_DIM // 2, axis=1)
    partner = jnp.where(lane < ROT_DIM // 2, fwd, bwd)
    return xs * c + partner * s


def _proj_a_kernel(h_ref, w_ref, c_ref, s_ref, g_ref,
                   q_ref, k_ref, v_ref, iq_ref, ki_ref, wi_ref):
    p = _dot(h_ref[...].astype(BF16), w_ref[...])
    c = c_ref[...]
    s = s_ref[...]
    lane = lax.broadcasted_iota(jnp.int32, c.shape, 1)
    lo = lane < A_HEAD_DIM
    n_blk = p.shape[0] // Q_BLOCK

    def put_heads(dst_ref, slab, first_head):
        for half in range(2):
            hd = first_head + half
            cols = slab[:, half * A_HEAD_DIM:(half + 1) * A_HEAD_DIM].astype(BF16)
            for qb in range(n_blk):
                dst_ref[qb, hd * Q_BLOCK:(hd + 1) * Q_BLOCK, :] = cols[qb * Q_BLOCK:(qb + 1) * Q_BLOCK, :]

    for i in range(4):
        sl = slice(i * LANES, (i + 1) * LANES)
        put_heads(q_ref, _rope_slab(p[:, sl], c, s) * (A_HEAD_DIM ** -0.5 * LOG2_E), 2 * i)
    kv = p[:, 512:640]
    kv = _rope_slab(kv, jnp.where(lo, c, 1.0), jnp.where(lo, s, 0.0))
    k_ref[...] = kv[:, :A_HEAD_DIM].astype(BF16)
    v_ones = jnp.where(lo, pltpu.roll(kv, A_HEAD_DIM, axis=1), jnp.where(lane == A_HEAD_DIM, 1.0, 0.0))
    v_ref[...] = v_ones.astype(BF16)
    for i in range(2):
        sl = slice(640 + i * LANES, 640 + (i + 1) * LANES)
        put_heads(iq_ref, _rope_slab(p[:, sl], c, s), 2 * i)
    last = p[:, 896:1024]
    mu = jnp.sum(jnp.where(lo, last, 0.0), axis=-1, keepdims=True) * (1.0 / IDX_DIM)
    xc = jnp.where(lo, last - mu, 0.0)
    var = jnp.sum(xc * xc, axis=-1, keepdims=True) * (1.0 / IDX_DIM)
    kin = xc * lax.rsqrt(var + LN_EPS) * g_ref[...]
    kin = _rope_slab(kin, jnp.where(lo, c, 1.0), jnp.where(lo, s, 0.0))
    ki_ref[...] = kin[:, :IDX_DIM].astype(BF16)
    wi = pltpu.roll(last, LANES - IDX_DIM, axis=1) * (IDX_HEADS ** -0.5 * IDX_DIM ** -0.5)
    for qb in range(n_blk):
        wi_ref[qb] = wi[qb * Q_BLOCK:(qb + 1) * Q_BLOCK, :].T[0:8, :]


def _proj_a(h, w_a, cos_t, sin_t, ikg, tm=512):
    n = h.shape[0]
    row = lambda w: pl.BlockSpec((tm, w), lambda i: (i, 0))
    stacked = lambda heads: pl.BlockSpec((tm // Q_BLOCK, heads * Q_BLOCK, A_HEAD_DIM), lambda i: (i, 0, 0))
    return pl.pallas_call(
        _proj_a_kernel,
        grid=(n // tm,),
        in_specs=[row(D_MODEL),
                  pl.BlockSpec((D_MODEL, 1024), lambda i: (0, 0)),
                  row(LANES), row(LANES),
                  pl.BlockSpec((1, LANES), lambda i: (0, 0))],
        out_specs=[stacked(A_HEADS), row(A_HEAD_DIM),
                   row(LANES),
                   stacked(IDX_HEADS), row(IDX_DIM),
                   pl.BlockSpec((tm // Q_BLOCK, 8, Q_BLOCK), lambda i: (i, 0, 0))],
        out_shape=[jax.ShapeDtypeStruct((n // Q_BLOCK, A_HEADS * Q_BLOCK, A_HEAD_DIM), BF16),
                   jax.ShapeDtypeStruct((n, A_HEAD_DIM), BF16),
                   jax.ShapeDtypeStruct((n, LANES), BF16),
                   jax.ShapeDtypeStruct((n // Q_BLOCK, IDX_HEADS * Q_BLOCK, IDX_DIM), BF16),
                   jax.ShapeDtypeStruct((n, IDX_DIM), BF16),
                   jax.ShapeDtypeStruct((n // Q_BLOCK, 8, Q_BLOCK), F32)],
        compiler_params=_cparams(("parallel",)),
        name="proj_a",
    )(h, w_a, cos_t, sin_t, ikg)


KEY_CHUNK = 512
FIELD_BITS = 15
LOW_BITS = 32 - 2 * FIELD_BITS
FIELD_MASK = (1 << FIELD_BITS) - 1
GUARD_BITS = -2147450880
PAIR_ONE = 0x00010001


def _dsa_kernel(q_ref, iq_ref, wi_ref, k_ref, v_ref, ki_ref, o_ref,
                key_ref, pk_ref, bias_ref, *, seq):
    j = pl.program_id(1)
    n_sel = min(TOPK_MAX, seq // 4)
    n_kc = (j * Q_BLOCK + Q_BLOCK + KEY_CHUNK - 1) // KEY_CHUNK
    qpos = j * Q_BLOCK + lax.broadcasted_iota(jnp.int32, (1, Q_BLOCK), 1)
    kk = jnp.minimum(n_sel, qpos + 1).astype(F32)
    row_pos = lax.broadcasted_iota(jnp.int32, (KEY_CHUNK, 1), 0)

    def key_rows(c):
        return pl.ds(pl.multiple_of(c * KEY_CHUNK, KEY_CHUNK), KEY_CHUNK)

    def chunk_loop(body, init):
        def pair(i, carry):
            return body(2 * i + 1, body(2 * i, carry))
        carry = lax.fori_loop(0, n_kc // 2, pair, init)
        return lax.cond(n_kc % 2 == 1, lambda cr: body(n_kc - 1, cr), lambda cr: cr, carry)

    iq = iq_ref[...]
    wi = wi_ref[...]

    def score_chunk(c, carry):
        d = jnp.maximum(_dot_nt(ki_ref[key_rows(c), :], iq), 0.0)
        score = wi[0:1, :] * d[:, 0:Q_BLOCK]
        for h in range(1, IDX_HEADS):
            score = score + wi[h:h + 1, :] * d[:, h * Q_BLOCK:(h + 1) * Q_BLOCK]
        bits = pltpu.bitcast(score + 0.0, jnp.int32)
        key = bits ^ ((bits >> 31) & 0x7FFFFFFF)
        key = jnp.where(c * KEY_CHUNK + row_pos <= qpos, key, INT_MIN)
        key_ref[c] = key
        pk_ref[c] = pack(lax.shift_right_logical(key ^ INT_MIN, FIELD_BITS + LOW_BITS))
        return carry

    def pack(f):
        hf = KEY_CHUNK // 2
        return (f[:hf] << 16) | f[hf:] | GUARD_BITS

    chunk_loop(score_chunk, 0)

    def count(pred):
        def body(c, acc):
            m = jnp.where(pred(key_ref[c]), 1.0, 0.0)
            return acc + jnp.sum(m.reshape(KEY_CHUNK // 64, 64, Q_BLOCK), axis=0)
        acc = chunk_loop(body, jnp.zeros((64, Q_BLOCK), F32))
        return jnp.sum(acc, axis=0, keepdims=True)

    def count_packed(cand):
        both = (cand << 16) | cand

        def body(c, acc):
            flags = ((pk_ref[c] - both) >> 15) & PAIR_ONE
            parts = [flags[t * 64:(t + 1) * 64] for t in range(KEY_CHUNK // 128)]
            while len(parts) > 1:
                parts = [a + b for a, b in zip(parts[::2], parts[1::2])]
            return acc + parts[0]

        acc = chunk_loop(body, jnp.zeros((64, Q_BLOCK), jnp.int32))
        return jnp.sum(((acc & 0xFFFF) + (acc >> 16)).astype(F32), axis=0, keepdims=True)

    def search_field(target):
        def step(i, t):
            cand = t | jnp.left_shift(jnp.int32(1), FIELD_BITS - 1 - i)
            return jnp.where(count_packed(cand) >= target, cand, t)
        return lax.fori_loop(0, FIELD_BITS, step, jnp.zeros((1, Q_BLOCK), jnp.int32))

    t_a = search_field(kk)
    top_of_a = ((t_a << (FIELD_BITS + LOW_BITS)) | ((1 << (FIELD_BITS + LOW_BITS)) - 1)) ^ INT_MIN
    need_b = kk - count(lambda kc: kc > top_of_a)

    def repack(c, carry):
        u = key_ref[c] ^ INT_MIN
        member = lax.shift_right_logical(u, FIELD_BITS + LOW_BITS) == t_a
        pk_ref[c] = pack(jnp.where(member, lax.shift_right_logical(u, LOW_BITS) & FIELD_MASK, 0))
        return carry

    chunk_loop(repack, 0)
    base = (t_a << (FIELD_BITS + LOW_BITS)) | (search_field(need_b) << LOW_BITS)

    def low_step(i, t_c):
        bit = jnp.left_shift(jnp.int32(1), LOW_BITS - 1 - i)
        cand_s = (base | t_c | bit) ^ INT_MIN
        return jnp.where(count(lambda kc: kc >= cand_s) >= kk, t_c | bit, t_c)

    thr = (base | lax.fori_loop(0, LOW_BITS, low_step, jnp.zeros((1, Q_BLOCK), jnp.int32))) ^ INT_MIN
    need = kk - count(lambda kc: kc > thr)
    r_i = lax.broadcasted_iota(jnp.int32, (LANES, LANES), 0)
    c_i = lax.broadcasted_iota(jnp.int32, (LANES, LANES), 1)
    lower = jnp.where(c_i < r_i, 1.0, 0.0).astype(BF16)

    def fill(c, carry):
        for t in range(KEY_CHUNK // LANES):
            rows = slice(t * LANES, (t + 1) * LANES)
            kt = key_ref[c, rows, :]
            eq = kt == thr
            eq_f = jnp.where(eq, 1.0, 0.0)
            pre = _dot(lower, eq_f.astype(BF16)) + carry
            sel = (kt > thr) | (eq & (pre < need))
            bias_ref[c, rows, :] = jnp.where(sel, 0.0, NEG_BIG)
            carry = carry + jnp.sum(eq_f, axis=0, keepdims=True)
        return carry

    chunk_loop(fill, jnp.zeros((1, Q_BLOCK), F32))

    q = q_ref[...]
    cols = A_HEADS * Q_BLOCK

    def attn_chunk(c, carry):
        m, acc = carry
        rows = key_rows(c)
        lg = _dot_nt(k_ref[rows, :], q) + jnp.concatenate([bias_ref[c]] * A_HEADS, axis=1)
        m_new = jnp.maximum(m, jnp.max(lg, axis=0, keepdims=True))
        p = jnp.exp2(lg - m_new).astype(BF16)
        pv = lax.dot_general(v_ref[rows, :], p, (((0,), (0,)), ((), ())),
                             preferred_element_type=F32)
        return m_new, jnp.exp2(m - m_new) * acc + pv

    init = (jnp.full((1, cols), -3e38, F32), jnp.zeros((LANES, cols), F32))
    _, acc = chunk_loop(attn_chunk, init)
    o = acc[:A_HEAD_DIM] / acc[A_HEAD_DIM:A_HEAD_DIM + 1]
    for h in range(A_HEADS):
        o_ref[:, h * A_HEAD_DIM:(h + 1) * A_HEAD_DIM] = (
            o[:, h * Q_BLOCK:(h + 1) * Q_BLOCK].T.astype(BF16))


def _dsa(q, iq, wi, k, v, ki, bsz, seq):
    nq = seq // Q_BLOCK
    n_kc = seq // KEY_CHUNK
    stacked = lambda heads: pl.BlockSpec((None, heads * Q_BLOCK, A_HEAD_DIM),
                                         lambda b, j: (b * nq + j, 0, 0))
    kspec = lambda w: pl.BlockSpec((seq, w), lambda b, j: (b, 0))
    return pl.pallas_call(
        functools.partial(_dsa_kernel, seq=seq),
        grid=(bsz, nq),
        in_specs=[stacked(A_HEADS), stacked(IDX_HEADS),
                  pl.BlockSpec((None, 8, Q_BLOCK), lambda b, j: (b * nq + j, 0, 0)),
                  kspec(A_HEAD_DIM),
                  kspec(LANES),
                  kspec(IDX_DIM)],
        out_specs=pl.BlockSpec((Q_BLOCK, A_OUT), lambda b, j: (b * nq + j, 0)),
        out_shape=jax.ShapeDtypeStruct((bsz * seq, A_OUT), BF16),
        scratch_shapes=[pltpu.VMEM((n_kc, KEY_CHUNK, Q_BLOCK), jnp.int32),
                        pltpu.VMEM((n_kc, KEY_CHUNK // 2, Q_BLOCK), jnp.int32),
                        pltpu.VMEM((n_kc, KEY_CHUNK, Q_BLOCK), F32)],
        compiler_params=_cparams(("parallel", "parallel")),
        name="dsa",
    )(q, iq, wi, k, v, ki)


GLA_QK = B_HEADS * B_KEY_DIM
GLA_REP = N_SUB * GLA_QK
GLA_COLS = 2 * GLA_QK + 2 * B_OUT + LANES
GLA_V0 = 2 * GLA_QK
GLA_GROUP = 8


def _gla_kernel(h_ref, w_ref, wa_ref, ba_ref, ng_ref, o_ref, p_ref, g_ref, acc_ref, st_ref, *, ts):
    @pl.when(pl.program_id(1) == 0)
    def _():
        st_ref[...] = jnp.zeros_like(st_ref)

    p_ref[...] = _dot(h_ref[...].astype(BF16), w_ref[...])
    x = _dot(p_ref[:, GLA_V0 + 2 * B_OUT:].astype(BF16), wa_ref[...]) + ba_ref[...]
    g_ref[...] = (jnp.minimum(x, 0.0) - jnp.log1p(jnp.exp(-jnp.abs(x)))) * (1.0 / B_GATE_TAU)

    t_i = lax.broadcasted_iota(jnp.int32, (B_CHUNK, B_CHUNK), 0)
    s_i = lax.broadcasted_iota(jnp.int32, (B_CHUNK, B_CHUNK), 1)
    tri = t_i >= s_i
    tri_b = jnp.where(tri, 1.0, 0.0).astype(BF16)
    lane_sub = (lax.broadcasted_iota(jnp.int32, (1, GLA_REP), 1) // B_KEY_DIM) % N_SUB
    row_sub = lax.broadcasted_iota(jnp.int32, (B_CHUNK, 1), 0) // B_SUB
    q_mask = lane_sub <= row_sub
    k_mask = lane_sub == row_sub
    low_half = lax.broadcasted_iota(jnp.int32, (B_CHUNK, LANES), 1) < B_KEY_DIM

    def replicate(x):
        outs = []
        for pair in range(GLA_QK // LANES):
            v = x[:, pair * LANES:(pair + 1) * LANES]
            r = pltpu.roll(v, B_KEY_DIM, axis=1)
            outs += [jnp.where(low_half, v, r)] * (N_SUB // 2) + [jnp.where(low_half, r, v)] * (N_SUB // 2)
        return jnp.concatenate(outs, axis=1)

    def prepare(c):
        rows = pl.ds(pl.multiple_of(c * B_CHUNK, B_CHUNK), B_CHUNK)
        g = g_ref[rows, :]
        g_hi = g.astype(BF16)
        g_mid = (g - g_hi.astype(F32)).astype(BF16)
        g_lo = (g - g_hi.astype(F32) - g_mid.astype(F32)).astype(BF16)
        b = replicate(_dot(tri_b, g_hi) + _dot(tri_b, g_mid) + _dot(tri_b, g_lo))
        ref = jnp.zeros((1, GLA_REP), F32)
        for jj in range(1, N_SUB):
            ref = ref + jnp.where(lane_sub == jj, b[jj * B_SUB - 1:jj * B_SUB, :], 0.0)
        d = b - ref
        e_q = jnp.where(q_mask, jnp.exp(jnp.minimum(d, 0.0)), 0.0)
        e_k = jnp.where(k_mask, jnp.exp(jnp.where(k_mask, -d, 0.0)), 0.0)
        b_last = b[B_CHUNK - 1:B_CHUNK, :]
        q_c = (replicate(p_ref[rows, 0:GLA_QK]) * e_q).astype(BF16)
        k_raw = replicate(p_ref[rows, GLA_QK:2 * GLA_QK])
        k_c = (k_raw * e_k).astype(BF16)
        k_st = (k_raw * jnp.exp(b_last - b)).astype(BF16)
        return rows, q_c, k_c, k_st, jnp.exp(b_last)

    def chunk_group(i, carry):
        prep = [prepare(GLA_GROUP * i + u) for u in range(GLA_GROUP)]
        att, upd, val = {}, {}, {}
        for u, (rows, q_c, k_c, k_st, _) in enumerate(prep):
            for hh in range(B_HEADS):
                base = hh * N_SUB * B_KEY_DIM
                wide = slice(base, base + N_SUB * B_KEY_DIM)
                val[u, hh] = p_ref[rows, GLA_V0 + hh * B_VAL_DIM:GLA_V0 + (hh + 1) * B_VAL_DIM].astype(BF16)
                att[u, hh] = jnp.where(tri, _dot_nt(q_c[:, wide], k_c[:, wide]), 0.0).astype(BF16)
                upd[u, hh] = _dot(val[u, hh].T, k_st[:, base:base + B_KEY_DIM])
        for hh in range(B_HEADS):
            base = hh * N_SUB * B_KEY_DIM
            st = st_ref[hh]
            for u, (rows, q_c, _, _, e_last) in enumerate(prep):
                o = _dot(att[u, hh], val[u, hh]) + _dot_nt(q_c[:, base:base + B_KEY_DIM], st.astype(BF16))
                acc_ref[rows, hh * B_VAL_DIM:(hh + 1) * B_VAL_DIM] = o
                st = st * e_last[:, base:base + B_KEY_DIM] + upd[u, hh]
            st_ref[hh] = st
        return carry

    lax.fori_loop(0, ts // (GLA_GROUP * B_CHUNK), chunk_group, 0)

    r = p_ref[:, GLA_V0 + B_OUT:GLA_V0 + 2 * B_OUT]
    gate = r * (1.0 / (1.0 + jnp.exp(-r)))
    ng = ng_ref[...]
    for hh in range(B_HEADS):
        sl = slice(hh * B_VAL_DIM, (hh + 1) * B_VAL_DIM)
        o = acc_ref[:, sl]
        y = o * lax.rsqrt(jnp.mean(o * o, axis=-1, keepdims=True) + RMS_EPS) * ng
        o_ref[:, sl] = (gate[:, sl] * y).astype(BF16)


def _gla(h, w_b, wa, ba, norm_g, bsz, seq, ts=512):
    ns = seq // ts
    const = lambda shape: pl.BlockSpec(shape, lambda b, i: (0, 0))
    return pl.pallas_call(
        functools.partial(_gla_kernel, ts=ts),
        grid=(bsz, ns),
        in_specs=[pl.BlockSpec((ts, D_MODEL), lambda b, i: (b * ns + i, 0)),
                  const((D_MODEL, GLA_COLS)), const((LANES, GLA_QK)),
                  const((1, GLA_QK)), const((1, B_VAL_DIM))],
        out_specs=pl.BlockSpec((ts, B_OUT), lambda b, i: (b * ns + i, 0)),
        out_shape=jax.ShapeDtypeStruct((bsz * seq, B_OUT), BF16),
        scratch_shapes=[pltpu.VMEM((ts, GLA_COLS), F32),
                        pltpu.VMEM((ts, GLA_QK), F32),
                        pltpu.VMEM((ts, B_OUT), F32),
                        pltpu.VMEM((B_HEADS, B_VAL_DIM, B_KEY_DIM), F32)],
        compiler_params=_cparams(("parallel", "arbitrary")),
        name="gla",
    )(h, w_b, wa, ba, norm_g)


def _gmlp_kernel(h_ref, w_ref, lg_ref, lb_ref, ws_ref, bs_ref, o_ref, *, tm):
    z = _dot(h_ref[...].astype(BF16), w_ref[...])
    z = z * (0.5 * (1.0 + jnp.tanh(np.sqrt(2.0 / np.pi) * (z + 0.044715 * (z * z * z)))))
    u = z[:, :C_OUT]
    v = _layer_norm(z[:, C_OUT:], lg_ref[...], lb_ref[...]).astype(BF16)
    t_i = lax.broadcasted_iota(jnp.int32, (C_CHUNK, C_CHUNK), 0)
    s_i = lax.broadcasted_iota(jnp.int32, (C_CHUNK, C_CHUNK), 1)
    bs = bs_ref[...]
    for g in range(C_GROUPS):
        w = jnp.where(t_i >= s_i, ws_ref[g], 0.0).astype(BF16)
        sl = slice(g * C_GROUP_DIM, (g + 1) * C_GROUP_DIM)
        for c in range(tm // C_CHUNK):
            rows = slice(c * C_CHUNK, (c + 1) * C_CHUNK)
            mixed = _dot(w, v[rows, sl]) + bs[:, g:g + 1]
            o_ref[rows, sl] = (u[rows, sl] * mixed).astype(BF16)


def _gmlp(h, w_c, ln_g, ln_b, ws, bs_t, tm=512):
    n = h.shape[0]
    return pl.pallas_call(
        functools.partial(_gmlp_kernel, tm=tm),
        grid=(n // tm,),
        in_specs=[pl.BlockSpec((tm, D_MODEL), lambda i: (i, 0)),
                  pl.BlockSpec((D_MODEL, 2 * C_OUT), lambda i: (0, 0)),
                  pl.BlockSpec((1, C_OUT), lambda i: (0, 0)),
                  pl.BlockSpec((1, C_OUT), lambda i: (0, 0)),
                  pl.BlockSpec((C_GROUPS, C_CHUNK, C_CHUNK), lambda i: (0, 0, 0)),
                  pl.BlockSpec((C_CHUNK, LANES), lambda i: (0, 0))],
        out_specs=pl.BlockSpec((tm, C_OUT), lambda i: (i, 0)),
        out_shape=jax.ShapeDtypeStruct((n, C_OUT), BF16),
        compiler_params=_cparams(("parallel",)),
        name="gmlp",
    )(h, w_c, ln_g, ln_b, ws, bs_t)


def _sigmoid(x):
    return 1.0 / (1.0 + jnp.exp(-x))


def _merge_kernel(h_ref, ya_ref, yb_ref, yc_ref, wg_ref, wa_ref, wb_ref, wc_ref, wo_ref,
                  lg_ref, lb_ref, wr_ref, wrl_ref, br_ref, x_ref, oh_ref, route_ref):
    h = h_ref[...]
    hb = h.astype(BF16)
    acc = None
    for i, (y_ref, wbr_ref) in enumerate(((ya_ref, wa_ref), (yb_ref, wb_ref), (yc_ref, wc_ref))):
        gate = _sigmoid(_dot(hb, wg_ref[:, i * D_MODEL:(i + 1) * D_MODEL]))
        term = gate * _dot(y_ref[...], wbr_ref[...])
        acc = term if acc is None else acc + term
    mix = _dot(acc.astype(BF16), wo_ref[...])
    x1 = _layer_norm(DN_ALPHA * h + mix, lg_ref[...], lb_ref[...])
    x_ref[...] = x1

    x_hi = x1.astype(BF16)
    x_lo = (x1 - x_hi.astype(F32)).astype(BF16)
    logit = (_dot(x_hi, wr_ref[...]) + _dot(x_hi, wrl_ref[...]) + _dot(x_lo, wr_ref[...])) + br_ref[...]
    lane = lax.broadcasted_iota(jnp.int32, logit.shape, 1)
    big = jnp.int32(LANES)
    is_g = (lane >= N_EXPERTS) & (lane < N_EXPERTS + N_GROUPS)
    gl = jnp.where(is_g, logit, -jnp.inf)
    g_max = jnp.max(gl, axis=-1, keepdims=True)
    g_idx = jnp.min(jnp.where(is_g & (gl == g_max), lane, big), axis=-1, keepdims=True) - N_EXPERTS
    g_p = 1.0 / jnp.sum(jnp.where(is_g, jnp.exp(gl - g_max), 0.0), axis=-1, keepdims=True)
    lo = g_idx * EXPERTS_PER_GROUP
    in_g = (lane >= lo) & (lane < lo + EXPERTS_PER_GROUP)
    el = jnp.where(in_g, logit, -jnp.inf)
    e1 = jnp.max(el, axis=-1, keepdims=True)
    i1 = jnp.min(jnp.where(in_g & (el == e1), lane, big), axis=-1, keepdims=True)
    el2 = jnp.where(lane == i1, -jnp.inf, el)
    e2 = jnp.max(el2, axis=-1, keepdims=True)
    i2 = jnp.min(jnp.where(in_g & (lane != i1) & (el2 == e2), lane, big), axis=-1, keepdims=True)
    z2 = jnp.exp(e2 - e1)
    den = 1.0 + z2
    oh_ref[...] = jnp.where((lane == i1) | (lane == i2), 1.0, 0.0).astype(BF16)
    route_ref[...] = jnp.where(lane == 0, i1.astype(F32),
                               jnp.where(lane == 1, i2.astype(F32),
                                         jnp.where(lane == 2, (1.0 / den) * g_p,
                                                   jnp.where(lane == 3, (z2 / den) * g_p, 0.0))))


def _merge(h, ya, yb, yc, wg, wa, wb, wc, wo, ln_g, ln_b, wr, wr_lo, br, tm=1024):
    n = h.shape[0]
    row = lambda w: pl.BlockSpec((tm, w), lambda i: (i, 0))
    const = lambda r, c: pl.BlockSpec((r, c), lambda i: (0, 0), pipeline_mode=pl.Buffered(1))
    return pl.pallas_call(
        _merge_kernel,
        grid=(n // tm,),
        in_specs=[row(D_MODEL), row(A_OUT), row(B_OUT), row(C_OUT),
                  const(D_MODEL, 3 * D_MODEL), const(A_OUT, D_MODEL), const(B_OUT, D_MODEL),
                  const(C_OUT, D_MODEL), const(D_MODEL, D_MODEL),
                  const(1, D_MODEL), const(1, D_MODEL), const(D_MODEL, LANES), const(D_MODEL, LANES),
                  const(1, LANES)],
        out_specs=[row(D_MODEL), row(LANES), row(LANES)],
        out_shape=[jax.ShapeDtypeStruct((n, D_MODEL), F32),
                   jax.ShapeDtypeStruct((n, LANES), BF16),
                   jax.ShapeDtypeStruct((n, LANES), F32)],
        compiler_params=_cparams(("parallel",)),
        name="merge",
    )(h, ya, yb, yc, wg, wa, wb, wc, wo, ln_g, ln_b, wr, wr_lo, br)


FFN_TILE = 512
MOE_TOK = 512


def _rank_kernel(oh_ref, route_ref, r_ref, cnt_ref, carry_ref):
    @pl.when(pl.program_id(0) == 0)
    def _():
        carry_ref[...] = jnp.zeros_like(carry_ref)

    oh = oh_ref[...]
    tt = oh.shape[0]
    r_i = lax.broadcasted_iota(jnp.int32, (tt, tt), 0)
    c_i = lax.broadcasted_iota(jnp.int32, (tt, tt), 1)
    lower = jnp.where(c_i < r_i, 1.0, 0.0).astype(BF16)
    rank = _dot(lower, oh) + carry_ref[...]
    route = route_ref[...]
    lane = lax.broadcasted_iota(jnp.int32, rank.shape, 1)
    i1 = route[:, 0:1].astype(jnp.int32)
    i2 = route[:, 1:2].astype(jnp.int32)
    r1 = jnp.sum(jnp.where(lane == i1, rank, 0.0), axis=-1, keepdims=True)
    r2 = jnp.sum(jnp.where(lane == i2, rank, 0.0), axis=-1, keepdims=True)
    r_ref[...] = jnp.where(lane == 0, r1, jnp.where(lane == 1, r2, 0.0))
    carry_ref[...] += jnp.sum(oh.astype(F32), axis=0, keepdims=True)
    cnt_ref[...] = carry_ref[...]


def _rank(oh, route, tt=512):
    n = oh.shape[0]
    return pl.pallas_call(
        _rank_kernel,
        grid=(n // tt,),
        in_specs=[pl.BlockSpec((tt, LANES), lambda i: (i, 0)),
                  pl.BlockSpec((tt, LANES), lambda i: (i, 0))],
        out_specs=[pl.BlockSpec((tt, LANES), lambda i: (i, 0)),
                   pl.BlockSpec((1, LANES), lambda i: (0, 0))],
        out_shape=[jax.ShapeDtypeStruct((n, LANES), F32),
                   jax.ShapeDtypeStruct((1, LANES), F32)],
        scratch_shapes=[pltpu.VMEM((1, LANES), F32)],
        compiler_params=_cparams(("arbitrary",)),
        name="moe_rank",
    )(oh, route)


def _row_copy(src_ref, src_row, dst_ref, dst_row, sem):
    return pltpu.make_async_copy(src_ref.at[pl.ds(src_row, 1), :], dst_ref.at[pl.ds(dst_row, 1), :], sem)


def _dispatch_kernel(pos1_ref, pos2_ref, x_ref, init_ref, xs_ref, sem):
    del init_ref
    base = pl.program_id(0) * MOE_TOK

    def issue(t, carry):
        _row_copy(x_ref, t, xs_ref, pos1_ref[base + t], sem).start()
        _row_copy(x_ref, t, xs_ref, pos2_ref[base + t], sem).start()
        return carry

    lax.fori_loop(0, MOE_TOK, issue, 0, unroll=32)
    for _ in range(2):
        pltpu.make_async_copy(x_ref, xs_ref.at[pl.ds(0, MOE_TOK), :], sem).wait()


def _dispatch(pos1, pos2, x1, n_slots, spare):
    n = x1.shape[0]
    init = jnp.zeros((n_slots, D_MODEL), F32) if spare is None else spare
    return pl.pallas_call(
        _dispatch_kernel,
        grid_spec=pltpu.PrefetchScalarGridSpec(
            num_scalar_prefetch=2,
            grid=(n // MOE_TOK,),
            in_specs=[pl.BlockSpec((MOE_TOK, D_MODEL), lambda i, p1, p2: (i, 0)),
                      pl.BlockSpec(memory_space=pl.ANY)],
            out_specs=pl.BlockSpec(memory_space=pl.ANY),
            scratch_shapes=[pltpu.SemaphoreType.DMA(())]),
        out_shape=jax.ShapeDtypeStruct((n_slots, D_MODEL), F32),
        input_output_aliases={3: 0},
        compiler_params=_cparams(("arbitrary",)),
        name="moe_dispatch",
    )(pos1, pos2, x1, init)


def _ffn_kernel(te_ref, nu_ref, xs_ref, wg_ref, wu_ref, wd_ref, ys_ref):
    del te_ref
    i = pl.program_id(0)

    @pl.when(i < nu_ref[0])
    def _():
        xb = xs_ref[...].astype(BF16)
        gte = _dot(xb, wg_ref[...].astype(BF16))
        hid = gte * _sigmoid(gte) * _dot(xb, wu_ref[...].astype(BF16))
        ys_ref[...] = _dot(hid.astype(BF16), wd_ref[...].astype(BF16))

    @pl.when(i >= nu_ref[0])
    def _():
        ys_ref[...] = jnp.zeros_like(ys_ref)


def _ffn(tile_expert, n_used, xs, w_gate, w_up, w_down, layer):
    n_slots = xs.shape[0]
    wspec = lambda r, c: pl.BlockSpec((None, None, r, c), lambda i, te, nu: (layer, te[i], 0, 0))
    return pl.pallas_call(
        _ffn_kernel,
        grid_spec=pltpu.PrefetchScalarGridSpec(
            num_scalar_prefetch=2,
            grid=(n_slots // FFN_TILE,),
            in_specs=[pl.BlockSpec((FFN_TILE, D_MODEL), lambda i, te, nu: (i, 0)),
                      wspec(D_MODEL, D_EXPERT), wspec(D_MODEL, D_EXPERT), wspec(D_EXPERT, D_MODEL)],
            out_specs=pl.BlockSpec((FFN_TILE, D_MODEL), lambda i, te, nu: (i, 0))),
        out_shape=jax.ShapeDtypeStruct((n_slots, D_MODEL), F32),
        compiler_params=_cparams(("arbitrary",)),
        name="moe_ffn",
    )(tile_expert, n_used, xs, w_gate, w_up, w_down)


def _combine_kernel(pos1_ref, pos2_ref, x_ref, route_ref, lg_ref, lb_ref, ys_ref, o_ref, buf_ref, sem):
    i = pl.program_id(0)

    def gather(step, slot):
        base = step * MOE_TOK

        def issue(t, carry):
            _row_copy(ys_ref, pos1_ref[base + t], buf_ref.at[slot, 0], t, sem.at[slot]).start()
            _row_copy(ys_ref, pos2_ref[base + t], buf_ref.at[slot, 1], t, sem.at[slot]).start()
            return carry

        lax.fori_loop(0, MOE_TOK, issue, 0, unroll=32)

    @pl.when(i == 0)
    def _():
        gather(0, 0)

    @pl.when(i + 1 < pl.num_programs(0))
    def _():
        gather(i + 1, (i + 1) % 2)

    slot = i % 2
    for s in range(2):
        pltpu.make_async_copy(ys_ref.at[pl.ds(0, MOE_TOK), :], buf_ref.at[slot, s], sem.at[slot]).wait()
    route = route_ref[...]
    ffn = route[:, 2:3] * buf_ref[slot, 0] + route[:, 3:4] * buf_ref[slot, 1]
    o_ref[...] = _layer_norm(DN_ALPHA * x_ref[...] + ffn, lg_ref[...], lb_ref[...])


def _combine(pos1, pos2, x1, route, ln_g, ln_b, ys):
    n = x1.shape[0]
    row = lambda w: pl.BlockSpec((MOE_TOK, w), lambda i, p1, p2: (i, 0))
    const = lambda w: pl.BlockSpec((1, w), lambda i, p1, p2: (0, 0))
    return pl.pallas_call(
        _combine_kernel,
        grid_spec=pltpu.PrefetchScalarGridSpec(
            num_scalar_prefetch=2,
            grid=(n // MOE_TOK,),
            in_specs=[row(D_MODEL), row(LANES), const(D_MODEL), const(D_MODEL),
                      pl.BlockSpec(memory_space=pl.ANY)],
            out_specs=row(D_MODEL),
            scratch_shapes=[pltpu.VMEM((2, 2, MOE_TOK, D_MODEL), F32),
                            pltpu.SemaphoreType.DMA((2,))]),
        out_shape=jax.ShapeDtypeStruct((n, D_MODEL), F32),
        compiler_params=_cparams(("arbitrary",)),
        name="moe_combine",
    )(pos1, pos2, x1, route, ln_g, ln_b, ys)


def _moe(x1, oh, route, w_gate, w_up, w_down, layer, ln_g, ln_b, spare):
    n = x1.shape[0]
    n_slots = TOP_K_INNER * n + N_EXPERTS * FFN_TILE
    rank, cnt = _rank(oh, route)
    cnt = cnt[0, :N_EXPERTS].astype(jnp.int32)
    padded = (cnt + FFN_TILE - 1) // FFN_TILE * FFN_TILE
    ends = jnp.cumsum(padded)
    offs = ends - padded
    experts = jnp.arange(N_EXPERTS, dtype=jnp.int32)
    i1 = route[:, 0].astype(jnp.int32)
    i2 = route[:, 1].astype(jnp.int32)
    off_of = lambda idx: jnp.sum(jnp.where(idx[:, None] == experts[None, :], offs[None, :], 0), axis=1)
    pos1 = off_of(i1) + rank[:, 0].astype(jnp.int32)
    pos2 = off_of(i2) + rank[:, 1].astype(jnp.int32)
    tile_start = jnp.arange(n_slots // FFN_TILE, dtype=jnp.int32) * FFN_TILE
    tile_expert = jnp.minimum(jnp.sum(tile_start[:, None] >= ends[None, :], axis=1), N_EXPERTS - 1)
    n_used = (ends[-1] // FFN_TILE).reshape(1)

    xs = _dispatch(pos1, pos2, x1, n_slots, spare)
    ys = _ffn(tile_expert.astype(jnp.int32), n_used.astype(jnp.int32), xs, w_gate, w_up, w_down, layer)
    return _combine(pos1, pos2, x1, route, ln_g, ln_b, ys), ys


def _rope_tables(positions):
    inv_freq = ROPE_THETA ** (-jnp.arange(0, ROT_DIM, 2, dtype=F32) / ROT_DIM)
    ang = positions.astype(F32).reshape(-1, 1) * inv_freq
    cos, sin = jnp.cos(ang), jnp.sin(ang)
    n = ang.shape[0]
    half = ROT_DIM // 2
    c64 = jnp.concatenate([cos, cos, jnp.ones((n, A_HEAD_DIM - ROT_DIM), F32)], axis=1)
    s64 = jnp.concatenate([-sin, sin, jnp.zeros((n, A_HEAD_DIM - ROT_DIM), F32)], axis=1)
    del half
    return jnp.tile(c64, (1, 2)), jnp.tile(s64, (1, 2))


def _split_w_in(w):
    widths = (A_OUT, A_HEAD_DIM, A_HEAD_DIM, IDX_HEADS * IDX_DIM, IDX_DIM, IDX_HEADS,
              B_HEADS * B_KEY_DIM, B_HEADS * B_KEY_DIM, B_OUT, B_GATE_RANK, B_OUT,
              2 * C_OUT, D_MODEL, D_MODEL, D_MODEL)
    pts = np.cumsum((0,) + widths)
    return [w[:, int(pts[i]):int(pts[i + 1])] for i in range(len(widths))]


def kernel(x, positions, ln_in_g, ln_in_b, w_in, idx_k_g, gla_wa2, gla_ba, gla_norm_g, gm_ln_g, gm_ln_b, gm_ws, gm_bs, w_branch_a, w_branch_b, w_branch_c, w_out, ln1_g, ln1_b, w_rg, b_rg, w_re, b_re, w_gate, w_up, w_down, ln2_g, ln2_b):
    bsz, seq, d = x.shape
    n = bsz * seq
    cos_t, sin_t = _rope_tables(positions)
    h = _entry_ln(x.reshape(n, d), ln_in_g, ln_in_b)
    spare = None
    for l in range(DEPTH):
        (a_q, a_k, a_v, i_q, i_k, i_w, b_q, b_k, b_v, b_glr, b_r, c_uv,
         g_a, g_b, g_c) = _split_w_in(w_in[l])
        zpad = lambda c: jnp.zeros((d, c), F32)
        w_a = jnp.concatenate([a_q, a_k, a_v, i_q, i_k, i_w, zpad(LANES - IDX_DIM - IDX_HEADS)],
                              axis=1).astype(BF16)
        w_b = jnp.concatenate([b_q * (B_KEY_DIM ** -0.5), b_k, b_v, b_r,
                               b_glr, zpad(LANES - B_GATE_RANK)], axis=1).astype(BF16)
        wa_pad = jnp.concatenate([gla_wa2[l],
                                  jnp.zeros((LANES - B_GATE_RANK, GLA_QK), F32)], axis=0).astype(BF16)
        w_g = jnp.concatenate([g_a, g_b, g_c], axis=1).astype(BF16)
        ikg = jnp.concatenate([idx_k_g[l], jnp.zeros((LANES - IDX_DIM,), F32)]).reshape(1, LANES)
        bs_t = jnp.concatenate([gm_bs[l].T, jnp.zeros((C_CHUNK, LANES - C_GROUPS), F32)], axis=1)
        w_r = jnp.concatenate([w_re[l], w_rg[l], zpad(LANES - N_GROUPS - N_EXPERTS)], axis=1)
        w_r_hi = w_r.astype(BF16)
        w_r_lo = (w_r - w_r_hi.astype(F32)).astype(BF16)
        b_r_all = jnp.concatenate([b_re[l], b_rg[l],
                                   jnp.zeros((LANES - N_GROUPS - N_EXPERTS,), F32)]).reshape(1, LANES)

        q, k, v, iq, ki, wi = _proj_a(h, w_a, cos_t, sin_t, ikg)
        y_a = _dsa(q, iq, wi, k, v, ki, bsz, seq)
        y_b = _gla(h, w_b, wa_pad, gla_ba[l].reshape(1, -1), gla_norm_g[l].reshape(1, -1), bsz, seq)
        y_c = _gmlp(h, c_uv.astype(BF16), gm_ln_g[l].reshape(1, -1), gm_ln_b[l].reshape(1, -1),
                    gm_ws[l], bs_t)
        x1, oh, route = _merge(h, y_a, y_b, y_c, w_g,
                               w_branch_a[l].astype(BF16), w_branch_b[l].astype(BF16),
                               w_branch_c[l].astype(BF16), w_out[l].astype(BF16),
                               ln1_g[l].reshape(1, -1), ln1_b[l].reshape(1, -1), w_r_hi, w_r_lo, b_r_all)
        h, spare = _moe(x1, oh, route, w_gate, w_up, w_down, l,
                        ln2_g[l].reshape(1, -1), ln2_b[l].reshape(1, -1), spare)
    return h.reshape(bsz, seq, d)
```

```python
import functools

import numpy as np
import jax
import jax.numpy as jnp
from jax import lax
from jax.experimental import pallas as pl
from jax.experimental.pallas import tpu as pltpu

F32 = jnp.float32
BF16 = jnp.bfloat16

D_MODEL = 1024
DEPTH = 2
A_HEADS = 8
A_HEAD_DIM = 64
A_OUT = 512
IDX_HEADS = 4
IDX_DIM = 64
TOPK_MAX = 256
Q_BLOCK = 256
ROPE_THETA = 500000.0
ROT_DIM = 16
B_HEADS = 4
B_KEY_DIM = 64
B_VAL_DIM = 128
B_OUT = 512
B_GATE_RANK = 16
B_GATE_TAU = 16.0
B_CHUNK = 64
B_SUB = 16
N_SUB = B_CHUNK // B_SUB
C_GROUPS = 4
C_GROUP_DIM = 128
C_OUT = 512
C_CHUNK = 128
N_GROUPS = 4
EXPERTS_PER_GROUP = 8
N_EXPERTS = 32
D_EXPERT = 256
TOP_K_INNER = 2
DN_ALPHA = (2 * DEPTH) ** 0.25
LN_EPS = 1e-5
RMS_EPS = 1e-6

LANES = 128
SUBLANES = 8
NEG_BIG = -1e30
LOG2_E = 1.4426950408889634
INT_MIN = -2 ** 31
VMEM_LIMIT = 56 * 1024 * 1024


def _cparams(sem):
    return pltpu.CompilerParams(dimension_semantics=sem, vmem_limit_bytes=VMEM_LIMIT)


def _layer_norm(x, g, b):
    mu = jnp.mean(x, axis=-1, keepdims=True)
    xc = x - mu
    var = jnp.mean(xc * xc, axis=-1, keepdims=True)
    return xc * lax.rsqrt(var + LN_EPS) * g + b


def _dot(a, b):
    return jnp.dot(a, b, preferred_element_type=F32)


def _dot_nt(a, b):
    return lax.dot_general(a, b, (((1,), (1,)), ((), ())), preferred_element_type=F32)


def _ln_kernel(x_ref, g_ref, b_ref, o_ref):
    o_ref[...] = _layer_norm(x_ref[...], g_ref[...], b_ref[...])


def _entry_ln(x, g, b, tm=1024):
    n = x.shape[0]
    return pl.pallas_call(
        _ln_kernel,
        grid=(n // tm,),
        in_specs=[pl.BlockSpec((tm, D_MODEL), lambda i: (i, 0)),
                  pl.BlockSpec((1, D_MODEL), lambda i: (0, 0)),
                  pl.BlockSpec((1, D_MODEL), lambda i: (0, 0))],
        out_specs=pl.BlockSpec((tm, D_MODEL), lambda i: (i, 0)),
        out_shape=jax.ShapeDtypeStruct((n, D_MODEL), F32),
        compiler_params=_cparams(("parallel",)),
        name="entry_ln",
    )(x, g.reshape(1, -1), b.reshape(1, -1))


def _rope_slab(xs, c, s):
    lane = lax.broadcasted_iota(jnp.int32, xs.shape, 1) % A_HEAD_DIM
    fwd = pltpu.roll(xs, LANES - ROT_DIM // 2, axis=1)
    bwd = pltpu.roll(xs, ROT_DIM // 2, axis=1)
    partner = jnp.where(lane < ROT_DIM // 2, fwd, bwd)
    return xs * c + partner * s


def _proj_a_kernel(h_ref, w_ref, c_ref, s_ref, g_ref,
                   q_ref, k_ref, v_ref, iq_ref, ki_ref, wi_ref):
    p = _dot(h_ref[...].astype(BF16), w_ref[...])
    c = c_ref[...]
    s = s_ref[...]
    lane = lax.broadcasted_iota(jnp.int32, c.shape, 1)
    lo = lane < A_HEAD_DIM
    n_blk = p.shape[0] // Q_BLOCK

    def put_heads(dst_ref, slab, first_head):
        for half in range(2):
            hd = first_head + half
            cols = slab[:, half * A_HEAD_DIM:(half + 1) * A_HEAD_DIM].astype(BF16)
            for qb in range(n_blk):
                dst_ref[qb, hd * Q_BLOCK:(hd + 1) * Q_BLOCK, :] = cols[qb * Q_BLOCK:(qb + 1) * Q_BLOCK, :]

    for i in range(4):
        sl = slice(i * LANES, (i + 1) * LANES)
        put_heads(q_ref, _rope_slab(p[:, sl], c, s) * (A_HEAD_DIM ** -0.5 * LOG2_E), 2 * i)
    kv = p[:, 512:640]
    kv = _rope_slab(kv, jnp.where(lo, c, 1.0), jnp.where(lo, s, 0.0))
    k_ref[...] = kv[:, :A_HEAD_DIM].astype(BF16)
    v_ones = jnp.where(lo, pltpu.roll(kv, A_HEAD_DIM, axis=1), jnp.where(lane == A_HEAD_DIM, 1.0, 0.0))
    v_ref[...] = v_ones.astype(BF16)
    for i in range(2):
        sl = slice(640 + i * LANES, 640 + (i + 1) * LANES)
        put_heads(iq_ref, _rope_slab(p[:, sl], c, s), 2 * i)
    last = p[:, 896:1024]
    mu = jnp.sum(jnp.where(lo, last, 0.0), axis=-1, keepdims=True) * (1.0 / IDX_DIM)
    xc = jnp.where(lo, last - mu, 0.0)
    var = jnp.sum(xc * xc, axis=-1, keepdims=True) * (1.0 / IDX_DIM)
    kin = xc * lax.rsqrt(var + LN_EPS) * g_ref[...]
    kin = _rope_slab(kin, jnp.where(lo, c, 1.0), jnp.where(lo, s, 0.0))
    ki_ref[...] = kin[:, :IDX_DIM].astype(BF16)
    wi = pltpu.roll(last, LANES - IDX_DIM, axis=1) * (IDX_HEADS ** -0.5 * IDX_DIM ** -0.5)
    for qb in range(n_blk):
        wi_ref[qb] = wi[qb * Q_BLOCK:(qb + 1) * Q_BLOCK, :].T[0:SUBLANES, :]


def _proj_a(h, w_a, cos_t, sin_t, ikg, tm=512):
    n = h.shape[0]
    row = lambda w: pl.BlockSpec((tm, w), lambda i: (i, 0))
    stacked = lambda heads: pl.BlockSpec((tm // Q_BLOCK, heads * Q_BLOCK, A_HEAD_DIM), lambda i: (i, 0, 0))
    return pl.pallas_call(
        _proj_a_kernel,
        grid=(n // tm,),
        in_specs=[row(D_MODEL),
                  pl.BlockSpec((D_MODEL, 1024), lambda i: (0, 0)),
                  row(LANES), row(LANES),
                  pl.BlockSpec((1, LANES), lambda i: (0, 0))],
        out_specs=[stacked(A_HEADS), row(A_HEAD_DIM),
                   row(LANES),
                   stacked(IDX_HEADS), row(IDX_DIM),
                   pl.BlockSpec((tm // Q_BLOCK, SUBLANES, Q_BLOCK), lambda i: (i, 0, 0))],
        out_shape=[jax.ShapeDtypeStruct((n // Q_BLOCK, A_HEADS * Q_BLOCK, A_HEAD_DIM), BF16),
                   jax.ShapeDtypeStruct((n, A_HEAD_DIM), BF16),
                   jax.ShapeDtypeStruct((n, LANES), BF16),
                   jax.ShapeDtypeStruct((n // Q_BLOCK, IDX_HEADS * Q_BLOCK, IDX_DIM), BF16),
                   jax.ShapeDtypeStruct((n, IDX_DIM), BF16),
                   jax.ShapeDtypeStruct((n // Q_BLOCK, SUBLANES, Q_BLOCK), F32)],
        compiler_params=_cparams(("parallel",)),
        name="proj_a",
    )(h, w_a, cos_t, sin_t, ikg)


KEY_CHUNK = 512
COUNT_ROWS = 64
FIELD_BITS = 15
LOW_BITS = 32 - 2 * FIELD_BITS
FIELD_MASK = (1 << FIELD_BITS) - 1
FIELD_STRIDE = FIELD_BITS + 1
HALF_WORD_MASK = (1 << FIELD_STRIDE) - 1
GUARD_BITS = ((1 << FIELD_BITS) | (1 << (FIELD_BITS + FIELD_STRIDE))) - 2 ** 32
PAIR_ONE = 1 | (1 << FIELD_STRIDE)


def _dsa_kernel(q_ref, iq_ref, wi_ref, k_ref, v_ref, ki_ref, o_ref,
                key_ref, pk_ref, bias_ref, *, seq):
    j = pl.program_id(1)
    n_sel = min(TOPK_MAX, seq // 4)
    n_kc = (j * Q_BLOCK + Q_BLOCK + KEY_CHUNK - 1) // KEY_CHUNK
    qpos = j * Q_BLOCK + lax.broadcasted_iota(jnp.int32, (1, Q_BLOCK), 1)
    kk = jnp.minimum(n_sel, qpos + 1).astype(F32)
    row_pos = lax.broadcasted_iota(jnp.int32, (KEY_CHUNK, 1), 0)

    def key_rows(c):
        return pl.ds(pl.multiple_of(c * KEY_CHUNK, KEY_CHUNK), KEY_CHUNK)

    def chunk_loop(body, init):
        def pair(i, carry):
            return body(2 * i + 1, body(2 * i, carry))
        carry = lax.fori_loop(0, n_kc // 2, pair, init)
        return lax.cond(n_kc % 2 == 1, lambda cr: body(n_kc - 1, cr), lambda cr: cr, carry)

    iq = iq_ref[...]
    wi = wi_ref[...]

    def score_chunk(c, carry):
        d = jnp.maximum(_dot_nt(ki_ref[key_rows(c), :], iq), 0.0)
        score = wi[0:1, :] * d[:, 0:Q_BLOCK]
        for h in range(1, IDX_HEADS):
            score = score + wi[h:h + 1, :] * d[:, h * Q_BLOCK:(h + 1) * Q_BLOCK]
        bits = pltpu.bitcast(score + 0.0, jnp.int32)
        key = bits ^ ((bits >> 31) & 0x7FFFFFFF)
        key = jnp.where(c * KEY_CHUNK + row_pos <= qpos, key, INT_MIN)
        key_ref[c] = key
        pk_ref[c] = pack(lax.shift_right_logical(key ^ INT_MIN, FIELD_BITS + LOW_BITS))
        return carry

    def pack(f):
        hf = KEY_CHUNK // 2
        return (f[:hf] << FIELD_STRIDE) | f[hf:] | GUARD_BITS

    chunk_loop(score_chunk, 0)

    def count(pred):
        def body(c, acc):
            m = jnp.where(pred(key_ref[c]), 1.0, 0.0)
            return acc + jnp.sum(m.reshape(KEY_CHUNK // COUNT_ROWS, COUNT_ROWS, Q_BLOCK), axis=0)
        acc = chunk_loop(body, jnp.zeros((COUNT_ROWS, Q_BLOCK), F32))
        return jnp.sum(acc, axis=0, keepdims=True)

    def count_packed(cand):
        both = (cand << FIELD_STRIDE) | cand

        def body(c, acc):
            flags = ((pk_ref[c] - both) >> FIELD_BITS) & PAIR_ONE
            parts = [flags[t * COUNT_ROWS:(t + 1) * COUNT_ROWS] for t in range(KEY_CHUNK // 2 // COUNT_ROWS)]
            while len(parts) > 1:
                parts = [a + b for a, b in zip(parts[::2], parts[1::2])]
            return acc + parts[0]

        acc = chunk_loop(body, jnp.zeros((COUNT_ROWS, Q_BLOCK), jnp.int32))
        return jnp.sum(((acc & HALF_WORD_MASK) + (acc >> FIELD_STRIDE)).astype(F32), axis=0, keepdims=True)

    def search_field(target):
        def step(i, t):
            cand = t | jnp.left_shift(jnp.int32(1), FIELD_BITS - 1 - i)
            return jnp.where(count_packed(cand) >= target, cand, t)
        return lax.fori_loop(0, FIELD_BITS, step, jnp.zeros((1, Q_BLOCK), jnp.int32))

    t_a = search_field(kk)
    top_of_a = ((t_a << (FIELD_BITS + LOW_BITS)) | ((1 << (FIELD_BITS + LOW_BITS)) - 1)) ^ INT_MIN
    need_b = kk - count(lambda kc: kc > top_of_a)

    def repack(c, carry):
        u = key_ref[c] ^ INT_MIN
        member = lax.shift_right_logical(u, FIELD_BITS + LOW_BITS) == t_a
        pk_ref[c] = pack(jnp.where(member, lax.shift_right_logical(u, LOW_BITS) & FIELD_MASK, 0))
        return carry

    chunk_loop(repack, 0)
    base = (t_a << (FIELD_BITS + LOW_BITS)) | (search_field(need_b) << LOW_BITS)

    def low_step(i, t_c):
        bit = jnp.left_shift(jnp.int32(1), LOW_BITS - 1 - i)
        cand_s = (base | t_c | bit) ^ INT_MIN
        return jnp.where(count(lambda kc: kc >= cand_s) >= kk, t_c | bit, t_c)

    thr = (base | lax.fori_loop(0, LOW_BITS, low_step, jnp.zeros((1, Q_BLOCK), jnp.int32))) ^ INT_MIN
    need = kk - count(lambda kc: kc > thr)
    r_i = lax.broadcasted_iota(jnp.int32, (LANES, LANES), 0)
    c_i = lax.broadcasted_iota(jnp.int32, (LANES, LANES), 1)
    lower = jnp.where(c_i < r_i, 1.0, 0.0).astype(BF16)

    def fill(c, carry):
        for t in range(KEY_CHUNK // LANES):
            rows = slice(t * LANES, (t + 1) * LANES)
            kt = key_ref[c, rows, :]
            eq = kt == thr
            eq_f = jnp.where(eq, 1.0, 0.0)
            pre = _dot(lower, eq_f.astype(BF16)) + carry
            sel = (kt > thr) | (eq & (pre < need))
            bias_ref[c, rows, :] = jnp.where(sel, 0.0, NEG_BIG)
            carry = carry + jnp.sum(eq_f, axis=0, keepdims=True)
        return carry

    chunk_loop(fill, jnp.zeros((1, Q_BLOCK), F32))

    q = q_ref[...]
    cols = A_HEADS * Q_BLOCK

    def attn_chunk(c, carry):
        m, acc = carry
        rows = key_rows(c)
        lg = _dot_nt(k_ref[rows, :], q) + jnp.concatenate([bias_ref[c]] * A_HEADS, axis=1)
        m_new = jnp.maximum(m, jnp.max(lg, axis=0, keepdims=True))
        p = jnp.exp2(lg - m_new).astype(BF16)
        pv = lax.dot_general(v_ref[rows, :], p, (((0,), (0,)), ((), ())),
                             preferred_element_type=F32)
        return m_new, jnp.exp2(m - m_new) * acc + pv

    init = (jnp.full((1, cols), -3e38, F32), jnp.zeros((LANES, cols), F32))
    _, acc = chunk_loop(attn_chunk, init)
    o = acc[:A_HEAD_DIM] / acc[A_HEAD_DIM:A_HEAD_DIM + 1]
    for h in range(A_HEADS):
        o_ref[:, h * A_HEAD_DIM:(h + 1) * A_HEAD_DIM] = (
            o[:, h * Q_BLOCK:(h + 1) * Q_BLOCK].T.astype(BF16))


def _dsa(q, iq, wi, k, v, ki, bsz, seq):
    nq = seq // Q_BLOCK
    n_kc = seq // KEY_CHUNK
    stacked = lambda heads: pl.BlockSpec((None, heads * Q_BLOCK, A_HEAD_DIM),
                                         lambda b, j: (b * nq + j, 0, 0))
    kspec = lambda w: pl.BlockSpec((seq, w), lambda b, j: (b, 0))
    return pl.pallas_call(
        functools.partial(_dsa_kernel, seq=seq),
        grid=(bsz, nq),
        in_specs=[stacked(A_HEADS), stacked(IDX_HEADS),
                  pl.BlockSpec((None, SUBLANES, Q_BLOCK), lambda b, j: (b * nq + j, 0, 0)),
                  kspec(A_HEAD_DIM),
                  kspec(LANES),
                  kspec(IDX_DIM)],
        out_specs=pl.BlockSpec((Q_BLOCK, A_OUT), lambda b, j: (b * nq + j, 0)),
        out_shape=jax.ShapeDtypeStruct((bsz * seq, A_OUT), BF16),
        scratch_shapes=[pltpu.VMEM((n_kc, KEY_CHUNK, Q_BLOCK), jnp.int32),
                        pltpu.VMEM((n_kc, KEY_CHUNK // 2, Q_BLOCK), jnp.int32),
                        pltpu.VMEM((n_kc, KEY_CHUNK, Q_BLOCK), F32)],
        compiler_params=_cparams(("parallel", "parallel")),
        name="dsa",
    )(q, iq, wi, k, v, ki)


GLA_QK = B_HEADS * B_KEY_DIM
GLA_REP = N_SUB * GLA_QK
GLA_COLS = 2 * GLA_QK + 2 * B_OUT + LANES
GLA_V0 = 2 * GLA_QK
GLA_GROUP = 8


def _gla_kernel(h_ref, w_ref, wa_ref, ba_ref, ng_ref, o_ref, p_ref, g_ref, acc_ref, st_ref, *, ts):
    @pl.when(pl.program_id(1) == 0)
    def _():
        st_ref[...] = jnp.zeros_like(st_ref)

    p_ref[...] = _dot(h_ref[...].astype(BF16), w_ref[...])
    x = _dot(p_ref[:, GLA_V0 + 2 * B_OUT:].astype(BF16), wa_ref[...]) + ba_ref[...]
    g_ref[...] = (jnp.minimum(x, 0.0) - jnp.log1p(jnp.exp(-jnp.abs(x)))) * (1.0 / B_GATE_TAU)

    t_i = lax.broadcasted_iota(jnp.int32, (B_CHUNK, B_CHUNK), 0)
    s_i = lax.broadcasted_iota(jnp.int32, (B_CHUNK, B_CHUNK), 1)
    tri = t_i >= s_i
    tri_b = jnp.where(tri, 1.0, 0.0).astype(BF16)
    lane_sub = (lax.broadcasted_iota(jnp.int32, (1, GLA_REP), 1) // B_KEY_DIM) % N_SUB
    row_sub = lax.broadcasted_iota(jnp.int32, (B_CHUNK, 1), 0) // B_SUB
    q_mask = lane_sub <= row_sub
    k_mask = lane_sub == row_sub
    low_half = lax.broadcasted_iota(jnp.int32, (B_CHUNK, LANES), 1) < B_KEY_DIM

    def replicate(x):
        outs = []
        for pair in range(GLA_QK // LANES):
            v = x[:, pair * LANES:(pair + 1) * LANES]
            r = pltpu.roll(v, B_KEY_DIM, axis=1)
            outs += [jnp.where(low_half, v, r)] * (N_SUB // 2) + [jnp.where(low_half, r, v)] * (N_SUB // 2)
        return jnp.concatenate(outs, axis=1)

    def prepare(c):
        rows = pl.ds(pl.multiple_of(c * B_CHUNK, B_CHUNK), B_CHUNK)
        g = g_ref[rows, :]
        g_hi = g.astype(BF16)
        g_mid = (g - g_hi.astype(F32)).astype(BF16)
        g_lo = (g - g_hi.astype(F32) - g_mid.astype(F32)).astype(BF16)
        b = replicate(_dot(tri_b, g_hi) + _dot(tri_b, g_mid) + _dot(tri_b, g_lo))
        ref = jnp.zeros((1, GLA_REP), F32)
        for jj in range(1, N_SUB):
            ref = ref + jnp.where(lane_sub == jj, b[jj * B_SUB - 1:jj * B_SUB, :], 0.0)
        d = b - ref
        e_q = jnp.where(q_mask, jnp.exp(jnp.minimum(d, 0.0)), 0.0)
        e_k = jnp.where(k_mask, jnp.exp(jnp.where(k_mask, -d, 0.0)), 0.0)
        b_last = b[B_CHUNK - 1:B_CHUNK, :]
        q_c = (replicate(p_ref[rows, 0:GLA_QK]) * e_q).astype(BF16)
        k_raw = replicate(p_ref[rows, GLA_QK:2 * GLA_QK])
        k_c = (k_raw * e_k).astype(BF16)
        k_st = (k_raw * jnp.exp(b_last - b)).astype(BF16)
        return rows, q_c, k_c, k_st, jnp.exp(b_last)

    def chunk_group(i, carry):
        prep = [prepare(GLA_GROUP * i + u) for u in range(GLA_GROUP)]
        att, upd, val = {}, {}, {}
        for u, (rows, q_c, k_c, k_st, _) in enumerate(prep):
            for hh in range(B_HEADS):
                base = hh * N_SUB * B_KEY_DIM
                wide = slice(base, base + N_SUB * B_KEY_DIM)
                val[u, hh] = p_ref[rows, GLA_V0 + hh * B_VAL_DIM:GLA_V0 + (hh + 1) * B_VAL_DIM].astype(BF16)
                att[u, hh] = jnp.where(tri, _dot_nt(q_c[:, wide], k_c[:, wide]), 0.0).astype(BF16)
                upd[u, hh] = _dot(val[u, hh].T, k_st[:, base:base + B_KEY_DIM])
        for hh in range(B_HEADS):
            base = hh * N_SUB * B_KEY_DIM
            st = st_ref[hh]
            for u, (rows, q_c, _, _, e_last) in enumerate(prep):
                o = _dot(att[u, hh], val[u, hh]) + _dot_nt(q_c[:, base:base + B_KEY_DIM], st.astype(BF16))
                acc_ref[rows, hh * B_VAL_DIM:(hh + 1) * B_VAL_DIM] = o
                st = st * e_last[:, base:base + B_KEY_DIM] + upd[u, hh]
            st_ref[hh] = st
        return carry

    lax.fori_loop(0, ts // (GLA_GROUP * B_CHUNK), chunk_group, 0)

    r = p_ref[:, GLA_V0 + B_OUT:GLA_V0 + 2 * B_OUT]
    gate = r * (1.0 / (1.0 + jnp.exp(-r)))
    ng = ng_ref[...]
    for hh in range(B_HEADS):
        sl = slice(hh * B_VAL_DIM, (hh + 1) * B_VAL_DIM)
        o = acc_ref[:, sl]
        y = o * lax.rsqrt(jnp.mean(o * o, axis=-1, keepdims=True) + RMS_EPS) * ng
        o_ref[:, sl] = (gate[:, sl] * y).astype(BF16)


def _gla(h, w_b, wa, ba, norm_g, bsz, seq, ts=512):
    ns = seq // ts
    const = lambda shape: pl.BlockSpec(shape, lambda b, i: (0, 0))
    return pl.pallas_call(
        functools.partial(_gla_kernel, ts=ts),
        grid=(bsz, ns),
        in_specs=[pl.BlockSpec((ts, D_MODEL), lambda b, i: (b * ns + i, 0)),
                  const((D_MODEL, GLA_COLS)), const((LANES, GLA_QK)),
                  const((1, GLA_QK)), const((1, B_VAL_DIM))],
        out_specs=pl.BlockSpec((ts, B_OUT), lambda b, i: (b * ns + i, 0)),
        out_shape=jax.ShapeDtypeStruct((bsz * seq, B_OUT), BF16),
        scratch_shapes=[pltpu.VMEM((ts, GLA_COLS), F32),
                        pltpu.VMEM((ts, GLA_QK), F32),
                        pltpu.VMEM((ts, B_OUT), F32),
                        pltpu.VMEM((B_HEADS, B_VAL_DIM, B_KEY_DIM), F32)],
        compiler_params=_cparams(("parallel", "arbitrary")),
        name="gla",
    )(h, w_b, wa, ba, norm_g)


def _gmlp_kernel(h_ref, w_ref, lg_ref, lb_ref, ws_ref, bs_ref, o_ref, *, tm):
    z = _dot(h_ref[...].astype(BF16), w_ref[...])
    z = z * (0.5 * (1.0 + jnp.tanh(np.sqrt(2.0 / np.pi) * (z + 0.044715 * (z * z * z)))))
    u = z[:, :C_OUT]
    v = _layer_norm(z[:, C_OUT:], lg_ref[...], lb_ref[...]).astype(BF16)
    t_i = lax.broadcasted_iota(jnp.int32, (C_CHUNK, C_CHUNK), 0)
    s_i = lax.broadcasted_iota(jnp.int32, (C_CHUNK, C_CHUNK), 1)
    bs = bs_ref[...]
    for g in range(C_GROUPS):
        w = jnp.where(t_i >= s_i, ws_ref[g], 0.0).astype(BF16)
        sl = slice(g * C_GROUP_DIM, (g + 1) * C_GROUP_DIM)
        for c in range(tm // C_CHUNK):
            rows = slice(c * C_CHUNK, (c + 1) * C_CHUNK)
            mixed = _dot(w, v[rows, sl]) + bs[:, g:g + 1]
            o_ref[rows, sl] = (u[rows, sl] * mixed).astype(BF16)


def _gmlp(h, w_c, ln_g, ln_b, ws, bs_t, tm=512):
    n = h.shape[0]
    return pl.pallas_call(
        functools.partial(_gmlp_kernel, tm=tm),
        grid=(n // tm,),
        in_specs=[pl.BlockSpec((tm, D_MODEL), lambda i: (i, 0)),
                  pl.BlockSpec((D_MODEL, 2 * C_OUT), lambda i: (0, 0)),
                  pl.BlockSpec((1, C_OUT), lambda i: (0, 0)),
                  pl.BlockSpec((1, C_OUT), lambda i: (0, 0)),
                  pl.BlockSpec((C_GROUPS, C_CHUNK, C_CHUNK), lambda i: (0, 0, 0)),
                  pl.BlockSpec((C_CHUNK, LANES), lambda i: (0, 0))],
        out_specs=pl.BlockSpec((tm, C_OUT), lambda i: (i, 0)),
        out_shape=jax.ShapeDtypeStruct((n, C_OUT), BF16),
        compiler_params=_cparams(("parallel",)),
        name="gmlp",
    )(h, w_c, ln_g, ln_b, ws, bs_t)


def _sigmoid(x):
    return 1.0 / (1.0 + jnp.exp(-x))


def _merge_kernel(h_ref, ya_ref, yb_ref, yc_ref, wg_ref, wa_ref, wb_ref, wc_ref, wo_ref,
                  lg_ref, lb_ref, wr_ref, wrl_ref, br_ref, x_ref, oh_ref, route_ref):
    h = h_ref[...]
    hb = h.astype(BF16)
    acc = None
    for i, (y_ref, wbr_ref) in enumerate(((ya_ref, wa_ref), (yb_ref, wb_ref), (yc_ref, wc_ref))):
        gate = _sigmoid(_dot(hb, wg_ref[:, i * D_MODEL:(i + 1) * D_MODEL]))
        term = gate * _dot(y_ref[...], wbr_ref[...])
        acc = term if acc is None else acc + term
    mix = _dot(acc.astype(BF16), wo_ref[...])
    x1 = _layer_norm(DN_ALPHA * h + mix, lg_ref[...], lb_ref[...])
    x_ref[...] = x1

    x_hi = x1.astype(BF16)
    x_lo = (x1 - x_hi.astype(F32)).astype(BF16)
    logit = (_dot(x_hi, wr_ref[...]) + _dot(x_hi, wrl_ref[...]) + _dot(x_lo, wr_ref[...])) + br_ref[...]
    lane = lax.broadcasted_iota(jnp.int32, logit.shape, 1)
    big = jnp.int32(LANES)
    is_g = (lane >= N_EXPERTS) & (lane < N_EXPERTS + N_GROUPS)
    gl = jnp.where(is_g, logit, -jnp.inf)
    g_max = jnp.max(gl, axis=-1, keepdims=True)
    g_idx = jnp.min(jnp.where(is_g & (gl == g_max), lane, big), axis=-1, keepdims=True) - N_EXPERTS
    g_p = 1.0 / jnp.sum(jnp.where(is_g, jnp.exp(gl - g_max), 0.0), axis=-1, keepdims=True)
    lo = g_idx * EXPERTS_PER_GROUP
    in_g = (lane >= lo) & (lane < lo + EXPERTS_PER_GROUP)
    el = jnp.where(in_g, logit, -jnp.inf)
    e1 = jnp.max(el, axis=-1, keepdims=True)
    i1 = jnp.min(jnp.where(in_g & (el == e1), lane, big), axis=-1, keepdims=True)
    el2 = jnp.where(lane == i1, -jnp.inf, el)
    e2 = jnp.max(el2, axis=-1, keepdims=True)
    i2 = jnp.min(jnp.where(in_g & (lane != i1) & (el2 == e2), lane, big), axis=-1, keepdims=True)
    z2 = jnp.exp(e2 - e1)
    den = 1.0 + z2
    oh_ref[...] = jnp.where((lane == i1) | (lane == i2), 1.0, 0.0).astype(BF16)
    route_ref[...] = jnp.where(lane == 0, i1.astype(F32),
                               jnp.where(lane == 1, i2.astype(F32),
                                         jnp.where(lane == 2, (1.0 / den) * g_p,
                                                   jnp.where(lane == 3, (z2 / den) * g_p, 0.0))))


def _merge(h, ya, yb, yc, wg, wa, wb, wc, wo, ln_g, ln_b, wr, wr_lo, br, tm=1024):
    n = h.shape[0]
    row = lambda w: pl.BlockSpec((tm, w), lambda i: (i, 0))
    const = lambda r, c: pl.BlockSpec((r, c), lambda i: (0, 0), pipeline_mode=pl.Buffered(1))
    return pl.pallas_call(
        _merge_kernel,
        grid=(n // tm,),
        in_specs=[row(D_MODEL), row(A_OUT), row(B_OUT), row(C_OUT),
                  const(D_MODEL, 3 * D_MODEL), const(A_OUT, D_MODEL), const(B_OUT, D_MODEL),
                  const(C_OUT, D_MODEL), const(D_MODEL, D_MODEL),
                  const(1, D_MODEL), const(1, D_MODEL), const(D_MODEL, LANES), const(D_MODEL, LANES),
                  const(1, LANES)],
        out_specs=[row(D_MODEL), row(LANES), row(LANES)],
        out_shape=[jax.ShapeDtypeStruct((n, D_MODEL), F32),
                   jax.ShapeDtypeStruct((n, LANES), BF16),
                   jax.ShapeDtypeStruct((n, LANES), F32)],
        compiler_params=_cparams(("parallel",)),
        name="merge",
    )(h, ya, yb, yc, wg, wa, wb, wc, wo, ln_g, ln_b, wr, wr_lo, br)


FFN_TILE = 512
MOE_TOK = 512


def _rank_kernel(oh_ref, route_ref, r_ref, cnt_ref, carry_ref):
    @pl.when(pl.program_id(0) == 0)
    def _():
        carry_ref[...] = jnp.zeros_like(carry_ref)

    oh = oh_ref[...]
    tt = oh.shape[0]
    r_i = lax.broadcasted_iota(jnp.int32, (tt, tt), 0)
    c_i = lax.broadcasted_iota(jnp.int32, (tt, tt), 1)
    lower = jnp.where(c_i < r_i, 1.0, 0.0).astype(BF16)
    rank = _dot(lower, oh) + carry_ref[...]
    route = route_ref[...]
    lane = lax.broadcasted_iota(jnp.int32, rank.shape, 1)
    i1 = route[:, 0:1].astype(jnp.int32)
    i2 = route[:, 1:2].astype(jnp.int32)
    r1 = jnp.sum(jnp.where(lane == i1, rank, 0.0), axis=-1, keepdims=True)
    r2 = jnp.sum(jnp.where(lane == i2, rank, 0.0), axis=-1, keepdims=True)
    r_ref[...] = jnp.where(lane == 0, r1, jnp.where(lane == 1, r2, 0.0))
    carry_ref[...] += jnp.sum(oh.astype(F32), axis=0, keepdims=True)
    cnt_ref[...] = carry_ref[...]


def _rank(oh, route, tt=512):
    n = oh.shape[0]
    return pl.pallas_call(
        _rank_kernel,
        grid=(n // tt,),
        in_specs=[pl.BlockSpec((tt, LANES), lambda i: (i, 0)),
                  pl.BlockSpec((tt, LANES), lambda i: (i, 0))],
        out_specs=[pl.BlockSpec((tt, LANES), lambda i: (i, 0)),
                   pl.BlockSpec((1, LANES), lambda i: (0, 0))],
        out_shape=[jax.ShapeDtypeStruct((n, LANES), F32),
                   jax.ShapeDtypeStruct((1, LANES), F32)],
        scratch_shapes=[pltpu.VMEM((1, LANES), F32)],
        compiler_params=_cparams(("arbitrary",)),
        name="moe_rank",
    )(oh, route)


def _row_copy(src_ref, src_row, dst_ref, dst_row, sem):
    return pltpu.make_async_copy(src_ref.at[pl.ds(src_row, 1), :], dst_ref.at[pl.ds(dst_row, 1), :], sem)


def _dispatch_kernel(pos1_ref, pos2_ref, x_ref, init_ref, xs_ref, sem):
    del init_ref
    base = pl.program_id(0) * MOE_TOK

    def issue(t, carry):
        _row_copy(x_ref, t, xs_ref, pos1_ref[base + t], sem).start()
        _row_copy(x_ref, t, xs_ref, pos2_ref[base + t], sem).start()
        return carry

    lax.fori_loop(0, MOE_TOK, issue, 0, unroll=32)
    for _ in range(2):
        pltpu.make_async_copy(x_ref, xs_ref.at[pl.ds(0, MOE_TOK), :], sem).wait()


def _dispatch(pos1, pos2, x1, n_slots, spare):
    n = x1.shape[0]
    init = jnp.zeros((n_slots, D_MODEL), F32) if spare is None else spare
    return pl.pallas_call(
        _dispatch_kernel,
        grid_spec=pltpu.PrefetchScalarGridSpec(
            num_scalar_prefetch=2,
            grid=(n // MOE_TOK,),
            in_specs=[pl.BlockSpec((MOE_TOK, D_MODEL), lambda i, p1, p2: (i, 0)),
                      pl.BlockSpec(memory_space=pl.ANY)],
            out_specs=pl.BlockSpec(memory_space=pl.ANY),
            scratch_shapes=[pltpu.SemaphoreType.DMA(())]),
        out_shape=jax.ShapeDtypeStruct((n_slots, D_MODEL), F32),
        input_output_aliases={3: 0},
        compiler_params=_cparams(("arbitrary",)),
        name="moe_dispatch",
    )(pos1, pos2, x1, init)


def _ffn_kernel(te_ref, nu_ref, xs_ref, wg_ref, wu_ref, wd_ref, ys_ref):
    del te_ref
    i = pl.program_id(0)

    @pl.when(i < nu_ref[0])
    def _():
        xb = xs_ref[...].astype(BF16)
        gte = _dot(xb, wg_ref[...].astype(BF16))
        hid = gte * _sigmoid(gte) * _dot(xb, wu_ref[...].astype(BF16))
        ys_ref[...] = _dot(hid.astype(BF16), wd_ref[...].astype(BF16))

    @pl.when(i >= nu_ref[0])
    def _():
        ys_ref[...] = jnp.zeros_like(ys_ref)


def _ffn(tile_expert, n_used, xs, w_gate, w_up, w_down, layer):
    n_slots = xs.shape[0]
    wspec = lambda r, c: pl.BlockSpec((None, None, r, c), lambda i, te, nu: (layer, te[i], 0, 0))
    return pl.pallas_call(
        _ffn_kernel,
        grid_spec=pltpu.PrefetchScalarGridSpec(
            num_scalar_prefetch=2,
            grid=(n_slots // FFN_TILE,),
            in_specs=[pl.BlockSpec((FFN_TILE, D_MODEL), lambda i, te, nu: (i, 0)),
                      wspec(D_MODEL, D_EXPERT), wspec(D_MODEL, D_EXPERT), wspec(D_EXPERT, D_MODEL)],
            out_specs=pl.BlockSpec((FFN_TILE, D_MODEL), lambda i, te, nu: (i, 0))),
        out_shape=jax.ShapeDtypeStruct((n_slots, D_MODEL), F32),
        compiler_params=_cparams(("arbitrary",)),
        name="moe_ffn",
    )(tile_expert, n_used, xs, w_gate, w_up, w_down)


def _combine_kernel(pos1_ref, pos2_ref, x_ref, route_ref, lg_ref, lb_ref, ys_ref, o_ref, buf_ref, sem):
    i = pl.program_id(0)

    def gather(step, slot):
        base = step * MOE_TOK

        def issue(t, carry):
            _row_copy(ys_ref, pos1_ref[base + t], buf_ref.at[slot, 0], t, sem.at[slot]).start()
            _row_copy(ys_ref, pos2_ref[base + t], buf_ref.at[slot, 1], t, sem.at[slot]).start()
            return carry

        lax.fori_loop(0, MOE_TOK, issue, 0, unroll=32)

    @pl.when(i == 0)
    def _():
        gather(0, 0)

    @pl.when(i + 1 < pl.num_programs(0))
    def _():
        gather(i + 1, (i + 1) % 2)

    slot = i % 2
    for s in range(2):
        pltpu.make_async_copy(ys_ref.at[pl.ds(0, MOE_TOK), :], buf_ref.at[slot, s], sem.at[slot]).wait()
    route = route_ref[...]
    ffn = route[:, 2:3] * buf_ref[slot, 0] + route[:, 3:4] * buf_ref[slot, 1]
    o_ref[...] = _layer_norm(DN_ALPHA * x_ref[...] + ffn, lg_ref[...], lb_ref[...])


def _combine(pos1, pos2, x1, route, ln_g, ln_b, ys):
    n = x1.shape[0]
    row = lambda w: pl.BlockSpec((MOE_TOK, w), lambda i, p1, p2: (i, 0))
    const = lambda w: pl.BlockSpec((1, w), lambda i, p1, p2: (0, 0))
    return pl.pallas_call(
        _combine_kernel,
        grid_spec=pltpu.PrefetchScalarGridSpec(
            num_scalar_prefetch=2,
            grid=(n // MOE_TOK,),
            in_specs=[row(D_MODEL), row(LANES), const(D_MODEL), const(D_MODEL),
                      pl.BlockSpec(memory_space=pl.ANY)],
            out_specs=row(D_MODEL),
            scratch_shapes=[pltpu.VMEM((2, 2, MOE_TOK, D_MODEL), F32),
                            pltpu.SemaphoreType.DMA((2,))]),
        out_shape=jax.ShapeDtypeStruct((n, D_MODEL), F32),
        compiler_params=_cparams(("arbitrary",)),
        name="moe_combine",
    )(pos1, pos2, x1, route, ln_g, ln_b, ys)


def _moe(x1, oh, route, w_gate, w_up, w_down, layer, ln_g, ln_b, spare):
    n = x1.shape[0]
    n_slots = TOP_K_INNER * n + N_EXPERTS * FFN_TILE
    rank, cnt = _rank(oh, route)
    cnt = cnt[0, :N_EXPERTS].astype(jnp.int32)
    padded = (cnt + FFN_TILE - 1) // FFN_TILE * FFN_TILE
    ends = jnp.cumsum(padded)
    offs = ends - padded
    experts = jnp.arange(N_EXPERTS, dtype=jnp.int32)
    i1 = route[:, 0].astype(jnp.int32)
    i2 = route[:, 1].astype(jnp.int32)
    off_of = lambda idx: jnp.sum(jnp.where(idx[:, None] == experts[None, :], offs[None, :], 0), axis=1)
    pos1 = off_of(i1) + rank[:, 0].astype(jnp.int32)
    pos2 = off_of(i2) + rank[:, 1].astype(jnp.int32)
    tile_start = jnp.arange(n_slots // FFN_TILE, dtype=jnp.int32) * FFN_TILE
    tile_expert = jnp.minimum(jnp.sum(tile_start[:, None] >= ends[None, :], axis=1), N_EXPERTS - 1)
    n_used = (ends[-1] // FFN_TILE).reshape(1)

    xs = _dispatch(pos1, pos2, x1, n_slots, spare)
    ys = _ffn(tile_expert.astype(jnp.int32), n_used.astype(jnp.int32), xs, w_gate, w_up, w_down, layer)
    return _combine(pos1, pos2, x1, route, ln_g, ln_b, ys), ys


def _rope_tables(positions):
    inv_freq = ROPE_THETA ** (-jnp.arange(0, ROT_DIM, 2, dtype=F32) / ROT_DIM)
    ang = positions.astype(F32).reshape(-1, 1) * inv_freq
    cos, sin = jnp.cos(ang), jnp.sin(ang)
    n = ang.shape[0]
    half = ROT_DIM // 2
    c64 = jnp.concatenate([cos, cos, jnp.ones((n, A_HEAD_DIM - ROT_DIM), F32)], axis=1)
    s64 = jnp.concatenate([-sin, sin, jnp.zeros((n, A_HEAD_DIM - ROT_DIM), F32)], axis=1)
    del half
    return jnp.tile(c64, (1, 2)), jnp.tile(s64, (1, 2))


def _split_w_in(w):
    widths = (A_OUT, A_HEAD_DIM, A_HEAD_DIM, IDX_HEADS * IDX_DIM, IDX_DIM, IDX_HEADS,
              B_HEADS * B_KEY_DIM, B_HEADS * B_KEY_DIM, B_OUT, B_GATE_RANK, B_OUT,
              2 * C_OUT, D_MODEL, D_MODEL, D_MODEL)
    pts = np.cumsum((0,) + widths)
    return [w[:, int(pts[i]):int(pts[i + 1])] for i in range(len(widths))]


def kernel(x, positions, ln_in_g, ln_in_b, w_in, idx_k_g, gla_wa2, gla_ba, gla_norm_g, gm_ln_g, gm_ln_b, gm_ws, gm_bs, w_branch_a, w_branch_b, w_branch_c, w_out, ln1_g, ln1_b, w_rg, b_rg, w_re, b_re, w_gate, w_up, w_down, ln2_g, ln2_b):
    bsz, seq, d = x.shape
    n = bsz * seq
    cos_t, sin_t = _rope_tables(positions)
    h = _entry_ln(x.reshape(n, d), ln_in_g, ln_in_b)
    spare = None
    for l in range(DEPTH):
        (a_q, a_k, a_v, i_q, i_k, i_w, b_q, b_k, b_v, b_glr, b_r, c_uv,
         g_a, g_b, g_c) = _split_w_in(w_in[l])
        zpad = lambda c: jnp.zeros((d, c), F32)
        w_a = jnp.concatenate([a_q, a_k, a_v, i_q, i_k, i_w, zpad(LANES - IDX_DIM - IDX_HEADS)],
                              axis=1).astype(BF16)
        w_b = jnp.concatenate([b_q * (B_KEY_DIM ** -0.5), b_k, b_v, b_r,
                               b_glr, zpad(LANES - B_GATE_RANK)], axis=1).astype(BF16)
        wa_pad = jnp.concatenate([gla_wa2[l],
                                  jnp.zeros((LANES - B_GATE_RANK, GLA_QK), F32)], axis=0).astype(BF16)
        w_g = jnp.concatenate([g_a, g_b, g_c], axis=1).astype(BF16)
        ikg = jnp.concatenate([idx_k_g[l], jnp.zeros((LANES - IDX_DIM,), F32)]).reshape(1, LANES)
        bs_t = jnp.concatenate([gm_bs[l].T, jnp.zeros((C_CHUNK, LANES - C_GROUPS), F32)], axis=1)
        w_r = jnp.concatenate([w_re[l], w_rg[l], zpad(LANES - N_GROUPS - N_EXPERTS)], axis=1)
        w_r_hi = w_r.astype(BF16)
        w_r_lo = (w_r - w_r_hi.astype(F32)).astype(BF16)
        b_r_all = jnp.concatenate([b_re[l], b_rg[l],
                                   jnp.zeros((LANES - N_GROUPS - N_EXPERTS,), F32)]).reshape(1, LANES)

        q, k, v, iq, ki, wi = _proj_a(h, w_a, cos_t, sin_t, ikg)
        y_a = _dsa(q, iq, wi, k, v, ki, bsz, seq)
        y_b = _gla(h, w_b, wa_pad, gla_ba[l].reshape(1, -1), gla_norm_g[l].reshape(1, -1), bsz, seq)
        y_c = _gmlp(h, c_uv.astype(BF16), gm_ln_g[l].reshape(1, -1), gm_ln_b[l].reshape(1, -1),
                    gm_ws[l], bs_t)
        x1, oh, route = _merge(h, y_a, y_b, y_c, w_g,
                               w_branch_a[l].astype(BF16), w_branch_b[l].astype(BF16),
                               w_branch_c[l].astype(BF16), w_out[l].astype(BF16),
                               ln1_g[l].reshape(1, -1), ln1_b[l].reshape(1, -1), w_r_hi, w_r_lo, b_r_all)
        h, spare = _moe(x1, oh, route, w_gate, w_up, w_down, l,
                        ln2_g[l].reshape(1, -1), ln2_b[l].reshape(1, -1), spare)
    return h.reshape(bsz, seq, d)
```

```python
import functools

import numpy as np
import jax
import jax.numpy as jnp
from jax import lax
from jax.experimental import pallas as pl
from jax.experimental.pallas import tpu as pltpu

F32 = jnp.float32
BF16 = jnp.bfloat16

D_MODEL = 1024
DEPTH = 2
A_HEADS = 8
A_HEAD_DIM = 64
A_OUT = 512
IDX_HEADS = 4
IDX_DIM = 64
TOPK_MAX = 256
Q_BLOCK = 256
ROPE_THETA = 500000.0
ROT_DIM = 16
B_HEADS = 4
B_KEY_DIM = 64
B_VAL_DIM = 128
B_OUT = 512
B_GATE_RANK = 16
B_GATE_TAU = 16.0
B_CHUNK = 64
B_SUB = 16
N_SUB = B_CHUNK // B_SUB
C_GROUPS = 4
C_GROUP_DIM = 128
C_OUT = 512
C_CHUNK = 128
N_GROUPS = 4
EXPERTS_PER_GROUP = 8
N_EXPERTS = 32
D_EXPERT = 256
TOP_K_INNER = 2
DN_ALPHA = (2 * DEPTH) ** 0.25
LN_EPS = 1e-5
RMS_EPS = 1e-6

LANES = 128
SUBLANES = 8
NEG_BIG = -1e30
LOG2_E = 1.4426950408889634
INT_MIN = -2 ** 31
VMEM_LIMIT = 56 * 1024 * 1024


def _cparams(sem):
    return pltpu.CompilerParams(dimension_semantics=sem, vmem_limit_bytes=VMEM_LIMIT)


def _layer_norm(x, g, b):
    mu = jnp.mean(x, axis=-1, keepdims=True)
    xc = x - mu
    var = jnp.mean(xc * xc, axis=-1, keepdims=True)
    return xc * lax.rsqrt(var + LN_EPS) * g + b


def _dot(a, b):
    return jnp.dot(a, b, preferred_element_type=F32)


def _dot_nt(a, b):
    return lax.dot_general(a, b, (((1,), (1,)), ((), ())), preferred_element_type=F32)


def _ln_kernel(x_ref, g_ref, b_ref, o_ref):
    o_ref[...] = _layer_norm(x_ref[...], g_ref[...], b_ref[...])


def _entry_ln(x, g, b, tm=1024):
    n = x.shape[0]
    return pl.pallas_call(
        _ln_kernel,
        grid=(n // tm,),
        in_specs=[pl.BlockSpec((tm, D_MODEL), lambda i: (i, 0)),
                  pl.BlockSpec((1, D_MODEL), lambda i: (0, 0)),
                  pl.BlockSpec((1, D_MODEL), lambda i: (0, 0))],
        out_specs=pl.BlockSpec((tm, D_MODEL), lambda i: (i, 0)),
        out_shape=jax.ShapeDtypeStruct((n, D_MODEL), F32),
        compiler_params=_cparams(("parallel",)),
        name="entry_ln",
    )(x, g.reshape(1, -1), b.reshape(1, -1))


def _rope_slab(xs, c, s):
    lane = lax.broadcasted_iota(jnp.int32, xs.shape, 1) % A_HEAD_DIM
    fwd = pltpu.roll(xs, LANES - ROT_DIM // 2, axis=1)
    bwd = pltpu.roll(xs, ROT_DIM // 2, axis=1)
    partner = jnp.where(lane < ROT_DIM // 2, fwd, bwd)
    return xs * c + partner * s


def _proj_a_kernel(h_ref, w_ref, c_ref, s_ref, g_ref,
                   q_ref, k_ref, v_ref, iq_ref, ki_ref, wi_ref):
    p = _dot(h_ref[...].astype(BF16), w_ref[...])
    c = c_ref[...]
    s = s_ref[...]
    lane = lax.broadcasted_iota(jnp.int32, c.shape, 1)
    lo = lane < A_HEAD_DIM
    n_blk = p.shape[0] // Q_BLOCK

    def put_heads(dst_ref, slab, first_head):
        for half in range(2):
            hd = first_head + half
            cols = slab[:, half * A_HEAD_DIM:(half + 1) * A_HEAD_DIM].astype(BF16)
            for qb in range(n_blk):
                dst_ref[qb, hd * Q_BLOCK:(hd + 1) * Q_BLOCK, :] = cols[qb * Q_BLOCK:(qb + 1) * Q_BLOCK, :]

    for i in range(4):
        sl = slice(i * LANES, (i + 1) * LANES)
        put_heads(q_ref, _rope_slab(p[:, sl], c, s) * (A_HEAD_DIM ** -0.5 * LOG2_E), 2 * i)
    kv = p[:, 512:640]
    kv = _rope_slab(kv, jnp.where(lo, c, 1.0), jnp.where(lo, s, 0.0))
    k_ref[...] = kv[:, :A_HEAD_DIM].astype(BF16)
    v_ones = jnp.where(lo, pltpu.roll(kv, A_HEAD_DIM, axis=1), jnp.where(lane == A_HEAD_DIM, 1.0, 0.0))
    v_ref[...] = v_ones.astype(BF16)
    for i in range(2):
        sl = slice(640 + i * LANES, 640 + (i + 1) * LANES)
        put_heads(iq_ref, _rope_slab(p[:, sl], c, s), 2 * i)
    last = p[:, 896:1024]
    mu = jnp.sum(jnp.where(lo, last, 0.0), axis=-1, keepdims=True) * (1.0 / IDX_DIM)
    xc = jnp.where(lo, last - mu, 0.0)
    var = jnp.sum(xc * xc, axis=-1, keepdims=True) * (1.0 / IDX_DIM)
    kin = xc * lax.rsqrt(var + LN_EPS) * g_ref[...]
    kin = _rope_slab(kin, jnp.where(lo, c, 1.0), jnp.where(lo, s, 0.0))
    ki_ref[...] = kin[:, :IDX_DIM].astype(BF16)
    wi = pltpu.roll(last, LANES - IDX_DIM, axis=1) * (IDX_HEADS ** -0.5 * IDX_DIM ** -0.5)
    for qb in range(n_blk):
        wi_ref[qb] = wi[qb * Q_BLOCK:(qb + 1) * Q_BLOCK, :].T[0:SUBLANES, :]


def _proj_a(h, w_a, cos_t, sin_t, ikg, tm=512):
    n = h.shape[0]
    row = lambda w: pl.BlockSpec((tm, w), lambda i: (i, 0))
    stacked = lambda heads: pl.BlockSpec((tm // Q_BLOCK, heads * Q_BLOCK, A_HEAD_DIM), lambda i: (i, 0, 0))
    return pl.pallas_call(
        _proj_a_kernel,
        grid=(n // tm,),
        in_specs=[row(D_MODEL),
                  pl.BlockSpec((D_MODEL, 1024), lambda i: (0, 0)),
                  row(LANES), row(LANES),
                  pl.BlockSpec((1, LANES), lambda i: (0, 0))],
        out_specs=[stacked(A_HEADS), row(A_HEAD_DIM),
                   row(LANES),
                   stacked(IDX_HEADS), row(IDX_DIM),
                   pl.BlockSpec((tm // Q_BLOCK, SUBLANES, Q_BLOCK), lambda i: (i, 0, 0))],
        out_shape=[jax.ShapeDtypeStruct((n // Q_BLOCK, A_HEADS * Q_BLOCK, A_HEAD_DIM), BF16),
                   jax.ShapeDtypeStruct((n, A_HEAD_DIM), BF16),
                   jax.ShapeDtypeStruct((n, LANES), BF16),
                   jax.ShapeDtypeStruct((n // Q_BLOCK, IDX_HEADS * Q_BLOCK, IDX_DIM), BF16),
                   jax.ShapeDtypeStruct((n, IDX_DIM), BF16),
                   jax.ShapeDtypeStruct((n // Q_BLOCK, SUBLANES, Q_BLOCK), F32)],
        compiler_params=_cparams(("parallel",)),
        name="proj_a",
    )(h, w_a, cos_t, sin_t, ikg)


KEY_CHUNK = 512
COUNT_ROWS = 16
FIELD_BITS = 15
LOW_BITS = 32 - 2 * FIELD_BITS
FIELD_MASK = (1 << FIELD_BITS) - 1
FIELD_STRIDE = FIELD_BITS + 1
HALF_WORD_MASK = (1 << FIELD_STRIDE) - 1
GUARD_BITS = ((1 << FIELD_BITS) | (1 << (FIELD_BITS + FIELD_STRIDE))) - 2 ** 32
PAIR_ONE = 1 | (1 << FIELD_STRIDE)


def _dsa_kernel(q_ref, iq_ref, wi_ref, k_ref, v_ref, ki_ref, o_ref,
                key_ref, pk_ref, bias_ref, *, seq):
    j = pl.program_id(1)
    n_sel = min(TOPK_MAX, seq // 4)
    n_kc = (j * Q_BLOCK + Q_BLOCK + KEY_CHUNK - 1) // KEY_CHUNK
    qpos = j * Q_BLOCK + lax.broadcasted_iota(jnp.int32, (1, Q_BLOCK), 1)
    kk = jnp.minimum(n_sel, qpos + 1).astype(F32)
    row_pos = lax.broadcasted_iota(jnp.int32, (KEY_CHUNK, 1), 0)

    def key_rows(c):
        return pl.ds(pl.multiple_of(c * KEY_CHUNK, KEY_CHUNK), KEY_CHUNK)

    def chunk_loop(body, init):
        def pair(i, carry):
            return body(2 * i + 1, body(2 * i, carry))
        carry = lax.fori_loop(0, n_kc // 2, pair, init)
        return lax.cond(n_kc % 2 == 1, lambda cr: body(n_kc - 1, cr), lambda cr: cr, carry)

    iq = iq_ref[...]
    wi = wi_ref[...]

    def score_chunk(c, carry):
        d = jnp.maximum(_dot_nt(ki_ref[key_rows(c), :], iq), 0.0)
        score = wi[0:1, :] * d[:, 0:Q_BLOCK]
        for h in range(1, IDX_HEADS):
            score = score + wi[h:h + 1, :] * d[:, h * Q_BLOCK:(h + 1) * Q_BLOCK]
        bits = pltpu.bitcast(score + 0.0, jnp.int32)
        key = bits ^ ((bits >> 31) & 0x7FFFFFFF)
        key = jnp.where(c * KEY_CHUNK + row_pos <= qpos, key, INT_MIN)
        key_ref[c] = key
        pk_ref[c] = pack(lax.shift_right_logical(key ^ INT_MIN, FIELD_BITS + LOW_BITS))
        return carry

    def pack(f):
        hf = KEY_CHUNK // 2
        return (f[:hf] << FIELD_STRIDE) | f[hf:] | GUARD_BITS

    chunk_loop(score_chunk, 0)

    def count(pred):
        def body(c, acc):
            m = jnp.where(pred(key_ref[c]), 1.0, 0.0)
            return acc + jnp.sum(m.reshape(KEY_CHUNK // COUNT_ROWS, COUNT_ROWS, Q_BLOCK), axis=0)
        acc = chunk_loop(body, jnp.zeros((COUNT_ROWS, Q_BLOCK), F32))
        return jnp.sum(acc, axis=0, keepdims=True)

    def count_packed(cand):
        both = (cand << FIELD_STRIDE) | cand

        def body(c, acc):
            flags = ((pk_ref[c] - both) >> FIELD_BITS) & PAIR_ONE
            parts = [flags[t * COUNT_ROWS:(t + 1) * COUNT_ROWS] for t in range(KEY_CHUNK // 2 // COUNT_ROWS)]
            while len(parts) > 1:
                parts = [a + b for a, b in zip(parts[::2], parts[1::2])]
            return acc + parts[0]

        acc = chunk_loop(body, jnp.zeros((COUNT_ROWS, Q_BLOCK), jnp.int32))
        return jnp.sum(((acc & HALF_WORD_MASK) + (acc >> FIELD_STRIDE)).astype(F32), axis=0, keepdims=True)

    def search_field(target):
        def step(i, t):
            cand = t | jnp.left_shift(jnp.int32(1), FIELD_BITS - 1 - i)
            return jnp.where(count_packed(cand) >= target, cand, t)
        return lax.fori_loop(0, FIELD_BITS, step, jnp.zeros((1, Q_BLOCK), jnp.int32))

    t_a = search_field(kk)
    top_of_a = ((t_a << (FIELD_BITS + LOW_BITS)) | ((1 << (FIELD_BITS + LOW_BITS)) - 1)) ^ INT_MIN
    need_b = kk - count(lambda kc: kc > top_of_a)

    def repack(c, carry):
        u = key_ref[c] ^ INT_MIN
        member = lax.shift_right_logical(u, FIELD_BITS + LOW_BITS) == t_a
        pk_ref[c] = pack(jnp.where(member, lax.shift_right_logical(u, LOW_BITS) & FIELD_MASK, 0))
        return carry

    chunk_loop(repack, 0)
    base = (t_a << (FIELD_BITS + LOW_BITS)) | (search_field(need_b) << LOW_BITS)

    def low_step(i, t_c):
        bit = jnp.left_shift(jnp.int32(1), LOW_BITS - 1 - i)
        cand_s = (base | t_c | bit) ^ INT_MIN
        return jnp.where(count(lambda kc: kc >= cand_s) >= kk, t_c | bit, t_c)

    thr = (base | lax.fori_loop(0, LOW_BITS, low_step, jnp.zeros((1, Q_BLOCK), jnp.int32))) ^ INT_MIN
    need = kk - count(lambda kc: kc > thr)
    r_i = lax.broadcasted_iota(jnp.int32, (LANES, LANES), 0)
    c_i = lax.broadcasted_iota(jnp.int32, (LANES, LANES), 1)
    lower = jnp.where(c_i < r_i, 1.0, 0.0).astype(BF16)

    def fill(c, carry):
        for t in range(KEY_CHUNK // LANES):
            rows = slice(t * LANES, (t + 1) * LANES)
            kt = key_ref[c, rows, :]
            eq = kt == thr
            eq_f = jnp.where(eq, 1.0, 0.0)
            pre = _dot(lower, eq_f.astype(BF16)) + carry
            sel = (kt > thr) | (eq & (pre < need))
            bias_ref[c, rows, :] = jnp.where(sel, 0.0, NEG_BIG)
            carry = carry + jnp.sum(eq_f, axis=0, keepdims=True)
        return carry

    chunk_loop(fill, jnp.zeros((1, Q_BLOCK), F32))

    q = q_ref[...]
    cols = A_HEADS * Q_BLOCK

    def attn_chunk(c, carry):
        m, acc = carry
        rows = key_rows(c)
        lg = _dot_nt(k_ref[rows, :], q) + jnp.concatenate([bias_ref[c]] * A_HEADS, axis=1)
        m_new = jnp.maximum(m, jnp.max(lg, axis=0, keepdims=True))
        p = jnp.exp2(lg - m_new).astype(BF16)
        pv = lax.dot_general(v_ref[rows, :], p, (((0,), (0,)), ((), ())),
                             preferred_element_type=F32)
        return m_new, jnp.exp2(m - m_new) * acc + pv

    init = (jnp.full((1, cols), -3e38, F32), jnp.zeros((LANES, cols), F32))
    _, acc = chunk_loop(attn_chunk, init)
    o = acc[:A_HEAD_DIM] / acc[A_HEAD_DIM:A_HEAD_DIM + 1]
    for h in range(A_HEADS):
        o_ref[:, h * A_HEAD_DIM:(h + 1) * A_HEAD_DIM] = (
            o[:, h * Q_BLOCK:(h + 1) * Q_BLOCK].T.astype(BF16))


def _dsa(q, iq, wi, k, v, ki, bsz, seq):
    nq = seq // Q_BLOCK
    n_kc = seq // KEY_CHUNK
    stacked = lambda heads: pl.BlockSpec((None, heads * Q_BLOCK, A_HEAD_DIM),
                                         lambda b, j: (b * nq + j, 0, 0))
    kspec = lambda w: pl.BlockSpec((seq, w), lambda b, j: (b, 0))
    return pl.pallas_call(
        functools.partial(_dsa_kernel, seq=seq),
        grid=(bsz, nq),
        in_specs=[stacked(A_HEADS), stacked(IDX_HEADS),
                  pl.BlockSpec((None, SUBLANES, Q_BLOCK), lambda b, j: (b * nq + j, 0, 0)),
                  kspec(A_HEAD_DIM),
                  kspec(LANES),
                  kspec(IDX_DIM)],
        out_specs=pl.BlockSpec((Q_BLOCK, A_OUT), lambda b, j: (b * nq + j, 0)),
        out_shape=jax.ShapeDtypeStruct((bsz * seq, A_OUT), BF16),
        scratch_shapes=[pltpu.VMEM((n_kc, KEY_CHUNK, Q_BLOCK), jnp.int32),
                        pltpu.VMEM((n_kc, KEY_CHUNK // 2, Q_BLOCK), jnp.int32),
                        pltpu.VMEM((n_kc, KEY_CHUNK, Q_BLOCK), F32)],
        compiler_params=_cparams(("parallel", "parallel")),
        name="dsa",
    )(q, iq, wi, k, v, ki)


GLA_QK = B_HEADS * B_KEY_DIM
GLA_REP = N_SUB * GLA_QK
GLA_COLS = 2 * GLA_QK + 2 * B_OUT + LANES
GLA_V0 = 2 * GLA_QK
GLA_GROUP = 8


def _gla_kernel(h_ref, w_ref, wa_ref, ba_ref, ng_ref, o_ref, p_ref, g_ref, acc_ref, st_ref, *, ts):
    @pl.when(pl.program_id(1) == 0)
    def _():
        st_ref[...] = jnp.zeros_like(st_ref)

    p_ref[...] = _dot(h_ref[...].astype(BF16), w_ref[...])
    x = _dot(p_ref[:, GLA_V0 + 2 * B_OUT:].astype(BF16), wa_ref[...]) + ba_ref[...]
    g_ref[...] = (jnp.minimum(x, 0.0) - jnp.log1p(jnp.exp(-jnp.abs(x)))) * (1.0 / B_GATE_TAU)

    t_i = lax.broadcasted_iota(jnp.int32, (B_CHUNK, B_CHUNK), 0)
    s_i = lax.broadcasted_iota(jnp.int32, (B_CHUNK, B_CHUNK), 1)
    tri = t_i >= s_i
    tri_b = jnp.where(tri, 1.0, 0.0).astype(BF16)
    lane_sub = (lax.broadcasted_iota(jnp.int32, (1, GLA_REP), 1) // B_KEY_DIM) % N_SUB
    row_sub = lax.broadcasted_iota(jnp.int32, (B_CHUNK, 1), 0) // B_SUB
    q_mask = lane_sub <= row_sub
    k_mask = lane_sub == row_sub
    low_half = lax.broadcasted_iota(jnp.int32, (B_CHUNK, LANES), 1) < B_KEY_DIM

    def replicate(x):
        outs = []
        for pair in range(GLA_QK // LANES):
            v = x[:, pair * LANES:(pair + 1) * LANES]
            r = pltpu.roll(v, B_KEY_DIM, axis=1)
            outs += [jnp.where(low_half, v, r)] * (N_SUB // 2) + [jnp.where(low_half, r, v)] * (N_SUB // 2)
        return jnp.concatenate(outs, axis=1)

    def prepare(c):
        rows = pl.ds(pl.multiple_of(c * B_CHUNK, B_CHUNK), B_CHUNK)
        g = g_ref[rows, :]
        g_hi = g.astype(BF16)
        g_mid = (g - g_hi.astype(F32)).astype(BF16)
        g_lo = (g - g_hi.astype(F32) - g_mid.astype(F32)).astype(BF16)
        b = replicate(_dot(tri_b, g_hi) + _dot(tri_b, g_mid) + _dot(tri_b, g_lo))
        ref = jnp.zeros((1, GLA_REP), F32)
        for jj in range(1, N_SUB):
            ref = ref + jnp.where(lane_sub == jj, b[jj * B_SUB - 1:jj * B_SUB, :], 0.0)
        d = b - ref
        e_q = jnp.where(q_mask, jnp.exp(jnp.minimum(d, 0.0)), 0.0)
        e_k = jnp.where(k_mask, jnp.exp(jnp.where(k_mask, -d, 0.0)), 0.0)
        b_last = b[B_CHUNK - 1:B_CHUNK, :]
        q_c = (replicate(p_ref[rows, 0:GLA_QK]) * e_q).astype(BF16)
        k_raw = replicate(p_ref[rows, GLA_QK:2 * GLA_QK])
        k_c = (k_raw * e_k).astype(BF16)
        k_st = (k_raw * jnp.exp(b_last - b)).astype(BF16)
        return rows, q_c, k_c, k_st, jnp.exp(b_last)

    def chunk_group(i, carry):
        prep = [prepare(GLA_GROUP * i + u) for u in range(GLA_GROUP)]
        att, upd, val = {}, {}, {}
        for u, (rows, q_c, k_c, k_st, _) in enumerate(prep):
            for hh in range(B_HEADS):
                base = hh * N_SUB * B_KEY_DIM
                wide = slice(base, base + N_SUB * B_KEY_DIM)
                val[u, hh] = p_ref[rows, GLA_V0 + hh * B_VAL_DIM:GLA_V0 + (hh + 1) * B_VAL_DIM].astype(BF16)
                att[u, hh] = jnp.where(tri, _dot_nt(q_c[:, wide], k_c[:, wide]), 0.0).astype(BF16)
                upd[u, hh] = _dot(val[u, hh].T, k_st[:, base:base + B_KEY_DIM])
        for hh in range(B_HEADS):
            base = hh * N_SUB * B_KEY_DIM
            st = st_ref[hh]
            for u, (rows, q_c, _, _, e_last) in enumerate(prep):
                o = _dot(att[u, hh], val[u, hh]) + _dot_nt(q_c[:, base:base + B_KEY_DIM], st.astype(BF16))
                acc_ref[rows, hh * B_VAL_DIM:(hh + 1) * B_VAL_DIM] = o
                st = st * e_last[:, base:base + B_KEY_DIM] + upd[u, hh]
            st_ref[hh] = st
        return carry

    lax.fori_loop(0, ts // (GLA_GROUP * B_CHUNK), chunk_group, 0)

    r = p_ref[:, GLA_V0 + B_OUT:GLA_V0 + 2 * B_OUT]
    gate = r * (1.0 / (1.0 + jnp.exp(-r)))
    ng = ng_ref[...]
    for hh in range(B_HEADS):
        sl = slice(hh * B_VAL_DIM, (hh + 1) * B_VAL_DIM)
        o = acc_ref[:, sl]
        y = o * lax.rsqrt(jnp.mean(o * o, axis=-1, keepdims=True) + RMS_EPS) * ng
        o_ref[:, sl] = (gate[:, sl] * y).astype(BF16)


def _gla(h, w_b, wa, ba, norm_g, bsz, seq, ts=512):
    ns = seq // ts
    const = lambda shape: pl.BlockSpec(shape, lambda b, i: (0, 0))
    return pl.pallas_call(
        functools.partial(_gla_kernel, ts=ts),
        grid=(bsz, ns),
        in_specs=[pl.BlockSpec((ts, D_MODEL), lambda b, i: (b * ns + i, 0)),
                  const((D_MODEL, GLA_COLS)), const((LANES, GLA_QK)),
                  const((1, GLA_QK)), const((1, B_VAL_DIM))],
        out_specs=pl.BlockSpec((ts, B_OUT), lambda b, i: (b * ns + i, 0)),
        out_shape=jax.ShapeDtypeStruct((bsz * seq, B_OUT), BF16),
        scratch_shapes=[pltpu.VMEM((ts, GLA_COLS), F32),
                        pltpu.VMEM((ts, GLA_QK), F32),
                        pltpu.VMEM((ts, B_OUT), F32),
                        pltpu.VMEM((B_HEADS, B_VAL_DIM, B_KEY_DIM), F32)],
        compiler_params=_cparams(("parallel", "arbitrary")),
        name="gla",
    )(h, w_b, wa, ba, norm_g)


def _gmlp_kernel(h_ref, w_ref, lg_ref, lb_ref, ws_ref, bs_ref, o_ref, *, tm):
    z = _dot(h_ref[...].astype(BF16), w_ref[...])
    z = z * (0.5 * (1.0 + jnp.tanh(np.sqrt(2.0 / np.pi) * (z + 0.044715 * (z * z * z)))))
    u = z[:, :C_OUT]
    v = _layer_norm(z[:, C_OUT:], lg_ref[...], lb_ref[...]).astype(BF16)
    t_i = lax.broadcasted_iota(jnp.int32, (C_CHUNK, C_CHUNK), 0)
    s_i = lax.broadcasted_iota(jnp.int32, (C_CHUNK, C_CHUNK), 1)
    bs = bs_ref[...]
    for g in range(C_GROUPS):
        w = jnp.where(t_i >= s_i, ws_ref[g], 0.0).astype(BF16)
        sl = slice(g * C_GROUP_DIM, (g + 1) * C_GROUP_DIM)
        for c in range(tm // C_CHUNK):
            rows = slice(c * C_CHUNK, (c + 1) * C_CHUNK)
            mixed = _dot(w, v[rows, sl]) + bs[:, g:g + 1]
            o_ref[rows, sl] = (u[rows, sl] * mixed).astype(BF16)


def _gmlp(h, w_c, ln_g, ln_b, ws, bs_t, tm=512):
    n = h.shape[0]
    return pl.pallas_call(
        functools.partial(_gmlp_kernel, tm=tm),
        grid=(n // tm,),
        in_specs=[pl.BlockSpec((tm, D_MODEL), lambda i: (i, 0)),
                  pl.BlockSpec((D_MODEL, 2 * C_OUT), lambda i: (0, 0)),
                  pl.BlockSpec((1, C_OUT), lambda i: (0, 0)),
                  pl.BlockSpec((1, C_OUT), lambda i: (0, 0)),
                  pl.BlockSpec((C_GROUPS, C_CHUNK, C_CHUNK), lambda i: (0, 0, 0)),
                  pl.BlockSpec((C_CHUNK, LANES), lambda i: (0, 0))],
        out_specs=pl.BlockSpec((tm, C_OUT), lambda i: (i, 0)),
        out_shape=jax.ShapeDtypeStruct((n, C_OUT), BF16),
        compiler_params=_cparams(("parallel",)),
        name="gmlp",
    )(h, w_c, ln_g, ln_b, ws, bs_t)


def _sigmoid(x):
    return 1.0 / (1.0 + jnp.exp(-x))


def _merge_kernel(h_ref, ya_ref, yb_ref, yc_ref, wg_ref, wa_ref, wb_ref, wc_ref, wo_ref,
                  lg_ref, lb_ref, wr_ref, wrl_ref, br_ref, x_ref, oh_ref, route_ref):
    h = h_ref[...]
    hb = h.astype(BF16)
    acc = None
    for i, (y_ref, wbr_ref) in enumerate(((ya_ref, wa_ref), (yb_ref, wb_ref), (yc_ref, wc_ref))):
        gate = _sigmoid(_dot(hb, wg_ref[:, i * D_MODEL:(i + 1) * D_MODEL]))
        term = gate * _dot(y_ref[...], wbr_ref[...])
        acc = term if acc is None else acc + term
    mix = _dot(acc.astype(BF16), wo_ref[...])
    x1 = _layer_norm(DN_ALPHA * h + mix, lg_ref[...], lb_ref[...])
    x_ref[...] = x1

    x_hi = x1.astype(BF16)
    x_lo = (x1 - x_hi.astype(F32)).astype(BF16)
    logit = (_dot(x_hi, wr_ref[...]) + _dot(x_hi, wrl_ref[...]) + _dot(x_lo, wr_ref[...])) + br_ref[...]
    lane = lax.broadcasted_iota(jnp.int32, logit.shape, 1)
    big = jnp.int32(LANES)
    is_g = (lane >= N_EXPERTS) & (lane < N_EXPERTS + N_GROUPS)
    gl = jnp.where(is_g, logit, -jnp.inf)
    g_max = jnp.max(gl, axis=-1, keepdims=True)
    g_idx = jnp.min(jnp.where(is_g & (gl == g_max), lane, big), axis=-1, keepdims=True) - N_EXPERTS
    g_p = 1.0 / jnp.sum(jnp.where(is_g, jnp.exp(gl - g_max), 0.0), axis=-1, keepdims=True)
    lo = g_idx * EXPERTS_PER_GROUP
    in_g = (lane >= lo) & (lane < lo + EXPERTS_PER_GROUP)
    el = jnp.where(in_g, logit, -jnp.inf)
    e1 = jnp.max(el, axis=-1, keepdims=True)
    i1 = jnp.min(jnp.where(in_g & (el == e1), lane, big), axis=-1, keepdims=True)
    el2 = jnp.where(lane == i1, -jnp.inf, el)
    e2 = jnp.max(el2, axis=-1, keepdims=True)
    i2 = jnp.min(jnp.where(in_g & (lane != i1) & (el2 == e2), lane, big), axis=-1, keepdims=True)
    z2 = jnp.exp(e2 - e1)
    den = 1.0 + z2
    oh_ref[...] = jnp.where((lane == i1) | (lane == i2), 1.0, 0.0).astype(BF16)
    route_ref[...] = jnp.where(lane == 0, i1.astype(F32),
                               jnp.where(lane == 1, i2.astype(F32),
                                         jnp.where(lane == 2, (1.0 / den) * g_p,
                                                   jnp.where(lane == 3, (z2 / den) * g_p, 0.0))))


def _merge(h, ya, yb, yc, wg, wa, wb, wc, wo, ln_g, ln_b, wr, wr_lo, br, tm=1024):
    n = h.shape[0]
    row = lambda w: pl.BlockSpec((tm, w), lambda i: (i, 0))
    const = lambda r, c: pl.BlockSpec((r, c), lambda i: (0, 0), pipeline_mode=pl.Buffered(1))
    return pl.pallas_call(
        _merge_kernel,
        grid=(n // tm,),
        in_specs=[row(D_MODEL), row(A_OUT), row(B_OUT), row(C_OUT),
                  const(D_MODEL, 3 * D_MODEL), const(A_OUT, D_MODEL), const(B_OUT, D_MODEL),
                  const(C_OUT, D_MODEL), const(D_MODEL, D_MODEL),
                  const(1, D_MODEL), const(1, D_MODEL), const(D_MODEL, LANES), const(D_MODEL, LANES),
                  const(1, LANES)],
        out_specs=[row(D_MODEL), row(LANES), row(LANES)],
        out_shape=[jax.ShapeDtypeStruct((n, D_MODEL), F32),
                   jax.ShapeDtypeStruct((n, LANES), BF16),
                   jax.ShapeDtypeStruct((n, LANES), F32)],
        compiler_params=_cparams(("parallel",)),
        name="merge",
    )(h, ya, yb, yc, wg, wa, wb, wc, wo, ln_g, ln_b, wr, wr_lo, br)


FFN_TILE = 512
MOE_TOK = 1024


def _rank_kernel(oh_ref, route_ref, r_ref, cnt_ref, carry_ref):
    @pl.when(pl.program_id(0) == 0)
    def _():
        carry_ref[...] = jnp.zeros_like(carry_ref)

    oh = oh_ref[...]
    tt = oh.shape[0]
    r_i = lax.broadcasted_iota(jnp.int32, (tt, tt), 0)
    c_i = lax.broadcasted_iota(jnp.int32, (tt, tt), 1)
    lower = jnp.where(c_i < r_i, 1.0, 0.0).astype(BF16)
    rank = _dot(lower, oh) + carry_ref[...]
    route = route_ref[...]
    lane = lax.broadcasted_iota(jnp.int32, rank.shape, 1)
    i1 = route[:, 0:1].astype(jnp.int32)
    i2 = route[:, 1:2].astype(jnp.int32)
    r1 = jnp.sum(jnp.where(lane == i1, rank, 0.0), axis=-1, keepdims=True)
    r2 = jnp.sum(jnp.where(lane == i2, rank, 0.0), axis=-1, keepdims=True)
    r_ref[...] = jnp.where(lane == 0, r1, jnp.where(lane == 1, r2, 0.0))
    carry_ref[...] += jnp.sum(oh.astype(F32), axis=0, keepdims=True)
    cnt_ref[...] = carry_ref[...]


def _rank(oh, route, tt=512):
    n = oh.shape[0]
    return pl.pallas_call(
        _rank_kernel,
        grid=(n // tt,),
        in_specs=[pl.BlockSpec((tt, LANES), lambda i: (i, 0)),
                  pl.BlockSpec((tt, LANES), lambda i: (i, 0))],
        out_specs=[pl.BlockSpec((tt, LANES), lambda i: (i, 0)),
                   pl.BlockSpec((1, LANES), lambda i: (0, 0))],
        out_shape=[jax.ShapeDtypeStruct((n, LANES), F32),
                   jax.ShapeDtypeStruct((1, LANES), F32)],
        scratch_shapes=[pltpu.VMEM((1, LANES), F32)],
        compiler_params=_cparams(("arbitrary",)),
        name="moe_rank",
    )(oh, route)


def _row_copy(src_ref, src_row, dst_ref, dst_row, sem):
    return pltpu.make_async_copy(src_ref.at[pl.ds(src_row, 1), :], dst_ref.at[pl.ds(dst_row, 1), :], sem)


def _dispatch_kernel(pos1_ref, pos2_ref, x_ref, init_ref, xs_ref, sem):
    del init_ref
    base = pl.program_id(0) * MOE_TOK

    def issue(t, carry):
        _row_copy(x_ref, t, xs_ref, pos1_ref[base + t], sem).start()
        _row_copy(x_ref, t, xs_ref, pos2_ref[base + t], sem).start()
        return carry

    lax.fori_loop(0, MOE_TOK, issue, 0, unroll=32)
    for _ in range(2):
        pltpu.make_async_copy(x_ref, xs_ref.at[pl.ds(0, MOE_TOK), :], sem).wait()


def _dispatch(pos1, pos2, x1, n_slots, spare):
    n = x1.shape[0]
    init = jnp.zeros((n_slots, D_MODEL), F32) if spare is None else spare
    return pl.pallas_call(
        _dispatch_kernel,
        grid_spec=pltpu.PrefetchScalarGridSpec(
            num_scalar_prefetch=2,
            grid=(n // MOE_TOK,),
            in_specs=[pl.BlockSpec((MOE_TOK, D_MODEL), lambda i, p1, p2: (i, 0)),
                      pl.BlockSpec(memory_space=pl.ANY)],
            out_specs=pl.BlockSpec(memory_space=pl.ANY),
            scratch_shapes=[pltpu.SemaphoreType.DMA(())]),
        out_shape=jax.ShapeDtypeStruct((n_slots, D_MODEL), F32),
        input_output_aliases={3: 0},
        compiler_params=_cparams(("arbitrary",)),
        name="moe_dispatch",
    )(pos1, pos2, x1, init)


def _ffn_kernel(te_ref, nu_ref, xs_ref, wg_ref, wu_ref, wd_ref, ys_ref):
    del te_ref
    i = pl.program_id(0)

    @pl.when(i < nu_ref[0])
    def _():
        xb = xs_ref[...].astype(BF16)
        gte = _dot(xb, wg_ref[...].astype(BF16))
        hid = gte * _sigmoid(gte) * _dot(xb, wu_ref[...].astype(BF16))
        ys_ref[...] = _dot(hid.astype(BF16), wd_ref[...].astype(BF16))

    @pl.when(i >= nu_ref[0])
    def _():
        ys_ref[...] = jnp.zeros_like(ys_ref)


def _ffn(tile_expert, n_used, xs, w_gate, w_up, w_down, layer):
    n_slots = xs.shape[0]
    wspec = lambda r, c: pl.BlockSpec((None, None, r, c), lambda i, te, nu: (layer, te[i], 0, 0))
    return pl.pallas_call(
        _ffn_kernel,
        grid_spec=pltpu.PrefetchScalarGridSpec(
            num_scalar_prefetch=2,
            grid=(n_slots // FFN_TILE,),
            in_specs=[pl.BlockSpec((FFN_TILE, D_MODEL), lambda i, te, nu: (i, 0)),
                      wspec(D_MODEL, D_EXPERT), wspec(D_MODEL, D_EXPERT), wspec(D_EXPERT, D_MODEL)],
            out_specs=pl.BlockSpec((FFN_TILE, D_MODEL), lambda i, te, nu: (i, 0))),
        out_shape=jax.ShapeDtypeStruct((n_slots, D_MODEL), F32),
        compiler_params=_cparams(("arbitrary",)),
        name="moe_ffn",
    )(tile_expert, n_used, xs, w_gate, w_up, w_down)


def _combine_kernel(pos1_ref, pos2_ref, x_ref, route_ref, lg_ref, lb_ref, ys_ref, o_ref, buf_ref, sem):
    i = pl.program_id(0)

    def gather(step, slot):
        base = step * MOE_TOK

        def issue(t, carry):
            _row_copy(ys_ref, pos1_ref[base + t], buf_ref.at[slot, 0], t, sem.at[slot]).start()
            _row_copy(ys_ref, pos2_ref[base + t], buf_ref.at[slot, 1], t, sem.at[slot]).start()
            return carry

        lax.fori_loop(0, MOE_TOK, issue, 0, unroll=32)

    @pl.when(i == 0)
    def _():
        gather(0, 0)

    @pl.when(i + 1 < pl.num_programs(0))
    def _():
        gather(i + 1, (i + 1) % 2)

    slot = i % 2
    for s in range(2):
        pltpu.make_async_copy(ys_ref.at[pl.ds(0, MOE_TOK), :], buf_ref.at[slot, s], sem.at[slot]).wait()
    route = route_ref[...]
    ffn = route[:, 2:3] * buf_ref[slot, 0] + route[:, 3:4] * buf_ref[slot, 1]
    o_ref[...] = _layer_norm(DN_ALPHA * x_ref[...] + ffn, lg_ref[...], lb_ref[...])


def _combine(pos1, pos2, x1, route, ln_g, ln_b, ys):
    n = x1.shape[0]
    row = lambda w: pl.BlockSpec((MOE_TOK, w), lambda i, p1, p2: (i, 0))
    const = lambda w: pl.BlockSpec((1, w), lambda i, p1, p2: (0, 0))
    return pl.pallas_call(
        _combine_kernel,
        grid_spec=pltpu.PrefetchScalarGridSpec(
            num_scalar_prefetch=2,
            grid=(n // MOE_TOK,),
            in_specs=[row(D_MODEL), row(LANES), const(D_MODEL), const(D_MODEL),
                      pl.BlockSpec(memory_space=pl.ANY)],
            out_specs=row(D_MODEL),
            scratch_shapes=[pltpu.VMEM((2, 2, MOE_TOK, D_MODEL), F32),
                            pltpu.SemaphoreType.DMA((2,))]),
        out_shape=jax.ShapeDtypeStruct((n, D_MODEL), F32),
        compiler_params=_cparams(("arbitrary",)),
        name="moe_combine",
    )(pos1, pos2, x1, route, ln_g, ln_b, ys)


def _moe(x1, oh, route, w_gate, w_up, w_down, layer, ln_g, ln_b, spare):
    n = x1.shape[0]
    n_slots = TOP_K_INNER * n + N_EXPERTS * FFN_TILE
    rank, cnt = _rank(oh, route)
    cnt = cnt[0, :N_EXPERTS].astype(jnp.int32)
    padded = (cnt + FFN_TILE - 1) // FFN_TILE * FFN_TILE
    ends = jnp.cumsum(padded)
    offs = ends - padded
    experts = jnp.arange(N_EXPERTS, dtype=jnp.int32)
    i1 = route[:, 0].astype(jnp.int32)
    i2 = route[:, 1].astype(jnp.int32)
    off_of = lambda idx: jnp.sum(jnp.where(idx[:, None] == experts[None, :], offs[None, :], 0), axis=1)
    pos1 = off_of(i1) + rank[:, 0].astype(jnp.int32)
    pos2 = off_of(i2) + rank[:, 1].astype(jnp.int32)
    tile_start = jnp.arange(n_slots // FFN_TILE, dtype=jnp.int32) * FFN_TILE
    tile_expert = jnp.minimum(jnp.sum(tile_start[:, None] >= ends[None, :], axis=1), N_EXPERTS - 1)
    n_used = (ends[-1] // FFN_TILE).reshape(1)

    xs = _dispatch(pos1, pos2, x1, n_slots, spare)
    ys = _ffn(tile_expert.astype(jnp.int32), n_used.astype(jnp.int32), xs, w_gate, w_up, w_down, layer)
    return _combine(pos1, pos2, x1, route, ln_g, ln_b, ys), ys


def _rope_tables(positions):
    inv_freq = ROPE_THETA ** (-jnp.arange(0, ROT_DIM, 2, dtype=F32) / ROT_DIM)
    ang = positions.astype(F32).reshape(-1, 1) * inv_freq
    cos, sin = jnp.cos(ang), jnp.sin(ang)
    n = ang.shape[0]
    half = ROT_DIM // 2
    c64 = jnp.concatenate([cos, cos, jnp.ones((n, A_HEAD_DIM - ROT_DIM), F32)], axis=1)
    s64 = jnp.concatenate([-sin, sin, jnp.zeros((n, A_HEAD_DIM - ROT_DIM), F32)], axis=1)
    del half
    return jnp.tile(c64, (1, 2)), jnp.tile(s64, (1, 2))


def _split_w_in(w):
    widths = (A_OUT, A_HEAD_DIM, A_HEAD_DIM, IDX_HEADS * IDX_DIM, IDX_DIM, IDX_HEADS,
              B_HEADS * B_KEY_DIM, B_HEADS * B_KEY_DIM, B_OUT, B_GATE_RANK, B_OUT,
              2 * C_OUT, D_MODEL, D_MODEL, D_MODEL)
    pts = np.cumsum((0,) + widths)
    return [w[:, int(pts[i]):int(pts[i + 1])] for i in range(len(widths))]


def kernel(x, positions, ln_in_g, ln_in_b, w_in, idx_k_g, gla_wa2, gla_ba, gla_norm_g, gm_ln_g, gm_ln_b, gm_ws, gm_bs, w_branch_a, w_branch_b, w_branch_c, w_out, ln1_g, ln1_b, w_rg, b_rg, w_re, b_re, w_gate, w_up, w_down, ln2_g, ln2_b):
    bsz, seq, d = x.shape
    n = bsz * seq
    cos_t, sin_t = _rope_tables(positions)
    h = _entry_ln(x.reshape(n, d), ln_in_g, ln_in_b)
    spare = None
    for l in range(DEPTH):
        (a_q, a_k, a_v, i_q, i_k, i_w, b_q, b_k, b_v, b_glr, b_r, c_uv,
         g_a, g_b, g_c) = _split_w_in(w_in[l])
        zpad = lambda c: jnp.zeros((d, c), F32)
        w_a = jnp.concatenate([a_q, a_k, a_v, i_q, i_k, i_w, zpad(LANES - IDX_DIM - IDX_HEADS)],
                              axis=1).astype(BF16)
        w_b = jnp.concatenate([b_q * (B_KEY_DIM ** -0.5), b_k, b_v, b_r,
                               b_glr, zpad(LANES - B_GATE_RANK)], axis=1).astype(BF16)
        wa_pad = jnp.concatenate([gla_wa2[l],
                                  jnp.zeros((LANES - B_GATE_RANK, GLA_QK), F32)], axis=0).astype(BF16)
        w_g = jnp.concatenate([g_a, g_b, g_c], axis=1).astype(BF16)
        ikg = jnp.concatenate([idx_k_g[l], jnp.zeros((LANES - IDX_DIM,), F32)]).reshape(1, LANES)
        bs_t = jnp.concatenate([gm_bs[l].T, jnp.zeros((C_CHUNK, LANES - C_GROUPS), F32)], axis=1)
        w_r = jnp.concatenate([w_re[l], w_rg[l], zpad(LANES - N_GROUPS - N_EXPERTS)], axis=1)
        w_r_hi = w_r.astype(BF16)
        w_r_lo = (w_r - w_r_hi.astype(F32)).astype(BF16)
        b_r_all = jnp.concatenate([b_re[l], b_rg[l],
                                   jnp.zeros((LANES - N_GROUPS - N_EXPERTS,), F32)]).reshape(1, LANES)

        q, k, v, iq, ki, wi = _proj_a(h, w_a, cos_t, sin_t, ikg)
        y_a = _dsa(q, iq, wi, k, v, ki, bsz, seq)
        y_b = _gla(h, w_b, wa_pad, gla_ba[l].reshape(1, -1), gla_norm_g[l].reshape(1, -1), bsz, seq)
        y_c = _gmlp(h, c_uv.astype(BF16), gm_ln_g[l].reshape(1, -1), gm_ln_b[l].reshape(1, -1),
                    gm_ws[l], bs_t)
        x1, oh, route = _merge(h, y_a, y_b, y_c, w_g,
                               w_branch_a[l].astype(BF16), w_branch_b[l].astype(BF16),
                               w_branch_c[l].astype(BF16), w_out[l].astype(BF16),
                               ln1_g[l].reshape(1, -1), ln1_b[l].reshape(1, -1), w_r_hi, w_r_lo, b_r_all)
        h, spare = _moe(x1, oh, route, w_gate, w_up, w_down, l,
                        ln2_g[l].reshape(1, -1), ln2_b[l].reshape(1, -1), spare)
    return h.reshape(bsz, seq, d)
```

```python
import functools

import numpy as np
import jax
import jax.numpy as jnp
from jax import lax
from jax.experimental import pallas as pl
from jax.experimental.pallas import tpu as pltpu

F32 = jnp.float32
BF16 = jnp.bfloat16

D_MODEL = 1024
DEPTH = 2
A_HEADS = 8
A_HEAD_DIM = 64
A_OUT = 512
IDX_HEADS = 4
IDX_DIM = 64
TOPK_MAX = 256
Q_BLOCK = 256
ROPE_THETA = 500000.0
ROT_DIM = 16
B_HEADS = 4
B_KEY_DIM = 64
B_VAL_DIM = 128
B_OUT = 512
B_GATE_RANK = 16
B_GATE_TAU = 16.0
B_CHUNK = 64
B_SUB = 16
N_SUB = B_CHUNK // B_SUB
C_GROUPS = 4
C_GROUP_DIM = 128
C_OUT = 512
C_CHUNK = 128
N_GROUPS = 4
EXPERTS_PER_GROUP = 8
N_EXPERTS = 32
D_EXPERT = 256
TOP_K_INNER = 2
DN_ALPHA = (2 * DEPTH) ** 0.25
LN_EPS = 1e-5
RMS_EPS = 1e-6

LANES = 128
SUBLANES = 8
NEG_BIG = -1e30
LOG2_E = 1.4426950408889634
INT_MIN = -2 ** 31
VMEM_LIMIT = 56 * 1024 * 1024


def _cparams(sem):
    return pltpu.CompilerParams(dimension_semantics=sem, vmem_limit_bytes=VMEM_LIMIT)


def _layer_norm(x, g, b):
    mu = jnp.mean(x, axis=-1, keepdims=True)
    xc = x - mu
    var = jnp.mean(xc * xc, axis=-1, keepdims=True)
    return xc * lax.rsqrt(var + LN_EPS) * g + b


def _dot(a, b):
    return jnp.dot(a, b, preferred_element_type=F32)


def _dot_nt(a, b):
    return lax.dot_general(a, b, (((1,), (1,)), ((), ())), preferred_element_type=F32)


def _ln_kernel(x_ref, g_ref, b_ref, o_ref):
    o_ref[...] = _layer_norm(x_ref[...], g_ref[...], b_ref[...])


def _entry_ln(x, g, b, tm=1024):
    n = x.shape[0]
    return pl.pallas_call(
        _ln_kernel,
        grid=(n // tm,),
        in_specs=[pl.BlockSpec((tm, D_MODEL), lambda i: (i, 0)),
                  pl.BlockSpec((1, D_MODEL), lambda i: (0, 0)),
                  pl.BlockSpec((1, D_MODEL), lambda i: (0, 0))],
        out_specs=pl.BlockSpec((tm, D_MODEL), lambda i: (i, 0)),
        out_shape=jax.ShapeDtypeStruct((n, D_MODEL), F32),
        compiler_params=_cparams(("parallel",)),
        name="entry_ln",
    )(x, g.reshape(1, -1), b.reshape(1, -1))


def _rope_slab(xs, c, s):
    lane = lax.broadcasted_iota(jnp.int32, xs.shape, 1) % A_HEAD_DIM
    fwd = pltpu.roll(xs, LANES - ROT_DIM // 2, axis=1)
    bwd = pltpu.roll(xs, ROT_DIM // 2, axis=1)
    partner = jnp.where(lane < ROT_DIM // 2, fwd, bwd)
    return xs * c + partner * s


def _proj_a_kernel(h_ref, w_ref, c_ref, s_ref, g_ref,
                   q_ref, k_ref, v_ref, iq_ref, ki_ref, wi_ref):
    p = _dot(h_ref[...].astype(BF16), w_ref[...])
    c = c_ref[...]
    s = s_ref[...]
    lane = lax.broadcasted_iota(jnp.int32, c.shape, 1)
    lo = lane < A_HEAD_DIM
    n_blk = p.shape[0] // Q_BLOCK

    def put_heads(dst_ref, slab, first_head):
        for half in range(2):
            hd = first_head + half
            cols = slab[:, half * A_HEAD_DIM:(half + 1) * A_HEAD_DIM].astype(BF16)
            for qb in range(n_blk):
                dst_ref[qb, hd * Q_BLOCK:(hd + 1) * Q_BLOCK, :] = cols[qb * Q_BLOCK:(qb + 1) * Q_BLOCK, :]

    for i in range(4):
        sl = slice(i * LANES, (i + 1) * LANES)
        put_heads(q_ref, _rope_slab(p[:, sl], c, s) * (A_HEAD_DIM ** -0.5 * LOG2_E), 2 * i)
    kv = p[:, 512:640]
    kv = _rope_slab(kv, jnp.where(lo, c, 1.0), jnp.where(lo, s, 0.0))
    k_ref[...] = kv[:, :A_HEAD_DIM].astype(BF16)
    v_ones = jnp.where(lo, pltpu.roll(kv, A_HEAD_DIM, axis=1), jnp.where(lane == A_HEAD_DIM, 1.0, 0.0))
    v_ref[...] = v_ones.astype(BF16)
    for i in range(2):
        sl = slice(640 + i * LANES, 640 + (i + 1) * LANES)
        put_heads(iq_ref, _rope_slab(p[:, sl], c, s), 2 * i)
    last = p[:, 896:1024]
    mu = jnp.sum(jnp.where(lo, last, 0.0), axis=-1, keepdims=True) * (1.0 / IDX_DIM)
    xc = jnp.where(lo, last - mu, 0.0)
    var = jnp.sum(xc * xc, axis=-1, keepdims=True) * (1.0 / IDX_DIM)
    kin = xc * lax.rsqrt(var + LN_EPS) * g_ref[...]
    kin = _rope_slab(kin, jnp.where(lo, c, 1.0), jnp.where(lo, s, 0.0))
    ki_ref[...] = kin[:, :IDX_DIM].astype(BF16)
    wi = pltpu.roll(last, LANES - IDX_DIM, axis=1) * (IDX_HEADS ** -0.5 * IDX_DIM ** -0.5)
    for qb in range(n_blk):
        wi_ref[qb] = wi[qb * Q_BLOCK:(qb + 1) * Q_BLOCK, :].T[0:SUBLANES, :]


def _proj_a(h, w_a, cos_t, sin_t, ikg, tm=512):
    n = h.shape[0]
    row = lambda w: pl.BlockSpec((tm, w), lambda i: (i, 0))
    stacked = lambda heads: pl.BlockSpec((tm // Q_BLOCK, heads * Q_BLOCK, A_HEAD_DIM), lambda i: (i, 0, 0))
    return pl.pallas_call(
        _proj_a_kernel,
        grid=(n // tm,),
        in_specs=[row(D_MODEL),
                  pl.BlockSpec((D_MODEL, 1024), lambda i: (0, 0)),
                  row(LANES), row(LANES),
                  pl.BlockSpec((1, LANES), lambda i: (0, 0))],
        out_specs=[stacked(A_HEADS), row(A_HEAD_DIM),
                   row(LANES),
                   stacked(IDX_HEADS), row(IDX_DIM),
                   pl.BlockSpec((tm // Q_BLOCK, SUBLANES, Q_BLOCK), lambda i: (i, 0, 0))],
        out_shape=[jax.ShapeDtypeStruct((n // Q_BLOCK, A_HEADS * Q_BLOCK, A_HEAD_DIM), BF16),
                   jax.ShapeDtypeStruct((n, A_HEAD_DIM), BF16),
                   jax.ShapeDtypeStruct((n, LANES), BF16),
                   jax.ShapeDtypeStruct((n // Q_BLOCK, IDX_HEADS * Q_BLOCK, IDX_DIM), BF16),
                   jax.ShapeDtypeStruct((n, IDX_DIM), BF16),
                   jax.ShapeDtypeStruct((n // Q_BLOCK, SUBLANES, Q_BLOCK), F32)],
        compiler_params=_cparams(("parallel",)),
        name="proj_a",
    )(h, w_a, cos_t, sin_t, ikg)


KEY_CHUNK = 512
COUNT_ROWS = 16
FIELD_BITS = 15
LOW_BITS = 32 - 2 * FIELD_BITS
FIELD_MASK = (1 << FIELD_BITS) - 1
FIELD_STRIDE = FIELD_BITS + 1
HALF_WORD_MASK = (1 << FIELD_STRIDE) - 1
GUARD_BITS = ((1 << FIELD_BITS) | (1 << (FIELD_BITS + FIELD_STRIDE))) - 2 ** 32
PAIR_ONE = 1 | (1 << FIELD_STRIDE)


def _dsa_kernel(q_ref, iq_ref, wi_ref, k_ref, v_ref, ki_ref, o_ref,
                key_ref, pk_ref, bias_ref, *, seq):
    j = pl.program_id(1)
    n_sel = min(TOPK_MAX, seq // 4)
    n_kc = (j * Q_BLOCK + Q_BLOCK + KEY_CHUNK - 1) // KEY_CHUNK
    qpos = j * Q_BLOCK + lax.broadcasted_iota(jnp.int32, (1, Q_BLOCK), 1)
    kk = jnp.minimum(n_sel, qpos + 1).astype(F32)
    row_pos = lax.broadcasted_iota(jnp.int32, (KEY_CHUNK, 1), 0)

    def key_rows(c):
        return pl.ds(pl.multiple_of(c * KEY_CHUNK, KEY_CHUNK), KEY_CHUNK)

    def chunk_loop(body, init):
        def pair(i, carry):
            return body(2 * i + 1, body(2 * i, carry))
        carry = lax.fori_loop(0, n_kc // 2, pair, init)
        return lax.cond(n_kc % 2 == 1, lambda cr: body(n_kc - 1, cr), lambda cr: cr, carry)

    iq = iq_ref[...]
    wi = wi_ref[...]

    def score_chunk(c, carry):
        d = jnp.maximum(_dot_nt(ki_ref[key_rows(c), :], iq), 0.0)
        score = wi[0:1, :] * d[:, 0:Q_BLOCK]
        for h in range(1, IDX_HEADS):
            score = score + wi[h:h + 1, :] * d[:, h * Q_BLOCK:(h + 1) * Q_BLOCK]
        bits = pltpu.bitcast(score + 0.0, jnp.int32)
        key = bits ^ ((bits >> 31) & 0x7FFFFFFF)
        key = jnp.where(c * KEY_CHUNK + row_pos <= qpos, key, INT_MIN)
        key_ref[c] = key
        pk_ref[c] = pack(lax.shift_right_logical(key ^ INT_MIN, FIELD_BITS + LOW_BITS))
        return carry

    def pack(f):
        hf = KEY_CHUNK // 2
        return (f[:hf] << FIELD_STRIDE) | f[hf:] | GUARD_BITS

    chunk_loop(score_chunk, 0)

    def count(pred):
        def body(c, acc):
            m = jnp.where(pred(key_ref[c]), 1.0, 0.0)
            return acc + jnp.sum(m.reshape(KEY_CHUNK // COUNT_ROWS, COUNT_ROWS, Q_BLOCK), axis=0)
        acc = chunk_loop(body, jnp.zeros((COUNT_ROWS, Q_BLOCK), F32))
        return jnp.sum(acc, axis=0, keepdims=True)

    def count_packed(cand):
        both = (cand << FIELD_STRIDE) | cand

        def body(c, acc):
            flags = ((pk_ref[c] - both) >> FIELD_BITS) & PAIR_ONE
            parts = [flags[t * COUNT_ROWS:(t + 1) * COUNT_ROWS] for t in range(KEY_CHUNK // 2 // COUNT_ROWS)]
            while len(parts) > 1:
                parts = [a + b for a, b in zip(parts[::2], parts[1::2])]
            return acc + parts[0]

        acc = chunk_loop(body, jnp.zeros((COUNT_ROWS, Q_BLOCK), jnp.int32))
        return jnp.sum(((acc & HALF_WORD_MASK) + (acc >> FIELD_STRIDE)).astype(F32), axis=0, keepdims=True)

    def search_field(target):
        def step(i, t):
            cand = t | jnp.left_shift(jnp.int32(1), FIELD_BITS - 1 - i)
            return jnp.where(count_packed(cand) >= target, cand, t)
        return lax.fori_loop(0, FIELD_BITS, step, jnp.zeros((1, Q_BLOCK), jnp.int32))

    t_a = search_field(kk)
    top_of_a = ((t_a << (FIELD_BITS + LOW_BITS)) | ((1 << (FIELD_BITS + LOW_BITS)) - 1)) ^ INT_MIN
    need_b = kk - count(lambda kc: kc > top_of_a)

    def repack(c, carry):
        u = key_ref[c] ^ INT_MIN
        member = lax.shift_right_logical(u, FIELD_BITS + LOW_BITS) == t_a
        pk_ref[c] = pack(jnp.where(member, lax.shift_right_logical(u, LOW_BITS) & FIELD_MASK, 0))
        return carry

    chunk_loop(repack, 0)
    base = (t_a << (FIELD_BITS + LOW_BITS)) | (search_field(need_b) << LOW_BITS)

    def low_step(i, t_c):
        bit = jnp.left_shift(jnp.int32(1), LOW_BITS - 1 - i)
        cand_s = (base | t_c | bit) ^ INT_MIN
        return jnp.where(count(lambda kc: kc >= cand_s) >= kk, t_c | bit, t_c)

    thr = (base | lax.fori_loop(0, LOW_BITS, low_step, jnp.zeros((1, Q_BLOCK), jnp.int32))) ^ INT_MIN
    need = kk - count(lambda kc: kc > thr)
    r_i = lax.broadcasted_iota(jnp.int32, (LANES, LANES), 0)
    c_i = lax.broadcasted_iota(jnp.int32, (LANES, LANES), 1)
    lower = jnp.where(c_i < r_i, 1.0, 0.0).astype(BF16)

    def fill(c, carry):
        for t in range(KEY_CHUNK // LANES):
            rows = slice(t * LANES, (t + 1) * LANES)
            kt = key_ref[c, rows, :]
            eq = kt == thr
            eq_f = jnp.where(eq, 1.0, 0.0)
            pre = _dot(lower, eq_f.astype(BF16)) + carry
            sel = (kt > thr) | (eq & (pre < need))
            bias_ref[c, rows, :] = jnp.where(sel, 0.0, NEG_BIG)
            carry = carry + jnp.sum(eq_f, axis=0, keepdims=True)
        return carry

    chunk_loop(fill, jnp.zeros((1, Q_BLOCK), F32))

    q = q_ref[...]
    cols = A_HEADS * Q_BLOCK

    def attn_chunk(c, carry):
        m, acc = carry
        rows = key_rows(c)
        lg = _dot_nt(k_ref[rows, :], q) + jnp.concatenate([bias_ref[c]] * A_HEADS, axis=1)
        m_new = jnp.maximum(m, jnp.max(lg, axis=0, keepdims=True))
        p = jnp.exp2(lg - m_new).astype(BF16)
        pv = lax.dot_general(v_ref[rows, :], p, (((0,), (0,)), ((), ())),
                             preferred_element_type=F32)
        return m_new, jnp.exp2(m - m_new) * acc + pv

    init = (jnp.full((1, cols), -3e38, F32), jnp.zeros((LANES, cols), F32))
    _, acc = chunk_loop(attn_chunk, init)
    o = acc[:A_HEAD_DIM] / acc[A_HEAD_DIM:A_HEAD_DIM + 1]
    for h in range(A_HEADS):
        o_ref[:, h * A_HEAD_DIM:(h + 1) * A_HEAD_DIM] = (
            o[:, h * Q_BLOCK:(h + 1) * Q_BLOCK].T.astype(BF16))


def _dsa(q, iq, wi, k, v, ki, bsz, seq):
    nq = seq // Q_BLOCK
    n_kc = seq // KEY_CHUNK
    stacked = lambda heads: pl.BlockSpec((None, heads * Q_BLOCK, A_HEAD_DIM),
                                         lambda b, j: (b * nq + j, 0, 0))
    kspec = lambda w: pl.BlockSpec((seq, w), lambda b, j: (b, 0))
    return pl.pallas_call(
        functools.partial(_dsa_kernel, seq=seq),
        grid=(bsz, nq),
        in_specs=[stacked(A_HEADS), stacked(IDX_HEADS),
                  pl.BlockSpec((None, SUBLANES, Q_BLOCK), lambda b, j: (b * nq + j, 0, 0)),
                  kspec(A_HEAD_DIM),
                  kspec(LANES),
                  kspec(IDX_DIM)],
        out_specs=pl.BlockSpec((Q_BLOCK, A_OUT), lambda b, j: (b * nq + j, 0)),
        out_shape=jax.ShapeDtypeStruct((bsz * seq, A_OUT), BF16),
        scratch_shapes=[pltpu.VMEM((n_kc, KEY_CHUNK, Q_BLOCK), jnp.int32),
                        pltpu.VMEM((n_kc, KEY_CHUNK // 2, Q_BLOCK), jnp.int32),
                        pltpu.VMEM((n_kc, KEY_CHUNK, Q_BLOCK), F32)],
        compiler_params=_cparams(("parallel", "parallel")),
        name="dsa",
    )(q, iq, wi, k, v, ki)


GLA_QK = B_HEADS * B_KEY_DIM
GLA_REP = N_SUB * GLA_QK
GLA_COLS = 2 * GLA_QK + 2 * B_OUT + LANES
GLA_V0 = 2 * GLA_QK
GLA_GROUP = 8


def _gla_kernel(h_ref, w_ref, wa_ref, ba_ref, ng_ref, o_ref, p_ref, g_ref, acc_ref, st_ref, *, ts):
    @pl.when(pl.program_id(1) == 0)
    def _():
        st_ref[...] = jnp.zeros_like(st_ref)

    p_ref[...] = _dot(h_ref[...].astype(BF16), w_ref[...])
    x = _dot(p_ref[:, GLA_V0 + 2 * B_OUT:].astype(BF16), wa_ref[...]) + ba_ref[...]
    g_ref[...] = (jnp.minimum(x, 0.0) - jnp.log1p(jnp.exp(-jnp.abs(x)))) * (1.0 / B_GATE_TAU)

    t_i = lax.broadcasted_iota(jnp.int32, (B_CHUNK, B_CHUNK), 0)
    s_i = lax.broadcasted_iota(jnp.int32, (B_CHUNK, B_CHUNK), 1)
    tri = t_i >= s_i
    tri_b = jnp.where(tri, 1.0, 0.0).astype(BF16)
    lane_sub = (lax.broadcasted_iota(jnp.int32, (1, GLA_REP), 1) // B_KEY_DIM) % N_SUB
    row_sub = lax.broadcasted_iota(jnp.int32, (B_CHUNK, 1), 0) // B_SUB
    q_mask = lane_sub <= row_sub
    k_mask = lane_sub == row_sub
    low_half = lax.broadcasted_iota(jnp.int32, (B_CHUNK, LANES), 1) < B_KEY_DIM

    def replicate(x):
        outs = []
        for pair in range(GLA_QK // LANES):
            v = x[:, pair * LANES:(pair + 1) * LANES]
            r = pltpu.roll(v, B_KEY_DIM, axis=1)
            outs += [jnp.where(low_half, v, r)] * (N_SUB // 2) + [jnp.where(low_half, r, v)] * (N_SUB // 2)
        return jnp.concatenate(outs, axis=1)

    def prepare(c):
        rows = pl.ds(pl.multiple_of(c * B_CHUNK, B_CHUNK), B_CHUNK)
        g = g_ref[rows, :]
        g_hi = g.astype(BF16)
        g_mid = (g - g_hi.astype(F32)).astype(BF16)
        g_lo = (g - g_hi.astype(F32) - g_mid.astype(F32)).astype(BF16)
        b = replicate(_dot(tri_b, g_hi) + _dot(tri_b, g_mid) + _dot(tri_b, g_lo))
        ref = jnp.zeros((1, GLA_REP), F32)
        for jj in range(1, N_SUB):
            ref = ref + jnp.where(lane_sub == jj, b[jj * B_SUB - 1:jj * B_SUB, :], 0.0)
        d = b - ref
        e_q = jnp.where(q_mask, jnp.exp(jnp.minimum(d, 0.0)), 0.0)
        e_k = jnp.where(k_mask, jnp.exp(jnp.where(k_mask, -d, 0.0)), 0.0)
        b_last = b[B_CHUNK - 1:B_CHUNK, :]
        q_c = (replicate(p_ref[rows, 0:GLA_QK]) * e_q).astype(BF16)
        k_raw = replicate(p_ref[rows, GLA_QK:2 * GLA_QK])
        k_c = (k_raw * e_k).astype(BF16)
        k_st = (k_raw * jnp.exp(b_last - b)).astype(BF16)
        return rows, q_c, k_c, k_st, jnp.exp(b_last)

    def chunk_group(i, carry):
        prep = [prepare(GLA_GROUP * i + u) for u in range(GLA_GROUP)]
        att, upd, val = {}, {}, {}
        for u, (rows, q_c, k_c, k_st, _) in enumerate(prep):
            for hh in range(B_HEADS):
                base = hh * N_SUB * B_KEY_DIM
                wide = slice(base, base + N_SUB * B_KEY_DIM)
                val[u, hh] = p_ref[rows, GLA_V0 + hh * B_VAL_DIM:GLA_V0 + (hh + 1) * B_VAL_DIM].astype(BF16)
                att[u, hh] = jnp.where(tri, _dot_nt(q_c[:, wide], k_c[:, wide]), 0.0).astype(BF16)
                upd[u, hh] = _dot(val[u, hh].T, k_st[:, base:base + B_KEY_DIM])
        for hh in range(B_HEADS):
            base = hh * N_SUB * B_KEY_DIM
            st = st_ref[hh]
            for u, (rows, q_c, _, _, e_last) in enumerate(prep):
                o = _dot(att[u, hh], val[u, hh]) + _dot_nt(q_c[:, base:base + B_KEY_DIM], st.astype(BF16))
                acc_ref[rows, hh * B_VAL_DIM:(hh + 1) * B_VAL_DIM] = o
                st = st * e_last[:, base:base + B_KEY_DIM] + upd[u, hh]
            st_ref[hh] = st
        return carry

    lax.fori_loop(0, ts // (GLA_GROUP * B_CHUNK), chunk_group, 0)

    r = p_ref[:, GLA_V0 + B_OUT:GLA_V0 + 2 * B_OUT]
    gate = r * (1.0 / (1.0 + jnp.exp(-r)))
    ng = ng_ref[...]
    for hh in range(B_HEADS):
        sl = slice(hh * B_VAL_DIM, (hh + 1) * B_VAL_DIM)
        o = acc_ref[:, sl]
        y = o * lax.rsqrt(jnp.mean(o * o, axis=-1, keepdims=True) + RMS_EPS) * ng
        o_ref[:, sl] = (gate[:, sl] * y).astype(BF16)


def _gla(h, w_b, wa, ba, norm_g, bsz, seq, ts=512):
    ns = seq // ts
    const = lambda shape: pl.BlockSpec(shape, lambda b, i: (0, 0))
    return pl.pallas_call(
        functools.partial(_gla_kernel, ts=ts),
        grid=(bsz, ns),
        in_specs=[pl.BlockSpec((ts, D_MODEL), lambda b, i: (b * ns + i, 0)),
                  const((D_MODEL, GLA_COLS)), const((LANES, GLA_QK)),
                  const((1, GLA_QK)), const((1, B_VAL_DIM))],
        out_specs=pl.BlockSpec((ts, B_OUT), lambda b, i: (b * ns + i, 0)),
        out_shape=jax.ShapeDtypeStruct((bsz * seq, B_OUT), BF16),
        scratch_shapes=[pltpu.VMEM((ts, GLA_COLS), F32),
                        pltpu.VMEM((ts, GLA_QK), F32),
                        pltpu.VMEM((ts, B_OUT), F32),
                        pltpu.VMEM((B_HEADS, B_VAL_DIM, B_KEY_DIM), F32)],
        compiler_params=_cparams(("parallel", "arbitrary")),
        name="gla",
    )(h, w_b, wa, ba, norm_g)


def _gmlp_kernel(h_ref, w_ref, lg_ref, lb_ref, ws_ref, bs_ref, o_ref, *, tm):
    z = _dot(h_ref[...].astype(BF16), w_ref[...])
    z = z * (0.5 * (1.0 + jnp.tanh(np.sqrt(2.0 / np.pi) * (z + 0.044715 * (z * z * z)))))
    u = z[:, :C_OUT]
    v = _layer_norm(z[:, C_OUT:], lg_ref[...], lb_ref[...]).astype(BF16)
    t_i = lax.broadcasted_iota(jnp.int32, (C_CHUNK, C_CHUNK), 0)
    s_i = lax.broadcasted_iota(jnp.int32, (C_CHUNK, C_CHUNK), 1)
    bs = bs_ref[...]
    for g in range(C_GROUPS):
        w = jnp.where(t_i >= s_i, ws_ref[g], 0.0).astype(BF16)
        sl = slice(g * C_GROUP_DIM, (g + 1) * C_GROUP_DIM)
        for c in range(tm // C_CHUNK):
            rows = slice(c * C_CHUNK, (c + 1) * C_CHUNK)
            mixed = _dot(w, v[rows, sl]) + bs[:, g:g + 1]
            o_ref[rows, sl] = (u[rows, sl] * mixed).astype(BF16)


def _gmlp(h, w_c, ln_g, ln_b, ws, bs_t, tm=512):
    n = h.shape[0]
    return pl.pallas_call(
        functools.partial(_gmlp_kernel, tm=tm),
        grid=(n // tm,),
        in_specs=[pl.BlockSpec((tm, D_MODEL), lambda i: (i, 0)),
                  pl.BlockSpec((D_MODEL, 2 * C_OUT), lambda i: (0, 0)),
                  pl.BlockSpec((1, C_OUT), lambda i: (0, 0)),
                  pl.BlockSpec((1, C_OUT), lambda i: (0, 0)),
                  pl.BlockSpec((C_GROUPS, C_CHUNK, C_CHUNK), lambda i: (0, 0, 0)),
                  pl.BlockSpec((C_CHUNK, LANES), lambda i: (0, 0))],
        out_specs=pl.BlockSpec((tm, C_OUT), lambda i: (i, 0)),
        out_shape=jax.ShapeDtypeStruct((n, C_OUT), BF16),
        compiler_params=_cparams(("parallel",)),
        name="gmlp",
    )(h, w_c, ln_g, ln_b, ws, bs_t)


def _sigmoid(x):
    return 1.0 / (1.0 + jnp.exp(-x))


def _merge_kernel(h_ref, ya_ref, yb_ref, yc_ref, wg_ref, wa_ref, wb_ref, wc_ref, wo_ref,
                  lg_ref, lb_ref, wr_ref, wrl_ref, br_ref, x_ref, oh_ref, route_ref):
    h = h_ref[...]
    hb = h.astype(BF16)
    acc = None
    for i, (y_ref, wbr_ref) in enumerate(((ya_ref, wa_ref), (yb_ref, wb_ref), (yc_ref, wc_ref))):
        gate = _sigmoid(_dot(hb, wg_ref[:, i * D_MODEL:(i + 1) * D_MODEL]))
        term = gate * _dot(y_ref[...], wbr_ref[...])
        acc = term if acc is None else acc + term
    mix = _dot(acc.astype(BF16), wo_ref[...])
    x1 = _layer_norm(DN_ALPHA * h + mix, lg_ref[...], lb_ref[...])
    x_ref[...] = x1

    x_hi = x1.astype(BF16)
    x_lo = (x1 - x_hi.astype(F32)).astype(BF16)
    logit = (_dot(x_hi, wr_ref[...]) + _dot(x_hi, wrl_ref[...]) + _dot(x_lo, wr_ref[...])) + br_ref[...]
    lane = lax.broadcasted_iota(jnp.int32, logit.shape, 1)
    big = jnp.int32(LANES)
    is_g = (lane >= N_EXPERTS) & (lane < N_EXPERTS + N_GROUPS)
    gl = jnp.where(is_g, logit, -jnp.inf)
    g_max = jnp.max(gl, axis=-1, keepdims=True)
    g_idx = jnp.min(jnp.where(is_g & (gl == g_max), lane, big), axis=-1, keepdims=True) - N_EXPERTS
    g_p = 1.0 / jnp.sum(jnp.where(is_g, jnp.exp(gl - g_max), 0.0), axis=-1, keepdims=True)
    lo = g_idx * EXPERTS_PER_GROUP
    in_g = (lane >= lo) & (lane < lo + EXPERTS_PER_GROUP)
    el = jnp.where(in_g, logit, -jnp.inf)
    e1 = jnp.max(el, axis=-1, keepdims=True)
    i1 = jnp.min(jnp.where(in_g & (el == e1), lane, big), axis=-1, keepdims=True)
    el2 = jnp.where(lane == i1, -jnp.inf, el)
    e2 = jnp.max(el2, axis=-1, keepdims=True)
    i2 = jnp.min(jnp.where(in_g & (lane != i1) & (el2 == e2), lane, big), axis=-1, keepdims=True)
    z2 = jnp.exp(e2 - e1)
    den = 1.0 + z2
    oh_ref[...] = jnp.where((lane == i1) | (lane == i2), 1.0, 0.0).astype(BF16)
    route_ref[...] = jnp.where(lane == 0, i1.astype(F32),
                               jnp.where(lane == 1, i2.astype(F32),
                                         jnp.where(lane == 2, (1.0 / den) * g_p,
                                                   jnp.where(lane == 3, (z2 / den) * g_p, 0.0))))


def _merge(h, ya, yb, yc, wg, wa, wb, wc, wo, ln_g, ln_b, wr, wr_lo, br, tm=1024):
    n = h.shape[0]
    row = lambda w: pl.BlockSpec((tm, w), lambda i: (i, 0))
    const = lambda r, c: pl.BlockSpec((r, c), lambda i: (0, 0), pipeline_mode=pl.Buffered(1))
    return pl.pallas_call(
        _merge_kernel,
        grid=(n // tm,),
        in_specs=[row(D_MODEL), row(A_OUT), row(B_OUT), row(C_OUT),
                  const(D_MODEL, 3 * D_MODEL), const(A_OUT, D_MODEL), const(B_OUT, D_MODEL),
                  const(C_OUT, D_MODEL), const(D_MODEL, D_MODEL),
                  const(1, D_MODEL), const(1, D_MODEL), const(D_MODEL, LANES), const(D_MODEL, LANES),
                  const(1, LANES)],
        out_specs=[row(D_MODEL), row(LANES), row(LANES)],
        out_shape=[jax.ShapeDtypeStruct((n, D_MODEL), F32),
                   jax.ShapeDtypeStruct((n, LANES), BF16),
                   jax.ShapeDtypeStruct((n, LANES), F32)],
        compiler_params=_cparams(("parallel",)),
        name="merge",
    )(h, ya, yb, yc, wg, wa, wb, wc, wo, ln_g, ln_b, wr, wr_lo, br)


FFN_TILE = 512
MOE_TOK = 512
COMBINE_GROUP = 128


def _rank_kernel(oh_ref, route_ref, r_ref, cnt_ref, carry_ref):
    @pl.when(pl.program_id(0) == 0)
    def _():
        carry_ref[...] = jnp.zeros_like(carry_ref)

    oh = oh_ref[...]
    tt = oh.shape[0]
    r_i = lax.broadcasted_iota(jnp.int32, (tt, tt), 0)
    c_i = lax.broadcasted_iota(jnp.int32, (tt, tt), 1)
    lower = jnp.where(c_i < r_i, 1.0, 0.0).astype(BF16)
    rank = _dot(lower, oh) + carry_ref[...]
    route = route_ref[...]
    lane = lax.broadcasted_iota(jnp.int32, rank.shape, 1)
    i1 = route[:, 0:1].astype(jnp.int32)
    i2 = route[:, 1:2].astype(jnp.int32)
    r1 = jnp.sum(jnp.where(lane == i1, rank, 0.0), axis=-1, keepdims=True)
    r2 = jnp.sum(jnp.where(lane == i2, rank, 0.0), axis=-1, keepdims=True)
    r_ref[...] = jnp.where(lane == 0, r1, jnp.where(lane == 1, r2, 0.0))
    carry_ref[...] += jnp.sum(oh.astype(F32), axis=0, keepdims=True)
    cnt_ref[...] = carry_ref[...]


def _rank(oh, route, tt=512):
    n = oh.shape[0]
    return pl.pallas_call(
        _rank_kernel,
        grid=(n // tt,),
        in_specs=[pl.BlockSpec((tt, LANES), lambda i: (i, 0)),
                  pl.BlockSpec((tt, LANES), lambda i: (i, 0))],
        out_specs=[pl.BlockSpec((tt, LANES), lambda i: (i, 0)),
                   pl.BlockSpec((1, LANES), lambda i: (0, 0))],
        out_shape=[jax.ShapeDtypeStruct((n, LANES), F32),
                   jax.ShapeDtypeStruct((1, LANES), F32)],
        scratch_shapes=[pltpu.VMEM((1, LANES), F32)],
        compiler_params=_cparams(("arbitrary",)),
        name="moe_rank",
    )(oh, route)


def _row_copy(src_ref, src_row, dst_ref, dst_row, sem):
    return pltpu.make_async_copy(src_ref.at[pl.ds(src_row, 1), :], dst_ref.at[pl.ds(dst_row, 1), :], sem)


def _dispatch_kernel(pos1_ref, pos2_ref, x_ref, init_ref, xs_ref, sem):
    del init_ref
    base = pl.program_id(0) * MOE_TOK

    def issue(t, carry):
        _row_copy(x_ref, t, xs_ref, pos1_ref[base + t], sem).start()
        _row_copy(x_ref, t, xs_ref, pos2_ref[base + t], sem).start()
        return carry

    lax.fori_loop(0, MOE_TOK, issue, 0, unroll=32)
    for _ in range(2):
        pltpu.make_async_copy(x_ref, xs_ref.at[pl.ds(0, MOE_TOK), :], sem).wait()


def _dispatch(pos1, pos2, x1, n_slots, spare):
    n = x1.shape[0]
    init = jnp.zeros((n_slots, D_MODEL), F32) if spare is None else spare
    return pl.pallas_call(
        _dispatch_kernel,
        grid_spec=pltpu.PrefetchScalarGridSpec(
            num_scalar_prefetch=2,
            grid=(n // MOE_TOK,),
            in_specs=[pl.BlockSpec((MOE_TOK, D_MODEL), lambda i, p1, p2: (i, 0)),
                      pl.BlockSpec(memory_space=pl.ANY)],
            out_specs=pl.BlockSpec(memory_space=pl.ANY),
            scratch_shapes=[pltpu.SemaphoreType.DMA(())]),
        out_shape=jax.ShapeDtypeStruct((n_slots, D_MODEL), F32),
        input_output_aliases={3: 0},
        compiler_params=_cparams(("arbitrary",)),
        name="moe_dispatch",
    )(pos1, pos2, x1, init)


def _ffn_kernel(te_ref, nu_ref, xs_ref, wg_ref, wu_ref, wd_ref, ys_ref):
    del te_ref
    i = pl.program_id(0)

    @pl.when(i < nu_ref[0])
    def _():
        xb = xs_ref[...].astype(BF16)
        gte = _dot(xb, wg_ref[...].astype(BF16))
        hid = gte * _sigmoid(gte) * _dot(xb, wu_ref[...].astype(BF16))
        ys_ref[...] = _dot(hid.astype(BF16), wd_ref[...].astype(BF16))

    @pl.when(i >= nu_ref[0])
    def _():
        ys_ref[...] = jnp.zeros_like(ys_ref)


def _ffn(tile_expert, n_used, xs, w_gate, w_up, w_down, layer):
    n_slots = xs.shape[0]
    wspec = lambda r, c: pl.BlockSpec((None, None, r, c), lambda i, te, nu: (layer, te[i], 0, 0))
    return pl.pallas_call(
        _ffn_kernel,
        grid_spec=pltpu.PrefetchScalarGridSpec(
            num_scalar_prefetch=2,
            grid=(n_slots // FFN_TILE,),
            in_specs=[pl.BlockSpec((FFN_TILE, D_MODEL), lambda i, te, nu: (i, 0)),
                      wspec(D_MODEL, D_EXPERT), wspec(D_MODEL, D_EXPERT), wspec(D_EXPERT, D_MODEL)],
            out_specs=pl.BlockSpec((FFN_TILE, D_MODEL), lambda i, te, nu: (i, 0))),
        out_shape=jax.ShapeDtypeStruct((n_slots, D_MODEL), F32),
        compiler_params=_cparams(("arbitrary",)),
        name="moe_ffn",
    )(tile_expert, n_used, xs, w_gate, w_up, w_down)


def _combine_kernel(pos1_ref, pos2_ref, x_ref, route_ref, lg_ref, lb_ref, ys_ref, o_ref, buf_ref, sem):
    i = pl.program_id(0)
    slot = i % 2
    n_groups = MOE_TOK // COMBINE_GROUP

    def issue_group(step, dst, g):
        first = g * COMBINE_GROUP
        base = step * MOE_TOK + first
        for u in range(COMBINE_GROUP):
            _row_copy(ys_ref, pos1_ref[base + u], buf_ref.at[dst, 0], first + u, sem.at[dst]).start()
            _row_copy(ys_ref, pos2_ref[base + u], buf_ref.at[dst, 1], first + u, sem.at[dst]).start()

    def finish_group(g):
        rows = pl.ds(pl.multiple_of(g * COMBINE_GROUP, COMBINE_GROUP), COMBINE_GROUP)
        route = route_ref[rows, :]
        ffn = route[:, 2:3] * buf_ref[slot, 0, rows, :] + route[:, 3:4] * buf_ref[slot, 1, rows, :]
        o_ref[rows, :] = _layer_norm(DN_ALPHA * x_ref[rows, :] + ffn, lg_ref[...], lb_ref[...])

    @pl.when(i == 0)
    def _():
        def prime(g, carry):
            issue_group(0, 0, g)
            return carry
        lax.fori_loop(0, n_groups, prime, 0)

    for s in range(2):
        pltpu.make_async_copy(ys_ref.at[pl.ds(0, MOE_TOK), :], buf_ref.at[slot, s], sem.at[slot]).wait()

    @pl.when(i + 1 < pl.num_programs(0))
    def _():
        def body(g, carry):
            finish_group(g)
            issue_group(i + 1, 1 - slot, g)
            return carry
        lax.fori_loop(0, n_groups, body, 0)

    @pl.when(i + 1 == pl.num_programs(0))
    def _():
        def body(g, carry):
            finish_group(g)
            return carry
        lax.fori_loop(0, n_groups, body, 0)


def _combine(pos1, pos2, x1, route, ln_g, ln_b, ys):
    n = x1.shape[0]
    row = lambda w: pl.BlockSpec((MOE_TOK, w), lambda i, p1, p2: (i, 0))
    const = lambda w: pl.BlockSpec((1, w), lambda i, p1, p2: (0, 0))
    return pl.pallas_call(
        _combine_kernel,
        grid_spec=pltpu.PrefetchScalarGridSpec(
            num_scalar_prefetch=2,
            grid=(n // MOE_TOK,),
            in_specs=[row(D_MODEL), row(LANES), const(D_MODEL), const(D_MODEL),
                      pl.BlockSpec(memory_space=pl.ANY)],
            out_specs=row(D_MODEL),
            scratch_shapes=[pltpu.VMEM((2, 2, MOE_TOK, D_MODEL), F32),
                            pltpu.SemaphoreType.DMA((2,))]),
        out_shape=jax.ShapeDtypeStruct((n, D_MODEL), F32),
        compiler_params=_cparams(("arbitrary",)),
        name="moe_combine",
    )(pos1, pos2, x1, route, ln_g, ln_b, ys)


def _moe(x1, oh, route, w_gate, w_up, w_down, layer, ln_g, ln_b, spare):
    n = x1.shape[0]
    n_slots = TOP_K_INNER * n + N_EXPERTS * FFN_TILE
    rank, cnt = _rank(oh, route)
    cnt = cnt[0, :N_EXPERTS].astype(jnp.int32)
    padded = (cnt + FFN_TILE - 1) // FFN_TILE * FFN_TILE
    ends = jnp.cumsum(padded)
    offs = ends - padded
    experts = jnp.arange(N_EXPERTS, dtype=jnp.int32)
    i1 = route[:, 0].astype(jnp.int32)
    i2 = route[:, 1].astype(jnp.int32)
    off_of = lambda idx: jnp.sum(jnp.where(idx[:, None] == experts[None, :], offs[None, :], 0), axis=1)
    pos1 = off_of(i1) + rank[:, 0].astype(jnp.int32)
    pos2 = off_of(i2) + rank[:, 1].astype(jnp.int32)
    tile_start = jnp.arange(n_slots // FFN_TILE, dtype=jnp.int32) * FFN_TILE
    tile_expert = jnp.minimum(jnp.sum(tile_start[:, None] >= ends[None, :], axis=1), N_EXPERTS - 1)
    n_used = (ends[-1] // FFN_TILE).reshape(1)

    xs = _dispatch(pos1, pos2, x1, n_slots, spare)
    ys = _ffn(tile_expert.astype(jnp.int32), n_used.astype(jnp.int32), xs, w_gate, w_up, w_down, layer)
    return _combine(pos1, pos2, x1, route, ln_g, ln_b, ys), ys


def _rope_tables(positions):
    inv_freq = ROPE_THETA ** (-jnp.arange(0, ROT_DIM, 2, dtype=F32) / ROT_DIM)
    ang = positions.astype(F32).reshape(-1, 1) * inv_freq
    cos, sin = jnp.cos(ang), jnp.sin(ang)
    n = ang.shape[0]
    half = ROT_DIM // 2
    c64 = jnp.concatenate([cos, cos, jnp.ones((n, A_HEAD_DIM - ROT_DIM), F32)], axis=1)
    s64 = jnp.concatenate([-sin, sin, jnp.zeros((n, A_HEAD_DIM - ROT_DIM), F32)], axis=1)
    del half
    return jnp.tile(c64, (1, 2)), jnp.tile(s64, (1, 2))


def _split_w_in(w):
    widths = (A_OUT, A_HEAD_DIM, A_HEAD_DIM, IDX_HEADS * IDX_DIM, IDX_DIM, IDX_HEADS,
              B_HEADS * B_KEY_DIM, B_HEADS * B_KEY_DIM, B_OUT, B_GATE_RANK, B_OUT,
              2 * C_OUT, D_MODEL, D_MODEL, D_MODEL)
    pts = np.cumsum((0,) + widths)
    return [w[:, int(pts[i]):int(pts[i + 1])] for i in range(len(widths))]


def kernel(x, positions, ln_in_g, ln_in_b, w_in, idx_k_g, gla_wa2, gla_ba, gla_norm_g, gm_ln_g, gm_ln_b, gm_ws, gm_bs, w_branch_a, w_branch_b, w_branch_c, w_out, ln1_g, ln1_b, w_rg, b_rg, w_re, b_re, w_gate, w_up, w_down, ln2_g, ln2_b):
    bsz, seq, d = x.shape
    n = bsz * seq
    cos_t, sin_t = _rope_tables(positions)
    h = _entry_ln(x.reshape(n, d), ln_in_g, ln_in_b)
    spare = None
    for l in range(DEPTH):
        (a_q, a_k, a_v, i_q, i_k, i_w, b_q, b_k, b_v, b_glr, b_r, c_uv,
         g_a, g_b, g_c) = _split_w_in(w_in[l])
        zpad = lambda c: jnp.zeros((d, c), F32)
        w_a = jnp.concatenate([a_q, a_k, a_v, i_q, i_k, i_w, zpad(LANES - IDX_DIM - IDX_HEADS)],
                              axis=1).astype(BF16)
        w_b = jnp.concatenate([b_q * (B_KEY_DIM ** -0.5), b_k, b_v, b_r,
                               b_glr, zpad(LANES - B_GATE_RANK)], axis=1).astype(BF16)
        wa_pad = jnp.concatenate([gla_wa2[l],
                                  jnp.zeros((LANES - B_GATE_RANK, GLA_QK), F32)], axis=0).astype(BF16)
        w_g = jnp.concatenate([g_a, g_b, g_c], axis=1).astype(BF16)
        ikg = jnp.concatenate([idx_k_g[l], jnp.zeros((LANES - IDX_DIM,), F32)]).reshape(1, LANES)
        bs_t = jnp.concatenate([gm_bs[l].T, jnp.zeros((C_CHUNK, LANES - C_GROUPS), F32)], axis=1)
        w_r = jnp.concatenate([w_re[l], w_rg[l], zpad(LANES - N_GROUPS - N_EXPERTS)], axis=1)
        w_r_hi = w_r.astype(BF16)
        w_r_lo = (w_r - w_r_hi.astype(F32)).astype(BF16)
        b_r_all = jnp.concatenate([b_re[l], b_rg[l],
                                   jnp.zeros((LANES - N_GROUPS - N_EXPERTS,), F32)]).reshape(1, LANES)

        q, k, v, iq, ki, wi = _proj_a(h, w_a, cos_t, sin_t, ikg)
        y_a = _dsa(q, iq, wi, k, v, ki, bsz, seq)
        y_b = _gla(h, w_b, wa_pad, gla_ba[l].reshape(1, -1), gla_norm_g[l].reshape(1, -1), bsz, seq)
        y_c = _gmlp(h, c_uv.astype(BF16), gm_ln_g[l].reshape(1, -1), gm_ln_b[l].reshape(1, -1),
                    gm_ws[l], bs_t)
        x1, oh, route = _merge(h, y_a, y_b, y_c, w_g,
                               w_branch_a[l].astype(BF16), w_branch_b[l].astype(BF16),
                               w_branch_c[l].astype(BF16), w_out[l].astype(BF16),
                               ln1_g[l].reshape(1, -1), ln1_b[l].reshape(1, -1), w_r_hi, w_r_lo, b_r_all)
        h, spare = _moe(x1, oh, route, w_gate, w_up, w_down, l,
                        ln2_g[l].reshape(1, -1), ln2_b[l].reshape(1, -1), spare)
    return h.reshape(bsz, seq, d)
```

```python
import functools

import numpy as np
import jax
import jax.numpy as jnp
from jax import lax
from jax.experimental import pallas as pl
from jax.experimental.pallas import tpu as pltpu

F32 = jnp.float32
BF16 = jnp.bfloat16

D_MODEL = 1024
DEPTH = 2
A_HEADS = 8
A_HEAD_DIM = 64
A_OUT = 512
IDX_HEADS = 4
IDX_DIM = 64
TOPK_MAX = 256
Q_BLOCK = 256
ROPE_THETA = 500000.0
ROT_DIM = 16
B_HEADS = 4
B_KEY_DIM = 64
B_VAL_DIM = 128
B_OUT = 512
B_GATE_RANK = 16
B_GATE_TAU = 16.0
B_CHUNK = 64
B_SUB = 16
N_SUB = B_CHUNK // B_SUB
C_GROUPS = 4
C_GROUP_DIM = 128
C_OUT = 512
C_CHUNK = 128
N_GROUPS = 4
EXPERTS_PER_GROUP = 8
N_EXPERTS = 32
D_EXPERT = 256
TOP_K_INNER = 2
DN_ALPHA = (2 * DEPTH) ** 0.25
LN_EPS = 1e-5
RMS_EPS = 1e-6

LANES = 128
SUBLANES = 8
NEG_BIG = -1e30
LOG2_E = 1.4426950408889634
INT_MIN = -2 ** 31
VMEM_LIMIT = 56 * 1024 * 1024


def _cparams(sem):
    return pltpu.CompilerParams(dimension_semantics=sem, vmem_limit_bytes=VMEM_LIMIT)


def _layer_norm(x, g, b):
    mu = jnp.mean(x, axis=-1, keepdims=True)
    xc = x - mu
    var = jnp.mean(xc * xc, axis=-1, keepdims=True)
    return xc * lax.rsqrt(var + LN_EPS) * g + b


def _dot(a, b):
    return jnp.dot(a, b, preferred_element_type=F32)


def _dot_nt(a, b):
    return lax.dot_general(a, b, (((1,), (1,)), ((), ())), preferred_element_type=F32)


def _ln_kernel(x_ref, g_ref, b_ref, o_ref):
    o_ref[...] = _layer_norm(x_ref[...], g_ref[...], b_ref[...])


def _entry_ln(x, g, b, tm=1024):
    n = x.shape[0]
    return pl.pallas_call(
        _ln_kernel,
        grid=(n // tm,),
        in_specs=[pl.BlockSpec((tm, D_MODEL), lambda i: (i, 0)),
                  pl.BlockSpec((1, D_MODEL), lambda i: (0, 0)),
                  pl.BlockSpec((1, D_MODEL), lambda i: (0, 0))],
        out_specs=pl.BlockSpec((tm, D_MODEL), lambda i: (i, 0)),
        out_shape=jax.ShapeDtypeStruct((n, D_MODEL), F32),
        compiler_params=_cparams(("parallel",)),
        name="entry_ln",
    )(x, g.reshape(1, -1), b.reshape(1, -1))


def _rope_slab(xs, c, s):
    lane = lax.broadcasted_iota(jnp.int32, xs.shape, 1) % A_HEAD_DIM
    fwd = pltpu.roll(xs, LANES - ROT_DIM // 2, axis=1)
    bwd = pltpu.roll(xs, ROT_DIM // 2, axis=1)
    partner = jnp.where(lane < ROT_DIM // 2, fwd, bwd)
    return xs * c + partner * s


def _proj_a_kernel(h_ref, w_ref, c_ref, s_ref, g_ref,
                   q_ref, k_ref, v_ref, iq_ref, ki_ref, wi_ref):
    p = _dot(h_ref[...].astype(BF16), w_ref[...])
    c = c_ref[...]
    s = s_ref[...]
    lane = lax.broadcasted_iota(jnp.int32, c.shape, 1)
    lo = lane < A_HEAD_DIM
    n_blk = p.shape[0] // Q_BLOCK

    def put_heads(dst_ref, slab, first_head):
        for half in range(2):
            hd = first_head + half
            cols = slab[:, half * A_HEAD_DIM:(half + 1) * A_HEAD_DIM].astype(BF16)
            for qb in range(n_blk):
                dst_ref[qb, hd * Q_BLOCK:(hd + 1) * Q_BLOCK, :] = cols[qb * Q_BLOCK:(qb + 1) * Q_BLOCK, :]

    for i in range(4):
        sl = slice(i * LANES, (i + 1) * LANES)
        put_heads(q_ref, _rope_slab(p[:, sl], c, s) * (A_HEAD_DIM ** -0.5 * LOG2_E), 2 * i)
    kv = p[:, 512:640]
    kv = _rope_slab(kv, jnp.where(lo, c, 1.0), jnp.where(lo, s, 0.0))
    k_ref[...] = kv[:, :A_HEAD_DIM].astype(BF16)
    v_ones = jnp.where(lo, pltpu.roll(kv, A_HEAD_DIM, axis=1), jnp.where(lane == A_HEAD_DIM, 1.0, 0.0))
    v_ref[...] = v_ones.astype(BF16)
    for i in range(2):
        sl = slice(640 + i * LANES, 640 + (i + 1) * LANES)
        put_heads(iq_ref, _rope_slab(p[:, sl], c, s), 2 * i)
    last = p[:, 896:1024]
    mu = jnp.sum(jnp.where(lo, last, 0.0), axis=-1, keepdims=True) * (1.0 / IDX_DIM)
    xc = jnp.where(lo, last - mu, 0.0)
    var = jnp.sum(xc * xc, axis=-1, keepdims=True) * (1.0 / IDX_DIM)
    kin = xc * lax.rsqrt(var + LN_EPS) * g_ref[...]
    kin = _rope_slab(kin, jnp.where(lo, c, 1.0), jnp.where(lo, s, 0.0))
    ki_ref[...] = kin[:, :IDX_DIM].astype(BF16)
    wi = pltpu.roll(last, LANES - IDX_DIM, axis=1) * (IDX_HEADS ** -0.5 * IDX_DIM ** -0.5)
    for qb in range(n_blk):
        wi_ref[qb] = wi[qb * Q_BLOCK:(qb + 1) * Q_BLOCK, :].T[0:SUBLANES, :]


def _proj_a(h, w_a, cos_t, sin_t, ikg, tm=512):
    n = h.shape[0]
    row = lambda w: pl.BlockSpec((tm, w), lambda i: (i, 0))
    stacked = lambda heads: pl.BlockSpec((tm // Q_BLOCK, heads * Q_BLOCK, A_HEAD_DIM), lambda i: (i, 0, 0))
    return pl.pallas_call(
        _proj_a_kernel,
        grid=(n // tm,),
        in_specs=[row(D_MODEL),
                  pl.BlockSpec((D_MODEL, 1024), lambda i: (0, 0)),
                  row(LANES), row(LANES),
                  pl.BlockSpec((1, LANES), lambda i: (0, 0))],
        out_specs=[stacked(A_HEADS), row(A_HEAD_DIM),
                   row(LANES),
                   stacked(IDX_HEADS), row(IDX_DIM),
                   pl.BlockSpec((tm // Q_BLOCK, SUBLANES, Q_BLOCK), lambda i: (i, 0, 0))],
        out_shape=[jax.ShapeDtypeStruct((n // Q_BLOCK, A_HEADS * Q_BLOCK, A_HEAD_DIM), BF16),
                   jax.ShapeDtypeStruct((n, A_HEAD_DIM), BF16),
                   jax.ShapeDtypeStruct((n, LANES), BF16),
                   jax.ShapeDtypeStruct((n // Q_BLOCK, IDX_HEADS * Q_BLOCK, IDX_DIM), BF16),
                   jax.ShapeDtypeStruct((n, IDX_DIM), BF16),
                   jax.ShapeDtypeStruct((n // Q_BLOCK, SUBLANES, Q_BLOCK), F32)],
        compiler_params=_cparams(("parallel",)),
        name="proj_a",
    )(h, w_a, cos_t, sin_t, ikg)


KEY_CHUNK = 512
COUNT_ROWS = 16
FIELD_BITS = 15
LOW_BITS = 32 - 2 * FIELD_BITS
FIELD_MASK = (1 << FIELD_BITS) - 1
FIELD_STRIDE = FIELD_BITS + 1
HALF_WORD_MASK = (1 << FIELD_STRIDE) - 1
GUARD_BITS = ((1 << FIELD_BITS) | (1 << (FIELD_BITS + FIELD_STRIDE))) - 2 ** 32
PAIR_ONE = 1 | (1 << FIELD_STRIDE)


def _dsa_kernel(q_ref, iq_ref, wi_ref, k_ref, v_ref, ki_ref, o_ref,
                key_ref, pk_ref, bias_ref, *, seq):
    j = pl.program_id(1)
    n_sel = min(TOPK_MAX, seq // 4)
    n_kc = (j * Q_BLOCK + Q_BLOCK + KEY_CHUNK - 1) // KEY_CHUNK
    qpos = j * Q_BLOCK + lax.broadcasted_iota(jnp.int32, (1, Q_BLOCK), 1)
    kk = jnp.minimum(n_sel, qpos + 1).astype(F32)
    row_pos = lax.broadcasted_iota(jnp.int32, (KEY_CHUNK, 1), 0)

    def key_rows(c):
        return pl.ds(pl.multiple_of(c * KEY_CHUNK, KEY_CHUNK), KEY_CHUNK)

    def chunk_loop(body, init):
        def pair(i, carry):
            return body(2 * i + 1, body(2 * i, carry))
        carry = lax.fori_loop(0, n_kc // 2, pair, init)
        return lax.cond(n_kc % 2 == 1, lambda cr: body(n_kc - 1, cr), lambda cr: cr, carry)

    iq = iq_ref[...]
    wi = wi_ref[...]

    def score_chunk(c, carry):
        d = jnp.maximum(_dot_nt(ki_ref[key_rows(c), :], iq), 0.0)
        score = wi[0:1, :] * d[:, 0:Q_BLOCK]
        for h in range(1, IDX_HEADS):
            score = score + wi[h:h + 1, :] * d[:, h * Q_BLOCK:(h + 1) * Q_BLOCK]
        bits = pltpu.bitcast(score + 0.0, jnp.int32)
        key = bits ^ ((bits >> 31) & 0x7FFFFFFF)
        key = jnp.where(c * KEY_CHUNK + row_pos <= qpos, key, INT_MIN)
        key_ref[c] = key
        pk_ref[c] = pack(lax.shift_right_logical(key ^ INT_MIN, FIELD_BITS + LOW_BITS))
        return carry

    def pack(f):
        hf = KEY_CHUNK // 2
        return (f[:hf] << FIELD_STRIDE) | f[hf:] | GUARD_BITS

    chunk_loop(score_chunk, 0)

    def count(pred):
        def body(c, acc):
            m = jnp.where(pred(key_ref[c]), 1.0, 0.0)
            return acc + jnp.sum(m.reshape(KEY_CHUNK // COUNT_ROWS, COUNT_ROWS, Q_BLOCK), axis=0)
        acc = chunk_loop(body, jnp.zeros((COUNT_ROWS, Q_BLOCK), F32))
        return jnp.sum(acc, axis=0, keepdims=True)

    def count_packed(cand):
        both = (cand << FIELD_STRIDE) | cand

        def body(c, acc):
            flags = ((pk_ref[c] - both) >> FIELD_BITS) & PAIR_ONE
            parts = [flags[t * COUNT_ROWS:(t + 1) * COUNT_ROWS] for t in range(KEY_CHUNK // 2 // COUNT_ROWS)]
            while len(parts) > 1:
                parts = [a + b for a, b in zip(parts[::2], parts[1::2])]
            return acc + parts[0]

        acc = chunk_loop(body, jnp.zeros((COUNT_ROWS, Q_BLOCK), jnp.int32))
        return jnp.sum(((acc & HALF_WORD_MASK) + (acc >> FIELD_STRIDE)).astype(F32), axis=0, keepdims=True)

    def search_field(target):
        def step(i, t):
            cand = t | jnp.left_shift(jnp.int32(1), FIELD_BITS - 1 - i)
            return jnp.where(count_packed(cand) >= target, cand, t)
        return lax.fori_loop(0, FIELD_BITS, step, jnp.zeros((1, Q_BLOCK), jnp.int32))

    t_a = search_field(kk)
    top_of_a = ((t_a << (FIELD_BITS + LOW_BITS)) | ((1 << (FIELD_BITS + LOW_BITS)) - 1)) ^ INT_MIN
    need_b = kk - count(lambda kc: kc > top_of_a)

    def repack(c, carry):
        u = key_ref[c] ^ INT_MIN
        member = lax.shift_right_logical(u, FIELD_BITS + LOW_BITS) == t_a
        pk_ref[c] = pack(jnp.where(member, lax.shift_right_logical(u, LOW_BITS) & FIELD_MASK, 0))
        return carry

    chunk_loop(repack, 0)
    base = (t_a << (FIELD_BITS + LOW_BITS)) | (search_field(need_b) << LOW_BITS)

    def low_step(i, t_c):
        bit = jnp.left_shift(jnp.int32(1), LOW_BITS - 1 - i)
        cand_s = (base | t_c | bit) ^ INT_MIN
        return jnp.where(count(lambda kc: kc >= cand_s) >= kk, t_c | bit, t_c)

    thr = (base | lax.fori_loop(0, LOW_BITS, low_step, jnp.zeros((1, Q_BLOCK), jnp.int32))) ^ INT_MIN
    need = kk - count(lambda kc: kc > thr)
    r_i = lax.broadcasted_iota(jnp.int32, (LANES, LANES), 0)
    c_i = lax.broadcasted_iota(jnp.int32, (LANES, LANES), 1)
    lower = jnp.where(c_i < r_i, 1.0, 0.0).astype(BF16)

    def fill(c, carry):
        for t in range(KEY_CHUNK // LANES):
            rows = slice(t * LANES, (t + 1) * LANES)
            kt = key_ref[c, rows, :]
            eq = kt == thr
            eq_f = jnp.where(eq, 1.0, 0.0)
            pre = _dot(lower, eq_f.astype(BF16)) + carry
            sel = (kt > thr) | (eq & (pre < need))
            bias_ref[c, rows, :] = jnp.where(sel, 0.0, NEG_BIG)
            carry = carry + jnp.sum(eq_f, axis=0, keepdims=True)
        return carry

    chunk_loop(fill, jnp.zeros((1, Q_BLOCK), F32))

    q = q_ref[...]
    cols = A_HEADS * Q_BLOCK

    def attn_chunk(c, carry):
        m, acc = carry
        rows = key_rows(c)
        lg = _dot_nt(k_ref[rows, :], q) + jnp.concatenate([bias_ref[c]] * A_HEADS, axis=1)
        m_new = jnp.maximum(m, jnp.max(lg, axis=0, keepdims=True))
        p = jnp.exp2(lg - m_new).astype(BF16)
        pv = lax.dot_general(v_ref[rows, :], p, (((0,), (0,)), ((), ())),
                             preferred_element_type=F32)
        return m_new, jnp.exp2(m - m_new) * acc + pv

    init = (jnp.full((1, cols), -3e38, F32), jnp.zeros((LANES, cols), F32))
    _, acc = chunk_loop(attn_chunk, init)
    o = acc[:A_HEAD_DIM] / acc[A_HEAD_DIM:A_HEAD_DIM + 1]
    for h in range(A_HEADS):
        o_ref[:, h * A_HEAD_DIM:(h + 1) * A_HEAD_DIM] = (
            o[:, h * Q_BLOCK:(h + 1) * Q_BLOCK].T.astype(BF16))


def _dsa(q, iq, wi, k, v, ki, bsz, seq):
    nq = seq // Q_BLOCK
    n_kc = seq // KEY_CHUNK
    stacked = lambda heads: pl.BlockSpec((None, heads * Q_BLOCK, A_HEAD_DIM),
                                         lambda b, j: (b * nq + j, 0, 0))
    kspec = lambda w: pl.BlockSpec((seq, w), lambda b, j: (b, 0))
    return pl.pallas_call(
        functools.partial(_dsa_kernel, seq=seq),
        grid=(bsz, nq),
        in_specs=[stacked(A_HEADS), stacked(IDX_HEADS),
                  pl.BlockSpec((None, SUBLANES, Q_BLOCK), lambda b, j: (b * nq + j, 0, 0)),
                  kspec(A_HEAD_DIM),
                  kspec(LANES),
                  kspec(IDX_DIM)],
        out_specs=pl.BlockSpec((Q_BLOCK, A_OUT), lambda b, j: (b * nq + j, 0)),
        out_shape=jax.ShapeDtypeStruct((bsz * seq, A_OUT), BF16),
        scratch_shapes=[pltpu.VMEM((n_kc, KEY_CHUNK, Q_BLOCK), jnp.int32),
                        pltpu.VMEM((n_kc, KEY_CHUNK // 2, Q_BLOCK), jnp.int32),
                        pltpu.VMEM((n_kc, KEY_CHUNK, Q_BLOCK), F32)],
        compiler_params=_cparams(("parallel", "parallel")),
        name="dsa",
    )(q, iq, wi, k, v, ki)


GLA_QK = B_HEADS * B_KEY_DIM
GLA_REP = N_SUB * GLA_QK
GLA_COLS = 2 * GLA_QK + 2 * B_OUT + LANES
GLA_V0 = 2 * GLA_QK
GLA_GROUP = 8


def _gla_kernel(h_ref, w_ref, wa_ref, ba_ref, ng_ref, o_ref, p_ref, g_ref, acc_ref, st_ref, *, ts):
    @pl.when(pl.program_id(1) == 0)
    def _():
        st_ref[...] = jnp.zeros_like(st_ref)

    p_ref[...] = _dot(h_ref[...].astype(BF16), w_ref[...])
    x = _dot(p_ref[:, GLA_V0 + 2 * B_OUT:].astype(BF16), wa_ref[...]) + ba_ref[...]
    g_ref[...] = (jnp.minimum(x, 0.0) - jnp.log1p(jnp.exp(-jnp.abs(x)))) * (1.0 / B_GATE_TAU)

    t_i = lax.broadcasted_iota(jnp.int32, (B_CHUNK, B_CHUNK), 0)
    s_i = lax.broadcasted_iota(jnp.int32, (B_CHUNK, B_CHUNK), 1)
    tri = t_i >= s_i
    tri_b = jnp.where(tri, 1.0, 0.0).astype(BF16)
    lane_sub = (lax.broadcasted_iota(jnp.int32, (1, GLA_REP), 1) // B_KEY_DIM) % N_SUB
    row_sub = lax.broadcasted_iota(jnp.int32, (B_CHUNK, 1), 0) // B_SUB
    q_mask = lane_sub <= row_sub
    k_mask = lane_sub == row_sub
    low_half = lax.broadcasted_iota(jnp.int32, (B_CHUNK, LANES), 1) < B_KEY_DIM

    def replicate(x):
        outs = []
        for pair in range(GLA_QK // LANES):
            v = x[:, pair * LANES:(pair + 1) * LANES]
            r = pltpu.roll(v, B_KEY_DIM, axis=1)
            outs += [jnp.where(low_half, v, r)] * (N_SUB // 2) + [jnp.where(low_half, r, v)] * (N_SUB // 2)
        return jnp.concatenate(outs, axis=1)

    def prepare(c):
        rows = pl.ds(pl.multiple_of(c * B_CHUNK, B_CHUNK), B_CHUNK)
        g = g_ref[rows, :]
        g_hi = g.astype(BF16)
        g_mid = (g - g_hi.astype(F32)).astype(BF16)
        g_lo = (g - g_hi.astype(F32) - g_mid.astype(F32)).astype(BF16)
        b = replicate(_dot(tri_b, g_hi) + _dot(tri_b, g_mid) + _dot(tri_b, g_lo))
        ref = jnp.zeros((1, GLA_REP), F32)
        for jj in range(1, N_SUB):
            ref = ref + jnp.where(lane_sub == jj, b[jj * B_SUB - 1:jj * B_SUB, :], 0.0)
        d = b - ref
        e_q = jnp.where(q_mask, jnp.exp(jnp.minimum(d, 0.0)), 0.0)
        e_k = jnp.where(k_mask, jnp.exp(jnp.where(k_mask, -d, 0.0)), 0.0)
        b_last = b[B_CHUNK - 1:B_CHUNK, :]
        q_c = (replicate(p_ref[rows, 0:GLA_QK]) * e_q).astype(BF16)
        k_raw = replicate(p_ref[rows, GLA_QK:2 * GLA_QK])
        k_c = (k_raw * e_k).astype(BF16)
        k_st = (k_raw * jnp.exp(b_last - b)).astype(BF16)
        return rows, q_c, k_c, k_st, jnp.exp(b_last)

    def chunk_group(i, carry):
        prep = [prepare(GLA_GROUP * i + u) for u in range(GLA_GROUP)]
        att, upd, val = {}, {}, {}
        for u, (rows, q_c, k_c, k_st, _) in enumerate(prep):
            for hh in range(B_HEADS):
                base = hh * N_SUB * B_KEY_DIM
                wide = slice(base, base + N_SUB * B_KEY_DIM)
                val[u, hh] = p_ref[rows, GLA_V0 + hh * B_VAL_DIM:GLA_V0 + (hh + 1) * B_VAL_DIM].astype(BF16)
                att[u, hh] = jnp.where(tri, _dot_nt(q_c[:, wide], k_c[:, wide]), 0.0).astype(BF16)
                upd[u, hh] = _dot(val[u, hh].T, k_st[:, base:base + B_KEY_DIM])
        for hh in range(B_HEADS):
            base = hh * N_SUB * B_KEY_DIM
            st = st_ref[hh]
            for u, (rows, q_c, _, _, e_last) in enumerate(prep):
                o = _dot(att[u, hh], val[u, hh]) + _dot_nt(q_c[:, base:base + B_KEY_DIM], st.astype(BF16))
                acc_ref[rows, hh * B_VAL_DIM:(hh + 1) * B_VAL_DIM] = o
                st = st * e_last[:, base:base + B_KEY_DIM] + upd[u, hh]
            st_ref[hh] = st
        return carry

    lax.fori_loop(0, ts // (GLA_GROUP * B_CHUNK), chunk_group, 0)

    r = p_ref[:, GLA_V0 + B_OUT:GLA_V0 + 2 * B_OUT]
    gate = r * (1.0 / (1.0 + jnp.exp(-r)))
    ng = ng_ref[...]
    for hh in range(B_HEADS):
        sl = slice(hh * B_VAL_DIM, (hh + 1) * B_VAL_DIM)
        o = acc_ref[:, sl]
        y = o * lax.rsqrt(jnp.mean(o * o, axis=-1, keepdims=True) + RMS_EPS) * ng
        o_ref[:, sl] = (gate[:, sl] * y).astype(BF16)


def _gla(h, w_b, wa, ba, norm_g, bsz, seq, ts=512):
    ns = seq // ts
    const = lambda shape: pl.BlockSpec(shape, lambda b, i: (0, 0))
    return pl.pallas_call(
        functools.partial(_gla_kernel, ts=ts),
        grid=(bsz, ns),
        in_specs=[pl.BlockSpec((ts, D_MODEL), lambda b, i: (b * ns + i, 0)),
                  const((D_MODEL, GLA_COLS)), const((LANES, GLA_QK)),
                  const((1, GLA_QK)), const((1, B_VAL_DIM))],
        out_specs=pl.BlockSpec((ts, B_OUT), lambda b, i: (b * ns + i, 0)),
        out_shape=jax.ShapeDtypeStruct((bsz * seq, B_OUT), BF16),
        scratch_shapes=[pltpu.VMEM((ts, GLA_COLS), F32),
                        pltpu.VMEM((ts, GLA_QK), F32),
                        pltpu.VMEM((ts, B_OUT), F32),
                        pltpu.VMEM((B_HEADS, B_VAL_DIM, B_KEY_DIM), F32)],
        compiler_params=_cparams(("parallel", "arbitrary")),
        name="gla",
    )(h, w_b, wa, ba, norm_g)


def _gmlp_kernel(h_ref, w_ref, lg_ref, lb_ref, ws_ref, bs_ref, o_ref, *, tm):
    z = _dot(h_ref[...].astype(BF16), w_ref[...])
    z = z * (0.5 * (1.0 + jnp.tanh(np.sqrt(2.0 / np.pi) * (z + 0.044715 * (z * z * z)))))
    u = z[:, :C_OUT]
    v = _layer_norm(z[:, C_OUT:], lg_ref[...], lb_ref[...]).astype(BF16)
    t_i = lax.broadcasted_iota(jnp.int32, (C_CHUNK, C_CHUNK), 0)
    s_i = lax.broadcasted_iota(jnp.int32, (C_CHUNK, C_CHUNK), 1)
    bs = bs_ref[...]
    for g in range(C_GROUPS):
        w = jnp.where(t_i >= s_i, ws_ref[g], 0.0).astype(BF16)
        sl = slice(g * C_GROUP_DIM, (g + 1) * C_GROUP_DIM)
        for c in range(tm // C_CHUNK):
            rows = slice(c * C_CHUNK, (c + 1) * C_CHUNK)
            mixed = _dot(w, v[rows, sl]) + bs[:, g:g + 1]
            o_ref[rows, sl] = (u[rows, sl] * mixed).astype(BF16)


def _gmlp(h, w_c, ln_g, ln_b, ws, bs_t, tm=512):
    n = h.shape[0]
    return pl.pallas_call(
        functools.partial(_gmlp_kernel, tm=tm),
        grid=(n // tm,),
        in_specs=[pl.BlockSpec((tm, D_MODEL), lambda i: (i, 0)),
                  pl.BlockSpec((D_MODEL, 2 * C_OUT), lambda i: (0, 0)),
                  pl.BlockSpec((1, C_OUT), lambda i: (0, 0)),
                  pl.BlockSpec((1, C_OUT), lambda i: (0, 0)),
                  pl.BlockSpec((C_GROUPS, C_CHUNK, C_CHUNK), lambda i: (0, 0, 0)),
                  pl.BlockSpec((C_CHUNK, LANES), lambda i: (0, 0))],
        out_specs=pl.BlockSpec((tm, C_OUT), lambda i: (i, 0)),
        out_shape=jax.ShapeDtypeStruct((n, C_OUT), BF16),
        compiler_params=_cparams(("parallel",)),
        name="gmlp",
    )(h, w_c, ln_g, ln_b, ws, bs_t)


def _sigmoid(x):
    return 1.0 / (1.0 + jnp.exp(-x))


def _merge_kernel(h_ref, ya_ref, yb_ref, yc_ref, wg_ref, wa_ref, wb_ref, wc_ref, wo_ref,
                  lg_ref, lb_ref, wr_ref, wrl_ref, br_ref, x_ref, oh_ref, route_ref):
    h = h_ref[...]
    hb = h.astype(BF16)
    acc = None
    for i, (y_ref, wbr_ref) in enumerate(((ya_ref, wa_ref), (yb_ref, wb_ref), (yc_ref, wc_ref))):
        gate = _sigmoid(_dot(hb, wg_ref[:, i * D_MODEL:(i + 1) * D_MODEL]))
        term = gate * _dot(y_ref[...], wbr_ref[...])
        acc = term if acc is None else acc + term
    mix = _dot(acc.astype(BF16), wo_ref[...])
    x1 = _layer_norm(DN_ALPHA * h + mix, lg_ref[...], lb_ref[...])
    x_ref[...] = x1

    x_hi = x1.astype(BF16)
    x_lo = (x1 - x_hi.astype(F32)).astype(BF16)
    logit = (_dot(x_hi, wr_ref[...]) + _dot(x_hi, wrl_ref[...]) + _dot(x_lo, wr_ref[...])) + br_ref[...]
    lane = lax.broadcasted_iota(jnp.int32, logit.shape, 1)
    big = jnp.int32(LANES)
    is_g = (lane >= N_EXPERTS) & (lane < N_EXPERTS + N_GROUPS)
    gl = jnp.where(is_g, logit, -jnp.inf)
    g_max = jnp.max(gl, axis=-1, keepdims=True)
    g_idx = jnp.min(jnp.where(is_g & (gl == g_max), lane, big), axis=-1, keepdims=True) - N_EXPERTS
    g_p = 1.0 / jnp.sum(jnp.where(is_g, jnp.exp(gl - g_max), 0.0), axis=-1, keepdims=True)
    lo = g_idx * EXPERTS_PER_GROUP
    in_g = (lane >= lo) & (lane < lo + EXPERTS_PER_GROUP)
    el = jnp.where(in_g, logit, -jnp.inf)
    e1 = jnp.max(el, axis=-1, keepdims=True)
    i1 = jnp.min(jnp.where(in_g & (el == e1), lane, big), axis=-1, keepdims=True)
    el2 = jnp.where(lane == i1, -jnp.inf, el)
    e2 = jnp.max(el2, axis=-1, keepdims=True)
    i2 = jnp.min(jnp.where(in_g & (lane != i1) & (el2 == e2), lane, big), axis=-1, keepdims=True)
    z2 = jnp.exp(e2 - e1)
    den = 1.0 + z2
    oh_ref[...] = jnp.where((lane == i1) | (lane == i2), 1.0, 0.0).astype(BF16)
    route_ref[...] = jnp.where(lane == 0, i1.astype(F32),
                               jnp.where(lane == 1, i2.astype(F32),
                                         jnp.where(lane == 2, (1.0 / den) * g_p,
                                                   jnp.where(lane == 3, (z2 / den) * g_p, 0.0))))


def _merge(h, ya, yb, yc, wg, wa, wb, wc, wo, ln_g, ln_b, wr, wr_lo, br, tm=1024):
    n = h.shape[0]
    row = lambda w: pl.BlockSpec((tm, w), lambda i: (i, 0))
    const = lambda r, c: pl.BlockSpec((r, c), lambda i: (0, 0), pipeline_mode=pl.Buffered(1))
    return pl.pallas_call(
        _merge_kernel,
        grid=(n // tm,),
        in_specs=[row(D_MODEL), row(A_OUT), row(B_OUT), row(C_OUT),
                  const(D_MODEL, 3 * D_MODEL), const(A_OUT, D_MODEL), const(B_OUT, D_MODEL),
                  const(C_OUT, D_MODEL), const(D_MODEL, D_MODEL),
                  const(1, D_MODEL), const(1, D_MODEL), const(D_MODEL, LANES), const(D_MODEL, LANES),
                  const(1, LANES)],
        out_specs=[row(D_MODEL), row(LANES), row(LANES)],
        out_shape=[jax.ShapeDtypeStruct((n, D_MODEL), F32),
                   jax.ShapeDtypeStruct((n, LANES), BF16),
                   jax.ShapeDtypeStruct((n, LANES), F32)],
        compiler_params=_cparams(("parallel",)),
        name="merge",
    )(h, ya, yb, yc, wg, wa, wb, wc, wo, ln_g, ln_b, wr, wr_lo, br)


FFN_TILE = 512
MOE_TOK = 512


def _rank_kernel(oh_ref, route_ref, r_ref, cnt_ref, carry_ref):
    @pl.when(pl.program_id(0) == 0)
    def _():
        carry_ref[...] = jnp.zeros_like(carry_ref)

    oh = oh_ref[...]
    tt = oh.shape[0]
    r_i = lax.broadcasted_iota(jnp.int32, (tt, tt), 0)
    c_i = lax.broadcasted_iota(jnp.int32, (tt, tt), 1)
    lower = jnp.where(c_i < r_i, 1.0, 0.0).astype(BF16)
    rank = _dot(lower, oh) + carry_ref[...]
    route = route_ref[...]
    lane = lax.broadcasted_iota(jnp.int32, rank.shape, 1)
    i1 = route[:, 0:1].astype(jnp.int32)
    i2 = route[:, 1:2].astype(jnp.int32)
    r1 = jnp.sum(jnp.where(lane == i1, rank, 0.0), axis=-1, keepdims=True)
    r2 = jnp.sum(jnp.where(lane == i2, rank, 0.0), axis=-1, keepdims=True)
    r_ref[...] = jnp.where(lane == 0, r1, jnp.where(lane == 1, r2, 0.0))
    carry_ref[...] += jnp.sum(oh.astype(F32), axis=0, keepdims=True)
    cnt_ref[...] = carry_ref[...]


def _rank(oh, route, tt=512):
    n = oh.shape[0]
    return pl.pallas_call(
        _rank_kernel,
        grid=(n // tt,),
        in_specs=[pl.BlockSpec((tt, LANES), lambda i: (i, 0)),
                  pl.BlockSpec((tt, LANES), lambda i: (i, 0))],
        out_specs=[pl.BlockSpec((tt, LANES), lambda i: (i, 0)),
                   pl.BlockSpec((1, LANES), lambda i: (0, 0))],
        out_shape=[jax.ShapeDtypeStruct((n, LANES), F32),
                   jax.ShapeDtypeStruct((1, LANES), F32)],
        scratch_shapes=[pltpu.VMEM((1, LANES), F32)],
        compiler_params=_cparams(("arbitrary",)),
        name="moe_rank",
    )(oh, route)


def _row_copy(src_ref, src_row, dst_ref, dst_row, sem):
    return pltpu.make_async_copy(src_ref.at[pl.ds(src_row, 1), :], dst_ref.at[pl.ds(dst_row, 1), :], sem)


def _dispatch_kernel(pos1_ref, pos2_ref, x_ref, init_ref, xs_ref, sem):
    del init_ref
    base = pl.program_id(0) * MOE_TOK

    def issue(t, carry):
        _row_copy(x_ref, t, xs_ref, pos1_ref[base + t], sem).start()
        _row_copy(x_ref, t, xs_ref, pos2_ref[base + t], sem).start(priority=1)
        return carry

    lax.fori_loop(0, MOE_TOK, issue, 0, unroll=32)
    for _ in range(2):
        pltpu.make_async_copy(x_ref, xs_ref.at[pl.ds(0, MOE_TOK), :], sem).wait()


def _dispatch(pos1, pos2, x1, n_slots, spare):
    n = x1.shape[0]
    init = jnp.zeros((n_slots, D_MODEL), F32) if spare is None else spare
    return pl.pallas_call(
        _dispatch_kernel,
        grid_spec=pltpu.PrefetchScalarGridSpec(
            num_scalar_prefetch=2,
            grid=(n // MOE_TOK,),
            in_specs=[pl.BlockSpec((MOE_TOK, D_MODEL), lambda i, p1, p2: (i, 0)),
                      pl.BlockSpec(memory_space=pl.ANY)],
            out_specs=pl.BlockSpec(memory_space=pl.ANY),
            scratch_shapes=[pltpu.SemaphoreType.DMA(())]),
        out_shape=jax.ShapeDtypeStruct((n_slots, D_MODEL), F32),
        input_output_aliases={3: 0},
        compiler_params=_cparams(("arbitrary",)),
        name="moe_dispatch",
    )(pos1, pos2, x1, init)


def _ffn_kernel(te_ref, nu_ref, xs_ref, wg_ref, wu_ref, wd_ref, ys_ref):
    del te_ref
    i = pl.program_id(0)

    @pl.when(i < nu_ref[0])
    def _():
        xb = xs_ref[...].astype(BF16)
        gte = _dot(xb, wg_ref[...].astype(BF16))
        hid = gte * _sigmoid(gte) * _dot(xb, wu_ref[...].astype(BF16))
        ys_ref[...] = _dot(hid.astype(BF16), wd_ref[...].astype(BF16))

    @pl.when(i >= nu_ref[0])
    def _():
        ys_ref[...] = jnp.zeros_like(ys_ref)


def _ffn(tile_expert, n_used, xs, w_gate, w_up, w_down, layer):
    n_slots = xs.shape[0]
    wspec = lambda r, c: pl.BlockSpec((None, None, r, c), lambda i, te, nu: (layer, te[i], 0, 0))
    return pl.pallas_call(
        _ffn_kernel,
        grid_spec=pltpu.PrefetchScalarGridSpec(
            num_scalar_prefetch=2,
            grid=(n_slots // FFN_TILE,),
            in_specs=[pl.BlockSpec((FFN_TILE, D_MODEL), lambda i, te, nu: (i, 0)),
                      wspec(D_MODEL, D_EXPERT), wspec(D_MODEL, D_EXPERT), wspec(D_EXPERT, D_MODEL)],
            out_specs=pl.BlockSpec((FFN_TILE, D_MODEL), lambda i, te, nu: (i, 0))),
        out_shape=jax.ShapeDtypeStruct((n_slots, D_MODEL), F32),
        compiler_params=_cparams(("arbitrary",)),
        name="moe_ffn",
    )(tile_expert, n_used, xs, w_gate, w_up, w_down)


def _combine_kernel(pos1_ref, pos2_ref, x_ref, route_ref, lg_ref, lb_ref, ys_ref, o_ref, buf_ref, sem):
    i = pl.program_id(0)

    def gather(step, slot):
        base = step * MOE_TOK

        def issue(t, carry):
            _row_copy(ys_ref, pos1_ref[base + t], buf_ref.at[slot, 0], t, sem.at[slot]).start()
            _row_copy(ys_ref, pos2_ref[base + t], buf_ref.at[slot, 1], t, sem.at[slot]).start(priority=1)
            return carry

        lax.fori_loop(0, MOE_TOK, issue, 0, unroll=32)

    @pl.when(i == 0)
    def _():
        gather(0, 0)

    @pl.when(i + 1 < pl.num_programs(0))
    def _():
        gather(i + 1, (i + 1) % 2)

    slot = i % 2
    for s in range(2):
        pltpu.make_async_copy(ys_ref.at[pl.ds(0, MOE_TOK), :], buf_ref.at[slot, s], sem.at[slot]).wait()
    route = route_ref[...]
    ffn = route[:, 2:3] * buf_ref[slot, 0] + route[:, 3:4] * buf_ref[slot, 1]
    o_ref[...] = _layer_norm(DN_ALPHA * x_ref[...] + ffn, lg_ref[...], lb_ref[...])


def _combine(pos1, pos2, x1, route, ln_g, ln_b, ys):
    n = x1.shape[0]
    row = lambda w: pl.BlockSpec((MOE_TOK, w), lambda i, p1, p2: (i, 0))
    const = lambda w: pl.BlockSpec((1, w), lambda i, p1, p2: (0, 0))
    return pl.pallas_call(
        _combine_kernel,
        grid_spec=pltpu.PrefetchScalarGridSpec(
            num_scalar_prefetch=2,
            grid=(n // MOE_TOK,),
            in_specs=[row(D_MODEL), row(LANES), const(D_MODEL), const(D_MODEL),
                      pl.BlockSpec(memory_space=pl.ANY)],
            out_specs=row(D_MODEL),
            scratch_shapes=[pltpu.VMEM((2, 2, MOE_TOK, D_MODEL), F32),
                            pltpu.SemaphoreType.DMA((2,))]),
        out_shape=jax.ShapeDtypeStruct((n, D_MODEL), F32),
        compiler_params=_cparams(("arbitrary",)),
        name="moe_combine",
    )(pos1, pos2, x1, route, ln_g, ln_b, ys)


def _moe(x1, oh, route, w_gate, w_up, w_down, layer, ln_g, ln_b, spare):
    n = x1.shape[0]
    n_slots = TOP_K_INNER * n + N_EXPERTS * FFN_TILE
    rank, cnt = _rank(oh, route)
    cnt = cnt[0, :N_EXPERTS].astype(jnp.int32)
    padded = (cnt + FFN_TILE - 1) // FFN_TILE * FFN_TILE
    ends = jnp.cumsum(padded)
    offs = ends - padded
    experts = jnp.arange(N_EXPERTS, dtype=jnp.int32)
    i1 = route[:, 0].astype(jnp.int32)
    i2 = route[:, 1].astype(jnp.int32)
    off_of = lambda idx: jnp.sum(jnp.where(idx[:, None] == experts[None, :], offs[None, :], 0), axis=1)
    pos1 = off_of(i1) + rank[:, 0].astype(jnp.int32)
    pos2 = off_of(i2) + rank[:, 1].astype(jnp.int32)
    tile_start = jnp.arange(n_slots // FFN_TILE, dtype=jnp.int32) * FFN_TILE
    tile_expert = jnp.minimum(jnp.sum(tile_start[:, None] >= ends[None, :], axis=1), N_EXPERTS - 1)
    n_used = (ends[-1] // FFN_TILE).reshape(1)

    xs = _dispatch(pos1, pos2, x1, n_slots, spare)
    ys = _ffn(tile_expert.astype(jnp.int32), n_used.astype(jnp.int32), xs, w_gate, w_up, w_down, layer)
    return _combine(pos1, pos2, x1, route, ln_g, ln_b, ys), ys


def _rope_tables(positions):
    inv_freq = ROPE_THETA ** (-jnp.arange(0, ROT_DIM, 2, dtype=F32) / ROT_DIM)
    ang = positions.astype(F32).reshape(-1, 1) * inv_freq
    cos, sin = jnp.cos(ang), jnp.sin(ang)
    n = ang.shape[0]
    half = ROT_DIM // 2
    c64 = jnp.concatenate([cos, cos, jnp.ones((n, A_HEAD_DIM - ROT_DIM), F32)], axis=1)
    s64 = jnp.concatenate([-sin, sin, jnp.zeros((n, A_HEAD_DIM - ROT_DIM), F32)], axis=1)
    del half
    return jnp.tile(c64, (1, 2)), jnp.tile(s64, (1, 2))


def _split_w_in(w):
    widths = (A_OUT, A_HEAD_DIM, A_HEAD_DIM, IDX_HEADS * IDX_DIM, IDX_DIM, IDX_HEADS,
              B_HEADS * B_KEY_DIM, B_HEADS * B_KEY_DIM, B_OUT, B_GATE_RANK, B_OUT,
              2 * C_OUT, D_MODEL, D_MODEL, D_MODEL)
    pts = np.cumsum((0,) + widths)
    return [w[:, int(pts[i]):int(pts[i + 1])] for i in range(len(widths))]


def kernel(x, positions, ln_in_g, ln_in_b, w_in, idx_k_g, gla_wa2, gla_ba, gla_norm_g, gm_ln_g, gm_ln_b, gm_ws, gm_bs, w_branch_a, w_branch_b, w_branch_c, w_out, ln1_g, ln1_b, w_rg, b_rg, w_re, b_re, w_gate, w_up, w_down, ln2_g, ln2_b):
    bsz, seq, d = x.shape
    n = bsz * seq
    cos_t, sin_t = _rope_tables(positions)
    h = _entry_ln(x.reshape(n, d), ln_in_g, ln_in_b)
    spare = None
    for l in range(DEPTH):
        (a_q, a_k, a_v, i_q, i_k, i_w, b_q, b_k, b_v, b_glr, b_r, c_uv,
         g_a, g_b, g_c) = _split_w_in(w_in[l])
        zpad = lambda c: jnp.zeros((d, c), F32)
        w_a = jnp.concatenate([a_q, a_k, a_v, i_q, i_k, i_w, zpad(LANES - IDX_DIM - IDX_HEADS)],
                              axis=1).astype(BF16)
        w_b = jnp.concatenate([b_q * (B_KEY_DIM ** -0.5), b_k, b_v, b_r,
                               b_glr, zpad(LANES - B_GATE_RANK)], axis=1).astype(BF16)
        wa_pad = jnp.concatenate([gla_wa2[l],
                                  jnp.zeros((LANES - B_GATE_RANK, GLA_QK), F32)], axis=0).astype(BF16)
        w_g = jnp.concatenate([g_a, g_b, g_c], axis=1).astype(BF16)
        ikg = jnp.concatenate([idx_k_g[l], jnp.zeros((LANES - IDX_DIM,), F32)]).reshape(1, LANES)
        bs_t = jnp.concatenate([gm_bs[l].T, jnp.zeros((C_CHUNK, LANES - C_GROUPS), F32)], axis=1)
        w_r = jnp.concatenate([w_re[l], w_rg[l], zpad(LANES - N_GROUPS - N_EXPERTS)], axis=1)
        w_r_hi = w_r.astype(BF16)
        w_r_lo = (w_r - w_r_hi.astype(F32)).astype(BF16)
        b_r_all = jnp.concatenate([b_re[l], b_rg[l],
                                   jnp.zeros((LANES - N_GROUPS - N_EXPERTS,), F32)]).reshape(1, LANES)

        q, k, v, iq, ki, wi = _proj_a(h, w_a, cos_t, sin_t, ikg)
        y_a = _dsa(q, iq, wi, k, v, ki, bsz, seq)
        y_b = _gla(h, w_b, wa_pad, gla_ba[l].reshape(1, -1), gla_norm_g[l].reshape(1, -1), bsz, seq)
        y_c = _gmlp(h, c_uv.astype(BF16), gm_ln_g[l].reshape(1, -1), gm_ln_b[l].reshape(1, -1),
                    gm_ws[l], bs_t)
        x1, oh, route = _merge(h, y_a, y_b, y_c, w_g,
                               w_branch_a[l].astype(BF16), w_branch_b[l].astype(BF16),
                               w_branch_c[l].astype(BF16), w_out[l].astype(BF16),
                               ln1_g[l].reshape(1, -1), ln1_b[l].reshape(1, -1), w_r_hi, w_r_lo, b_r_all)
        h, spare = _moe(x1, oh, route, w_gate, w_up, w_down, l,
                        ln2_g[l].reshape(1, -1), ln2_b[l].reshape(1, -1), spare)
    return h.reshape(bsz, seq, d)
```
